```python
import jax, jax.numpy as jnp
from jax import lax
import numpy as np

D_MODEL = 2048
BATCH = 8
SEQ = 2048
DEPTH = 4

CHUNK = 64
MEM_LEN = 256
Q_BLOCK = 128
N_MIXERS = 2
N_RET_LAYERS = (DEPTH + 1) // 2
N_MLA_LAYERS = DEPTH // 2

RET_HEADS = 8
RET_QK_DIM = D_MODEL // RET_HEADS
RET_V_DIM = 2 * D_MODEL // RET_HEADS
RET_QK_WIDTH = RET_HEADS * RET_QK_DIM
RET_V_WIDTH = RET_HEADS * RET_V_DIM
RET_IN_WIDTH = 2 * RET_QK_WIDTH + 2 * RET_V_WIDTH

MLA_HEADS = 16
MLA_Q_RANK = 512
MLA_KV_RANK = 512
MLA_NOPE = 128
MLA_ROPE = 64
MLA_V = 128
MLA_IN_WIDTH = MLA_Q_RANK + MLA_KV_RANK + MLA_ROPE

XA_HEADS = 4
XA_DIM = D_MODEL // XA_HEADS

D_FF = 5632
CONV_W = 3

ROPE_BASE = 10000.0
LN_EPS = 1e-5
RMS_EPS = 1e-6
NEG_INF = -1e30
DEEPNORM_ALPHA = (2 * DEPTH) ** 0.25
DEEPNORM_BETA = (8 * DEPTH) ** -0.25

kernel_name = "hybrid_retention_mla_streaming_encoder"


def layer_norm(x, g, b):
    xf = x.astype(jnp.float32)
    mu = jnp.mean(xf, -1, keepdims=True)
    var = jnp.mean(jnp.square(xf - mu), -1, keepdims=True)
    y = (xf - mu) * lax.rsqrt(var + LN_EPS)
    return (y * g.astype(jnp.float32) + b.astype(jnp.float32)).astype(x.dtype)


def rms_norm(x, g):
    xf = x.astype(jnp.float32)
    y = xf * lax.rsqrt(jnp.mean(xf * xf, -1, keepdims=True) + RMS_EPS)
    return (y * g.astype(jnp.float32)).astype(x.dtype)


def head_group_norm(o, g):
    B, S, H, dv = o.shape
    of = o.astype(jnp.float32)
    mu = jnp.mean(of, -1, keepdims=True)
    var = jnp.mean(jnp.square(of - mu), -1, keepdims=True)
    y = ((of - mu) * lax.rsqrt(var + LN_EPS)).reshape(B, S, H * dv)
    return (y * g.astype(jnp.float32)).astype(o.dtype)


def rotary(x, positions):
    d = x.shape[-1]
    inv_freq = ROPE_BASE ** (-jnp.arange(0, d, 2, dtype=jnp.float32) / d)
    ang = positions.astype(jnp.float32)[..., None] * inv_freq
    cos = jnp.cos(ang)[:, :, None, :]
    sin = jnp.sin(ang)[:, :, None, :]
    xf = x.astype(jnp.float32)
    x1, x2 = xf[..., : d // 2], xf[..., d // 2:]
    out = jnp.concatenate([x1 * cos - x2 * sin, x2 * cos + x1 * sin], -1)
    return out.astype(x.dtype)


def retention_mixer(x, positions, w_in, gn_g, w_out):
    B, S, _ = x.shape
    nc = S // CHUNK
    dt = x.dtype
    proj = x @ w_in
    q, k, v, g = jnp.split(proj, [RET_QK_WIDTH, 2 * RET_QK_WIDTH,
                                  2 * RET_QK_WIDTH + RET_V_WIDTH], axis=-1)
    q = rotary(q.reshape(B, S, RET_HEADS, RET_QK_DIM), positions)
    k = rotary(k.reshape(B, S, RET_HEADS, RET_QK_DIM), positions) * (RET_QK_DIM ** -0.5)
    v = v.reshape(B, S, RET_HEADS, RET_V_DIM)

    log_gamma = jnp.log(1.0 - 2.0 ** (-5.0 - jnp.arange(RET_HEADS, dtype=jnp.float32)))
    idx = jnp.arange(CHUNK, dtype=jnp.float32)
    intra = jnp.exp(log_gamma[:, None, None] * jnp.abs(idx[:, None] - idx[None, :])).astype(dt)
    q_decay = jnp.exp(idx[:, None] * log_gamma[None, :]).astype(dt)
    k_decay = jnp.exp((CHUNK - idx)[:, None] * log_gamma[None, :]).astype(dt)
    chunk_decay = jnp.exp(CHUNK * log_gamma).astype(dt)

    def to_chunks(t):
        return jnp.moveaxis(t.reshape(B, nc, CHUNK, RET_HEADS, t.shape[-1]), 1, 0)

    def step(state, qkv):
        qc, kc, vc = qkv
        scores = jnp.einsum('bihd,bjhd->bhij', qc, kc) * intra
        o_intra = jnp.einsum('bhij,bjhe->bihe', scores, vc)
        o_cross = jnp.einsum('bihd,bhde->bihe', qc, state) * q_decay[None, :, :, None]
        new_state = (state * chunk_decay[None, :, None, None]
                     + jnp.einsum('bjhd,bjhe->bhde', kc * k_decay[None, :, :, None], vc))
        return new_state, o_intra + o_cross

    state0 = jnp.zeros((B, RET_HEADS, RET_QK_DIM, RET_V_DIM), dt)
    _, o = lax.scan(step, state0, (to_chunks(q), to_chunks(k), to_chunks(v)))
    o = jnp.moveaxis(o, 0, 1).reshape(B, S, RET_HEADS, RET_V_DIM)
    y = jax.nn.silu(g) * head_group_norm(o, gn_g)
    return y @ w_out


def mla_mixer(x, positions, w_in, q_norm_g, w_uq, kv_norm_g, w_ukv, w_out):
    B, S, _ = x.shape
    proj = x @ w_in
    c_q, c_kv, k_rope = jnp.split(proj, [MLA_Q_RANK, MLA_Q_RANK + MLA_KV_RANK], axis=-1)
    c_q = rms_norm(c_q, q_norm_g)
    c_kv = rms_norm(c_kv, kv_norm_g)
    q = (c_q @ w_uq).reshape(B, S, MLA_HEADS, MLA_NOPE + MLA_ROPE)
    q_nope = q[..., :MLA_NOPE]
    q_rope = rotary(q[..., MLA_NOPE:], positions)
    k_rope = rotary(k_rope[:, :, None, :], positions)[:, :, 0]
    kv = (c_kv @ w_ukv).reshape(B, S, MLA_HEADS, MLA_NOPE + MLA_V)
    k_nope, v = kv[..., :MLA_NOPE], kv[..., MLA_NOPE:]
    scale = (MLA_NOPE + MLA_ROPE) ** -0.5
    chunk_id = jnp.arange(S) // CHUNK
    outs = []
    for start in range(0, S, Q_BLOCK):
        end = start + Q_BLOCK
        s = (jnp.einsum('bqhd,bkhd->bhqk', q_nope[:, start:end], k_nope[:, :end])
             + jnp.einsum('bqhd,bkd->bhqk', q_rope[:, start:end], k_rope[:, :end]))
        s = s.astype(jnp.float32) * scale
        mask = chunk_id[start:end, None] >= chunk_id[None, :end]
        s = jnp.where(mask[None, None], s, NEG_INF)
        p = jax.nn.softmax(s, axis=-1).astype(v.dtype)
        outs.append(jnp.einsum('bhqk,bkhe->bqhe', p, v[:, :end]))
    o = jnp.concatenate(outs, axis=1).reshape(B, S, MLA_HEADS * MLA_V)
    return o @ w_out


def memory_cross_attention(x, mem, w_q, w_kv, w_out):
    B, S, _ = x.shape
    M = mem.shape[1]
    q = (x @ w_q).reshape(B, S, XA_HEADS, XA_DIM)
    k, v = jnp.split(mem @ w_kv, 2, axis=-1)
    k = k.reshape(B, M, XA_HEADS, XA_DIM)
    v = v.reshape(B, M, XA_HEADS, XA_DIM)
    s = jnp.einsum('bqhd,bkhd->bhqk', q, k).astype(jnp.float32) * (XA_DIM ** -0.5)
    p = jax.nn.softmax(s, axis=-1).astype(v.dtype)
    o = jnp.einsum('bhqk,bkhd->bqhd', p, v).reshape(B, S, XA_HEADS * XA_DIM)
    return o @ w_out


def conv_ffn(x, w_up, conv_w, conv_b, w_down):
    h = x @ w_up
    h = lax.conv_general_dilated(h, conv_w[:, None, :], window_strides=(1,),
                                 padding=[(CONV_W - 1, 0)],
                                 dimension_numbers=('NWC', 'WIO', 'NWC'),
                                 feature_group_count=2 * D_FF) + conv_b
    gate, val = jnp.split(h, 2, axis=-1)
    return (jax.nn.silu(gate) * val) @ w_down


def _fwd_setup_inputs(seed: int = 0) -> dict:
    key = jax.random.key(seed)
    ks = iter(jax.random.split(key, 32))

    def w(shape, fan_in, scale=1.0):
        return jax.random.normal(next(ks), shape, jnp.float32) * (scale * fan_in ** -0.5)

    def gain(shape):
        return 1.0 + 0.01 * jax.random.normal(next(ks), shape, jnp.float32)

    def bias(shape):
        return 0.01 * jax.random.normal(next(ks), shape, jnp.float32)

    x = jax.random.normal(next(ks), (BATCH, SEQ, D_MODEL), jnp.float32)
    mem = jax.random.normal(next(ks), (BATCH, MEM_LEN, D_MODEL), jnp.float32)
    offset = jax.random.randint(next(ks), (BATCH, 1), 0, 4096, jnp.int32)
    positions = offset + jnp.arange(SEQ, dtype=jnp.int32)[None, :]
    return {
        "x": x,
        "mem": mem,
        "positions": positions,
        "ret_w_in": w((N_RET_LAYERS, D_MODEL, RET_IN_WIDTH), D_MODEL),
        "ret_gn_g": gain((N_RET_LAYERS, RET_V_WIDTH)),
        "ret_w_out": w((N_RET_LAYERS, RET_V_WIDTH, D_MODEL), RET_V_WIDTH, DEEPNORM_BETA),
        "mla_w_in": w((N_MLA_LAYERS, D_MODEL, MLA_IN_WIDTH), D_MODEL),
        "mla_q_norm_g": gain((N_MLA_LAYERS, MLA_Q_RANK)),
        "mla_w_uq": w((N_MLA_LAYERS, MLA_Q_RANK, MLA_HEADS * (MLA_NOPE + MLA_ROPE)), MLA_Q_RANK),
        "mla_kv_norm_g": gain((N_MLA_LAYERS, MLA_KV_RANK)),
        "mla_w_ukv": w((N_MLA_LAYERS, MLA_KV_RANK, MLA_HEADS * (MLA_NOPE + MLA_V)), MLA_KV_RANK),
        "mla_w_out": w((N_MLA_LAYERS, MLA_HEADS * MLA_V, D_MODEL), MLA_HEADS * MLA_V, DEEPNORM_BETA),
        "xa_w_q": w((DEPTH, D_MODEL, XA_HEADS * XA_DIM), D_MODEL),
        "xa_w_kv": w((DEPTH, D_MODEL, 2 * XA_HEADS * XA_DIM), D_MODEL),
        "xa_w_out": w((DEPTH, XA_HEADS * XA_DIM, D_MODEL), XA_HEADS * XA_DIM, DEEPNORM_BETA),
        "ffn_w_up": w((DEPTH, D_MODEL, 2 * D_FF), D_MODEL),
        "ffn_conv_w": w((DEPTH, CONV_W, 2 * D_FF), CONV_W),
        "ffn_conv_b": bias((DEPTH, 2 * D_FF)),
        "ffn_w_down": w((DEPTH, D_FF, D_MODEL), D_FF, DEEPNORM_BETA),
        "ln_mix_g": gain((DEPTH, D_MODEL)),
        "ln_mix_b": bias((DEPTH, D_MODEL)),
        "ln_mem_g": gain((DEPTH, D_MODEL)),
        "ln_mem_b": bias((DEPTH, D_MODEL)),
        "ln_ffn_g": gain((DEPTH, D_MODEL)),
        "ln_ffn_b": bias((DEPTH, D_MODEL)),
    }


def _fwd_reference(x, mem, positions, ret_w_in, ret_gn_g, ret_w_out, mla_w_in, mla_q_norm_g,
              mla_w_uq, mla_kv_norm_g, mla_w_ukv, mla_w_out, xa_w_q, xa_w_kv, xa_w_out,
              ffn_w_up, ffn_conv_w, ffn_conv_b, ffn_w_down, ln_mix_g, ln_mix_b,
              ln_mem_g, ln_mem_b, ln_ffn_g, ln_ffn_b):
    h = x
    for layer in range(DEPTH):
        j = layer // N_MIXERS
        if layer % N_MIXERS == 0:
            mix = retention_mixer(h, positions, ret_w_in[j], ret_gn_g[j], ret_w_out[j])
        else:
            mix = mla_mixer(h, positions, mla_w_in[j], mla_q_norm_g[j], mla_w_uq[j],
                            mla_kv_norm_g[j], mla_w_ukv[j], mla_w_out[j])
        h = layer_norm(DEEPNORM_ALPHA * h + mix, ln_mix_g[layer], ln_mix_b[layer])
        h = layer_norm(DEEPNORM_ALPHA * h
                       + memory_cross_attention(h, mem, xa_w_q[layer], xa_w_kv[layer], xa_w_out[layer]),
                       ln_mem_g[layer], ln_mem_b[layer])
        h = layer_norm(DEEPNORM_ALPHA * h
                       + conv_ffn(h, ffn_w_up[layer], ffn_conv_w[layer], ffn_conv_b[layer], ffn_w_down[layer]),
                       ln_ffn_g[layer], ln_ffn_b[layer])
    return h


import jax as _jax
import jax.numpy as _jnp

TWIN_FORMAT = 'train_step'
FWD_PARAMS = ['x', 'mem', 'positions', 'ret_w_in', 'ret_gn_g', 'ret_w_out', 'mla_w_in', 'mla_q_norm_g', 'mla_w_uq', 'mla_kv_norm_g', 'mla_w_ukv', 'mla_w_out', 'xa_w_q', 'xa_w_kv', 'xa_w_out', 'ffn_w_up', 'ffn_conv_w', 'ffn_conv_b', 'ffn_w_down', 'ln_mix_g', 'ln_mix_b', 'ln_mem_g', 'ln_mem_b', 'ln_ffn_g', 'ln_ffn_b']
TWIN_WEIGHTS = ['ret_w_in', 'ret_gn_g', 'ret_w_out', 'mla_w_in', 'mla_q_norm_g', 'mla_w_uq', 'mla_kv_norm_g', 'mla_w_ukv', 'mla_w_out', 'xa_w_q', 'xa_w_kv', 'xa_w_out', 'ffn_w_up', 'ffn_conv_w', 'ffn_conv_b', 'ffn_w_down', 'ln_mix_g', 'ln_mix_b', 'ln_mem_g', 'ln_mem_b', 'ln_ffn_g', 'ln_ffn_b']
TWIN_DIFF_INPUT = 'x'
TWIN_INPUTS = ['x', 'mem', 'positions', 'ret_w_in', 'ret_gn_g', 'ret_w_out', 'mla_w_in', 'mla_q_norm_g', 'mla_w_uq', 'mla_kv_norm_g', 'mla_w_ukv', 'mla_w_out', 'xa_w_q', 'xa_w_kv', 'xa_w_out', 'ffn_w_up', 'ffn_conv_w', 'ffn_conv_b', 'ffn_w_down', 'ln_mix_g', 'ln_mix_b', 'ln_mem_g', 'ln_mem_b', 'ln_ffn_g', 'ln_ffn_b', 'loss_target', 'm_ret_w_in', 'm_ret_gn_g', 'm_ret_w_out', 'm_mla_w_in', 'm_mla_q_norm_g', 'm_mla_w_uq', 'm_mla_kv_norm_g', 'm_mla_w_ukv', 'm_mla_w_out', 'm_xa_w_q', 'm_xa_w_kv', 'm_xa_w_out', 'm_ffn_w_up', 'm_ffn_conv_w', 'm_ffn_conv_b', 'm_ffn_w_down', 'm_ln_mix_g', 'm_ln_mix_b', 'm_ln_mem_g', 'm_ln_mem_b', 'm_ln_ffn_g', 'm_ln_ffn_b', 'v_ret_w_in', 'v_ret_gn_g', 'v_ret_w_out', 'v_mla_w_in', 'v_mla_q_norm_g', 'v_mla_w_uq', 'v_mla_kv_norm_g', 'v_mla_w_ukv', 'v_mla_w_out', 'v_xa_w_q', 'v_xa_w_kv', 'v_xa_w_out', 'v_ffn_w_up', 'v_ffn_conv_w', 'v_ffn_conv_b', 'v_ffn_w_down', 'v_ln_mix_g', 'v_ln_mix_b', 'v_ln_mem_g', 'v_ln_mem_b', 'v_ln_ffn_g', 'v_ln_ffn_b']
TWIN_OUTPUTS = ['loss', 'grad_x', 'grad_ret_w_in', 'grad_ret_gn_g', 'grad_ret_w_out', 'grad_mla_w_in', 'grad_mla_q_norm_g', 'grad_mla_w_uq', 'grad_mla_kv_norm_g', 'grad_mla_w_ukv', 'grad_mla_w_out', 'grad_xa_w_q', 'grad_xa_w_kv', 'grad_xa_w_out', 'grad_ffn_w_up', 'grad_ffn_conv_w', 'grad_ffn_conv_b', 'grad_ffn_w_down', 'grad_ln_mix_g', 'grad_ln_mix_b', 'grad_ln_mem_g', 'grad_ln_mem_b', 'grad_ln_ffn_g', 'grad_ln_ffn_b', 'delta_ret_w_in', 'delta_ret_gn_g', 'delta_ret_w_out', 'delta_mla_w_in', 'delta_mla_q_norm_g', 'delta_mla_w_uq', 'delta_mla_kv_norm_g', 'delta_mla_w_ukv', 'delta_mla_w_out', 'delta_xa_w_q', 'delta_xa_w_kv', 'delta_xa_w_out', 'delta_ffn_w_up', 'delta_ffn_conv_w', 'delta_ffn_conv_b', 'delta_ffn_w_down', 'delta_ln_mix_g', 'delta_ln_mix_b', 'delta_ln_mem_g', 'delta_ln_mem_b', 'delta_ln_ffn_g', 'delta_ln_ffn_b', 'new_m_ret_w_in', 'new_m_ret_gn_g', 'new_m_ret_w_out', 'new_m_mla_w_in', 'new_m_mla_q_norm_g', 'new_m_mla_w_uq', 'new_m_mla_kv_norm_g', 'new_m_mla_w_ukv', 'new_m_mla_w_out', 'new_m_xa_w_q', 'new_m_xa_w_kv', 'new_m_xa_w_out', 'new_m_ffn_w_up', 'new_m_ffn_conv_w', 'new_m_ffn_conv_b', 'new_m_ffn_w_down', 'new_m_ln_mix_g', 'new_m_ln_mix_b', 'new_m_ln_mem_g', 'new_m_ln_mem_b', 'new_m_ln_ffn_g', 'new_m_ln_ffn_b', 'new_v_ret_w_in', 'new_v_ret_gn_g', 'new_v_ret_w_out', 'new_v_mla_w_in', 'new_v_mla_q_norm_g', 'new_v_mla_w_uq', 'new_v_mla_kv_norm_g', 'new_v_mla_w_ukv', 'new_v_mla_w_out', 'new_v_xa_w_q', 'new_v_xa_w_kv', 'new_v_xa_w_out', 'new_v_ffn_w_up', 'new_v_ffn_conv_w', 'new_v_ffn_conv_b', 'new_v_ffn_w_down', 'new_v_ln_mix_g', 'new_v_ln_mix_b', 'new_v_ln_mem_g', 'new_v_ln_mem_b', 'new_v_ln_ffn_g', 'new_v_ln_ffn_b']
TWIN_LEAF_KINDS = {'loss': 'loss', 'grad_x': 'grad_x', 'grad_ret_w_in': 'grad_w', 'grad_ret_gn_g': 'grad_w', 'grad_ret_w_out': 'grad_w', 'grad_mla_w_in': 'grad_w', 'grad_mla_q_norm_g': 'grad_w', 'grad_mla_w_uq': 'grad_w', 'grad_mla_kv_norm_g': 'grad_w', 'grad_mla_w_ukv': 'grad_w', 'grad_mla_w_out': 'grad_w', 'grad_xa_w_q': 'grad_w', 'grad_xa_w_kv': 'grad_w', 'grad_xa_w_out': 'grad_w', 'grad_ffn_w_up': 'grad_w', 'grad_ffn_conv_w': 'grad_w', 'grad_ffn_conv_b': 'grad_w', 'grad_ffn_w_down': 'grad_w', 'grad_ln_mix_g': 'grad_w', 'grad_ln_mix_b': 'grad_w', 'grad_ln_mem_g': 'grad_w', 'grad_ln_mem_b': 'grad_w', 'grad_ln_ffn_g': 'grad_w', 'grad_ln_ffn_b': 'grad_w', 'delta_ret_w_in': 'delta_w', 'delta_ret_gn_g': 'delta_w', 'delta_ret_w_out': 'delta_w', 'delta_mla_w_in': 'delta_w', 'delta_mla_q_norm_g': 'delta_w', 'delta_mla_w_uq': 'delta_w', 'delta_mla_kv_norm_g': 'delta_w', 'delta_mla_w_ukv': 'delta_w', 'delta_mla_w_out': 'delta_w', 'delta_xa_w_q': 'delta_w', 'delta_xa_w_kv': 'delta_w', 'delta_xa_w_out': 'delta_w', 'delta_ffn_w_up': 'delta_w', 'delta_ffn_conv_w': 'delta_w', 'delta_ffn_conv_b': 'delta_w', 'delta_ffn_w_down': 'delta_w', 'delta_ln_mix_g': 'delta_w', 'delta_ln_mix_b': 'delta_w', 'delta_ln_mem_g': 'delta_w', 'delta_ln_mem_b': 'delta_w', 'delta_ln_ffn_g': 'delta_w', 'delta_ln_ffn_b': 'delta_w', 'new_m_ret_w_in': 'new_m', 'new_m_ret_gn_g': 'new_m', 'new_m_ret_w_out': 'new_m', 'new_m_mla_w_in': 'new_m', 'new_m_mla_q_norm_g': 'new_m', 'new_m_mla_w_uq': 'new_m', 'new_m_mla_kv_norm_g': 'new_m', 'new_m_mla_w_ukv': 'new_m', 'new_m_mla_w_out': 'new_m', 'new_m_xa_w_q': 'new_m', 'new_m_xa_w_kv': 'new_m', 'new_m_xa_w_out': 'new_m', 'new_m_ffn_w_up': 'new_m', 'new_m_ffn_conv_w': 'new_m', 'new_m_ffn_conv_b': 'new_m', 'new_m_ffn_w_down': 'new_m', 'new_m_ln_mix_g': 'new_m', 'new_m_ln_mix_b': 'new_m', 'new_m_ln_mem_g': 'new_m', 'new_m_ln_mem_b': 'new_m', 'new_m_ln_ffn_g': 'new_m', 'new_m_ln_ffn_b': 'new_m', 'new_v_ret_w_in': 'new_v', 'new_v_ret_gn_g': 'new_v', 'new_v_ret_w_out': 'new_v', 'new_v_mla_w_in': 'new_v', 'new_v_mla_q_norm_g': 'new_v', 'new_v_mla_w_uq': 'new_v', 'new_v_mla_kv_norm_g': 'new_v', 'new_v_mla_w_ukv': 'new_v', 'new_v_mla_w_out': 'new_v', 'new_v_xa_w_q': 'new_v', 'new_v_xa_w_kv': 'new_v', 'new_v_xa_w_out': 'new_v', 'new_v_ffn_w_up': 'new_v', 'new_v_ffn_conv_w': 'new_v', 'new_v_ffn_conv_b': 'new_v', 'new_v_ffn_w_down': 'new_v', 'new_v_ln_mix_g': 'new_v', 'new_v_ln_mix_b': 'new_v', 'new_v_ln_mem_g': 'new_v', 'new_v_ln_mem_b': 'new_v', 'new_v_ln_ffn_g': 'new_v', 'new_v_ln_ffn_b': 'new_v'}


def _forward(args):
    return _fwd_reference(*[args[k] for k in FWD_PARAMS])


def _output_shape():
    out = _jax.eval_shape(lambda: _forward(_fwd_setup_inputs(0)))
    return out.shape, out.dtype

N_MICROBATCH = 1
ADAM_LR = 0.001
ADAM_B1 = 0.9
ADAM_B2 = 0.999
ADAM_EPS = 1e-08
ADAM_WD = 0.01
ADAM_STEP = 10
PER_EXAMPLE_BATCH_AXIS = {'x': 0, 'mem': 0, 'positions': 0, 'loss_target': 0}
SHARED_INPUTS = []
_WEIGHT_DTYPES = {'ret_w_in': _jnp.float32, 'ret_gn_g': _jnp.float32, 'ret_w_out': _jnp.float32, 'mla_w_in': _jnp.float32, 'mla_q_norm_g': _jnp.float32, 'mla_w_uq': _jnp.float32, 'mla_kv_norm_g': _jnp.float32, 'mla_w_ukv': _jnp.float32, 'mla_w_out': _jnp.float32, 'xa_w_q': _jnp.float32, 'xa_w_kv': _jnp.float32, 'xa_w_out': _jnp.float32, 'ffn_w_up': _jnp.float32, 'ffn_conv_w': _jnp.float32, 'ffn_conv_b': _jnp.float32, 'ffn_w_down': _jnp.float32, 'ln_mix_g': _jnp.float32, 'ln_mix_b': _jnp.float32, 'ln_mem_g': _jnp.float32, 'ln_mem_b': _jnp.float32, 'ln_ffn_g': _jnp.float32, 'ln_ffn_b': _jnp.float32}
MOMENT_SCALE = {'ret_w_in': 8.050821e-03, 'ret_gn_g': 6.959709e-03, 'ret_w_out': 2.316103e-02, 'mla_w_in': 5.587807e-03, 'mla_q_norm_g': 4.826622e-03, 'mla_w_uq': 1.898334e-03, 'mla_kv_norm_g': 6.489385e-03, 'mla_w_ukv': 2.253743e-03, 'mla_w_out': 6.025138e-03, 'xa_w_q': 1.676221e-03, 'xa_w_kv': 1.719557e-03, 'xa_w_out': 4.221463e-03, 'ffn_w_up': 6.016437e-03, 'ffn_conv_w': 6.012817e-03, 'ffn_conv_b': 6.106947e-03, 'ffn_w_down': 2.333191e-02, 'ln_mix_g': 1.376321e-01, 'ln_mix_b': 9.066813e-02, 'ln_mem_g': 1.377417e-01, 'ln_mem_b': 9.067721e-02, 'ln_ffn_g': 4.004774e+00, 'ln_ffn_b': 1.569122e-01}


def _to_microbatches(a, axis):
    t = _jnp.moveaxis(a, axis, 0)
    t = t.reshape((N_MICROBATCH, t.shape[0] // N_MICROBATCH) + t.shape[1:])
    return _jnp.moveaxis(t, 1, axis + 1)


def setup_inputs(seed: int = 0) -> dict:
    inp = _fwd_setup_inputs(seed)
    key = _jax.random.fold_in(_jax.random.key(seed), 7919)
    shape, _ = _output_shape()
    out = dict(inp)
    out["loss_target"] = _jax.random.normal(_jax.random.fold_in(key, 0), shape, _jnp.float32)
    for i, name in enumerate(TWIN_WEIGHTS):
        w = inp[name].astype(_jnp.float32)
        if MOMENT_SCALE is None:
            s = _jnp.sqrt(_jnp.mean(_jnp.square(w)) + 1e-30)
        else:
            s = MOMENT_SCALE[name]
        km, kv = _jax.random.split(_jax.random.fold_in(key, i + 1))
        out[name] = w
        out["m_" + name] = s * _jax.random.normal(km, w.shape, _jnp.float32)
        out["v_" + name] = (s * s) * _jax.random.uniform(kv, w.shape, _jnp.float32, 0.5, 1.5)
    if N_MICROBATCH > 1:
        for name, axis in PER_EXAMPLE_BATCH_AXIS.items():
            out[name] = _to_microbatches(out[name], axis)
    return {'x': out['x'], 'mem': out['mem'], 'positions': out['positions'], 'ret_w_in': out['ret_w_in'], 'ret_gn_g': out['ret_gn_g'], 'ret_w_out': out['ret_w_out'], 'mla_w_in': out['mla_w_in'], 'mla_q_norm_g': out['mla_q_norm_g'], 'mla_w_uq': out['mla_w_uq'], 'mla_kv_norm_g': out['mla_kv_norm_g'], 'mla_w_ukv': out['mla_w_ukv'], 'mla_w_out': out['mla_w_out'], 'xa_w_q': out['xa_w_q'], 'xa_w_kv': out['xa_w_kv'], 'xa_w_out': out['xa_w_out'], 'ffn_w_up': out['ffn_w_up'], 'ffn_conv_w': out['ffn_conv_w'], 'ffn_conv_b': out['ffn_conv_b'], 'ffn_w_down': out['ffn_w_down'], 'ln_mix_g': out['ln_mix_g'], 'ln_mix_b': out['ln_mix_b'], 'ln_mem_g': out['ln_mem_g'], 'ln_mem_b': out['ln_mem_b'], 'ln_ffn_g': out['ln_ffn_g'], 'ln_ffn_b': out['ln_ffn_b'], 'loss_target': out['loss_target'], 'm_ret_w_in': out['m_ret_w_in'], 'm_ret_gn_g': out['m_ret_gn_g'], 'm_ret_w_out': out['m_ret_w_out'], 'm_mla_w_in': out['m_mla_w_in'], 'm_mla_q_norm_g': out['m_mla_q_norm_g'], 'm_mla_w_uq': out['m_mla_w_uq'], 'm_mla_kv_norm_g': out['m_mla_kv_norm_g'], 'm_mla_w_ukv': out['m_mla_w_ukv'], 'm_mla_w_out': out['m_mla_w_out'], 'm_xa_w_q': out['m_xa_w_q'], 'm_xa_w_kv': out['m_xa_w_kv'], 'm_xa_w_out': out['m_xa_w_out'], 'm_ffn_w_up': out['m_ffn_w_up'], 'm_ffn_conv_w': out['m_ffn_conv_w'], 'm_ffn_conv_b': out['m_ffn_conv_b'], 'm_ffn_w_down': out['m_ffn_w_down'], 'm_ln_mix_g': out['m_ln_mix_g'], 'm_ln_mix_b': out['m_ln_mix_b'], 'm_ln_mem_g': out['m_ln_mem_g'], 'm_ln_mem_b': out['m_ln_mem_b'], 'm_ln_ffn_g': out['m_ln_ffn_g'], 'm_ln_ffn_b': out['m_ln_ffn_b'], 'v_ret_w_in': out['v_ret_w_in'], 'v_ret_gn_g': out['v_ret_gn_g'], 'v_ret_w_out': out['v_ret_w_out'], 'v_mla_w_in': out['v_mla_w_in'], 'v_mla_q_norm_g': out['v_mla_q_norm_g'], 'v_mla_w_uq': out['v_mla_w_uq'], 'v_mla_kv_norm_g': out['v_mla_kv_norm_g'], 'v_mla_w_ukv': out['v_mla_w_ukv'], 'v_mla_w_out': out['v_mla_w_out'], 'v_xa_w_q': out['v_xa_w_q'], 'v_xa_w_kv': out['v_xa_w_kv'], 'v_xa_w_out': out['v_xa_w_out'], 'v_ffn_w_up': out['v_ffn_w_up'], 'v_ffn_conv_w': out['v_ffn_conv_w'], 'v_ffn_conv_b': out['v_ffn_conv_b'], 'v_ffn_w_down': out['v_ffn_w_down'], 'v_ln_mix_g': out['v_ln_mix_g'], 'v_ln_mix_b': out['v_ln_mix_b'], 'v_ln_mem_g': out['v_ln_mem_g'], 'v_ln_mem_b': out['v_ln_mem_b'], 'v_ln_ffn_g': out['v_ln_ffn_g'], 'v_ln_ffn_b': out['v_ln_ffn_b']}


def _loss(weights, diff, rest, loss_target):
    with _jax.named_scope("forward"):
        args = {**rest, TWIN_DIFF_INPUT: diff, **{k: w.astype(_WEIGHT_DTYPES[k]) for k, w in weights.items()}}
        y = _forward(args)
    with _jax.named_scope("loss_head"):
        err = _jnp.square(y.astype(_jnp.float32) - loss_target)
        return 0.5 * _jnp.sum(_jnp.mean(err, axis=-1)) if err.ndim else 0.5 * err


def _adamw(w, g, m, v):
    m = ADAM_B1 * m + (1.0 - ADAM_B1) * g
    v = ADAM_B2 * v + (1.0 - ADAM_B2) * _jnp.square(g)
    m_hat = m / (1.0 - ADAM_B1 ** ADAM_STEP)
    v_hat = v / (1.0 - ADAM_B2 ** ADAM_STEP)
    delta = -ADAM_LR * (m_hat / (_jnp.sqrt(v_hat) + ADAM_EPS) + ADAM_WD * w)
    return delta, m, v


def reference(x, mem, positions, ret_w_in, ret_gn_g, ret_w_out, mla_w_in, mla_q_norm_g, mla_w_uq, mla_kv_norm_g, mla_w_ukv, mla_w_out, xa_w_q, xa_w_kv, xa_w_out, ffn_w_up, ffn_conv_w, ffn_conv_b, ffn_w_down, ln_mix_g, ln_mix_b, ln_mem_g, ln_mem_b, ln_ffn_g, ln_ffn_b, loss_target, m_ret_w_in, m_ret_gn_g, m_ret_w_out, m_mla_w_in, m_mla_q_norm_g, m_mla_w_uq, m_mla_kv_norm_g, m_mla_w_ukv, m_mla_w_out, m_xa_w_q, m_xa_w_kv, m_xa_w_out, m_ffn_w_up, m_ffn_conv_w, m_ffn_conv_b, m_ffn_w_down, m_ln_mix_g, m_ln_mix_b, m_ln_mem_g, m_ln_mem_b, m_ln_ffn_g, m_ln_ffn_b, v_ret_w_in, v_ret_gn_g, v_ret_w_out, v_mla_w_in, v_mla_q_norm_g, v_mla_w_uq, v_mla_kv_norm_g, v_mla_w_ukv, v_mla_w_out, v_xa_w_q, v_xa_w_kv, v_xa_w_out, v_ffn_w_up, v_ffn_conv_w, v_ffn_conv_b, v_ffn_w_down, v_ln_mix_g, v_ln_mix_b, v_ln_mem_g, v_ln_mem_b, v_ln_ffn_g, v_ln_ffn_b):
    given = dict(x=x, mem=mem, positions=positions, ret_w_in=ret_w_in, ret_gn_g=ret_gn_g, ret_w_out=ret_w_out, mla_w_in=mla_w_in, mla_q_norm_g=mla_q_norm_g, mla_w_uq=mla_w_uq, mla_kv_norm_g=mla_kv_norm_g, mla_w_ukv=mla_w_ukv, mla_w_out=mla_w_out, xa_w_q=xa_w_q, xa_w_kv=xa_w_kv, xa_w_out=xa_w_out, ffn_w_up=ffn_w_up, ffn_conv_w=ffn_conv_w, ffn_conv_b=ffn_conv_b, ffn_w_down=ffn_w_down, ln_mix_g=ln_mix_g, ln_mix_b=ln_mix_b, ln_mem_g=ln_mem_g, ln_mem_b=ln_mem_b, ln_ffn_g=ln_ffn_g, ln_ffn_b=ln_ffn_b, loss_target=loss_target, m_ret_w_in=m_ret_w_in, m_ret_gn_g=m_ret_gn_g, m_ret_w_out=m_ret_w_out, m_mla_w_in=m_mla_w_in, m_mla_q_norm_g=m_mla_q_norm_g, m_mla_w_uq=m_mla_w_uq, m_mla_kv_norm_g=m_mla_kv_norm_g, m_mla_w_ukv=m_mla_w_ukv, m_mla_w_out=m_mla_w_out, m_xa_w_q=m_xa_w_q, m_xa_w_kv=m_xa_w_kv, m_xa_w_out=m_xa_w_out, m_ffn_w_up=m_ffn_w_up, m_ffn_conv_w=m_ffn_conv_w, m_ffn_conv_b=m_ffn_conv_b, m_ffn_w_down=m_ffn_w_down, m_ln_mix_g=m_ln_mix_g, m_ln_mix_b=m_ln_mix_b, m_ln_mem_g=m_ln_mem_g, m_ln_mem_b=m_ln_mem_b, m_ln_ffn_g=m_ln_ffn_g, m_ln_ffn_b=m_ln_ffn_b, v_ret_w_in=v_ret_w_in, v_ret_gn_g=v_ret_gn_g, v_ret_w_out=v_ret_w_out, v_mla_w_in=v_mla_w_in, v_mla_q_norm_g=v_mla_q_norm_g, v_mla_w_uq=v_mla_w_uq, v_mla_kv_norm_g=v_mla_kv_norm_g, v_mla_w_ukv=v_mla_w_ukv, v_mla_w_out=v_mla_w_out, v_xa_w_q=v_xa_w_q, v_xa_w_kv=v_xa_w_kv, v_xa_w_out=v_xa_w_out, v_ffn_w_up=v_ffn_w_up, v_ffn_conv_w=v_ffn_conv_w, v_ffn_conv_b=v_ffn_conv_b, v_ffn_w_down=v_ffn_w_down, v_ln_mix_g=v_ln_mix_g, v_ln_mix_b=v_ln_mix_b, v_ln_mem_g=v_ln_mem_g, v_ln_mem_b=v_ln_mem_b, v_ln_ffn_g=v_ln_ffn_g, v_ln_ffn_b=v_ln_ffn_b)
    weights = {n: given[n] for n in TWIN_WEIGHTS}
    shared = {n: given[n] for n in SHARED_INPUTS}
    per_example = {n: given[n] for n in ['x', 'mem', 'positions']}
    grad_fn = _jax.value_and_grad(_loss, argnums=(0, 1))

    def one_microbatch(ex, loss_target):
        ex = dict(ex)
        diff = ex.pop(TWIN_DIFF_INPUT)
        return grad_fn(weights, diff, {**shared, **ex}, loss_target)

    if N_MICROBATCH == 1:
        loss, (grad_w, grad_x) = one_microbatch(per_example, given["loss_target"])
    else:
        def body(carry, xs):
            loss_sum, grad_sum = carry
            l_k, (gw_k, gx_k) = one_microbatch(xs[0], xs[1])
            with _jax.named_scope("update"):
                return (loss_sum + l_k, _jax.tree.map(_jnp.add, grad_sum, gw_k)), gx_k

        init = (_jnp.zeros((), _jnp.float32), _jax.tree.map(_jnp.zeros_like, weights))
        (loss, grad_w), grad_x = _jax.lax.scan(body, init, (per_example, given["loss_target"]))
    with _jax.named_scope("update"):
        delta_w, new_m, new_v = {}, {}, {}
        for n in TWIN_WEIGHTS:
            delta_w[n], new_m[n], new_v[n] = _adamw(weights[n], grad_w[n], given["m_" + n], given["v_" + n])
    return (loss, grad_x, *[grad_w[n] for n in TWIN_WEIGHTS], *[delta_w[n] for n in TWIN_WEIGHTS],
            *[new_m[n] for n in TWIN_WEIGHTS], *[new_v[n] for n in TWIN_WEIGHTS])
```

```python
import functools
import math

import jax
import jax.numpy as jnp
from jax import lax
from jax.experimental import pallas as pl
from jax.experimental.pallas import tpu as pltpu

F32 = jnp.float32
MXU_DTYPE = jnp.bfloat16
PAY_DTYPE = jnp.bfloat16
MESH = pl.DeviceIdType.MESH
N_DEV = 8
N_CHIP = 4

DEPTH = 4
CHUNK = 64
RET_HEADS = 8
MLA_HEADS = 16
MLA_Q_RANK = 512
MLA_KV_RANK = 512
MLA_NOPE = 128
MLA_ROPE = 64
MLA_V = 128
MLA_PAD = 256
XA_HEADS = 4
ROPE_BASE = 10000.0
LN_EPS = 1e-5
RMS_EPS = 1e-6
NEG_INF = -1e30
ADAM_LR = 0.001
ADAM_B1 = 0.9
ADAM_B2 = 0.999
ADAM_EPS = 1e-08
ADAM_WD = 0.01
ADAM_STEP = 10

LANES = 128
VMEM_LIMIT_BYTES = 48 * 1024 * 1024
TILE_M = 1024
TILE_N = 1024
TILE_K = 512
ATTN_TILE = 256
ROW_TILE = 256


def _tile(n, cap, mult=LANES):
    best = None
    for t in range(mult, min(n, cap) + 1, mult):
        if n % t == 0:
            best = t
    return n if best is None else best


def _params(n_axes):
    return pltpu.CompilerParams(dimension_semantics=("arbitrary",) * n_axes,
                                vmem_limit_bytes=VMEM_LIMIT_BYTES)


def mm(a, b, *, mode="nn", out_dtype=F32, b_layer=None, add=None, add_scale=1.0,
       stack=None, stack_layer=None, name):
    if mode == "nn":
        (M, K), N = a.shape, b.shape[-1]
    elif mode == "nt":
        (M, K), N = a.shape, b.shape[-2]
    else:
        (K, M), N = a.shape, b.shape[-1]
    tm, tn, tk = _tile(M, TILE_M), _tile(N, TILE_N), _tile(K, TILE_K)
    nk = K // tk
    dims = {"nn": (((1,), (0,)), ((), ())), "nt": (((1,), (1,)), ((), ())),
            "tn": (((0,), (0,)), ((), ()))}[mode]

    a_spec = (pl.BlockSpec((tk, tm), lambda i, j, k: (k, i)) if mode == "tn"
              else pl.BlockSpec((tm, tk), lambda i, j, k: (i, k)))
    b_blk, b_idx = ((tn, tk), lambda i, j, k: (j, k)) if mode == "nt" else ((tk, tn), lambda i, j, k: (k, j))
    if b_layer is None:
        b_spec = pl.BlockSpec(b_blk, b_idx)
    else:
        b_spec = pl.BlockSpec((None,) + b_blk, lambda i, j, k: (b_layer,) + b_idx(i, j, k))
    in_specs, operands = [a_spec, b_spec], [a, b]
    if add is not None:
        in_specs.append(pl.BlockSpec((tm, tn), lambda i, j, k: (i, j)))
        operands.append(add)
    aliases = {}
    if stack is None:
        out_shape = jax.ShapeDtypeStruct((M, N), out_dtype)
        out_spec = pl.BlockSpec((tm, tn), lambda i, j, k: (i, j))
    else:
        out_shape = jax.ShapeDtypeStruct(stack.shape, stack.dtype)
        out_spec = pl.BlockSpec((None, tm, tn), lambda i, j, k: (stack_layer, i, j))
        in_specs.append(pl.BlockSpec(memory_space=pl.ANY))
        aliases = {len(operands): 0}
        operands.append(stack)
    n_in = len(operands)

    def body(*refs):
        a_ref, b_ref = refs[0], refs[1]
        add_ref = refs[2] if add is not None else None
        o_ref, acc_ref = refs[n_in], refs[n_in + 1]
        k = pl.program_id(2)

        @pl.when(k == 0)
        def _():
            acc_ref[...] = jnp.zeros_like(acc_ref)

        acc_ref[...] += lax.dot_general(a_ref[...].astype(MXU_DTYPE), b_ref[...].astype(MXU_DTYPE),
                                        dims, preferred_element_type=F32)

        @pl.when(k == nk - 1)
        def _():
            r = acc_ref[...]
            if add_ref is not None:
                r = r + add_scale * add_ref[...].astype(F32)
            o_ref[...] = r.astype(o_ref.dtype)

    return pl.pallas_call(
        body, out_shape=out_shape, grid=(M // tm, N // tn, nk), in_specs=in_specs, out_specs=out_spec,
        scratch_shapes=[pltpu.VMEM((tm, tn), F32)], input_output_aliases=aliases,
        compiler_params=_params(3), name=name)(*operands)


def ln_fwd(h, f, g, b, alpha, name):
    S, D = h.shape
    tm = _tile(S, ROW_TILE, 8)

    def body(h_ref, f_ref, g_ref, b_ref, y_ref, yb_ref, xhat_ref, rstd_ref):
        z = alpha * h_ref[...] + f_ref[...]
        mu = jnp.mean(z, axis=-1, keepdims=True)
        zc = z - mu
        var = jnp.mean(zc * zc, axis=-1, keepdims=True)
        rstd = lax.rsqrt(var + LN_EPS)
        xhat = zc * rstd
        y = xhat * g_ref[...] + b_ref[...]
        y_ref[...] = y
        yb_ref[...] = y.astype(yb_ref.dtype)
        xhat_ref[...] = xhat
        rstd_ref[...] = rstd

    row = pl.BlockSpec((tm, D), lambda i: (i, 0))
    vec = pl.BlockSpec((1, D), lambda i: (0, 0))
    return pl.pallas_call(
        body,
        out_shape=(jax.ShapeDtypeStruct((S, D), F32), jax.ShapeDtypeStruct((S, D), MXU_DTYPE),
                   jax.ShapeDtypeStruct((S, D), F32), jax.ShapeDtypeStruct((S, 1), F32)),
        grid=(S // tm,), in_specs=[row, row, vec, vec],
        out_specs=(row, row, row, pl.BlockSpec((tm, 1), lambda i: (i, 0))),
        compiler_params=_params(1), name=name)(h, f, g, b)


def ln_bwd(dy, xhat, rstd, g, name):
    S, D = dy.shape
    tm = _tile(S, ROW_TILE, 8)

    def body(dy_ref, xhat_ref, rstd_ref, g_ref, dz_ref, dzb_ref, dg_ref, db_ref):
        @pl.when(pl.program_id(0) == 0)
        def _():
            dg_ref[...] = jnp.zeros_like(dg_ref)
            db_ref[...] = jnp.zeros_like(db_ref)

        dy = dy_ref[...]
        xhat = xhat_ref[...]
        dxh = dy * g_ref[...]
        m1 = jnp.mean(dxh, axis=-1, keepdims=True)
        m2 = jnp.mean(dxh * xhat, axis=-1, keepdims=True)
        dz = rstd_ref[...] * (dxh - m1 - xhat * m2)
        dz_ref[...] = dz
        dzb_ref[...] = dz.astype(dzb_ref.dtype)
        dg_ref[...] += jnp.sum(dy * xhat, axis=0, keepdims=True)
        db_ref[...] += jnp.sum(dy, axis=0, keepdims=True)

    row = pl.BlockSpec((tm, D), lambda i: (i, 0))
    vec = pl.BlockSpec((1, D), lambda i: (0, 0))
    return pl.pallas_call(
        body,
        out_shape=(jax.ShapeDtypeStruct((S, D), F32), jax.ShapeDtypeStruct((S, D), MXU_DTYPE),
                   jax.ShapeDtypeStruct((1, D), F32), jax.ShapeDtypeStruct((1, D), F32)),
        grid=(S // tm,), in_specs=[row, row, pl.BlockSpec((tm, 1), lambda i: (i, 0)), vec],
        out_specs=(row, row, vec, vec), compiler_params=_params(1), name=name)(dy, xhat, rstd, g)


def rms_fwd(x, col, width, g, name):
    S = x.shape[0]
    tm = _tile(S, ROW_TILE, 8)

    def body(x_ref, g_ref, y_ref):
        xv = x_ref[...]
        r = lax.rsqrt(jnp.mean(xv * xv, axis=-1, keepdims=True) + RMS_EPS)
        y_ref[...] = (xv * r * g_ref[...]).astype(y_ref.dtype)

    return pl.pallas_call(
        body, out_shape=jax.ShapeDtypeStruct((S, width), MXU_DTYPE), grid=(S // tm,),
        in_specs=[pl.BlockSpec((tm, width), lambda i: (i, col)), pl.BlockSpec((1, width), lambda i: (0, 0))],
        out_specs=pl.BlockSpec((tm, width), lambda i: (i, 0)), compiler_params=_params(1), name=name)(x, g)


def rms_bwd(dy, x, col, width, g, name):
    S = x.shape[0]
    tm = _tile(S, ROW_TILE, 8)

    def body(dy_ref, x_ref, g_ref, dx_ref, dg_ref):
        @pl.when(pl.program_id(0) == 0)
        def _():
            dg_ref[...] = jnp.zeros_like(dg_ref)

        xv = x_ref[...]
        dy = dy_ref[...]
        r = lax.rsqrt(jnp.mean(xv * xv, axis=-1, keepdims=True) + RMS_EPS)
        dxn = dy * g_ref[...]
        m = jnp.mean(dxn * xv, axis=-1, keepdims=True)
        dx_ref[...] = r * (dxn - xv * (r * r * m))
        dg_ref[...] += jnp.sum(dy * xv * r, axis=0, keepdims=True)

    return pl.pallas_call(
        body, out_shape=(jax.ShapeDtypeStruct((S, width), F32), jax.ShapeDtypeStruct((1, width), F32)),
        grid=(S // tm,),
        in_specs=[pl.BlockSpec((tm, width), lambda i: (i, 0)), pl.BlockSpec((tm, width), lambda i: (i, col)),
                  pl.BlockSpec((1, width), lambda i: (0, 0))],
        out_specs=(pl.BlockSpec((tm, width), lambda i: (i, 0)), pl.BlockSpec((1, width), lambda i: (0, 0))),
        compiler_params=_params(1), name=name)(dy, x, g)


def ret_rope_fwd(proj, cos, sin, n_groups, half, k_from, k_scale, name):
    S = proj.shape[0]
    W = n_groups * 2 * half
    tm = _tile(S, ROW_TILE, 8)

    def body(x_ref, c_ref, s_ref, o_ref):
        c, s = c_ref[...], s_ref[...]
        for gi in range(n_groups):
            lo = gi * 2 * half
            x1 = x_ref[:, lo:lo + half]
            x2 = x_ref[:, lo + half:lo + 2 * half]
            sc = k_scale if gi >= k_from else 1.0
            o_ref[:, lo:lo + half] = ((x1 * c - x2 * s) * sc).astype(o_ref.dtype)
            o_ref[:, lo + half:lo + 2 * half] = ((x2 * c + x1 * s) * sc).astype(o_ref.dtype)

    tab = pl.BlockSpec((tm, half), lambda i: (i, 0))
    return pl.pallas_call(
        body, out_shape=jax.ShapeDtypeStruct((S, W), MXU_DTYPE), grid=(S // tm,),
        in_specs=[pl.BlockSpec((tm, W), lambda i: (i, 0)), tab, tab],
        out_specs=pl.BlockSpec((tm, W), lambda i: (i, 0)), compiler_params=_params(1), name=name)(proj, cos, sin)


def ret_rope_bwd(dq, dk, cos, sin, half, k_scale, name):
    S, Wq = dq.shape
    n_heads = Wq // (2 * half)
    tm = _tile(S, ROW_TILE, 8)

    def body(dq_ref, dk_ref, c_ref, s_ref, o_ref):
        c, s = c_ref[...], s_ref[...]
        for part, (d_ref, sc) in enumerate(((dq_ref, 1.0), (dk_ref, k_scale))):
            for hi in range(n_heads):
                lo = hi * 2 * half
                d1 = d_ref[:, lo:lo + half]
                d2 = d_ref[:, lo + half:lo + 2 * half]
                base = part * Wq + lo
                o_ref[:, base:base + half] = ((d1 * c + d2 * s) * sc).astype(o_ref.dtype)
                o_ref[:, base + half:base + 2 * half] = ((d2 * c - d1 * s) * sc).astype(o_ref.dtype)

    tab = pl.BlockSpec((tm, half), lambda i: (i, 0))
    row = pl.BlockSpec((tm, Wq), lambda i: (i, 0))
    return pl.pallas_call(
        body, out_shape=jax.ShapeDtypeStruct((S, 2 * Wq), MXU_DTYPE), grid=(S // tm,),
        in_specs=[row, row, tab, tab], out_specs=pl.BlockSpec((tm, 2 * Wq), lambda i: (i, 0)),
        compiler_params=_params(1), name=name)(dq, dk, cos, sin)


def mla_rope(x, ta, tb, n_heads, *, backward, head_sum, out_dtype, name):
    S = x.shape[0]
    W = n_heads * MLA_PAD
    tm = _tile(S, ROW_TILE, 8)

    def body(x_ref, a_ref, b_ref, o_ref, *rest):
        a, b = a_ref[...], b_ref[...]
        total = jnp.zeros((tm, LANES), F32)
        for hi in range(n_heads):
            lo = hi * MLA_PAD
            o_ref[:, lo:lo + MLA_NOPE] = x_ref[:, lo:lo + MLA_NOPE].astype(o_ref.dtype)
            t = x_ref[:, lo + MLA_NOPE:lo + MLA_PAD].astype(F32)
            if backward:
                r = t * a + pltpu.roll(t * b, LANES // 2, 1)
            else:
                r = t * a + pltpu.roll(t, LANES // 2, 1) * b
            o_ref[:, lo + MLA_NOPE:lo + MLA_PAD] = r.astype(o_ref.dtype)
            total = total + r
        if head_sum:
            rest[0][...] = total

    tab = pl.BlockSpec((tm, LANES), lambda i: (i, 0))
    row = pl.BlockSpec((tm, W), lambda i: (i, 0))
    out_shape = [jax.ShapeDtypeStruct((S, W), out_dtype)]
    out_specs = [row]
    if head_sum:
        out_shape.append(jax.ShapeDtypeStruct((S, LANES), F32))
        out_specs.append(tab)
    return pl.pallas_call(
        body, out_shape=tuple(out_shape), grid=(S // tm,), in_specs=[row, tab, tab],
        out_specs=tuple(out_specs), compiler_params=_params(1), name=name)(x, ta, tb)


def _visible(qi, kj, tq, tk):
    n = qi * tq + lax.broadcasted_iota(jnp.int32, (tq, tk), 0)
    m = kj * tk + lax.broadcasted_iota(jnp.int32, (tq, tk), 1)
    shift = CHUNK.bit_length() - 1
    vis = lax.shift_right_logical(m, shift) <= lax.shift_right_logical(n, shift)
    return vis, jnp.abs(n - m).astype(F32)


def attn_fwd(q, k, v, *, heads, softmax, masked, scale=1.0, log_gamma=None, name):
    (qa, dqk, q0, qs), (ka, _, k0, ks), (va, dv, v0, vs) = q, k, v
    Sq, Sk = qa.shape[0], ka.shape[0]
    tq, tk = _tile(Sq, ATTN_TILE, 8), _tile(Sk, ATTN_TILE, 8)
    nq, nk = Sq // tq, Sk // tk
    assert not masked or tq == tk

    def body(*refs):
        if softmax:
            q_ref, k_ref, v_ref, o_ref, lse_ref, m_ref, l_ref, acc_ref = refs
        else:
            q_ref, k_ref, v_ref, lg_ref, o_ref, acc_ref = refs
        qi, kj = pl.program_id(1), pl.program_id(2)

        @pl.when(kj == 0)
        def _():
            acc_ref[...] = jnp.zeros_like(acc_ref)
            if softmax:
                m_ref[...] = jnp.full_like(m_ref, NEG_INF)
                l_ref[...] = jnp.zeros_like(l_ref)

        def step():
            s = lax.dot_general(q_ref[...].astype(MXU_DTYPE), k_ref[...].astype(MXU_DTYPE),
                                (((1,), (1,)), ((), ())), preferred_element_type=F32)
            vb = v_ref[...].astype(MXU_DTYPE)
            if masked or not softmax:
                vis, dist = _visible(qi, kj, tq, tk)
            if softmax:
                s = s * scale
                if masked:
                    s = jnp.where(vis, s, NEG_INF)
                m_old = m_ref[...]
                m_new = jnp.maximum(m_old, jnp.max(s, axis=-1, keepdims=True))
                p = jnp.exp(s - m_new)
                corr = jnp.exp(m_old - m_new)
                l_ref[...] = corr * l_ref[...] + jnp.sum(p, axis=-1, keepdims=True)
                acc_ref[...] = corr * acc_ref[...] + jnp.dot(p.astype(MXU_DTYPE), vb, preferred_element_type=F32)
                m_ref[...] = m_new
            else:
                decay = jnp.where(vis, jnp.exp(lg_ref[0:1, 0:1] * dist), 0.0)
                p = s * decay
                acc_ref[...] += jnp.dot(p.astype(MXU_DTYPE), vb, preferred_element_type=F32)

        if masked:
            pl.when(kj <= qi)(step)
        else:
            step()

        @pl.when(kj == nk - 1)
        def _():
            if softmax:
                o_ref[...] = (acc_ref[...] / l_ref[...]).astype(o_ref.dtype)
                lse_ref[...] = m_ref[...] + jnp.log(l_ref[...])
            else:
                o_ref[...] = acc_ref[...].astype(o_ref.dtype)

    kcap = (lambda qi, kj: jnp.minimum(kj, qi)) if masked else (lambda qi, kj: kj)
    in_specs = [pl.BlockSpec((tq, dqk), lambda h, qi, kj: (qi, q0 + qs * h)),
                pl.BlockSpec((tk, dqk), lambda h, qi, kj: (kcap(qi, kj), k0 + ks * h)),
                pl.BlockSpec((tk, dv), lambda h, qi, kj: (kcap(qi, kj), v0 + vs * h))]
    operands = [qa, ka, va]
    out_shape = [jax.ShapeDtypeStruct((Sq, heads * dv), F32)]
    out_specs = [pl.BlockSpec((tq, dv), lambda h, qi, kj: (qi, h))]
    scratch = []
    if softmax:
        out_shape.append(jax.ShapeDtypeStruct((heads, Sq, 1), F32))
        out_specs.append(pl.BlockSpec((None, tq, 1), lambda h, qi, kj: (h, qi, 0)))
        scratch += [pltpu.VMEM((tq, 1), F32), pltpu.VMEM((tq, 1), F32)]
    else:
        in_specs.append(pl.BlockSpec((None, 8, LANES), lambda h, qi, kj: (h, 0, 0)))
        operands.append(log_gamma)
    scratch.append(pltpu.VMEM((tq, dv), F32))
    return pl.pallas_call(
        body, out_shape=tuple(out_shape), grid=(heads, nq, nk), in_specs=in_specs, out_specs=tuple(out_specs),
        scratch_shapes=scratch, compiler_params=_params(3), name=name)(*operands)


def attn_bwd(q, k, v, do, *, heads, softmax, masked, scale=1.0, log_gamma=None, o=None, lse=None, name):
    (qa, dqk, q0, qs), (ka, _, k0, ks), (va, dv, v0, vs) = q, k, v
    Sq, Sk = qa.shape[0], ka.shape[0]
    tq, tk = _tile(Sq, ATTN_TILE, 8), _tile(Sk, ATTN_TILE, 8)
    nq, nk = Sq // tq, Sk // tk
    assert not masked or tq == tk
    contract0 = (((0,), (0,)), ((), ()))

    def body(*refs):
        if softmax:
            q_ref, k_ref, v_ref, do_ref, o_ref, lse_ref, dq_ref, dk_ref, dv_ref = refs
        else:
            q_ref, k_ref, v_ref, do_ref, lg_ref, dq_ref, dk_ref, dv_ref = refs
        kj, qi = pl.program_id(1), pl.program_id(2)

        @pl.when(jnp.logical_and(kj == 0, qi == 0))
        def _():
            dq_ref[...] = jnp.zeros_like(dq_ref)

        @pl.when(qi == 0)
        def _():
            dk_ref[...] = jnp.zeros_like(dk_ref)
            dv_ref[...] = jnp.zeros_like(dv_ref)

        def step():
            qb = q_ref[...].astype(MXU_DTYPE)
            kb = k_ref[...].astype(MXU_DTYPE)
            vb = v_ref[...].astype(MXU_DTYPE)
            dob = do_ref[...].astype(MXU_DTYPE)
            s = lax.dot_general(qb, kb, (((1,), (1,)), ((), ())), preferred_element_type=F32)
            dp = lax.dot_general(dob, vb, (((1,), (1,)), ((), ())), preferred_element_type=F32)
            if masked or not softmax:
                vis, dist = _visible(qi, kj, tq, tk)
            if softmax:
                s = s * scale
                if masked:
                    s = jnp.where(vis, s, NEG_INF)
                p = jnp.exp(s - lse_ref[...])
                delta = jnp.sum(do_ref[...].astype(F32) * o_ref[...], axis=-1, keepdims=True)
                ds = p * (dp - delta) * scale
            else:
                decay = jnp.where(vis, jnp.exp(lg_ref[0:1, 0:1] * dist), 0.0)
                p = s * decay
                ds = dp * decay
            pb = p.astype(MXU_DTYPE)
            dsb = ds.astype(MXU_DTYPE)
            dv_ref[...] += lax.dot_general(pb, dob, contract0, preferred_element_type=F32)
            dk_ref[...] += lax.dot_general(dsb, qb, contract0, preferred_element_type=F32)
            rows = pl.ds(pl.multiple_of(qi * tq, tq), tq)
            dq_ref[rows, :] += jnp.dot(dsb, kb, preferred_element_type=F32)

        if masked:
            pl.when(qi >= kj)(step)
        else:
            step()

    qcap = (lambda kj, qi: jnp.maximum(qi, kj)) if masked else (lambda kj, qi: qi)
    in_specs = [pl.BlockSpec((tq, dqk), lambda h, kj, qi: (qcap(kj, qi), q0 + qs * h)),
                pl.BlockSpec((tk, dqk), lambda h, kj, qi: (kj, k0 + ks * h)),
                pl.BlockSpec((tk, dv), lambda h, kj, qi: (kj, v0 + vs * h)),
                pl.BlockSpec((tq, dv), lambda h, kj, qi: (qcap(kj, qi), h))]
    operands = [qa, ka, va, do]
    if softmax:
        in_specs += [pl.BlockSpec((tq, dv), lambda h, kj, qi: (qcap(kj, qi), h)),
                     pl.BlockSpec((None, tq, 1), lambda h, kj, qi: (h, qcap(kj, qi), 0))]
        operands += [o, lse]
    else:
        in_specs.append(pl.BlockSpec((None, 8, LANES), lambda h, kj, qi: (h, 0, 0)))
        operands.append(log_gamma)
    return pl.pallas_call(
        body,
        out_shape=(jax.ShapeDtypeStruct((Sq, heads * dqk), F32), jax.ShapeDtypeStruct((Sk, heads * dqk), F32),
                   jax.ShapeDtypeStruct((Sk, heads * dv), F32)),
        grid=(heads, nk, nq), in_specs=in_specs,
        out_specs=(pl.BlockSpec((Sq, dqk), lambda h, kj, qi: (0, h)),
                   pl.BlockSpec((tk, dqk), lambda h, kj, qi: (kj, h)),
                   pl.BlockSpec((tk, dv), lambda h, kj, qi: (kj, h))),
        compiler_params=_params(3), name=name)(*operands)


def _sigmoid(x):
    return 1.0 / (1.0 + jnp.exp(-x))


def ret_gate_fwd(o, proj, gate_col, gn, heads, name):
    S, W = o.shape
    dv = W // heads
    tm = _tile(S, ROW_TILE // 2, 8)

    def body(o_ref, g_ref, gn_ref, y_ref):
        for hi in range(heads):
            cols = slice(hi * dv, (hi + 1) * dv)
            oh = o_ref[:, cols]
            mu = jnp.mean(oh, axis=-1, keepdims=True)
            oc = oh - mu
            rstd = lax.rsqrt(jnp.mean(oc * oc, axis=-1, keepdims=True) + LN_EPS)
            gt = g_ref[:, cols]
            y_ref[:, cols] = (gt * _sigmoid(gt) * (oc * rstd) * gn_ref[:, cols]).astype(y_ref.dtype)

    row = pl.BlockSpec((tm, W), lambda i: (i, 0))
    return pl.pallas_call(
        body, out_shape=jax.ShapeDtypeStruct((S, W), MXU_DTYPE), grid=(S // tm,),
        in_specs=[row, pl.BlockSpec((tm, W), lambda i: (i, gate_col)), pl.BlockSpec((1, W), lambda i: (0, 0))],
        out_specs=row, compiler_params=_params(1), name=name)(o, proj, gn)


def ret_gate_bwd(dy, o, proj, gate_col, gn, heads, name):
    S, W = o.shape
    dv = W // heads
    tm = _tile(S, ROW_TILE // 2, 8)

    def body(dy_ref, o_ref, g_ref, gn_ref, do_ref, dgt_ref, dgn_ref):
        @pl.when(pl.program_id(0) == 0)
        def _():
            dgn_ref[...] = jnp.zeros_like(dgn_ref)

        for hi in range(heads):
            cols = slice(hi * dv, (hi + 1) * dv)
            oh = o_ref[:, cols]
            mu = jnp.mean(oh, axis=-1, keepdims=True)
            oc = oh - mu
            rstd = lax.rsqrt(jnp.mean(oc * oc, axis=-1, keepdims=True) + LN_EPS)
            xhat = oc * rstd
            gt = g_ref[:, cols]
            sg = _sigmoid(gt)
            gain = gn_ref[:, cols]
            dy = dy_ref[:, cols]
            dgt_ref[:, cols] = (dy * xhat * gain * (sg * (1.0 + gt * (1.0 - sg)))).astype(dgt_ref.dtype)
            dn = dy * (gt * sg)
            dgn_ref[:, cols] += jnp.sum(dn * xhat, axis=0, keepdims=True)
            dxh = dn * gain
            m1 = jnp.mean(dxh, axis=-1, keepdims=True)
            m2 = jnp.mean(dxh * xhat, axis=-1, keepdims=True)
            do_ref[:, cols] = (rstd * (dxh - m1 - xhat * m2)).astype(do_ref.dtype)

    row = pl.BlockSpec((tm, W), lambda i: (i, 0))
    vec = pl.BlockSpec((1, W), lambda i: (0, 0))
    return pl.pallas_call(
        body,
        out_shape=(jax.ShapeDtypeStruct((S, W), MXU_DTYPE), jax.ShapeDtypeStruct((S, W), MXU_DTYPE),
                   jax.ShapeDtypeStruct((1, W), F32)),
        grid=(S // tm,), in_specs=[row, row, pl.BlockSpec((tm, W), lambda i: (i, gate_col)), vec],
        out_specs=(row, row, vec), compiler_params=_params(1), name=name)(dy, o, proj, gn)


def _shift_down(x, s):
    rows = lax.broadcasted_iota(jnp.int32, x.shape, 0)
    return jnp.where(rows >= s, pltpu.roll(x, s, 0), 0.0)


def _shift_up(x, s):
    n = x.shape[0]
    rows = lax.broadcasted_iota(jnp.int32, x.shape, 0)
    return jnp.where(rows < n - s, pltpu.roll(x, n - s, 0), 0.0)


def _conv3(x, w_ref, b_ref):
    return (w_ref[2:3, :] * x + w_ref[1:2, :] * _shift_down(x, 1) + w_ref[0:1, :] * _shift_down(x, 2)
            + b_ref[...])


def conv_glu_fwd(hup, w, b, name):
    S, W2 = hup.shape
    F = W2 // 2
    tc = _tile(F, LANES)
    nb = F // tc

    def body(g_ref, v_ref, wg_ref, wv_ref, bg_ref, bv_ref, u_ref):
        cg = _conv3(g_ref[...], wg_ref, bg_ref)
        cv = _conv3(v_ref[...], wv_ref, bv_ref)
        u_ref[...] = (cg * _sigmoid(cg) * cv).astype(u_ref.dtype)

    def col(rows, off):
        return pl.BlockSpec((rows, tc), lambda j: (0, j + off))

    return pl.pallas_call(
        body, out_shape=jax.ShapeDtypeStruct((S, F), MXU_DTYPE), grid=(nb,),
        in_specs=[col(S, 0), col(S, nb), col(3, 0), col(3, nb), col(1, 0), col(1, nb)],
        out_specs=col(S, 0), compiler_params=_params(1), name=name)(hup, hup, w, w, b, b)


def conv_glu_bwd(du, hup, w, b, name):
    S, W2 = hup.shape
    F = W2 // 2
    tc = _tile(F, LANES)
    nb = F // tc

    def back(dc, x, w_ref, dh_ref, dw_ref, db_ref):
        dw_ref[2:3, :] = jnp.sum(dc * x, axis=0, keepdims=True)
        dw_ref[1:2, :] = jnp.sum(dc * _shift_down(x, 1), axis=0, keepdims=True)
        dw_ref[0:1, :] = jnp.sum(dc * _shift_down(x, 2), axis=0, keepdims=True)
        db_ref[...] = jnp.sum(dc, axis=0, keepdims=True)
        dh_ref[...] = (w_ref[2:3, :] * dc + w_ref[1:2, :] * _shift_up(dc, 1)
                       + w_ref[0:1, :] * _shift_up(dc, 2)).astype(dh_ref.dtype)

    def body(du_ref, g_ref, v_ref, wg_ref, wv_ref, bg_ref, bv_ref,
             dhg_ref, dhv_ref, dwg_ref, dwv_ref, dbg_ref, dbv_ref):
        xg, xv = g_ref[...], v_ref[...]
        cg = _conv3(xg, wg_ref, bg_ref)
        cv = _conv3(xv, wv_ref, bv_ref)
        sg = _sigmoid(cg)
        du = du_ref[...]
        back(du * cv * (sg * (1.0 + cg * (1.0 - sg))), xg, wg_ref, dhg_ref, dwg_ref, dbg_ref)
        back(du * (cg * sg), xv, wv_ref, dhv_ref, dwv_ref, dbv_ref)

    def col(rows, off):
        return pl.BlockSpec((rows, tc), lambda j: (0, j + off))

    return pl.pallas_call(
        body,
        out_shape=(jax.ShapeDtypeStruct((S, F), MXU_DTYPE), jax.ShapeDtypeStruct((S, F), MXU_DTYPE),
                   jax.ShapeDtypeStruct((3, F), F32), jax.ShapeDtypeStruct((3, F), F32),
                   jax.ShapeDtypeStruct((1, F), F32), jax.ShapeDtypeStruct((1, F), F32)),
        grid=(nb,),
        in_specs=[col(S, 0), col(S, 0), col(S, nb), col(3, 0), col(3, nb), col(1, 0), col(1, nb)],
        out_specs=(col(S, 0), col(S, 0), col(3, 0), col(3, 0), col(1, 0), col(1, 0)),
        compiler_params=_params(1), name=name)(du, hup, hup, w, w, b, b)


def loss_head(y, target, name):
    S, D = y.shape
    tm = _tile(S, ROW_TILE, 8)

    def body(y_ref, t_ref, dy_ref, loss_ref):
        @pl.when(pl.program_id(0) == 0)
        def _():
            loss_ref[...] = jnp.zeros_like(loss_ref)

        e = y_ref[...] - t_ref[...]
        dy_ref[...] = e * (1.0 / D)
        part = jnp.sum(jnp.sum(e * e, axis=-1, keepdims=True), axis=0, keepdims=True) * (0.5 / D)
        loss_ref[...] += jnp.broadcast_to(part, loss_ref.shape)

    row = pl.BlockSpec((tm, D), lambda i: (i, 0))
    return pl.pallas_call(
        body, out_shape=(jax.ShapeDtypeStruct((S, D), F32), jax.ShapeDtypeStruct((8, LANES), F32)),
        grid=(S // tm,), in_specs=[row, row], out_specs=(row, pl.BlockSpec((8, LANES), lambda i: (0, 0))),
        compiler_params=_params(1), name=name)(y, target)


def adam_update(parts, w, m, v, name):
    n_parts, R, C = parts.shape
    tr = _tile(R, max(8, (1 << 19) // C), 8)
    c1 = 1.0 - ADAM_B1 ** ADAM_STEP
    c2 = 1.0 - ADAM_B2 ** ADAM_STEP

    def body(p_ref, w_ref, m_ref, v_ref, g_ref, d_ref, nm_ref, nv_ref):
        g = p_ref[0].astype(F32)
        for pi in range(1, n_parts):
            g = g + p_ref[pi].astype(F32)
        nm = ADAM_B1 * m_ref[...] + (1.0 - ADAM_B1) * g
        nv = ADAM_B2 * v_ref[...] + (1.0 - ADAM_B2) * (g * g)
        g_ref[...] = g
        nm_ref[...] = nm
        nv_ref[...] = nv
        d_ref[...] = -ADAM_LR * ((nm / c1) / (jnp.sqrt(nv / c2) + ADAM_EPS) + ADAM_WD * w_ref[...])

    row = pl.BlockSpec((tr, C), lambda i: (i, 0))
    out = jax.ShapeDtypeStruct((R, C), F32)
    return pl.pallas_call(
        body, out_shape=(out, out, out, out), grid=(R // tr,),
        in_specs=[pl.BlockSpec((n_parts, tr, C), lambda i: (0, i, 0)), row, row, row],
        out_specs=(row, row, row, row), compiler_params=_params(1), name=name)(parts, w, m, v)


def pair_sum(own, got, core, name):
    _, R, C = own.shape
    tr = _tile(R, max(16, (1 << 19) // C), 16)

    def body(core_ref, a_ref, b_ref, o_ref):
        o_ref[...] = (a_ref[...].astype(F32) + b_ref[...].astype(F32)).astype(o_ref.dtype)

    grid_spec = pltpu.PrefetchScalarGridSpec(
        num_scalar_prefetch=1, grid=(N_CHIP, R // tr),
        in_specs=[pl.BlockSpec((None, tr, C), lambda j, i, core_ref: (2 * j + core_ref[0], i, 0)),
                  pl.BlockSpec((None, tr, C), lambda j, i, core_ref: (j, i, 0))],
        out_specs=pl.BlockSpec((None, tr, C), lambda j, i, core_ref: (j, i, 0)))
    return pl.pallas_call(
        body, out_shape=jax.ShapeDtypeStruct((N_CHIP, R, C), own.dtype), grid_spec=grid_spec,
        compiler_params=_params(2), name=name)(core, own, got)


def _place():
    return lax.axis_index("x"), lax.axis_index("y"), lax.axis_index("c")


def _other_chips(x, y):
    return [(1 - x, y), (x, 1 - y), (1 - x, 1 - y)]


def gather_small(block, name):
    m_per, n = block.shape

    def body(x_ref, out_ref, send_sems, recv_sems, local_sem):
        x, y, c = _place()
        me, sibling = (x, y, c), (x, y, 1 - c)
        chips = _other_chips(x, y)

        def rows(px, py, pc):
            return out_ref.at[pl.ds((4 * px + 2 * py + pc) * m_per, m_per), :]

        def copy(k, blk, to, src=None):
            return pltpu.make_async_remote_copy(
                src_ref=rows(*blk) if src is None else src, dst_ref=rows(*blk),
                send_sem=send_sems.at[k], recv_sem=recv_sems.at[k], device_id=to, device_id_type=MESH)

        mine = pltpu.make_async_copy(x_ref, rows(*me), local_sem)
        mine.start()
        first = [copy(0, me, sibling, src=x_ref)]
        first += [copy(1 + j, me, (*chip, c), src=x_ref) for j, chip in enumerate(chips)]
        for cp in first:
            cp.start()
        passed = [copy(4 + j, (*chip, c), sibling) for j, chip in enumerate(chips)]
        for j, chip in enumerate(chips):
            copy(1 + j, (*chip, c), me).wait_recv()
            passed[j].start()
        copy(0, sibling, me).wait_recv()
        for j, chip in enumerate(chips):
            copy(4 + j, (*chip, 1 - c), me).wait_recv()
        for cp in first + passed:
            cp.wait_send()
        mine.wait()

    return pl.pallas_call(
        body, out_shape=jax.ShapeDtypeStruct((N_DEV * m_per, n), block.dtype),
        in_specs=[pl.BlockSpec(memory_space=pltpu.VMEM)], out_specs=pl.BlockSpec(memory_space=pltpu.VMEM),
        scratch_shapes=[pltpu.SemaphoreType.DMA((7,)), pltpu.SemaphoreType.DMA((7,)), pltpu.SemaphoreType.DMA],
        compiler_params=pltpu.CompilerParams(vmem_limit_bytes=VMEM_LIMIT_BYTES), name=name)(block)


def gather_weight(shard, axis, name):
    L, k, n = shard.shape
    full = (L, N_DEV * k, n) if axis == 1 else (L, k, N_DEV * n)

    def body(x_ref, out_ref, send_sems, recv_sems, local_sem):
        x, y, c = _place()
        me, sibling = (x, y, c), (x, y, 1 - c)
        chips = _other_chips(x, y)

        def slab(px, py, pc):
            idx = 4 * px + 2 * py + pc
            if axis == 1:
                return out_ref.at[:, pl.ds(idx * k, k), :]
            return out_ref.at[:, :, pl.ds(idx * n, n)]

        def copy(kk, blk, to, src=None):
            return pltpu.make_async_remote_copy(
                src_ref=slab(*blk) if src is None else src, dst_ref=slab(*blk),
                send_sem=send_sems.at[kk], recv_sem=recv_sems.at[kk], device_id=to, device_id_type=MESH)

        mine = pltpu.make_async_copy(x_ref, slab(*me), local_sem)
        mine.start()
        first = [copy(0, me, sibling, src=x_ref)]
        first += [copy(1 + j, me, (*chip, c), src=x_ref) for j, chip in enumerate(chips)]
        for cp in first:
            cp.start()
        passed = [copy(4 + j, (*chip, c), sibling) for j, chip in enumerate(chips)]
        for j, chip in enumerate(chips):
            copy(1 + j, (*chip, c), me).wait_recv()
            passed[j].start()
        copy(0, sibling, me).wait_recv()
        for j, chip in enumerate(chips):
            copy(4 + j, (*chip, 1 - c), me).wait_recv()
        for cp in first + passed:
            cp.wait_send()
        mine.wait()

    return pl.pallas_call(
        body, out_shape=jax.ShapeDtypeStruct(full, shard.dtype),
        in_specs=[pl.BlockSpec(memory_space=pl.ANY)], out_specs=pl.BlockSpec(memory_space=pl.ANY),
        scratch_shapes=[pltpu.SemaphoreType.DMA((7,)), pltpu.SemaphoreType.DMA((7,)), pltpu.SemaphoreType.DMA],
        name=name)(shard)


def swap_with_sibling(slabs, name):
    _, R, C = slabs.shape

    def body(g_ref, got_ref, send_sems, recv_sems):
        x, y, c = _place()
        copies = [pltpu.make_async_remote_copy(
            src_ref=g_ref.at[2 * j + (1 - c)], dst_ref=got_ref.at[j], send_sem=send_sems.at[j],
            recv_sem=recv_sems.at[j], device_id=(x, y, 1 - c), device_id_type=MESH) for j in range(N_CHIP)]
        for cp in copies:
            cp.start()
        for cp in copies:
            cp.wait()

    return pl.pallas_call(
        body, out_shape=jax.ShapeDtypeStruct((N_CHIP, R, C), slabs.dtype),
        in_specs=[pl.BlockSpec(memory_space=pl.ANY)], out_specs=pl.BlockSpec(memory_space=pl.ANY),
        scratch_shapes=[pltpu.SemaphoreType.DMA((N_CHIP,)), pltpu.SemaphoreType.DMA((N_CHIP,))],
        name=name)(slabs)


def scatter_to_chips(sums, name):
    _, R, C = sums.shape

    def body(t_ref, got_ref, send_sems, recv_sems, local_sem):
        x, y, c = _place()
        mine = 2 * x + y
        own = pltpu.make_async_copy(t_ref.at[mine], got_ref.at[mine], local_sem)
        own.start()
        copies = []
        for j, (px, py) in enumerate(_other_chips(x, y)):
            copies.append(pltpu.make_async_remote_copy(
                src_ref=t_ref.at[2 * px + py], dst_ref=got_ref.at[mine], send_sem=send_sems.at[j],
                recv_sem=recv_sems.at[j], device_id=(px, py, c), device_id_type=MESH))
        for cp in copies:
            cp.start()
        for j, (px, py) in enumerate(_other_chips(x, y)):
            pltpu.make_async_remote_copy(
                src_ref=t_ref.at[mine], dst_ref=got_ref.at[2 * px + py], send_sem=send_sems.at[j],
                recv_sem=recv_sems.at[j], device_id=(px, py, c), device_id_type=MESH).wait_recv()
        for cp in copies:
            cp.wait_send()
        own.wait()

    return pl.pallas_call(
        body, out_shape=jax.ShapeDtypeStruct((N_CHIP, R, C), sums.dtype),
        in_specs=[pl.BlockSpec(memory_space=pl.ANY)], out_specs=pl.BlockSpec(memory_space=pl.ANY),
        scratch_shapes=[pltpu.SemaphoreType.DMA((3,)), pltpu.SemaphoreType.DMA((3,)), pltpu.SemaphoreType.DMA],
        name=name)(sums)


def _rope_tables(positions, d):
    inv_freq = ROPE_BASE ** (-jnp.arange(0, d, 2, dtype=F32) / d)
    ang = positions.astype(F32)[:, None] * inv_freq
    return jnp.cos(ang), jnp.sin(ang)


def _mla_pad(nope, rope):
    S, H, _ = nope.shape
    half = MLA_ROPE // 2
    z = jnp.zeros((S, H, LANES // 2 - half), nope.dtype)
    return jnp.concatenate([nope, rope[..., :half], z, rope[..., half:], z], axis=2).reshape(S, H * MLA_PAD)


def _mla_unpad(x, H):
    S = x.shape[0]
    half = MLA_ROPE // 2
    x3 = x.reshape(S, H, MLA_PAD)
    rope = jnp.concatenate([x3[..., MLA_NOPE:MLA_NOPE + half],
                            x3[..., MLA_NOPE + LANES // 2:MLA_NOPE + LANES // 2 + half]], axis=2)
    return x3[..., :MLA_NOPE], rope


def _to_slabs(g, axis):
    L, K, N = g.shape
    if axis == 1:
        k = K // N_DEV
        return g.reshape(L, N_DEV, k, N).transpose(1, 0, 2, 3).reshape(N_DEV, L * k, N)
    n = N // N_DEV
    return g.reshape(L, K, N_DEV, n).transpose(2, 0, 1, 3).reshape(N_DEV, L * K, n)


def kernel(x, mem, positions, ret_w_in, ret_gn_g, ret_w_out, mla_w_in, mla_q_norm_g, mla_w_uq, mla_kv_norm_g, mla_w_ukv, mla_w_out, xa_w_q, xa_w_kv, xa_w_out, ffn_w_up, ffn_conv_w, ffn_conv_b, ffn_w_down, ln_mix_g, ln_mix_b, ln_mem_g, ln_mem_b, ln_ffn_g, ln_ffn_b, loss_target, m_ret_w_in, m_ret_gn_g, m_ret_w_out, m_mla_w_in, m_mla_q_norm_g, m_mla_w_uq, m_mla_kv_norm_g, m_mla_w_ukv, m_mla_w_out, m_xa_w_q, m_xa_w_kv, m_xa_w_out, m_ffn_w_up, m_ffn_conv_w, m_ffn_conv_b, m_ffn_w_down, m_ln_mix_g, m_ln_mix_b, m_ln_mem_g, m_ln_mem_b, m_ln_ffn_g, m_ln_ffn_b, v_ret_w_in, v_ret_gn_g, v_ret_w_out, v_mla_w_in, v_mla_q_norm_g, v_mla_w_uq, v_mla_kv_norm_g, v_mla_w_ukv, v_mla_w_out, v_xa_w_q, v_xa_w_kv, v_xa_w_out, v_ffn_w_up, v_ffn_conv_w, v_ffn_conv_b, v_ffn_w_down, v_ln_mix_g, v_ln_mix_b, v_ln_mem_g, v_ln_mem_b, v_ln_ffn_g, v_ln_ffn_b):
    weights = dict(ret_w_in=ret_w_in, ret_gn_g=ret_gn_g, ret_w_out=ret_w_out, mla_w_in=mla_w_in,
                   mla_q_norm_g=mla_q_norm_g, mla_w_uq=mla_w_uq, mla_kv_norm_g=mla_kv_norm_g,
                   mla_w_ukv=mla_w_ukv, mla_w_out=mla_w_out, xa_w_q=xa_w_q, xa_w_kv=xa_w_kv, xa_w_out=xa_w_out,
                   ffn_w_up=ffn_w_up, ffn_conv_w=ffn_conv_w, ffn_conv_b=ffn_conv_b, ffn_w_down=ffn_w_down,
                   ln_mix_g=ln_mix_g, ln_mix_b=ln_mix_b, ln_mem_g=ln_mem_g, ln_mem_b=ln_mem_b,
                   ln_ffn_g=ln_ffn_g, ln_ffn_b=ln_ffn_b)
    mom_m = dict(ret_w_in=m_ret_w_in, ret_gn_g=m_ret_gn_g, ret_w_out=m_ret_w_out, mla_w_in=m_mla_w_in,
                 mla_q_norm_g=m_mla_q_norm_g, mla_w_uq=m_mla_w_uq, mla_kv_norm_g=m_mla_kv_norm_g,
                 mla_w_ukv=m_mla_w_ukv, mla_w_out=m_mla_w_out, xa_w_q=m_xa_w_q, xa_w_kv=m_xa_w_kv,
                 xa_w_out=m_xa_w_out, ffn_w_up=m_ffn_w_up, ffn_conv_w=m_ffn_conv_w, ffn_conv_b=m_ffn_conv_b,
                 ffn_w_down=m_ffn_w_down, ln_mix_g=m_ln_mix_g, ln_mix_b=m_ln_mix_b, ln_mem_g=m_ln_mem_g,
                 ln_mem_b=m_ln_mem_b, ln_ffn_g=m_ln_ffn_g, ln_ffn_b=m_ln_ffn_b)
    mom_v = dict(ret_w_in=v_ret_w_in, ret_gn_g=v_ret_gn_g, ret_w_out=v_ret_w_out, mla_w_in=v_mla_w_in,
                 mla_q_norm_g=v_mla_q_norm_g, mla_w_uq=v_mla_w_uq, mla_kv_norm_g=v_mla_kv_norm_g,
                 mla_w_ukv=v_mla_w_ukv, mla_w_out=v_mla_w_out, xa_w_q=v_xa_w_q, xa_w_kv=v_xa_w_kv,
                 xa_w_out=v_xa_w_out, ffn_w_up=v_ffn_w_up, ffn_conv_w=v_ffn_conv_w, ffn_conv_b=v_ffn_conv_b,
                 ffn_w_down=v_ffn_w_down, ln_mix_g=v_ln_mix_g, ln_mix_b=v_ln_mix_b, ln_mem_g=v_ln_mem_g,
                 ln_mem_b=v_ln_mem_b, ln_ffn_g=v_ln_ffn_g, ln_ffn_b=v_ln_ffn_b)
    order = list(weights)
    BIG = dict(ret_w_in=2, ret_w_out=1, mla_w_in=1, mla_w_uq=2, mla_w_ukv=2, mla_w_out=1,
               xa_w_q=1, xa_w_kv=2, xa_w_out=1, ffn_w_up=2, ffn_w_down=1)
    SMALL_CUT = ("ffn_conv_w", "mla_q_norm_g", "mla_kv_norm_g")
    REPLICATED = ("ret_gn_g", "ffn_conv_b", "ln_mix_g", "ln_mix_b", "ln_mem_g", "ln_mem_b", "ln_ffn_g", "ln_ffn_b")

    x = x[0]
    mem = mem[0]
    positions = positions[0]
    target = loss_target[0]
    S, D = x.shape
    depth = ln_mix_g.shape[0]
    alpha = (2 * depth) ** 0.25
    ret_dk = D // RET_HEADS
    ret_dv = 2 * D // RET_HEADS
    ret_qkw = RET_HEADS * ret_dk
    ret_vw = RET_HEADS * ret_dv
    xa_d = D // XA_HEADS
    assert 2 * ret_qkw == ret_vw and MLA_NOPE == LANES and MLA_V == LANES and MLA_ROPE == LANES // 2
    assert MLA_Q_RANK == MLA_KV_RANK and CHUNK & (CHUNK - 1) == 0
    core = lax.axis_index("c").astype(jnp.int32).reshape(1)
    dev = 4 * lax.axis_index("x") + 2 * lax.axis_index("y") + lax.axis_index("c")

    W = {n: gather_weight(weights[n].astype(PAY_DTYPE), axis, name="gather_" + n) for n, axis in BIG.items()}

    def pack(arrays):
        flat = jnp.concatenate([a.reshape(-1) for a in arrays])
        rows = -(-flat.shape[0] // (8 * LANES)) * 8
        return jnp.pad(flat, (0, rows * LANES - flat.shape[0])).reshape(rows, LANES)

    def unpack(flat, like):
        out, at = [], 0
        for a in like:
            out.append(flat[at:at + a.size].reshape(a.shape))
            at += a.size
        return out

    small_local = [weights[n] for n in SMALL_CUT]
    blk = pack(small_local)
    allsmall = gather_small(blk, name="gather_small_weights").reshape(N_DEV, -1)
    per_dev = [unpack(allsmall[d], small_local) for d in range(N_DEV)]
    conv_w_full, qg_full, kvg_full = [jnp.concatenate([per_dev[d][i] for d in range(N_DEV)], axis=-1)
                                      for i in range(len(SMALL_CUT))]

    rcos, rsin = _rope_tables(positions, ret_dk)
    mcos, msin = _rope_tables(positions, MLA_ROPE)
    zq = jnp.zeros((S, LANES // 2 - MLA_ROPE // 2), F32)
    mla_ta = jnp.concatenate([mcos, zq, mcos, zq], axis=1)
    mla_tb = jnp.concatenate([-msin, zq, msin, zq], axis=1)
    log_gamma = jnp.log(1.0 - 2.0 ** (-5.0 - jnp.arange(RET_HEADS, dtype=F32)))
    log_gamma = jnp.broadcast_to(log_gamma[:, None, None], (RET_HEADS, 8, LANES))
    mla_scale = (MLA_NOPE + MLA_ROPE) ** -0.5
    xa_scale = xa_d ** -0.5
    mem_b = mem.astype(MXU_DTYPE)

    def vec(a, l):
        return a[l][None, :]

    saved = []
    h = x
    hb = x.astype(MXU_DTYPE)
    for layer in range(depth):
        j = layer // 2
        sv = {}
        sv["hb_mix"] = hb
        if layer % 2 == 0:
            proj = mm(hb, W["ret_w_in"], b_layer=j, name="ret_in")
            qk = ret_rope_fwd(proj, rcos, rsin, 2 * RET_HEADS, ret_dk // 2, RET_HEADS, ret_dk ** -0.5, name="ret_rope")
            (o,) = attn_fwd((qk, ret_dk, 0, 1), (qk, ret_dk, RET_HEADS, 1), (proj, ret_dv, RET_HEADS, 1),
                            heads=RET_HEADS, softmax=False, masked=True, log_gamma=log_gamma, name="ret_attn")
            yb = ret_gate_fwd(o, proj, 2, vec(ret_gn_g, j), RET_HEADS, name="ret_gate")
            mix = mm(yb, W["ret_w_out"], b_layer=j, name="ret_out")
            sv.update(proj=proj, qk=qk, o=o, yb=yb)
        else:
            proj = mm(hb, W["mla_w_in"], b_layer=j, name="mla_in")
            cq = rms_fwd(proj, 0, MLA_Q_RANK, vec(qg_full, j), name="mla_q_norm")
            ckv = rms_fwd(proj, 1, MLA_KV_RANK, vec(kvg_full, j), name="mla_kv_norm")
            qf = mm(cq, W["mla_w_uq"], b_layer=j, name="mla_uq")
            kvf = mm(ckv, W["mla_w_ukv"], b_layer=j, name="mla_ukv")
            q3 = qf.reshape(S, MLA_HEADS, MLA_NOPE + MLA_ROPE)
            kv3 = kvf.reshape(S, MLA_HEADS, MLA_NOPE + MLA_V)
            k_rope = jnp.broadcast_to(proj[:, None, MLA_Q_RANK + MLA_KV_RANK:], (S, MLA_HEADS, MLA_ROPE))
            q_pad = _mla_pad(q3[..., :MLA_NOPE], q3[..., MLA_NOPE:])
            k_pad = _mla_pad(kv3[..., :MLA_NOPE], k_rope)
            (qr,) = mla_rope(q_pad, mla_ta, mla_tb, MLA_HEADS, backward=False, head_sum=False,
                             out_dtype=MXU_DTYPE, name="mla_rope_q")
            (kr,) = mla_rope(k_pad, mla_ta, mla_tb, MLA_HEADS, backward=False, head_sum=False,
                             out_dtype=MXU_DTYPE, name="mla_rope_k")
            o, lse = attn_fwd((qr, MLA_PAD, 0, 1), (kr, MLA_PAD, 0, 1), (kvf, MLA_V, 1, 2), heads=MLA_HEADS,
                              softmax=True, masked=True, scale=mla_scale, name="mla_attn")
            ob = o.astype(MXU_DTYPE)
            mix = mm(ob, W["mla_w_out"], b_layer=j, name="mla_out")
            sv.update(proj=proj, cq=cq, ckv=ckv, kvf=kvf, qr=qr, kr=kr, o=o, ob=ob, lse=lse)
        h, hb, sv["xhat_mix"], sv["rstd_mix"] = ln_fwd(h, mix, vec(ln_mix_g, layer), vec(ln_mix_b, layer), alpha,
                                                       name="ln_mix")
        sv["hb_mem"] = hb
        q = mm(hb, W["xa_w_q"], b_layer=layer, name="xa_q")
        kvm = mm(mem_b, W["xa_w_kv"], b_layer=layer, name="xa_kv")
        o, lse = attn_fwd((q, xa_d, 0, 1), (kvm, xa_d, 0, 1), (kvm, xa_d, XA_HEADS, 1), heads=XA_HEADS,
                          softmax=True, masked=False, scale=xa_scale, name="xa_attn")
        ob = o.astype(MXU_DTYPE)
        mix = mm(ob, W["xa_w_out"], b_layer=layer, name="xa_out")
        sv.update(xa_q=q, xa_kvm=kvm, xa_o=o, xa_ob=ob, xa_lse=lse)
        h, hb, sv["xhat_mem"], sv["rstd_mem"] = ln_fwd(h, mix, vec(ln_mem_g, layer), vec(ln_mem_b, layer), alpha,
                                                       name="ln_mem")
        sv["hb_ffn"] = hb
        hup = mm(hb, W["ffn_w_up"], b_layer=layer, name="ffn_up")
        u = conv_glu_fwd(hup, conv_w_full[layer], vec(ffn_conv_b, layer), name="ffn_conv")
        mix = mm(u, W["ffn_w_down"], b_layer=layer, name="ffn_down")
        sv.update(hup=hup, u=u)
        h, hb, sv["xhat_ffn"], sv["rstd_ffn"] = ln_fwd(h, mix, vec(ln_ffn_g, layer), vec(ln_ffn_b, layer), alpha,
                                                       name="ln_ffn")
        saved.append(sv)

    dh, loss_blk = loss_head(h, target, name="loss_head")
    loss = lax.psum(loss_blk[0, 0], ("x", "y", "c"))

    G = {n: lax.empty(W[n].shape, PAY_DTYPE) for n in BIG}
    small = {n: [None] * weights[n].shape[0] for n in REPLICATED + SMALL_CUT}

    def wgrad(name_, a, d, l, tag):
        G[name_] = mm(a, d, mode="tn", out_dtype=PAY_DTYPE, stack=G[name_], stack_layer=l, name=tag)

    for layer in reversed(range(depth)):
        j = layer // 2
        sv = saved[layer]
        dz, dzb, dg, db = ln_bwd(dh, sv["xhat_ffn"], sv["rstd_ffn"], vec(ln_ffn_g, layer), name="ln_ffn_bwd")
        small["ln_ffn_g"][layer], small["ln_ffn_b"][layer] = dg[0], db[0]
        wgrad("ffn_w_down", sv["u"], dzb, layer, "ffn_down_dw")
        du = mm(dzb, W["ffn_w_down"], mode="nt", b_layer=layer, name="ffn_down_dx")
        dhg, dhv, dwg, dwv, dbg, dbv = conv_glu_bwd(du, sv["hup"], conv_w_full[layer], vec(ffn_conv_b, layer),
                                                    name="ffn_conv_bwd")
        small["ffn_conv_w"][layer] = jnp.concatenate([dwg, dwv], axis=1)
        small["ffn_conv_b"][layer] = jnp.concatenate([dbg, dbv], axis=1)[0]
        dhup = jnp.concatenate([dhg, dhv], axis=1)
        wgrad("ffn_w_up", sv["hb_ffn"], dhup, layer, "ffn_up_dw")
        dh = mm(dhup, W["ffn_w_up"], mode="nt", b_layer=layer, add=dz, add_scale=alpha, name="ffn_up_dx")
        dz, dzb, dg, db = ln_bwd(dh, sv["xhat_mem"], sv["rstd_mem"], vec(ln_mem_g, layer), name="ln_mem_bwd")
        small["ln_mem_g"][layer], small["ln_mem_b"][layer] = dg[0], db[0]
        wgrad("xa_w_out", sv["xa_ob"], dzb, layer, "xa_out_dw")
        do = mm(dzb, W["xa_w_out"], mode="nt", b_layer=layer, name="xa_out_dx")
        dq, dk, dv = attn_bwd((sv["xa_q"], xa_d, 0, 1), (sv["xa_kvm"], xa_d, 0, 1), (sv["xa_kvm"], xa_d, XA_HEADS, 1),
                              do, heads=XA_HEADS, softmax=True, masked=False, scale=xa_scale, o=sv["xa_o"],
                              lse=sv["xa_lse"], name="xa_attn_bwd")
        dqb = dq.astype(MXU_DTYPE)
        wgrad("xa_w_kv", mem_b, jnp.concatenate([dk, dv], axis=1).astype(MXU_DTYPE), layer, "xa_kv_dw")
        wgrad("xa_w_q", sv["hb_mem"], dqb, layer, "xa_q_dw")
        dh = mm(dqb, W["xa_w_q"], mode="nt", b_layer=layer, add=dz, add_scale=alpha, name="xa_q_dx")
        dz, dzb, dg, db = ln_bwd(dh, sv["xhat_mix"], sv["rstd_mix"], vec(ln_mix_g, layer), name="ln_mix_bwd")
        small["ln_mix_g"][layer], small["ln_mix_b"][layer] = dg[0], db[0]
        if layer % 2 == 0:
            wgrad("ret_w_out", sv["yb"], dzb, j, "ret_out_dw")
            dy = mm(dzb, W["ret_w_out"], mode="nt", b_layer=j, name="ret_out_dx")
            do, dgate, dgn = ret_gate_bwd(dy, sv["o"], sv["proj"], 2, vec(ret_gn_g, j), RET_HEADS, name="ret_gate_bwd")
            small["ret_gn_g"][j] = dgn[0]
            dq, dk, dv = attn_bwd((sv["qk"], ret_dk, 0, 1), (sv["qk"], ret_dk, RET_HEADS, 1),
                                  (sv["proj"], ret_dv, RET_HEADS, 1), do, heads=RET_HEADS, softmax=False,
                                  masked=True, log_gamma=log_gamma, name="ret_attn_bwd")
            dqk = ret_rope_bwd(dq, dk, rcos, rsin, ret_dk // 2, ret_dk ** -0.5, name="ret_rope_bwd")
            dproj = jnp.concatenate([dqk, dv.astype(MXU_DTYPE), dgate], axis=1)
            wgrad("ret_w_in", sv["hb_mix"], dproj, j, "ret_in_dw")
            dh = mm(dproj, W["ret_w_in"], mode="nt", b_layer=j, add=dz, add_scale=alpha, name="ret_in_dx")
        else:
            wgrad("mla_w_out", sv["ob"], dzb, j, "mla_out_dw")
            do = mm(dzb, W["mla_w_out"], mode="nt", b_layer=j, name="mla_out_dx")
            dq, dk, dv = attn_bwd((sv["qr"], MLA_PAD, 0, 1), (sv["kr"], MLA_PAD, 0, 1), (sv["kvf"], MLA_V, 1, 2), do,
                                  heads=MLA_HEADS, softmax=True, masked=True, scale=mla_scale, o=sv["o"],
                                  lse=sv["lse"], name="mla_attn_bwd")
            (dq_un,) = mla_rope(dq, mla_ta, mla_tb, MLA_HEADS, backward=True, head_sum=False, out_dtype=F32,
                                name="mla_rope_q_bwd")
            dk_un, dk_rope_sum = mla_rope(dk, mla_ta, mla_tb, MLA_HEADS, backward=True, head_sum=True, out_dtype=F32,
                                          name="mla_rope_k_bwd")
            dq_nope, dq_rope = _mla_unpad(dq_un, MLA_HEADS)
            dqf = jnp.concatenate([dq_nope, dq_rope], axis=2).reshape(S, -1).astype(MXU_DTYPE)
            dk_nope, _ = _mla_unpad(dk_un, MLA_HEADS)
            dkvf = jnp.concatenate([dk_nope, dv.reshape(S, MLA_HEADS, MLA_V)], axis=2).reshape(S, -1).astype(MXU_DTYPE)
            half = MLA_ROPE // 2
            dk_rope = jnp.concatenate([dk_rope_sum[:, :half], dk_rope_sum[:, LANES // 2:LANES // 2 + half]], axis=1)
            wgrad("mla_w_uq", sv["cq"], dqf, j, "mla_uq_dw")
            dcq = mm(dqf, W["mla_w_uq"], mode="nt", b_layer=j, name="mla_uq_dx")
            wgrad("mla_w_ukv", sv["ckv"], dkvf, j, "mla_ukv_dw")
            dckv = mm(dkvf, W["mla_w_ukv"], mode="nt", b_layer=j, name="mla_ukv_dx")
            dcq_in, dqg = rms_bwd(dcq, sv["proj"], 0, MLA_Q_RANK, vec(qg_full, j), name="mla_q_norm_bwd")
            dckv_in, dkvg = rms_bwd(dckv, sv["proj"], 1, MLA_KV_RANK, vec(kvg_full, j), name="mla_kv_norm_bwd")
            small["mla_q_norm_g"][j], small["mla_kv_norm_g"][j] = dqg[0], dkvg[0]
            dproj = jnp.concatenate([dcq_in, dckv_in, dk_rope], axis=1).astype(MXU_DTYPE)
            wgrad("mla_w_in", sv["hb_mix"], dproj, j, "mla_in_dw")
            dh = mm(dproj, W["mla_w_in"], mode="nt", b_layer=j, add=dz, add_scale=alpha, name="mla_in_dx")
    grad_x = dh[None]

    grads, deltas, new_m, new_v = {}, {}, {}, {}
    for n, axis in BIG.items():
        slabs = _to_slabs(G[n], axis)
        got = swap_with_sibling(slabs, name="swap_" + n)
        sums = pair_sum(slabs, got, core, name="pairsum_" + n)
        parts = scatter_to_chips(sums, name="scatter_" + n)
        shp = weights[n].shape
        two_d = (shp[0] * shp[1], shp[2])
        outs = adam_update(parts, weights[n].reshape(two_d), mom_m[n].reshape(two_d), mom_v[n].reshape(two_d),
                           name="adam_" + n)
        grads[n], deltas[n], new_m[n], new_v[n] = [o_.reshape(shp) for o_ in outs]

    small_names = list(REPLICATED + SMALL_CUT)
    partial = [jnp.stack(small[n]) for n in small_names]
    allpart = gather_small(pack(partial), name="gather_small_grads")
    rows = allpart.shape[0] // N_DEV
    allpart = allpart.reshape(N_DEV, rows, LANES)

    rep_names = list(REPLICATED)
    rep_w = pack([weights[n] for n in rep_names])
    rep_m = pack([mom_m[n] for n in rep_names])
    rep_v = pack([mom_v[n] for n in rep_names])
    rep_rows = rep_w.shape[0]
    rep_size = sum(weights[n].size for n in rep_names)
    flat_parts = allpart.reshape(N_DEV, rows * LANES)
    rep_parts = jnp.pad(flat_parts[:, :rep_size], ((0, 0), (0, rep_rows * LANES - rep_size)))
    outs = adam_update(rep_parts.reshape(N_DEV, rep_rows, LANES), rep_w, rep_m, rep_v, name="adam_replicated")
    for o_, dst in zip(outs, (grads, deltas, new_m, new_v)):
        for n, a in zip(rep_names, unpack(o_.reshape(-1), [weights[n] for n in rep_names])):
            dst[n] = a

    cut_names = list(SMALL_CUT)
    cut_full = [jnp.stack(small[n]) for n in cut_names]
    cut_parts = []
    at = rep_size
    for n, a in zip(cut_names, cut_full):
        whole = flat_parts[:, at:at + a.size].reshape((N_DEV,) + a.shape)
        at += a.size
        width = weights[n].shape[-1]
        start = (0,) * (whole.ndim - 1) + (dev * width,)
        mine = lax.dynamic_slice(whole, start, whole.shape[:-1] + (width,))
        cut_parts.append(mine.reshape(N_DEV, -1))
    cut_parts = jnp.concatenate(cut_parts, axis=1)
    cut_w = pack([weights[n] for n in cut_names])
    cut_m = pack([mom_m[n] for n in cut_names])
    cut_v = pack([mom_v[n] for n in cut_names])
    cut_rows = cut_w.shape[0]
    cut_parts = jnp.pad(cut_parts, ((0, 0), (0, cut_rows * LANES - cut_parts.shape[1])))
    outs = adam_update(cut_parts.reshape(N_DEV, cut_rows, LANES), cut_w, cut_m, cut_v, name="adam_small_cut")
    for o_, dst in zip(outs, (grads, deltas, new_m, new_v)):
        for n, a in zip(cut_names, unpack(o_.reshape(-1), [weights[n] for n in cut_names])):
            dst[n] = a

    return (loss, grad_x, *[grads[n] for n in order], *[deltas[n] for n in order],
            *[new_m[n] for n in order], *[new_v[n] for n in order])
```

```python
import functools
import math

import jax
import jax.numpy as jnp
from jax import lax
from jax.experimental import pallas as pl
from jax.experimental.pallas import tpu as pltpu

F32 = jnp.float32
MXU_DTYPE = jnp.bfloat16
PAY_DTYPE = jnp.bfloat16
MESH = pl.DeviceIdType.MESH
N_DEV = 8
N_CHIP = 4

DEPTH = 4
CHUNK = 64
RET_HEADS = 8
MLA_HEADS = 16
MLA_Q_RANK = 512
MLA_KV_RANK = 512
MLA_NOPE = 128
MLA_ROPE = 64
MLA_V = 128
MLA_PAD = 256
XA_HEADS = 4
ROPE_BASE = 10000.0
LN_EPS = 1e-5
RMS_EPS = 1e-6
NEG_INF = -1e30
ADAM_LR = 0.001
ADAM_B1 = 0.9
ADAM_B2 = 0.999
ADAM_EPS = 1e-08
ADAM_WD = 0.01
ADAM_STEP = 10

LANES = 128
VMEM_LIMIT_BYTES = 48 * 1024 * 1024
TILE_M = 1024
TILE_N = 1024
TILE_K = 2048
ATTN_TILE = 512
ROW_TILE = 256


def _tile(n, cap, mult=LANES):
    best = None
    for t in range(mult, min(n, cap) + 1, mult):
        if n % t == 0:
            best = t
    return n if best is None else best


def _params(n_axes):
    return pltpu.CompilerParams(dimension_semantics=("arbitrary",) * n_axes,
                                vmem_limit_bytes=VMEM_LIMIT_BYTES)


def mm(a, b, *, mode="nn", out_dtype=F32, b_layer=None, add=None, add_scale=1.0,
       stack=None, stack_layer=None, name):
    if mode == "nn":
        (M, K), N = a.shape, b.shape[-1]
    elif mode == "nt":
        (M, K), N = a.shape, b.shape[-2]
    else:
        (K, M), N = a.shape, b.shape[-1]
    tm, tn, tk = _tile(M, TILE_M), _tile(N, TILE_N), _tile(K, TILE_K)
    nk = K // tk
    dims = {"nn": (((1,), (0,)), ((), ())), "nt": (((1,), (1,)), ((), ())),
            "tn": (((0,), (0,)), ((), ()))}[mode]

    a_spec = (pl.BlockSpec((tk, tm), lambda i, j, k: (k, i)) if mode == "tn"
              else pl.BlockSpec((tm, tk), lambda i, j, k: (i, k)))
    b_blk, b_idx = ((tn, tk), lambda i, j, k: (j, k)) if mode == "nt" else ((tk, tn), lambda i, j, k: (k, j))
    if b_layer is None:
        b_spec = pl.BlockSpec(b_blk, b_idx)
    else:
        b_spec = pl.BlockSpec((None,) + b_blk, lambda i, j, k: (b_layer,) + b_idx(i, j, k))
    in_specs, operands = [a_spec, b_spec], [a, b]
    if add is not None:
        in_specs.append(pl.BlockSpec((tm, tn), lambda i, j, k: (i, j)))
        operands.append(add)
    aliases = {}
    if stack is None:
        out_shape = jax.ShapeDtypeStruct((M, N), out_dtype)
        out_spec = pl.BlockSpec((tm, tn), lambda i, j, k: (i, j))
    else:
        out_shape = jax.ShapeDtypeStruct(stack.shape, stack.dtype)
        out_spec = pl.BlockSpec((None, tm, tn), lambda i, j, k: (stack_layer, i, j))
        in_specs.append(pl.BlockSpec(memory_space=pl.ANY))
        aliases = {len(operands): 0}
        operands.append(stack)
    n_in = len(operands)

    def body(*refs):
        a_ref, b_ref = refs[0], refs[1]
        add_ref = refs[2] if add is not None else None
        o_ref = refs[n_in]
        k = pl.program_id(2)
        part = lax.dot_general(a_ref[...].astype(MXU_DTYPE), b_ref[...].astype(MXU_DTYPE),
                               dims, preferred_element_type=F32)

        def finish(r):
            if add_ref is not None:
                r = r + add_scale * add_ref[...].astype(F32)
            o_ref[...] = r.astype(o_ref.dtype)

        if nk == 1:
            finish(part)
        else:
            acc_ref = refs[n_in + 1]

            @pl.when(k == 0)
            def _():
                acc_ref[...] = part

            @pl.when(jnp.logical_and(k > 0, k < nk - 1))
            def _():
                acc_ref[...] += part

            @pl.when(k == nk - 1)
            def _():
                finish(acc_ref[...] + part)

    return pl.pallas_call(
        body, out_shape=out_shape, grid=(M // tm, N // tn, nk), in_specs=in_specs, out_specs=out_spec,
        scratch_shapes=[pltpu.VMEM((tm, tn), F32)] if nk > 1 else [], input_output_aliases=aliases,
        compiler_params=_params(3), name=name)(*operands)


def ln_fwd(h, f, g, b, alpha, name):
    S, D = h.shape
    tm = _tile(S, ROW_TILE, 8)

    def body(h_ref, f_ref, g_ref, b_ref, y_ref, yb_ref, xhat_ref, rstd_ref):
        z = alpha * h_ref[...] + f_ref[...]
        mu = jnp.mean(z, axis=-1, keepdims=True)
        zc = z - mu
        var = jnp.mean(zc * zc, axis=-1, keepdims=True)
        rstd = lax.rsqrt(var + LN_EPS)
        xhat = zc * rstd
        y = xhat * g_ref[...] + b_ref[...]
        y_ref[...] = y
        yb_ref[...] = y.astype(yb_ref.dtype)
        xhat_ref[...] = xhat
        rstd_ref[...] = rstd

    row = pl.BlockSpec((tm, D), lambda i: (i, 0))
    vec = pl.BlockSpec((1, D), lambda i: (0, 0))
    return pl.pallas_call(
        body,
        out_shape=(jax.ShapeDtypeStruct((S, D), F32), jax.ShapeDtypeStruct((S, D), MXU_DTYPE),
                   jax.ShapeDtypeStruct((S, D), F32), jax.ShapeDtypeStruct((S, 1), F32)),
        grid=(S // tm,), in_specs=[row, row, vec, vec],
        out_specs=(row, row, row, pl.BlockSpec((tm, 1), lambda i: (i, 0))),
        compiler_params=_params(1), name=name)(h, f, g, b)


def ln_bwd(dy, xhat, rstd, g, name):
    S, D = dy.shape
    tm = _tile(S, ROW_TILE, 8)

    def body(dy_ref, xhat_ref, rstd_ref, g_ref, dz_ref, dzb_ref, dg_ref, db_ref):
        @pl.when(pl.program_id(0) == 0)
        def _():
            dg_ref[...] = jnp.zeros_like(dg_ref)
            db_ref[...] = jnp.zeros_like(db_ref)

        dy = dy_ref[...]
        xhat = xhat_ref[...]
        dxh = dy * g_ref[...]
        m1 = jnp.mean(dxh, axis=-1, keepdims=True)
        m2 = jnp.mean(dxh * xhat, axis=-1, keepdims=True)
        dz = rstd_ref[...] * (dxh - m1 - xhat * m2)
        dz_ref[...] = dz
        dzb_ref[...] = dz.astype(dzb_ref.dtype)
        dg_ref[...] += jnp.sum(dy * xhat, axis=0, keepdims=True)
        db_ref[...] += jnp.sum(dy, axis=0, keepdims=True)

    row = pl.BlockSpec((tm, D), lambda i: (i, 0))
    vec = pl.BlockSpec((1, D), lambda i: (0, 0))
    return pl.pallas_call(
        body,
        out_shape=(jax.ShapeDtypeStruct((S, D), F32), jax.ShapeDtypeStruct((S, D), MXU_DTYPE),
                   jax.ShapeDtypeStruct((1, D), F32), jax.ShapeDtypeStruct((1, D), F32)),
        grid=(S // tm,), in_specs=[row, row, pl.BlockSpec((tm, 1), lambda i: (i, 0)), vec],
        out_specs=(row, row, vec, vec), compiler_params=_params(1), name=name)(dy, xhat, rstd, g)


def rms_fwd(x, col, width, g, name):
    S = x.shape[0]
    tm = _tile(S, ROW_TILE, 8)

    def body(x_ref, g_ref, y_ref):
        xv = x_ref[...]
        r = lax.rsqrt(jnp.mean(xv * xv, axis=-1, keepdims=True) + RMS_EPS)
        y_ref[...] = (xv * r * g_ref[...]).astype(y_ref.dtype)

    return pl.pallas_call(
        body, out_shape=jax.ShapeDtypeStruct((S, width), MXU_DTYPE), grid=(S // tm,),
        in_specs=[pl.BlockSpec((tm, width), lambda i: (i, col)), pl.BlockSpec((1, width), lambda i: (0, 0))],
        out_specs=pl.BlockSpec((tm, width), lambda i: (i, 0)), compiler_params=_params(1), name=name)(x, g)


def rms_bwd(dy, x, col, width, g, name):
    S = x.shape[0]
    tm = _tile(S, ROW_TILE, 8)

    def body(dy_ref, x_ref, g_ref, dx_ref, dg_ref):
        @pl.when(pl.program_id(0) == 0)
        def _():
            dg_ref[...] = jnp.zeros_like(dg_ref)

        xv = x_ref[...]
        dy = dy_ref[...]
        r = lax.rsqrt(jnp.mean(xv * xv, axis=-1, keepdims=True) + RMS_EPS)
        dxn = dy * g_ref[...]
        m = jnp.mean(dxn * xv, axis=-1, keepdims=True)
        dx_ref[...] = r * (dxn - xv * (r * r * m))
        dg_ref[...] += jnp.sum(dy * xv * r, axis=0, keepdims=True)

    return pl.pallas_call(
        body, out_shape=(jax.ShapeDtypeStruct((S, width), F32), jax.ShapeDtypeStruct((1, width), F32)),
        grid=(S // tm,),
        in_specs=[pl.BlockSpec((tm, width), lambda i: (i, 0)), pl.BlockSpec((tm, width), lambda i: (i, col)),
                  pl.BlockSpec((1, width), lambda i: (0, 0))],
        out_specs=(pl.BlockSpec((tm, width), lambda i: (i, 0)), pl.BlockSpec((1, width), lambda i: (0, 0))),
        compiler_params=_params(1), name=name)(dy, x, g)


def ret_rope_fwd(proj, cos, sin, n_groups, half, k_from, k_scale, name):
    S = proj.shape[0]
    W = n_groups * 2 * half
    tm = _tile(S, ROW_TILE, 8)

    def body(x_ref, c_ref, s_ref, o_ref):
        c, s = c_ref[...], s_ref[...]
        for gi in range(n_groups):
            lo = gi * 2 * half
            x1 = x_ref[:, lo:lo + half]
            x2 = x_ref[:, lo + half:lo + 2 * half]
            sc = k_scale if gi >= k_from else 1.0
            o_ref[:, lo:lo + half] = ((x1 * c - x2 * s) * sc).astype(o_ref.dtype)
            o_ref[:, lo + half:lo + 2 * half] = ((x2 * c + x1 * s) * sc).astype(o_ref.dtype)

    tab = pl.BlockSpec((tm, half), lambda i: (i, 0))
    return pl.pallas_call(
        body, out_shape=jax.ShapeDtypeStruct((S, W), MXU_DTYPE), grid=(S // tm,),
        in_specs=[pl.BlockSpec((tm, W), lambda i: (i, 0)), tab, tab],
        out_specs=pl.BlockSpec((tm, W), lambda i: (i, 0)), compiler_params=_params(1), name=name)(proj, cos, sin)


def ret_rope_bwd(dq, dk, cos, sin, half, k_scale, name):
    S, Wq = dq.shape
    n_heads = Wq // (2 * half)
    tm = _tile(S, ROW_TILE, 8)

    def body(dq_ref, dk_ref, c_ref, s_ref, o_ref):
        c, s = c_ref[...], s_ref[...]
        for part, (d_ref, sc) in enumerate(((dq_ref, 1.0), (dk_ref, k_scale))):
            for hi in range(n_heads):
                lo = hi * 2 * half
                d1 = d_ref[:, lo:lo + half]
                d2 = d_ref[:, lo + half:lo + 2 * half]
                base = part * Wq + lo
                o_ref[:, base:base + half] = ((d1 * c + d2 * s) * sc).astype(o_ref.dtype)
                o_ref[:, base + half:base + 2 * half] = ((d2 * c - d1 * s) * sc).astype(o_ref.dtype)

    tab = pl.BlockSpec((tm, half), lambda i: (i, 0))
    row = pl.BlockSpec((tm, Wq), lambda i: (i, 0))
    return pl.pallas_call(
        body, out_shape=jax.ShapeDtypeStruct((S, 2 * Wq), MXU_DTYPE), grid=(S // tm,),
        in_specs=[row, row, tab, tab], out_specs=pl.BlockSpec((tm, 2 * Wq), lambda i: (i, 0)),
        compiler_params=_params(1), name=name)(dq, dk, cos, sin)


def mla_rope(x, ta, tb, n_heads, *, backward, head_sum, out_dtype, name):
    S = x.shape[0]
    W = n_heads * MLA_PAD
    tm = _tile(S, ROW_TILE, 8)

    def body(x_ref, a_ref, b_ref, o_ref, *rest):
        a, b = a_ref[...], b_ref[...]
        total = jnp.zeros((tm, LANES), F32)
        for hi in range(n_heads):
            lo = hi * MLA_PAD
            o_ref[:, lo:lo + MLA_NOPE] = x_ref[:, lo:lo + MLA_NOPE].astype(o_ref.dtype)
            t = x_ref[:, lo + MLA_NOPE:lo + MLA_PAD].astype(F32)
            if backward:
                r = t * a + pltpu.roll(t * b, LANES // 2, 1)
            else:
                r = t * a + pltpu.roll(t, LANES // 2, 1) * b
            o_ref[:, lo + MLA_NOPE:lo + MLA_PAD] = r.astype(o_ref.dtype)
            total = total + r
        if head_sum:
            rest[0][...] = total

    tab = pl.BlockSpec((tm, LANES), lambda i: (i, 0))
    row = pl.BlockSpec((tm, W), lambda i: (i, 0))
    out_shape = [jax.ShapeDtypeStruct((S, W), out_dtype)]
    out_specs = [row]
    if head_sum:
        out_shape.append(jax.ShapeDtypeStruct((S, LANES), F32))
        out_specs.append(tab)
    return pl.pallas_call(
        body, out_shape=tuple(out_shape), grid=(S // tm,), in_specs=[row, tab, tab],
        out_specs=tuple(out_specs), compiler_params=_params(1), name=name)(x, ta, tb)


def _visible(qi, kj, tq, tk):
    n = qi * tq + lax.broadcasted_iota(jnp.int32, (tq, tk), 0)
    m = kj * tk + lax.broadcasted_iota(jnp.int32, (tq, tk), 1)
    shift = CHUNK.bit_length() - 1
    vis = lax.shift_right_logical(m, shift) <= lax.shift_right_logical(n, shift)
    return vis, jnp.abs(n - m).astype(F32)


def attn_fwd(q, k, v, *, heads, softmax, masked, scale=1.0, log_gamma=None, name):
    (qa, dqk, q0, qs), (ka, _, k0, ks), (va, dv, v0, vs) = q, k, v
    Sq, Sk = qa.shape[0], ka.shape[0]
    tq, tk = _tile(Sq, ATTN_TILE, 8), _tile(Sk, ATTN_TILE, 8)
    nq, nk = Sq // tq, Sk // tk
    assert not masked or tq == tk

    def body(*refs):
        if softmax:
            q_ref, k_ref, v_ref, o_ref, lse_ref, m_ref, l_ref, acc_ref = refs
        else:
            q_ref, k_ref, v_ref, lg_ref, o_ref, acc_ref = refs
        qi, kj = pl.program_id(1), pl.program_id(2)

        @pl.when(kj == 0)
        def _():
            acc_ref[...] = jnp.zeros_like(acc_ref)
            if softmax:
                m_ref[...] = jnp.full_like(m_ref, NEG_INF)
                l_ref[...] = jnp.zeros_like(l_ref)

        def step(diagonal):
            s = lax.dot_general(q_ref[...].astype(MXU_DTYPE), k_ref[...].astype(MXU_DTYPE),
                                (((1,), (1,)), ((), ())), preferred_element_type=F32)
            vb = v_ref[...].astype(MXU_DTYPE)
            if diagonal or not softmax:
                vis, dist = _visible(qi, kj, tq, tk)
            if softmax:
                s = s * scale
                if diagonal:
                    s = jnp.where(vis, s, NEG_INF)
                m_old = m_ref[...]
                m_new = jnp.maximum(m_old, jnp.max(s, axis=-1, keepdims=True))
                p = jnp.exp(s - m_new)
                corr = jnp.exp(m_old - m_new)
                l_ref[...] = corr * l_ref[...] + jnp.sum(p, axis=-1, keepdims=True)
                acc_ref[...] = corr * acc_ref[...] + jnp.dot(p.astype(MXU_DTYPE), vb, preferred_element_type=F32)
                m_ref[...] = m_new
            else:
                decay = jnp.exp(lg_ref[0:1, 0:1] * dist)
                if diagonal:
                    decay = jnp.where(vis, decay, 0.0)
                p = s * decay
                acc_ref[...] += jnp.dot(p.astype(MXU_DTYPE), vb, preferred_element_type=F32)

        if masked:
            pl.when(kj < qi)(functools.partial(step, False))
            pl.when(kj == qi)(functools.partial(step, True))
        else:
            step(False)

        @pl.when(kj == nk - 1)
        def _():
            if softmax:
                o_ref[...] = (acc_ref[...] / l_ref[...]).astype(o_ref.dtype)
                lse_ref[...] = m_ref[...] + jnp.log(l_ref[...])
            else:
                o_ref[...] = acc_ref[...].astype(o_ref.dtype)

    kcap = (lambda qi, kj: jnp.minimum(kj, qi)) if masked else (lambda qi, kj: kj)
    in_specs = [pl.BlockSpec((tq, dqk), lambda h, qi, kj: (qi, q0 + qs * h)),
                pl.BlockSpec((tk, dqk), lambda h, qi, kj: (kcap(qi, kj), k0 + ks * h)),
                pl.BlockSpec((tk, dv), lambda h, qi, kj: (kcap(qi, kj), v0 + vs * h))]
    operands = [qa, ka, va]
    out_shape = [jax.ShapeDtypeStruct((Sq, heads * dv), F32)]
    out_specs = [pl.BlockSpec((tq, dv), lambda h, qi, kj: (qi, h))]
    scratch = []
    if softmax:
        out_shape.append(jax.ShapeDtypeStruct((heads, Sq, 1), F32))
        out_specs.append(pl.BlockSpec((None, tq, 1), lambda h, qi, kj: (h, qi, 0)))
        scratch += [pltpu.VMEM((tq, 1), F32), pltpu.VMEM((tq, 1), F32)]
    else:
        in_specs.append(pl.BlockSpec((None, 8, LANES), lambda h, qi, kj: (h, 0, 0)))
        operands.append(log_gamma)
    scratch.append(pltpu.VMEM((tq, dv), F32))
    return pl.pallas_call(
        body, out_shape=tuple(out_shape), grid=(heads, nq, nk), in_specs=in_specs, out_specs=tuple(out_specs),
        scratch_shapes=scratch, compiler_params=_params(3), name=name)(*operands)


def attn_bwd(q, k, v, do, *, heads, softmax, masked, scale=1.0, log_gamma=None, o=None, lse=None, name):
    (qa, dqk, q0, qs), (ka, _, k0, ks), (va, dv, v0, vs) = q, k, v
    Sq, Sk = qa.shape[0], ka.shape[0]
    tq, tk = _tile(Sq, ATTN_TILE, 8), _tile(Sk, ATTN_TILE, 8)
    nq, nk = Sq // tq, Sk // tk
    assert not masked or tq == tk
    contract0 = (((0,), (0,)), ((), ()))

    def body(*refs):
        if softmax:
            q_ref, k_ref, v_ref, do_ref, o_ref, lse_ref, dq_ref, dk_ref, dv_ref = refs
        else:
            q_ref, k_ref, v_ref, do_ref, lg_ref, dq_ref, dk_ref, dv_ref = refs
        kj, qi = pl.program_id(1), pl.program_id(2)

        @pl.when(jnp.logical_and(kj == 0, qi == 0))
        def _():
            dq_ref[...] = jnp.zeros_like(dq_ref)

        @pl.when(qi == 0)
        def _():
            dk_ref[...] = jnp.zeros_like(dk_ref)
            dv_ref[...] = jnp.zeros_like(dv_ref)

        def step(diagonal):
            qb = q_ref[...].astype(MXU_DTYPE)
            kb = k_ref[...].astype(MXU_DTYPE)
            vb = v_ref[...].astype(MXU_DTYPE)
            dob = do_ref[...].astype(MXU_DTYPE)
            s = lax.dot_general(qb, kb, (((1,), (1,)), ((), ())), preferred_element_type=F32)
            dp = lax.dot_general(dob, vb, (((1,), (1,)), ((), ())), preferred_element_type=F32)
            if diagonal or not softmax:
                vis, dist = _visible(qi, kj, tq, tk)
            if softmax:
                s = s * scale
                if diagonal:
                    s = jnp.where(vis, s, NEG_INF)
                p = jnp.exp(s - lse_ref[...])
                delta = jnp.sum(do_ref[...].astype(F32) * o_ref[...], axis=-1, keepdims=True)
                ds = p * (dp - delta) * scale
            else:
                decay = jnp.exp(lg_ref[0:1, 0:1] * dist)
                if diagonal:
                    decay = jnp.where(vis, decay, 0.0)
                p = s * decay
                ds = dp * decay
            pb = p.astype(MXU_DTYPE)
            dsb = ds.astype(MXU_DTYPE)
            dv_ref[...] += lax.dot_general(pb, dob, contract0, preferred_element_type=F32)
            dk_ref[...] += lax.dot_general(dsb, qb, contract0, preferred_element_type=F32)
            rows = pl.ds(pl.multiple_of(qi * tq, tq), tq)
            dq_ref[rows, :] += jnp.dot(dsb, kb, preferred_element_type=F32)

        if masked:
            pl.when(qi > kj)(functools.partial(step, False))
            pl.when(qi == kj)(functools.partial(step, True))
        else:
            step(False)

    qcap = (lambda kj, qi: jnp.maximum(qi, kj)) if masked else (lambda kj, qi: qi)
    in_specs = [pl.BlockSpec((tq, dqk), lambda h, kj, qi: (qcap(kj, qi), q0 + qs * h)),
                pl.BlockSpec((tk, dqk), lambda h, kj, qi: (kj, k0 + ks * h)),
                pl.BlockSpec((tk, dv), lambda h, kj, qi: (kj, v0 + vs * h)),
                pl.BlockSpec((tq, dv), lambda h, kj, qi: (qcap(kj, qi), h))]
    operands = [qa, ka, va, do]
    if softmax:
        in_specs += [pl.BlockSpec((tq, dv), lambda h, kj, qi: (qcap(kj, qi), h)),
                     pl.BlockSpec((None, tq, 1), lambda h, kj, qi: (h, qcap(kj, qi), 0))]
        operands += [o, lse]
    else:
        in_specs.append(pl.BlockSpec((None, 8, LANES), lambda h, kj, qi: (h, 0, 0)))
        operands.append(log_gamma)
    return pl.pallas_call(
        body,
        out_shape=(jax.ShapeDtypeStruct((Sq, heads * dqk), F32), jax.ShapeDtypeStruct((Sk, heads * dqk), F32),
                   jax.ShapeDtypeStruct((Sk, heads * dv), F32)),
        grid=(heads, nk, nq), in_specs=in_specs,
        out_specs=(pl.BlockSpec((Sq, dqk), lambda h, kj, qi: (0, h)),
                   pl.BlockSpec((tk, dqk), lambda h, kj, qi: (kj, h)),
                   pl.BlockSpec((tk, dv), lambda h, kj, qi: (kj, h))),
        compiler_params=_params(3), name=name)(*operands)


def _sigmoid(x):
    return 1.0 / (1.0 + jnp.exp(-x))


def ret_gate_fwd(o, proj, gate_col, gn, heads, name):
    S, W = o.shape
    dv = W // heads
    tm = _tile(S, ROW_TILE // 2, 8)

    def body(o_ref, g_ref, gn_ref, y_ref):
        for hi in range(heads):
            cols = slice(hi * dv, (hi + 1) * dv)
            oh = o_ref[:, cols]
            mu = jnp.mean(oh, axis=-1, keepdims=True)
            oc = oh - mu
            rstd = lax.rsqrt(jnp.mean(oc * oc, axis=-1, keepdims=True) + LN_EPS)
            gt = g_ref[:, cols]
            y_ref[:, cols] = (gt * _sigmoid(gt) * (oc * rstd) * gn_ref[:, cols]).astype(y_ref.dtype)

    row = pl.BlockSpec((tm, W), lambda i: (i, 0))
    return pl.pallas_call(
        body, out_shape=jax.ShapeDtypeStruct((S, W), MXU_DTYPE), grid=(S // tm,),
        in_specs=[row, pl.BlockSpec((tm, W), lambda i: (i, gate_col)), pl.BlockSpec((1, W), lambda i: (0, 0))],
        out_specs=row, compiler_params=_params(1), name=name)(o, proj, gn)


def ret_gate_bwd(dy, o, proj, gate_col, gn, heads, name):
    S, W = o.shape
    dv = W // heads
    tm = _tile(S, ROW_TILE // 2, 8)

    def body(dy_ref, o_ref, g_ref, gn_ref, do_ref, dgt_ref, dgn_ref):
        @pl.when(pl.program_id(0) == 0)
        def _():
            dgn_ref[...] = jnp.zeros_like(dgn_ref)

        for hi in range(heads):
            cols = slice(hi * dv, (hi + 1) * dv)
            oh = o_ref[:, cols]
            mu = jnp.mean(oh, axis=-1, keepdims=True)
            oc = oh - mu
            rstd = lax.rsqrt(jnp.mean(oc * oc, axis=-1, keepdims=True) + LN_EPS)
            xhat = oc * rstd
            gt = g_ref[:, cols]
            sg = _sigmoid(gt)
            gain = gn_ref[:, cols]
            dy = dy_ref[:, cols]
            dgt_ref[:, cols] = (dy * xhat * gain * (sg * (1.0 + gt * (1.0 - sg)))).astype(dgt_ref.dtype)
            dn = dy * (gt * sg)
            dgn_ref[:, cols] += jnp.sum(dn * xhat, axis=0, keepdims=True)
            dxh = dn * gain
            m1 = jnp.mean(dxh, axis=-1, keepdims=True)
            m2 = jnp.mean(dxh * xhat, axis=-1, keepdims=True)
            do_ref[:, cols] = (rstd * (dxh - m1 - xhat * m2)).astype(do_ref.dtype)

    row = pl.BlockSpec((tm, W), lambda i: (i, 0))
    vec = pl.BlockSpec((1, W), lambda i: (0, 0))
    return pl.pallas_call(
        body,
        out_shape=(jax.ShapeDtypeStruct((S, W), MXU_DTYPE), jax.ShapeDtypeStruct((S, W), MXU_DTYPE),
                   jax.ShapeDtypeStruct((1, W), F32)),
        grid=(S // tm,), in_specs=[row, row, pl.BlockSpec((tm, W), lambda i: (i, gate_col)), vec],
        out_specs=(row, row, vec), compiler_params=_params(1), name=name)(dy, o, proj, gn)


def _shift_down(x, s):
    rows = lax.broadcasted_iota(jnp.int32, x.shape, 0)
    return jnp.where(rows >= s, pltpu.roll(x, s, 0), 0.0)


def _shift_up(x, s):
    n = x.shape[0]
    rows = lax.broadcasted_iota(jnp.int32, x.shape, 0)
    return jnp.where(rows < n - s, pltpu.roll(x, n - s, 0), 0.0)


def _conv3(x, w_ref, b_ref):
    return (w_ref[2:3, :] * x + w_ref[1:2, :] * _shift_down(x, 1) + w_ref[0:1, :] * _shift_down(x, 2)
            + b_ref[...])


def conv_glu_fwd(hup, w, b, name):
    S, W2 = hup.shape
    F = W2 // 2
    tc = _tile(F, LANES)
    nb = F // tc

    def body(g_ref, v_ref, wg_ref, wv_ref, bg_ref, bv_ref, u_ref):
        cg = _conv3(g_ref[...], wg_ref, bg_ref)
        cv = _conv3(v_ref[...], wv_ref, bv_ref)
        u_ref[...] = (cg * _sigmoid(cg) * cv).astype(u_ref.dtype)

    def col(rows, off):
        return pl.BlockSpec((rows, tc), lambda j: (0, j + off))

    return pl.pallas_call(
        body, out_shape=jax.ShapeDtypeStruct((S, F), MXU_DTYPE), grid=(nb,),
        in_specs=[col(S, 0), col(S, nb), col(3, 0), col(3, nb), col(1, 0), col(1, nb)],
        out_specs=col(S, 0), compiler_params=_params(1), name=name)(hup, hup, w, w, b, b)


def conv_glu_bwd(du, hup, w, b, name):
    S, W2 = hup.shape
    F = W2 // 2
    tc = _tile(F, LANES)
    nb = F // tc

    def back(dc, x, w_ref, dh_ref, dw_ref, db_ref):
        dw_ref[2:3, :] = jnp.sum(dc * x, axis=0, keepdims=True)
        dw_ref[1:2, :] = jnp.sum(dc * _shift_down(x, 1), axis=0, keepdims=True)
        dw_ref[0:1, :] = jnp.sum(dc * _shift_down(x, 2), axis=0, keepdims=True)
        db_ref[...] = jnp.sum(dc, axis=0, keepdims=True)
        dh_ref[...] = (w_ref[2:3, :] * dc + w_ref[1:2, :] * _shift_up(dc, 1)
                       + w_ref[0:1, :] * _shift_up(dc, 2)).astype(dh_ref.dtype)

    def body(du_ref, g_ref, v_ref, wg_ref, wv_ref, bg_ref, bv_ref,
             dhg_ref, dhv_ref, dwg_ref, dwv_ref, dbg_ref, dbv_ref):
        xg, xv = g_ref[...], v_ref[...]
        cg = _conv3(xg, wg_ref, bg_ref)
        cv = _conv3(xv, wv_ref, bv_ref)
        sg = _sigmoid(cg)
        du = du_ref[...]
        back(du * cv * (sg * (1.0 + cg * (1.0 - sg))), xg, wg_ref, dhg_ref, dwg_ref, dbg_ref)
        back(du * (cg * sg), xv, wv_ref, dhv_ref, dwv_ref, dbv_ref)

    def col(rows, off):
        return pl.BlockSpec((rows, tc), lambda j: (0, j + off))

    return pl.pallas_call(
        body,
        out_shape=(jax.ShapeDtypeStruct((S, F), MXU_DTYPE), jax.ShapeDtypeStruct((S, F), MXU_DTYPE),
                   jax.ShapeDtypeStruct((3, F), F32), jax.ShapeDtypeStruct((3, F), F32),
                   jax.ShapeDtypeStruct((1, F), F32), jax.ShapeDtypeStruct((1, F), F32)),
        grid=(nb,),
        in_specs=[col(S, 0), col(S, 0), col(S, nb), col(3, 0), col(3, nb), col(1, 0), col(1, nb)],
        out_specs=(col(S, 0), col(S, 0), col(3, 0), col(3, 0), col(1, 0), col(1, 0)),
        compiler_params=_params(1), name=name)(du, hup, hup, w, w, b, b)


def loss_head(y, target, name):
    S, D = y.shape
    tm = _tile(S, ROW_TILE, 8)

    def body(y_ref, t_ref, dy_ref, loss_ref):
        @pl.when(pl.program_id(0) == 0)
        def _():
            loss_ref[...] = jnp.zeros_like(loss_ref)

        e = y_ref[...] - t_ref[...]
        dy_ref[...] = e * (1.0 / D)
        part = jnp.sum(jnp.sum(e * e, axis=-1, keepdims=True), axis=0, keepdims=True) * (0.5 / D)
        loss_ref[...] += jnp.broadcast_to(part, loss_ref.shape)

    row = pl.BlockSpec((tm, D), lambda i: (i, 0))
    return pl.pallas_call(
        body, out_shape=(jax.ShapeDtypeStruct((S, D), F32), jax.ShapeDtypeStruct((8, LANES), F32)),
        grid=(S // tm,), in_specs=[row, row], out_specs=(row, pl.BlockSpec((8, LANES), lambda i: (0, 0))),
        compiler_params=_params(1), name=name)(y, target)


def adam_update(parts, w, m, v, name):
    n_parts, R, C = parts.shape
    tr = _tile(R, max(8, (1 << 19) // C), 8)
    c1 = 1.0 - ADAM_B1 ** ADAM_STEP
    c2 = 1.0 - ADAM_B2 ** ADAM_STEP

    def body(p_ref, w_ref, m_ref, v_ref, g_ref, d_ref, nm_ref, nv_ref):
        g = p_ref[0].astype(F32)
        for pi in range(1, n_parts):
            g = g + p_ref[pi].astype(F32)
        nm = ADAM_B1 * m_ref[...] + (1.0 - ADAM_B1) * g
        nv = ADAM_B2 * v_ref[...] + (1.0 - ADAM_B2) * (g * g)
        g_ref[...] = g
        nm_ref[...] = nm
        nv_ref[...] = nv
        d_ref[...] = -ADAM_LR * ((nm / c1) / (jnp.sqrt(nv / c2) + ADAM_EPS) + ADAM_WD * w_ref[...])

    row = pl.BlockSpec((tr, C), lambda i: (i, 0))
    out = jax.ShapeDtypeStruct((R, C), F32)
    return pl.pallas_call(
        body, out_shape=(out, out, out, out), grid=(R // tr,),
        in_specs=[pl.BlockSpec((n_parts, tr, C), lambda i: (0, i, 0)), row, row, row],
        out_specs=(row, row, row, row), compiler_params=_params(1), name=name)(parts, w, m, v)


def pair_sum(own, got, core, name):
    _, R, C = own.shape
    tr = _tile(R, max(16, (1 << 19) // C), 16)

    def body(core_ref, a_ref, b_ref, o_ref):
        o_ref[...] = (a_ref[...].astype(F32) + b_ref[...].astype(F32)).astype(o_ref.dtype)

    grid_spec = pltpu.PrefetchScalarGridSpec(
        num_scalar_prefetch=1, grid=(N_CHIP, R // tr),
        in_specs=[pl.BlockSpec((None, tr, C), lambda j, i, core_ref: (2 * j + core_ref[0], i, 0)),
                  pl.BlockSpec((None, tr, C), lambda j, i, core_ref: (j, i, 0))],
        out_specs=pl.BlockSpec((None, tr, C), lambda j, i, core_ref: (j, i, 0)))
    return pl.pallas_call(
        body, out_shape=jax.ShapeDtypeStruct((N_CHIP, R, C), own.dtype), grid_spec=grid_spec,
        compiler_params=_params(2), name=name)(core, own, got)


def _place():
    return lax.axis_index("x"), lax.axis_index("y"), lax.axis_index("c")


def _other_chips(x, y):
    return [(1 - x, y), (x, 1 - y), (1 - x, 1 - y)]


def gather_small(block, name):
    m_per, n = block.shape

    def body(x_ref, out_ref, send_sems, recv_sems, local_sem):
        x, y, c = _place()
        me, sibling = (x, y, c), (x, y, 1 - c)
        chips = _other_chips(x, y)

        def rows(px, py, pc):
            return out_ref.at[pl.ds((4 * px + 2 * py + pc) * m_per, m_per), :]

        def copy(k, blk, to, src=None):
            return pltpu.make_async_remote_copy(
                src_ref=rows(*blk) if src is None else src, dst_ref=rows(*blk),
                send_sem=send_sems.at[k], recv_sem=recv_sems.at[k], device_id=to, device_id_type=MESH)

        mine = pltpu.make_async_copy(x_ref, rows(*me), local_sem)
        mine.start()
        first = [copy(0, me, sibling, src=x_ref)]
        first += [copy(1 + j, me, (*chip, c), src=x_ref) for j, chip in enumerate(chips)]
        for cp in first:
            cp.start()
        passed = [copy(4 + j, (*chip, c), sibling) for j, chip in enumerate(chips)]
        for j, chip in enumerate(chips):
            copy(1 + j, (*chip, c), me).wait_recv()
            passed[j].start()
        copy(0, sibling, me).wait_recv()
        for j, chip in enumerate(chips):
            copy(4 + j, (*chip, 1 - c), me).wait_recv()
        for cp in first + passed:
            cp.wait_send()
        mine.wait()

    return pl.pallas_call(
        body, out_shape=jax.ShapeDtypeStruct((N_DEV * m_per, n), block.dtype),
        in_specs=[pl.BlockSpec(memory_space=pltpu.VMEM)], out_specs=pl.BlockSpec(memory_space=pltpu.VMEM),
        scratch_shapes=[pltpu.SemaphoreType.DMA((7,)), pltpu.SemaphoreType.DMA((7,)), pltpu.SemaphoreType.DMA],
        compiler_params=pltpu.CompilerParams(vmem_limit_bytes=VMEM_LIMIT_BYTES), name=name)(block)


def gather_weight(shard, axis, name):
    L, k, n = shard.shape
    full = (L, N_DEV * k, n) if axis == 1 else (L, k, N_DEV * n)

    def body(x_ref, out_ref, send_sems, recv_sems, local_sem):
        x, y, c = _place()
        me, sibling = (x, y, c), (x, y, 1 - c)
        chips = _other_chips(x, y)

        def slab(px, py, pc):
            idx = 4 * px + 2 * py + pc
            if axis == 1:
                return out_ref.at[:, pl.ds(idx * k, k), :]
            return out_ref.at[:, :, pl.ds(idx * n, n)]

        def copy(kk, blk, to, src=None):
            return pltpu.make_async_remote_copy(
                src_ref=slab(*blk) if src is None else src, dst_ref=slab(*blk),
                send_sem=send_sems.at[kk], recv_sem=recv_sems.at[kk], device_id=to, device_id_type=MESH)

        mine = pltpu.make_async_copy(x_ref, slab(*me), local_sem)
        mine.start()
        first = [copy(0, me, sibling, src=x_ref)]
        first += [copy(1 + j, me, (*chip, c), src=x_ref) for j, chip in enumerate(chips)]
        for cp in first:
            cp.start()
        passed = [copy(4 + j, (*chip, c), sibling) for j, chip in enumerate(chips)]
        for j, chip in enumerate(chips):
            copy(1 + j, (*chip, c), me).wait_recv()
            passed[j].start()
        copy(0, sibling, me).wait_recv()
        for j, chip in enumerate(chips):
            copy(4 + j, (*chip, 1 - c), me).wait_recv()
        for cp in first + passed:
            cp.wait_send()
        mine.wait()

    return pl.pallas_call(
        body, out_shape=jax.ShapeDtypeStruct(full, shard.dtype),
        in_specs=[pl.BlockSpec(memory_space=pl.ANY)], out_specs=pl.BlockSpec(memory_space=pl.ANY),
        scratch_shapes=[pltpu.SemaphoreType.DMA((7,)), pltpu.SemaphoreType.DMA((7,)), pltpu.SemaphoreType.DMA],
        name=name)(shard)


def swap_with_sibling(slabs, name):
    _, R, C = slabs.shape

    def body(g_ref, got_ref, send_sems, recv_sems):
        x, y, c = _place()
        copies = [pltpu.make_async_remote_copy(
            src_ref=g_ref.at[2 * j + (1 - c)], dst_ref=got_ref.at[j], send_sem=send_sems.at[j],
            recv_sem=recv_sems.at[j], device_id=(x, y, 1 - c), device_id_type=MESH) for j in range(N_CHIP)]
        for cp in copies:
            cp.start()
        for cp in copies:
            cp.wait()

    return pl.pallas_call(
        body, out_shape=jax.ShapeDtypeStruct((N_CHIP, R, C), slabs.dtype),
        in_specs=[pl.BlockSpec(memory_space=pl.ANY)], out_specs=pl.BlockSpec(memory_space=pl.ANY),
        scratch_shapes=[pltpu.SemaphoreType.DMA((N_CHIP,)), pltpu.SemaphoreType.DMA((N_CHIP,))],
        name=name)(slabs)


def scatter_to_chips(sums, name):
    _, R, C = sums.shape

    def body(t_ref, got_ref, send_sems, recv_sems, local_sem):
        x, y, c = _place()
        mine = 2 * x + y
        own = pltpu.make_async_copy(t_ref.at[mine], got_ref.at[mine], local_sem)
        own.start()
        copies = []
        for j, (px, py) in enumerate(_other_chips(x, y)):
            copies.append(pltpu.make_async_remote_copy(
                src_ref=t_ref.at[2 * px + py], dst_ref=got_ref.at[mine], send_sem=send_sems.at[j],
                recv_sem=recv_sems.at[j], device_id=(px, py, c), device_id_type=MESH))
        for cp in copies:
            cp.start()
        for j, (px, py) in enumerate(_other_chips(x, y)):
            pltpu.make_async_remote_copy(
                src_ref=t_ref.at[mine], dst_ref=got_ref.at[2 * px + py], send_sem=send_sems.at[j],
                recv_sem=recv_sems.at[j], device_id=(px, py, c), device_id_type=MESH).wait_recv()
        for cp in copies:
            cp.wait_send()
        own.wait()

    return pl.pallas_call(
        body, out_shape=jax.ShapeDtypeStruct((N_CHIP, R, C), sums.dtype),
        in_specs=[pl.BlockSpec(memory_space=pl.ANY)], out_specs=pl.BlockSpec(memory_space=pl.ANY),
        scratch_shapes=[pltpu.SemaphoreType.DMA((3,)), pltpu.SemaphoreType.DMA((3,)), pltpu.SemaphoreType.DMA],
        name=name)(sums)


def _rope_tables(positions, d):
    inv_freq = ROPE_BASE ** (-jnp.arange(0, d, 2, dtype=F32) / d)
    ang = positions.astype(F32)[:, None] * inv_freq
    return jnp.cos(ang), jnp.sin(ang)


def _mla_pad(nope, rope):
    S, H, _ = nope.shape
    half = MLA_ROPE // 2
    z = jnp.zeros((S, H, LANES // 2 - half), nope.dtype)
    return jnp.concatenate([nope, rope[..., :half], z, rope[..., half:], z], axis=2).reshape(S, H * MLA_PAD)


def _mla_unpad(x, H):
    S = x.shape[0]
    half = MLA_ROPE // 2
    x3 = x.reshape(S, H, MLA_PAD)
    rope = jnp.concatenate([x3[..., MLA_NOPE:MLA_NOPE + half],
                            x3[..., MLA_NOPE + LANES // 2:MLA_NOPE + LANES // 2 + half]], axis=2)
    return x3[..., :MLA_NOPE], rope


def _to_slabs(g, axis):
    L, K, N = g.shape
    if axis == 1:
        k = K // N_DEV
        return g.reshape(L, N_DEV, k, N).transpose(1, 0, 2, 3).reshape(N_DEV, L * k, N)
    n = N // N_DEV
    return g.reshape(L, K, N_DEV, n).transpose(2, 0, 1, 3).reshape(N_DEV, L * K, n)


def kernel(x, mem, positions, ret_w_in, ret_gn_g, ret_w_out, mla_w_in, mla_q_norm_g, mla_w_uq, mla_kv_norm_g, mla_w_ukv, mla_w_out, xa_w_q, xa_w_kv, xa_w_out, ffn_w_up, ffn_conv_w, ffn_conv_b, ffn_w_down, ln_mix_g, ln_mix_b, ln_mem_g, ln_mem_b, ln_ffn_g, ln_ffn_b, loss_target, m_ret_w_in, m_ret_gn_g, m_ret_w_out, m_mla_w_in, m_mla_q_norm_g, m_mla_w_uq, m_mla_kv_norm_g, m_mla_w_ukv, m_mla_w_out, m_xa_w_q, m_xa_w_kv, m_xa_w_out, m_ffn_w_up, m_ffn_conv_w, m_ffn_conv_b, m_ffn_w_down, m_ln_mix_g, m_ln_mix_b, m_ln_mem_g, m_ln_mem_b, m_ln_ffn_g, m_ln_ffn_b, v_ret_w_in, v_ret_gn_g, v_ret_w_out, v_mla_w_in, v_mla_q_norm_g, v_mla_w_uq, v_mla_kv_norm_g, v_mla_w_ukv, v_mla_w_out, v_xa_w_q, v_xa_w_kv, v_xa_w_out, v_ffn_w_up, v_ffn_conv_w, v_ffn_conv_b, v_ffn_w_down, v_ln_mix_g, v_ln_mix_b, v_ln_mem_g, v_ln_mem_b, v_ln_ffn_g, v_ln_ffn_b):
    weights = dict(ret_w_in=ret_w_in, ret_gn_g=ret_gn_g, ret_w_out=ret_w_out, mla_w_in=mla_w_in,
                   mla_q_norm_g=mla_q_norm_g, mla_w_uq=mla_w_uq, mla_kv_norm_g=mla_kv_norm_g,
                   mla_w_ukv=mla_w_ukv, mla_w_out=mla_w_out, xa_w_q=xa_w_q, xa_w_kv=xa_w_kv, xa_w_out=xa_w_out,
                   ffn_w_up=ffn_w_up, ffn_conv_w=ffn_conv_w, ffn_conv_b=ffn_conv_b, ffn_w_down=ffn_w_down,
                   ln_mix_g=ln_mix_g, ln_mix_b=ln_mix_b, ln_mem_g=ln_mem_g, ln_mem_b=ln_mem_b,
                   ln_ffn_g=ln_ffn_g, ln_ffn_b=ln_ffn_b)
    mom_m = dict(ret_w_in=m_ret_w_in, ret_gn_g=m_ret_gn_g, ret_w_out=m_ret_w_out, mla_w_in=m_mla_w_in,
                 mla_q_norm_g=m_mla_q_norm_g, mla_w_uq=m_mla_w_uq, mla_kv_norm_g=m_mla_kv_norm_g,
                 mla_w_ukv=m_mla_w_ukv, mla_w_out=m_mla_w_out, xa_w_q=m_xa_w_q, xa_w_kv=m_xa_w_kv,
                 xa_w_out=m_xa_w_out, ffn_w_up=m_ffn_w_up, ffn_conv_w=m_ffn_conv_w, ffn_conv_b=m_ffn_conv_b,
                 ffn_w_down=m_ffn_w_down, ln_mix_g=m_ln_mix_g, ln_mix_b=m_ln_mix_b, ln_mem_g=m_ln_mem_g,
                 ln_mem_b=m_ln_mem_b, ln_ffn_g=m_ln_ffn_g, ln_ffn_b=m_ln_ffn_b)
    mom_v = dict(ret_w_in=v_ret_w_in, ret_gn_g=v_ret_gn_g, ret_w_out=v_ret_w_out, mla_w_in=v_mla_w_in,
                 mla_q_norm_g=v_mla_q_norm_g, mla_w_uq=v_mla_w_uq, mla_kv_norm_g=v_mla_kv_norm_g,
                 mla_w_ukv=v_mla_w_ukv, mla_w_out=v_mla_w_out, xa_w_q=v_xa_w_q, xa_w_kv=v_xa_w_kv,
                 xa_w_out=v_xa_w_out, ffn_w_up=v_ffn_w_up, ffn_conv_w=v_ffn_conv_w, ffn_conv_b=v_ffn_conv_b,
                 ffn_w_down=v_ffn_w_down, ln_mix_g=v_ln_mix_g, ln_mix_b=v_ln_mix_b, ln_mem_g=v_ln_mem_g,
                 ln_mem_b=v_ln_mem_b, ln_ffn_g=v_ln_ffn_g, ln_ffn_b=v_ln_ffn_b)
    order = list(weights)
    BIG = dict(ret_w_in=2, ret_w_out=1, mla_w_in=1, mla_w_uq=2, mla_w_ukv=2, mla_w_out=1,
               xa_w_q=1, xa_w_kv=2, xa_w_out=1, ffn_w_up=2, ffn_w_down=1)
    SMALL_CUT = ("ffn_conv_w", "mla_q_norm_g", "mla_kv_norm_g")
    REPLICATED = ("ret_gn_g", "ffn_conv_b", "ln_mix_g", "ln_mix_b", "ln_mem_g", "ln_mem_b", "ln_ffn_g", "ln_ffn_b")

    x = x[0]
    mem = mem[0]
    positions = positions[0]
    target = loss_target[0]
    S, D = x.shape
    depth = ln_mix_g.shape[0]
    alpha = (2 * depth) ** 0.25
    ret_dk = D // RET_HEADS
    ret_dv = 2 * D // RET_HEADS
    ret_qkw = RET_HEADS * ret_dk
    ret_vw = RET_HEADS * ret_dv
    xa_d = D // XA_HEADS
    assert 2 * ret_qkw == ret_vw and MLA_NOPE == LANES and MLA_V == LANES and MLA_ROPE == LANES // 2
    assert MLA_Q_RANK == MLA_KV_RANK and CHUNK & (CHUNK - 1) == 0
    core = lax.axis_index("c").astype(jnp.int32).reshape(1)
    dev = 4 * lax.axis_index("x") + 2 * lax.axis_index("y") + lax.axis_index("c")

    W = {n: gather_weight(weights[n].astype(PAY_DTYPE), axis, name="gather_" + n) for n, axis in BIG.items()}

    def pack(arrays):
        flat = jnp.concatenate([a.reshape(-1) for a in arrays])
        rows = -(-flat.shape[0] // (8 * LANES)) * 8
        return jnp.pad(flat, (0, rows * LANES - flat.shape[0])).reshape(rows, LANES)

    def unpack(flat, like):
        out, at = [], 0
        for a in like:
            out.append(flat[at:at + a.size].reshape(a.shape))
            at += a.size
        return out

    small_local = [weights[n] for n in SMALL_CUT]
    blk = pack(small_local)
    allsmall = gather_small(blk, name="gather_small_weights").reshape(N_DEV, -1)
    per_dev = [unpack(allsmall[d], small_local) for d in range(N_DEV)]
    conv_w_full, qg_full, kvg_full = [jnp.concatenate([per_dev[d][i] for d in range(N_DEV)], axis=-1)
                                      for i in range(len(SMALL_CUT))]

    rcos, rsin = _rope_tables(positions, ret_dk)
    mcos, msin = _rope_tables(positions, MLA_ROPE)
    zq = jnp.zeros((S, LANES // 2 - MLA_ROPE // 2), F32)
    mla_ta = jnp.concatenate([mcos, zq, mcos, zq], axis=1)
    mla_tb = jnp.concatenate([-msin, zq, msin, zq], axis=1)
    log_gamma = jnp.log(1.0 - 2.0 ** (-5.0 - jnp.arange(RET_HEADS, dtype=F32)))
    log_gamma = jnp.broadcast_to(log_gamma[:, None, None], (RET_HEADS, 8, LANES))
    mla_scale = (MLA_NOPE + MLA_ROPE) ** -0.5
    xa_scale = xa_d ** -0.5
    mem_b = mem.astype(MXU_DTYPE)

    def vec(a, l):
        return a[l][None, :]

    saved = []
    h = x
    hb = x.astype(MXU_DTYPE)
    for layer in range(depth):
        j = layer // 2
        sv = {}
        sv["hb_mix"] = hb
        if layer % 2 == 0:
            proj = mm(hb, W["ret_w_in"], b_layer=j, name="ret_in")
            qk = ret_rope_fwd(proj, rcos, rsin, 2 * RET_HEADS, ret_dk // 2, RET_HEADS, ret_dk ** -0.5, name="ret_rope")
            (o,) = attn_fwd((qk, ret_dk, 0, 1), (qk, ret_dk, RET_HEADS, 1), (proj, ret_dv, RET_HEADS, 1),
                            heads=RET_HEADS, softmax=False, masked=True, log_gamma=log_gamma, name="ret_attn")
            yb = ret_gate_fwd(o, proj, 2, vec(ret_gn_g, j), RET_HEADS, name="ret_gate")
            mix = mm(yb, W["ret_w_out"], b_layer=j, name="ret_out")
            sv.update(proj=proj, qk=qk, o=o, yb=yb)
        else:
            proj = mm(hb, W["mla_w_in"], b_layer=j, name="mla_in")
            cq = rms_fwd(proj, 0, MLA_Q_RANK, vec(qg_full, j), name="mla_q_norm")
            ckv = rms_fwd(proj, 1, MLA_KV_RANK, vec(kvg_full, j), name="mla_kv_norm")
            qf = mm(cq, W["mla_w_uq"], b_layer=j, name="mla_uq")
            kvf = mm(ckv, W["mla_w_ukv"], b_layer=j, name="mla_ukv")
            q3 = qf.reshape(S, MLA_HEADS, MLA_NOPE + MLA_ROPE)
            kv3 = kvf.reshape(S, MLA_HEADS, MLA_NOPE + MLA_V)
            k_rope = jnp.broadcast_to(proj[:, None, MLA_Q_RANK + MLA_KV_RANK:], (S, MLA_HEADS, MLA_ROPE))
            q_pad = _mla_pad(q3[..., :MLA_NOPE], q3[..., MLA_NOPE:])
            k_pad = _mla_pad(kv3[..., :MLA_NOPE], k_rope)
            (qr,) = mla_rope(q_pad, mla_ta, mla_tb, MLA_HEADS, backward=False, head_sum=False,
                             out_dtype=MXU_DTYPE, name="mla_rope_q")
            (kr,) = mla_rope(k_pad, mla_ta, mla_tb, MLA_HEADS, backward=False, head_sum=False,
                             out_dtype=MXU_DTYPE, name="mla_rope_k")
            o, lse = attn_fwd((qr, MLA_PAD, 0, 1), (kr, MLA_PAD, 0, 1), (kvf, MLA_V, 1, 2), heads=MLA_HEADS,
                              softmax=True, masked=True, scale=mla_scale, name="mla_attn")
            ob = o.astype(MXU_DTYPE)
            mix = mm(ob, W["mla_w_out"], b_layer=j, name="mla_out")
            sv.update(proj=proj, cq=cq, ckv=ckv, kvf=kvf, qr=qr, kr=kr, o=o, ob=ob, lse=lse)
        h, hb, sv["xhat_mix"], sv["rstd_mix"] = ln_fwd(h, mix, vec(ln_mix_g, layer), vec(ln_mix_b, layer), alpha,
                                                       name="ln_mix")
        sv["hb_mem"] = hb
        q = mm(hb, W["xa_w_q"], b_layer=layer, name="xa_q")
        kvm = mm(mem_b, W["xa_w_kv"], b_layer=layer, name="xa_kv")
        o, lse = attn_fwd((q, xa_d, 0, 1), (kvm, xa_d, 0, 1), (kvm, xa_d, XA_HEADS, 1), heads=XA_HEADS,
                          softmax=True, masked=False, scale=xa_scale, name="xa_attn")
        ob = o.astype(MXU_DTYPE)
        mix = mm(ob, W["xa_w_out"], b_layer=layer, name="xa_out")
        sv.update(xa_q=q, xa_kvm=kvm, xa_o=o, xa_ob=ob, xa_lse=lse)
        h, hb, sv["xhat_mem"], sv["rstd_mem"] = ln_fwd(h, mix, vec(ln_mem_g, layer), vec(ln_mem_b, layer), alpha,
                                                       name="ln_mem")
        sv["hb_ffn"] = hb
        hup = mm(hb, W["ffn_w_up"], b_layer=layer, name="ffn_up")
        u = conv_glu_fwd(hup, conv_w_full[layer], vec(ffn_conv_b, layer), name="ffn_conv")
        mix = mm(u, W["ffn_w_down"], b_layer=layer, name="ffn_down")
        sv.update(hup=hup, u=u)
        h, hb, sv["xhat_ffn"], sv["rstd_ffn"] = ln_fwd(h, mix, vec(ln_ffn_g, layer), vec(ln_ffn_b, layer), alpha,
                                                       name="ln_ffn")
        saved.append(sv)

    dh, loss_blk = loss_head(h, target, name="loss_head")
    loss = lax.psum(loss_blk[0, 0], ("x", "y", "c"))

    G = {n: lax.empty(W[n].shape, PAY_DTYPE) for n in BIG}
    small = {n: [None] * weights[n].shape[0] for n in REPLICATED + SMALL_CUT}

    def wgrad(name_, a, d, l, tag):
        G[name_] = mm(a, d, mode="tn", out_dtype=PAY_DTYPE, stack=G[name_], stack_layer=l, name=tag)

    for layer in reversed(range(depth)):
        j = layer // 2
        sv = saved[layer]
        dz, dzb, dg, db = ln_bwd(dh, sv["xhat_ffn"], sv["rstd_ffn"], vec(ln_ffn_g, layer), name="ln_ffn_bwd")
        small["ln_ffn_g"][layer], small["ln_ffn_b"][layer] = dg[0], db[0]
        wgrad("ffn_w_down", sv["u"], dzb, layer, "ffn_down_dw")
        du = mm(dzb, W["ffn_w_down"], mode="nt", b_layer=layer, name="ffn_down_dx")
        dhg, dhv, dwg, dwv, dbg, dbv = conv_glu_bwd(du, sv["hup"], conv_w_full[layer], vec(ffn_conv_b, layer),
                                                    name="ffn_conv_bwd")
        small["ffn_conv_w"][layer] = jnp.concatenate([dwg, dwv], axis=1)
        small["ffn_conv_b"][layer] = jnp.concatenate([dbg, dbv], axis=1)[0]
        dhup = jnp.concatenate([dhg, dhv], axis=1)
        wgrad("ffn_w_up", sv["hb_ffn"], dhup, layer, "ffn_up_dw")
        dh = mm(dhup, W["ffn_w_up"], mode="nt", b_layer=layer, add=dz, add_scale=alpha, name="ffn_up_dx")
        dz, dzb, dg, db = ln_bwd(dh, sv["xhat_mem"], sv["rstd_mem"], vec(ln_mem_g, layer), name="ln_mem_bwd")
        small["ln_mem_g"][layer], small["ln_mem_b"][layer] = dg[0], db[0]
        wgrad("xa_w_out", sv["xa_ob"], dzb, layer, "xa_out_dw")
        do = mm(dzb, W["xa_w_out"], mode="nt", b_layer=layer, name="xa_out_dx")
        dq, dk, dv = attn_bwd((sv["xa_q"], xa_d, 0, 1), (sv["xa_kvm"], xa_d, 0, 1), (sv["xa_kvm"], xa_d, XA_HEADS, 1),
                              do, heads=XA_HEADS, softmax=True, masked=False, scale=xa_scale, o=sv["xa_o"],
                              lse=sv["xa_lse"], name="xa_attn_bwd")
        dqb = dq.astype(MXU_DTYPE)
        wgrad("xa_w_kv", mem_b, jnp.concatenate([dk, dv], axis=1).astype(MXU_DTYPE), layer, "xa_kv_dw")
        wgrad("xa_w_q", sv["hb_mem"], dqb, layer, "xa_q_dw")
        dh = mm(dqb, W["xa_w_q"], mode="nt", b_layer=layer, add=dz, add_scale=alpha, name="xa_q_dx")
        dz, dzb, dg, db = ln_bwd(dh, sv["xhat_mix"], sv["rstd_mix"], vec(ln_mix_g, layer), name="ln_mix_bwd")
        small["ln_mix_g"][layer], small["ln_mix_b"][layer] = dg[0], db[0]
        if layer % 2 == 0:
            wgrad("ret_w_out", sv["yb"], dzb, j, "ret_out_dw")
            dy = mm(dzb, W["ret_w_out"], mode="nt", b_layer=j, name="ret_out_dx")
            do, dgate, dgn = ret_gate_bwd(dy, sv["o"], sv["proj"], 2, vec(ret_gn_g, j), RET_HEADS, name="ret_gate_bwd")
            small["ret_gn_g"][j] = dgn[0]
            dq, dk, dv = attn_bwd((sv["qk"], ret_dk, 0, 1), (sv["qk"], ret_dk, RET_HEADS, 1),
                                  (sv["proj"], ret_dv, RET_HEADS, 1), do, heads=RET_HEADS, softmax=False,
                                  masked=True, log_gamma=log_gamma, name="ret_attn_bwd")
            dqk = ret_rope_bwd(dq, dk, rcos, rsin, ret_dk // 2, ret_dk ** -0.5, name="ret_rope_bwd")
            dproj = jnp.concatenate([dqk, dv.astype(MXU_DTYPE), dgate], axis=1)
            wgrad("ret_w_in", sv["hb_mix"], dproj, j, "ret_in_dw")
            dh = mm(dproj, W["ret_w_in"], mode="nt", b_layer=j, add=dz, add_scale=alpha, name="ret_in_dx")
        else:
            wgrad("mla_w_out", sv["ob"], dzb, j, "mla_out_dw")
            do = mm(dzb, W["mla_w_out"], mode="nt", b_layer=j, name="mla_out_dx")
            dq, dk, dv = attn_bwd((sv["qr"], MLA_PAD, 0, 1), (sv["kr"], MLA_PAD, 0, 1), (sv["kvf"], MLA_V, 1, 2), do,
                                  heads=MLA_HEADS, softmax=True, masked=True, scale=mla_scale, o=sv["o"],
                                  lse=sv["lse"], name="mla_attn_bwd")
            (dq_un,) = mla_rope(dq, mla_ta, mla_tb, MLA_HEADS, backward=True, head_sum=False, out_dtype=F32,
                                name="mla_rope_q_bwd")
            dk_un, dk_rope_sum = mla_rope(dk, mla_ta, mla_tb, MLA_HEADS, backward=True, head_sum=True, out_dtype=F32,
                                          name="mla_rope_k_bwd")
            dq_nope, dq_rope = _mla_unpad(dq_un, MLA_HEADS)
            dqf = jnp.concatenate([dq_nope, dq_rope], axis=2).reshape(S, -1).astype(MXU_DTYPE)
            dk_nope, _ = _mla_unpad(dk_un, MLA_HEADS)
            dkvf = jnp.concatenate([dk_nope, dv.reshape(S, MLA_HEADS, MLA_V)], axis=2).reshape(S, -1).astype(MXU_DTYPE)
            half = MLA_ROPE // 2
            dk_rope = jnp.concatenate([dk_rope_sum[:, :half], dk_rope_sum[:, LANES // 2:LANES // 2 + half]], axis=1)
            wgrad("mla_w_uq", sv["cq"], dqf, j, "mla_uq_dw")
            dcq = mm(dqf, W["mla_w_uq"], mode="nt", b_layer=j, name="mla_uq_dx")
            wgrad("mla_w_ukv", sv["ckv"], dkvf, j, "mla_ukv_dw")
            dckv = mm(dkvf, W["mla_w_ukv"], mode="nt", b_layer=j, name="mla_ukv_dx")
            dcq_in, dqg = rms_bwd(dcq, sv["proj"], 0, MLA_Q_RANK, vec(qg_full, j), name="mla_q_norm_bwd")
            dckv_in, dkvg = rms_bwd(dckv, sv["proj"], 1, MLA_KV_RANK, vec(kvg_full, j), name="mla_kv_norm_bwd")
            small["mla_q_norm_g"][j], small["mla_kv_norm_g"][j] = dqg[0], dkvg[0]
            dproj = jnp.concatenate([dcq_in, dckv_in, dk_rope], axis=1).astype(MXU_DTYPE)
            wgrad("mla_w_in", sv["hb_mix"], dproj, j, "mla_in_dw")
            dh = mm(dproj, W["mla_w_in"], mode="nt", b_layer=j, add=dz, add_scale=alpha, name="mla_in_dx")
    grad_x = dh[None]

    grads, deltas, new_m, new_v = {}, {}, {}, {}
    for n, axis in BIG.items():
        slabs = _to_slabs(G[n], axis)
        got = swap_with_sibling(slabs, name="swap_" + n)
        sums = pair_sum(slabs, got, core, name="pairsum_" + n)
        parts = scatter_to_chips(sums, name="scatter_" + n)
        shp = weights[n].shape
        two_d = (shp[0] * shp[1], shp[2])
        outs = adam_update(parts, weights[n].reshape(two_d), mom_m[n].reshape(two_d), mom_v[n].reshape(two_d),
                           name="adam_" + n)
        grads[n], deltas[n], new_m[n], new_v[n] = [o_.reshape(shp) for o_ in outs]

    small_names = list(REPLICATED + SMALL_CUT)
    partial = [jnp.stack(small[n]) for n in small_names]
    allpart = gather_small(pack(partial), name="gather_small_grads")
    rows = allpart.shape[0] // N_DEV
    allpart = allpart.reshape(N_DEV, rows, LANES)

    rep_names = list(REPLICATED)
    rep_w = pack([weights[n] for n in rep_names])
    rep_m = pack([mom_m[n] for n in rep_names])
    rep_v = pack([mom_v[n] for n in rep_names])
    rep_rows = rep_w.shape[0]
    rep_size = sum(weights[n].size for n in rep_names)
    flat_parts = allpart.reshape(N_DEV, rows * LANES)
    rep_parts = jnp.pad(flat_parts[:, :rep_size], ((0, 0), (0, rep_rows * LANES - rep_size)))
    outs = adam_update(rep_parts.reshape(N_DEV, rep_rows, LANES), rep_w, rep_m, rep_v, name="adam_replicated")
    for o_, dst in zip(outs, (grads, deltas, new_m, new_v)):
        for n, a in zip(rep_names, unpack(o_.reshape(-1), [weights[n] for n in rep_names])):
            dst[n] = a

    cut_names = list(SMALL_CUT)
    cut_full = [jnp.stack(small[n]) for n in cut_names]
    cut_parts = []
    at = rep_size
    for n, a in zip(cut_names, cut_full):
        whole = flat_parts[:, at:at + a.size].reshape((N_DEV,) + a.shape)
        at += a.size
        width = weights[n].shape[-1]
        start = (0,) * (whole.ndim - 1) + (dev * width,)
        mine = lax.dynamic_slice(whole, start, whole.shape[:-1] + (width,))
        cut_parts.append(mine.reshape(N_DEV, -1))
    cut_parts = jnp.concatenate(cut_parts, axis=1)
    cut_w = pack([weights[n] for n in cut_names])
    cut_m = pack([mom_m[n] for n in cut_names])
    cut_v = pack([mom_v[n] for n in cut_names])
    cut_rows = cut_w.shape[0]
    cut_parts = jnp.pad(cut_parts, ((0, 0), (0, cut_rows * LANES - cut_parts.shape[1])))
    outs = adam_update(cut_parts.reshape(N_DEV, cut_rows, LANES), cut_w, cut_m, cut_v, name="adam_small_cut")
    for o_, dst in zip(outs, (grads, deltas, new_m, new_v)):
        for n, a in zip(cut_names, unpack(o_.reshape(-1), [weights[n] for n in cut_names])):
            dst[n] = a

    return (loss, grad_x, *[grads[n] for n in order], *[deltas[n] for n in order],
            *[new_m[n] for n in order], *[new_v[n] for n in order])
```

```python
import functools
import math

import jax
import jax.numpy as jnp
from jax import lax
from jax.experimental import pallas as pl
from jax.experimental.pallas import tpu as pltpu

F32 = jnp.float32
MXU_DTYPE = jnp.bfloat16
PAY_DTYPE = jnp.bfloat16
MESH = pl.DeviceIdType.MESH
N_DEV = 8
N_CHIP = 4

DEPTH = 4
CHUNK = 64
RET_HEADS = 8
MLA_HEADS = 16
MLA_Q_RANK = 512
MLA_KV_RANK = 512
MLA_NOPE = 128
MLA_ROPE = 64
MLA_V = 128
MLA_PAD = 256
XA_HEADS = 4
ROPE_BASE = 10000.0
LN_EPS = 1e-5
RMS_EPS = 1e-6
NEG_INF = -1e30
ADAM_LR = 0.001
ADAM_B1 = 0.9
ADAM_B2 = 0.999
ADAM_EPS = 1e-08
ADAM_WD = 0.01
ADAM_STEP = 10

LANES = 128
VMEM_LIMIT_BYTES = 48 * 1024 * 1024
TILE_M = 1024
TILE_N = 1024
TILE_K = 2048
SLAB_TILE_N = 1536
ATTN_TILE = 512
ROW_TILE = 256

MXU_FLOPS_PER_US = 7.0e8
LINK_BYTES_PER_US = 2.2e4
CHUNK_BYTES = 768 * 1024
HOST_CHUNKS = 6
ATTN_US = {"ret": (125.0, 160.0), "mla": (320.0, 270.0), "xa": (35.0, 35.0)}


def _tile(n, cap, mult=LANES):
    best = None
    for t in range(mult, min(n, cap) + 1, mult):
        if n % t == 0:
            best = t
    return n if best is None else best


def _params(n_axes):
    return pltpu.CompilerParams(dimension_semantics=("arbitrary",) * n_axes,
                                vmem_limit_bytes=VMEM_LIMIT_BYTES)


def _place():
    return lax.axis_index("x"), lax.axis_index("y"), lax.axis_index("c")


def _other_chips(x, y):
    return [(1 - x, y), (x, 1 - y), (1 - x, 1 - y)]


def _remote(src, dst, send, recv, to):
    return pltpu.make_async_remote_copy(src_ref=src, dst_ref=dst, send_sem=send, recv_sem=recv,
                                        device_id=to, device_id_type=MESH)


class _Chunk:
    def __init__(self, kind, src, key, dst_shape, layer, r0, rc, cols, axis=None, k=None, n=None):
        self.kind, self.src, self.key, self.dst_shape = kind, src, key, dst_shape
        self.layer, self.r0, self.rc, self.axis, self.k, self.n = layer, r0, rc, axis, k, n
        self.nbytes = rc * cols * jnp.dtype(PAY_DTYPE).itemsize

    def copies(self, src_ref, dst_ref, send, recv, loc, i):
        x, y, c = _place()
        chips = _other_chips(x, y)
        rows = pl.ds(self.r0, self.rc)
        if self.kind == "gather":
            peers = [(x, y, 1 - c)] + [(px, py, c) for px, py in chips]
            src = src_ref.at[self.layer, rows, :]

            def slab(px, py, pc):
                idx = 4 * px + 2 * py + pc
                if self.axis == 1:
                    return dst_ref.at[pl.ds(idx * self.k + self.r0, self.rc), :]
                return dst_ref.at[rows, pl.ds(idx * self.n, self.n)]

            sends = [_remote(src, slab(x, y, c), send.at[4 * i + t], recv.at[4 * i + t], p)
                     for t, p in enumerate(peers)]
            recvs = [_remote(src, slab(*p), send.at[4 * i + t], recv.at[4 * i + t], p)
                     for t, p in enumerate(peers)]
            return sends, recvs, pltpu.make_async_copy(src, slab(x, y, c), loc.at[i])
        mine = 2 * x + y
        own = src_ref.at[mine, rows, :]
        sends = [_remote(src_ref.at[2 * px + py, rows, :], dst_ref.at[self.layer, mine, rows, :],
                         send.at[4 * i + t], recv.at[4 * i + t], (px, py, c)) for t, (px, py) in enumerate(chips)]
        recvs = [_remote(own, dst_ref.at[self.layer, 2 * px + py, rows, :],
                         send.at[4 * i + t], recv.at[4 * i + t], (px, py, c)) for t, (px, py) in enumerate(chips)]
        return sends, recvs, pltpu.make_async_copy(own, dst_ref.at[self.layer, mine, rows, :], loc.at[i])


def _call(body, *, grid, in_specs, out_specs, out_shape, operands, scratch=(), name, chunks=(), bufs=None):
    n_in, n_out, n_scr = len(operands), len(out_shape), len(scratch)
    params = _params(len(grid)) if grid else pltpu.CompilerParams(vmem_limit_bytes=VMEM_LIMIT_BYTES)
    if not chunks:
        return pl.pallas_call(
            body, out_shape=tuple(out_shape), grid=grid, in_specs=list(in_specs), out_specs=tuple(out_specs),
            scratch_shapes=list(scratch), compiler_params=params, name=name)(*operands)
    srcs, keys = [], []
    for ch in chunks:
        if not any(ch.src is s for s in srcs):
            srcs.append(ch.src)
        if ch.key not in keys:
            keys.append(ch.key)
    shape_of = {ch.key: ch.dst_shape for ch in chunks}
    held = [k for k in keys if bufs.get(k) is not None]
    extra = srcs + [bufs[k] for k in held]
    aliases = {n_in + len(srcs) + i: n_out + keys.index(k) for i, k in enumerate(held)}
    anywhere = pl.BlockSpec(memory_space=pl.ANY)
    n_extra, n_keys, n_ch = len(extra), len(keys), len(chunks)

    def hosted(*refs):
        src_refs = refs[n_in:n_in + len(srcs)]
        o0 = n_in + n_extra
        dst_refs = refs[o0 + n_out:o0 + n_out + n_keys]
        s0 = o0 + n_out + n_keys
        send, recv, loc = refs[s0 + n_scr:]
        sends, recvs, locs = [], [], []
        for i, ch in enumerate(chunks):
            src_ref = src_refs[[ch.src is s for s in srcs].index(True)]
            s_, r_, l_ = ch.copies(src_ref, dst_refs[keys.index(ch.key)], send, recv, loc, i)
            sends += s_
            recvs += r_
            locs.append(l_)

        def start():
            for cp in locs + sends:
                cp.start()

        def finish():
            for cp in recvs:
                cp.wait_recv()
            for cp in sends:
                cp.wait_send()
            for cp in locs:
                cp.wait()

        if grid:
            first = functools.reduce(jnp.logical_and, [pl.program_id(a) == 0 for a in range(len(grid))])
            last = functools.reduce(jnp.logical_and, [pl.program_id(a) == grid[a] - 1 for a in range(len(grid))])
            pl.when(first)(start)
            body(*refs[:n_in], *refs[o0:o0 + n_out], *refs[s0:s0 + n_scr])
            pl.when(last)(finish)
        else:
            start()
            body(*refs[:n_in], *refs[o0:o0 + n_out], *refs[s0:s0 + n_scr])
            finish()

    res = pl.pallas_call(
        hosted, out_shape=tuple(out_shape) + tuple(jax.ShapeDtypeStruct(shape_of[k], PAY_DTYPE) for k in keys),
        grid=grid, in_specs=list(in_specs) + [anywhere] * n_extra, out_specs=tuple(out_specs) + (anywhere,) * n_keys,
        scratch_shapes=list(scratch) + [pltpu.SemaphoreType.DMA((4 * n_ch,)), pltpu.SemaphoreType.DMA((4 * n_ch,)),
                                        pltpu.SemaphoreType.DMA((n_ch,))],
        input_output_aliases=aliases, compiler_params=params, name=name)(*operands, *extra)
    for k, arr in zip(keys, res[n_out:]):
        bufs[k] = arr
    return res[:n_out]


class _Link:
    def __init__(self):
        self.queue, self.bufs = [], {}

    def push(self, kind, src, key, dst_shape, layer, rows, cols, **geometry):
        fits = [r for r in range(16, rows + 1, 16) if rows % r == 0 and r * cols * 2 <= CHUNK_BYTES]
        rc = max(fits) if fits else min(r for r in range(16, rows + 1, 16) if rows % r == 0)
        for r0 in range(0, rows, rc):
            self.queue.append(_Chunk(kind, src, key, dst_shape, layer, r0, rc, cols, **geometry))

    def take(self, est_us):
        budget, out = est_us * LINK_BYTES_PER_US, []
        while self.queue and len(out) < HOST_CHUNKS and self.queue[0].nbytes <= 2 * budget:
            budget -= self.queue[0].nbytes
            out.append(self.queue.pop(0))
        return out

    def flush(self, key, name):
        if key is None:
            n = len(self.queue)
        else:
            n = max([i + 1 for i, ch in enumerate(self.queue) if ch.key == key], default=0)
        while n > 0:
            group, self.queue = self.queue[:min(n, HOST_CHUNKS)], self.queue[min(n, HOST_CHUNKS):]
            n -= len(group)
            _call(lambda: None, grid=(), in_specs=[], out_specs=(), out_shape=(), operands=[], name=name,
                  chunks=group, bufs=self.bufs)


def forward_to_sibling(w, axis, k, n, name):
    def body(w_in, w_ref, send, recv):
        x, y, c = _place()
        sibling = (x, y, 1 - c)

        def slab(px, py, pc):
            idx = 4 * px + 2 * py + pc
            return w_ref.at[pl.ds(idx * k, k), :] if axis == 1 else w_ref.at[:, pl.ds(idx * n, n)]

        chips = _other_chips(x, y)
        sends = [_remote(slab(px, py, c), slab(px, py, c), send.at[t], recv.at[t], sibling)
                 for t, (px, py) in enumerate(chips)]
        recvs = [_remote(slab(px, py, 1 - c), slab(px, py, 1 - c), send.at[t], recv.at[t], sibling)
                 for t, (px, py) in enumerate(chips)]
        for cp in sends:
            cp.start()
        for cp in recvs:
            cp.wait_recv()
        for cp in sends:
            cp.wait_send()

    return pl.pallas_call(
        body, out_shape=jax.ShapeDtypeStruct(w.shape, w.dtype),
        in_specs=[pl.BlockSpec(memory_space=pl.ANY)], out_specs=pl.BlockSpec(memory_space=pl.ANY),
        scratch_shapes=[pltpu.SemaphoreType.DMA((3,)), pltpu.SemaphoreType.DMA((3,))],
        input_output_aliases={0: 0}, name=name)(w)


def mm(a, b, *, mode="nn", out_dtype=F32, add=None, add_scale=1.0, slab_width=None, link=None, name):
    if mode == "nn":
        (M, K), N = a.shape, b.shape[-1]
    elif mode == "nt":
        (M, K), N = a.shape, b.shape[-2]
    else:
        (K, M), N = a.shape, b.shape[-1]
    tm, tk = _tile(M, TILE_M), _tile(K, TILE_K)
    tn = _tile(N, TILE_N) if slab_width is None else _tile(slab_width, SLAB_TILE_N)
    nk = K // tk
    dims = {"nn": (((1,), (0,)), ((), ())), "nt": (((1,), (1,)), ((), ())),
            "tn": (((0,), (0,)), ((), ()))}[mode]

    a_spec = (pl.BlockSpec((tk, tm), lambda i, j, k: (k, i)) if mode == "tn"
              else pl.BlockSpec((tm, tk), lambda i, j, k: (i, k)))
    b_spec = (pl.BlockSpec((tn, tk), lambda i, j, k: (j, k)) if mode == "nt"
              else pl.BlockSpec((tk, tn), lambda i, j, k: (k, j)))
    in_specs, operands = [a_spec, b_spec], [a, b]
    if add is not None:
        in_specs.append(pl.BlockSpec((tm, tn), lambda i, j, k: (i, j)))
        operands.append(add)
    if slab_width is None:
        out_shape = jax.ShapeDtypeStruct((M, N), out_dtype)
        out_spec = pl.BlockSpec((tm, tn), lambda i, j, k: (i, j))
    else:
        per = slab_width // tn
        out_shape = jax.ShapeDtypeStruct((N // slab_width, M, slab_width), out_dtype)
        out_spec = pl.BlockSpec((None, tm, tn), lambda i, j, k: (j // per, i, j % per))
    n_in = len(operands)

    def body(*refs):
        a_ref, b_ref = refs[0], refs[1]
        add_ref = refs[2] if add is not None else None
        o_ref = refs[n_in]
        k = pl.program_id(2)
        part = lax.dot_general(a_ref[...].astype(MXU_DTYPE), b_ref[...].astype(MXU_DTYPE),
                               dims, preferred_element_type=F32)

        def finish(r):
            if add_ref is not None:
                r = r + add_scale * add_ref[...].astype(F32)
            o_ref[...] = r.astype(o_ref.dtype)

        if nk == 1:
            finish(part)
        else:
            acc_ref = refs[n_in + 1]

            @pl.when(k == 0)
            def _():
                acc_ref[...] = part

            @pl.when(jnp.logical_and(k > 0, k < nk - 1))
            def _():
                acc_ref[...] += part

            @pl.when(k == nk - 1)
            def _():
                finish(acc_ref[...] + part)

    chunks = link.take(2.0 * M * N * K / MXU_FLOPS_PER_US) if link is not None else ()
    (out,) = _call(body, grid=(M // tm, N // tn, nk), in_specs=in_specs, out_specs=(out_spec,), out_shape=(out_shape,),
                   operands=operands, scratch=[pltpu.VMEM((tm, tn), F32)] if nk > 1 else [], name=name,
                   chunks=chunks, bufs=link.bufs if link is not None else None)
    return out


def ln_fwd(h, f, g, b, alpha, name):
    S, D = h.shape
    tm = _tile(S, ROW_TILE, 8)

    def body(h_ref, f_ref, g_ref, b_ref, y_ref, yb_ref, xhat_ref, rstd_ref):
        z = alpha * h_ref[...] + f_ref[...]
        mu = jnp.mean(z, axis=-1, keepdims=True)
        zc = z - mu
        var = jnp.mean(zc * zc, axis=-1, keepdims=True)
        rstd = lax.rsqrt(var + LN_EPS)
        xhat = zc * rstd
        y = xhat * g_ref[...] + b_ref[...]
        y_ref[...] = y
        yb_ref[...] = y.astype(yb_ref.dtype)
        xhat_ref[...] = xhat
        rstd_ref[...] = rstd

    row = pl.BlockSpec((tm, D), lambda i: (i, 0))
    vec = pl.BlockSpec((1, D), lambda i: (0, 0))
    return pl.pallas_call(
        body,
        out_shape=(jax.ShapeDtypeStruct((S, D), F32), jax.ShapeDtypeStruct((S, D), MXU_DTYPE),
                   jax.ShapeDtypeStruct((S, D), F32), jax.ShapeDtypeStruct((S, 1), F32)),
        grid=(S // tm,), in_specs=[row, row, vec, vec],
        out_specs=(row, row, row, pl.BlockSpec((tm, 1), lambda i: (i, 0))),
        compiler_params=_params(1), name=name)(h, f, g, b)


def ln_bwd(dy, xhat, rstd, g, name):
    S, D = dy.shape
    tm = _tile(S, ROW_TILE, 8)

    def body(dy_ref, xhat_ref, rstd_ref, g_ref, dz_ref, dzb_ref, dg_ref, db_ref):
        @pl.when(pl.program_id(0) == 0)
        def _():
            dg_ref[...] = jnp.zeros_like(dg_ref)
            db_ref[...] = jnp.zeros_like(db_ref)

        dy = dy_ref[...]
        xhat = xhat_ref[...]
        dxh = dy * g_ref[...]
        m1 = jnp.mean(dxh, axis=-1, keepdims=True)
        m2 = jnp.mean(dxh * xhat, axis=-1, keepdims=True)
        dz = rstd_ref[...] * (dxh - m1 - xhat * m2)
        dz_ref[...] = dz
        dzb_ref[...] = dz.astype(dzb_ref.dtype)
        dg_ref[...] += jnp.sum(dy * xhat, axis=0, keepdims=True)
        db_ref[...] += jnp.sum(dy, axis=0, keepdims=True)

    row = pl.BlockSpec((tm, D), lambda i: (i, 0))
    vec = pl.BlockSpec((1, D), lambda i: (0, 0))
    return pl.pallas_call(
        body,
        out_shape=(jax.ShapeDtypeStruct((S, D), F32), jax.ShapeDtypeStruct((S, D), MXU_DTYPE),
                   jax.ShapeDtypeStruct((1, D), F32), jax.ShapeDtypeStruct((1, D), F32)),
        grid=(S // tm,), in_specs=[row, row, pl.BlockSpec((tm, 1), lambda i: (i, 0)), vec],
        out_specs=(row, row, vec, vec), compiler_params=_params(1), name=name)(dy, xhat, rstd, g)


def rms_fwd(x, col, width, g, name):
    S = x.shape[0]
    tm = _tile(S, ROW_TILE, 8)

    def body(x_ref, g_ref, y_ref):
        xv = x_ref[...]
        r = lax.rsqrt(jnp.mean(xv * xv, axis=-1, keepdims=True) + RMS_EPS)
        y_ref[...] = (xv * r * g_ref[...]).astype(y_ref.dtype)

    return pl.pallas_call(
        body, out_shape=jax.ShapeDtypeStruct((S, width), MXU_DTYPE), grid=(S // tm,),
        in_specs=[pl.BlockSpec((tm, width), lambda i: (i, col)), pl.BlockSpec((1, width), lambda i: (0, 0))],
        out_specs=pl.BlockSpec((tm, width), lambda i: (i, 0)), compiler_params=_params(1), name=name)(x, g)


def rms_bwd(dy, x, col, width, g, name):
    S = x.shape[0]
    tm = _tile(S, ROW_TILE, 8)

    def body(dy_ref, x_ref, g_ref, dx_ref, dg_ref):
        @pl.when(pl.program_id(0) == 0)
        def _():
            dg_ref[...] = jnp.zeros_like(dg_ref)

        xv = x_ref[...]
        dy = dy_ref[...]
        r = lax.rsqrt(jnp.mean(xv * xv, axis=-1, keepdims=True) + RMS_EPS)
        dxn = dy * g_ref[...]
        m = jnp.mean(dxn * xv, axis=-1, keepdims=True)
        dx_ref[...] = r * (dxn - xv * (r * r * m))
        dg_ref[...] += jnp.sum(dy * xv * r, axis=0, keepdims=True)

    return pl.pallas_call(
        body, out_shape=(jax.ShapeDtypeStruct((S, width), F32), jax.ShapeDtypeStruct((1, width), F32)),
        grid=(S // tm,),
        in_specs=[pl.BlockSpec((tm, width), lambda i: (i, 0)), pl.BlockSpec((tm, width), lambda i: (i, col)),
                  pl.BlockSpec((1, width), lambda i: (0, 0))],
        out_specs=(pl.BlockSpec((tm, width), lambda i: (i, 0)), pl.BlockSpec((1, width), lambda i: (0, 0))),
        compiler_params=_params(1), name=name)(dy, x, g)


def ret_rope_fwd(proj, cos, sin, n_groups, half, k_from, k_scale, name):
    S = proj.shape[0]
    W = n_groups * 2 * half
    tm = _tile(S, ROW_TILE, 8)

    def body(x_ref, c_ref, s_ref, o_ref):
        c, s = c_ref[...], s_ref[...]
        for gi in range(n_groups):
            lo = gi * 2 * half
            x1 = x_ref[:, lo:lo + half]
            x2 = x_ref[:, lo + half:lo + 2 * half]
            sc = k_scale if gi >= k_from else 1.0
            o_ref[:, lo:lo + half] = ((x1 * c - x2 * s) * sc).astype(o_ref.dtype)
            o_ref[:, lo + half:lo + 2 * half] = ((x2 * c + x1 * s) * sc).astype(o_ref.dtype)

    tab = pl.BlockSpec((tm, half), lambda i: (i, 0))
    return pl.pallas_call(
        body, out_shape=jax.ShapeDtypeStruct((S, W), MXU_DTYPE), grid=(S // tm,),
        in_specs=[pl.BlockSpec((tm, W), lambda i: (i, 0)), tab, tab],
        out_specs=pl.BlockSpec((tm, W), lambda i: (i, 0)), compiler_params=_params(1), name=name)(proj, cos, sin)


def ret_rope_bwd(dq, dk, cos, sin, half, k_scale, name):
    S, Wq = dq.shape
    n_heads = Wq // (2 * half)
    tm = _tile(S, ROW_TILE, 8)

    def body(dq_ref, dk_ref, c_ref, s_ref, o_ref):
        c, s = c_ref[...], s_ref[...]
        for part, (d_ref, sc) in enumerate(((dq_ref, 1.0), (dk_ref, k_scale))):
            for hi in range(n_heads):
                lo = hi * 2 * half
                d1 = d_ref[:, lo:lo + half]
                d2 = d_ref[:, lo + half:lo + 2 * half]
                base = part * Wq + lo
                o_ref[:, base:base + half] = ((d1 * c + d2 * s) * sc).astype(o_ref.dtype)
                o_ref[:, base + half:base + 2 * half] = ((d2 * c - d1 * s) * sc).astype(o_ref.dtype)

    tab = pl.BlockSpec((tm, half), lambda i: (i, 0))
    row = pl.BlockSpec((tm, Wq), lambda i: (i, 0))
    return pl.pallas_call(
        body, out_shape=jax.ShapeDtypeStruct((S, 2 * Wq), MXU_DTYPE), grid=(S // tm,),
        in_specs=[row, row, tab, tab], out_specs=pl.BlockSpec((tm, 2 * Wq), lambda i: (i, 0)),
        compiler_params=_params(1), name=name)(dq, dk, cos, sin)


def mla_rope(x, ta, tb, n_heads, *, backward, head_sum, out_dtype, name):
    S = x.shape[0]
    W = n_heads * MLA_PAD
    tm = _tile(S, ROW_TILE, 8)

    def body(x_ref, a_ref, b_ref, o_ref, *rest):
        a, b = a_ref[...], b_ref[...]
        total = jnp.zeros((tm, LANES), F32)
        for hi in range(n_heads):
            lo = hi * MLA_PAD
            o_ref[:, lo:lo + MLA_NOPE] = x_ref[:, lo:lo + MLA_NOPE].astype(o_ref.dtype)
            t = x_ref[:, lo + MLA_NOPE:lo + MLA_PAD].astype(F32)
            if backward:
                r = t * a + pltpu.roll(t * b, LANES // 2, 1)
            else:
                r = t * a + pltpu.roll(t, LANES // 2, 1) * b
            o_ref[:, lo + MLA_NOPE:lo + MLA_PAD] = r.astype(o_ref.dtype)
            total = total + r
        if head_sum:
            rest[0][...] = total

    tab = pl.BlockSpec((tm, LANES), lambda i: (i, 0))
    row = pl.BlockSpec((tm, W), lambda i: (i, 0))
    out_shape = [jax.ShapeDtypeStruct((S, W), out_dtype)]
    out_specs = [row]
    if head_sum:
        out_shape.append(jax.ShapeDtypeStruct((S, LANES), F32))
        out_specs.append(tab)
    return pl.pallas_call(
        body, out_shape=tuple(out_shape), grid=(S // tm,), in_specs=[row, tab, tab],
        out_specs=tuple(out_specs), compiler_params=_params(1), name=name)(x, ta, tb)


def _visible(qi, kj, tq, tk):
    n = qi * tq + lax.broadcasted_iota(jnp.int32, (tq, tk), 0)
    m = kj * tk + lax.broadcasted_iota(jnp.int32, (tq, tk), 1)
    shift = CHUNK.bit_length() - 1
    vis = lax.shift_right_logical(m, shift) <= lax.shift_right_logical(n, shift)
    return vis, jnp.abs(n - m).astype(F32)


def attn_fwd(q, k, v, *, heads, softmax, masked, scale=1.0, log_gamma=None, link=None, est_us=0.0, name):
    (qa, dqk, q0, qs), (ka, _, k0, ks), (va, dv, v0, vs) = q, k, v
    Sq, Sk = qa.shape[0], ka.shape[0]
    tq, tk = _tile(Sq, ATTN_TILE, 8), _tile(Sk, ATTN_TILE, 8)
    nq, nk = Sq // tq, Sk // tk
    assert not masked or tq == tk

    def body(*refs):
        if softmax:
            q_ref, k_ref, v_ref, o_ref, lse_ref, m_ref, l_ref, acc_ref = refs
        else:
            q_ref, k_ref, v_ref, lg_ref, o_ref, acc_ref = refs
        qi, kj = pl.program_id(1), pl.program_id(2)

        @pl.when(kj == 0)
        def _():
            acc_ref[...] = jnp.zeros_like(acc_ref)
            if softmax:
                m_ref[...] = jnp.full_like(m_ref, NEG_INF)
                l_ref[...] = jnp.zeros_like(l_ref)

        def step(diagonal):
            s = lax.dot_general(q_ref[...].astype(MXU_DTYPE), k_ref[...].astype(MXU_DTYPE),
                                (((1,), (1,)), ((), ())), preferred_element_type=F32)
            vb = v_ref[...].astype(MXU_DTYPE)
            if diagonal or not softmax:
                vis, dist = _visible(qi, kj, tq, tk)
            if softmax:
                s = s * scale
                if diagonal:
                    s = jnp.where(vis, s, NEG_INF)
                m_old = m_ref[...]
                m_new = jnp.maximum(m_old, jnp.max(s, axis=-1, keepdims=True))
                p = jnp.exp(s - m_new)
                corr = jnp.exp(m_old - m_new)
                l_ref[...] = corr * l_ref[...] + jnp.sum(p, axis=-1, keepdims=True)
                acc_ref[...] = corr * acc_ref[...] + jnp.dot(p.astype(MXU_DTYPE), vb, preferred_element_type=F32)
                m_ref[...] = m_new
            else:
                decay = jnp.exp(lg_ref[0:1, 0:1] * dist)
                if diagonal:
                    decay = jnp.where(vis, decay, 0.0)
                p = s * decay
                acc_ref[...] += jnp.dot(p.astype(MXU_DTYPE), vb, preferred_element_type=F32)

        if masked:
            pl.when(kj < qi)(functools.partial(step, False))
            pl.when(kj == qi)(functools.partial(step, True))
        else:
            step(False)

        @pl.when(kj == nk - 1)
        def _():
            if softmax:
                o_ref[...] = (acc_ref[...] / l_ref[...]).astype(o_ref.dtype)
                lse_ref[...] = m_ref[...] + jnp.log(l_ref[...])
            else:
                o_ref[...] = acc_ref[...].astype(o_ref.dtype)

    kcap = (lambda qi, kj: jnp.minimum(kj, qi)) if masked else (lambda qi, kj: kj)
    in_specs = [pl.BlockSpec((tq, dqk), lambda h, qi, kj: (qi, q0 + qs * h)),
                pl.BlockSpec((tk, dqk), lambda h, qi, kj: (kcap(qi, kj), k0 + ks * h)),
                pl.BlockSpec((tk, dv), lambda h, qi, kj: (kcap(qi, kj), v0 + vs * h))]
    operands = [qa, ka, va]
    out_shape = [jax.ShapeDtypeStruct((Sq, heads * dv), F32)]
    out_specs = [pl.BlockSpec((tq, dv), lambda h, qi, kj: (qi, h))]
    scratch = []
    if softmax:
        out_shape.append(jax.ShapeDtypeStruct((heads, Sq, 1), F32))
        out_specs.append(pl.BlockSpec((None, tq, 1), lambda h, qi, kj: (h, qi, 0)))
        scratch += [pltpu.VMEM((tq, 1), F32), pltpu.VMEM((tq, 1), F32)]
    else:
        in_specs.append(pl.BlockSpec((None, 8, LANES), lambda h, qi, kj: (h, 0, 0)))
        operands.append(log_gamma)
    scratch.append(pltpu.VMEM((tq, dv), F32))
    chunks = link.take(est_us) if link is not None else ()
    return _call(body, grid=(heads, nq, nk), in_specs=in_specs, out_specs=out_specs, out_shape=out_shape,
                 operands=operands, scratch=scratch, name=name, chunks=chunks,
                 bufs=link.bufs if link is not None else None)


def attn_bwd(q, k, v, do, *, heads, softmax, masked, scale=1.0, log_gamma=None, o=None, lse=None,
             link=None, est_us=0.0, name):
    (qa, dqk, q0, qs), (ka, _, k0, ks), (va, dv, v0, vs) = q, k, v
    Sq, Sk = qa.shape[0], ka.shape[0]
    tq, tk = _tile(Sq, ATTN_TILE, 8), _tile(Sk, ATTN_TILE, 8)
    nq, nk = Sq // tq, Sk // tk
    assert not masked or tq == tk
    contract0 = (((0,), (0,)), ((), ()))

    def body(*refs):
        if softmax:
            q_ref, k_ref, v_ref, do_ref, o_ref, lse_ref, dq_ref, dk_ref, dv_ref = refs
        else:
            q_ref, k_ref, v_ref, do_ref, lg_ref, dq_ref, dk_ref, dv_ref = refs
        kj, qi = pl.program_id(1), pl.program_id(2)

        @pl.when(jnp.logical_and(kj == 0, qi == 0))
        def _():
            dq_ref[...] = jnp.zeros_like(dq_ref)

        @pl.when(qi == 0)
        def _():
            dk_ref[...] = jnp.zeros_like(dk_ref)
            dv_ref[...] = jnp.zeros_like(dv_ref)

        def step(diagonal):
            qb = q_ref[...].astype(MXU_DTYPE)
            kb = k_ref[...].astype(MXU_DTYPE)
            vb = v_ref[...].astype(MXU_DTYPE)
            dob = do_ref[...].astype(MXU_DTYPE)
            s = lax.dot_general(qb, kb, (((1,), (1,)), ((), ())), preferred_element_type=F32)
            dp = lax.dot_general(dob, vb, (((1,), (1,)), ((), ())), preferred_element_type=F32)
            if diagonal or not softmax:
                vis, dist = _visible(qi, kj, tq, tk)
            if softmax:
                s = s * scale
                if diagonal:
                    s = jnp.where(vis, s, NEG_INF)
                p = jnp.exp(s - lse_ref[...])
                delta = jnp.sum(do_ref[...].astype(F32) * o_ref[...], axis=-1, keepdims=True)
                ds = p * (dp - delta) * scale
            else:
                decay = jnp.exp(lg_ref[0:1, 0:1] * dist)
                if diagonal:
                    decay = jnp.where(vis, decay, 0.0)
                p = s * decay
                ds = dp * decay
            pb = p.astype(MXU_DTYPE)
            dsb = ds.astype(MXU_DTYPE)
            dv_ref[...] += lax.dot_general(pb, dob, contract0, preferred_element_type=F32)
            dk_ref[...] += lax.dot_general(dsb, qb, contract0, preferred_element_type=F32)
            rows = pl.ds(pl.multiple_of(qi * tq, tq), tq)
            dq_ref[rows, :] += jnp.dot(dsb, kb, preferred_element_type=F32)

        if masked:
            pl.when(qi > kj)(functools.partial(step, False))
            pl.when(qi == kj)(functools.partial(step, True))
        else:
            step(False)

    qcap = (lambda kj, qi: jnp.maximum(qi, kj)) if masked else (lambda kj, qi: qi)
    in_specs = [pl.BlockSpec((tq, dqk), lambda h, kj, qi: (qcap(kj, qi), q0 + qs * h)),
                pl.BlockSpec((tk, dqk), lambda h, kj, qi: (kj, k0 + ks * h)),
                pl.BlockSpec((tk, dv), lambda h, kj, qi: (kj, v0 + vs * h)),
                pl.BlockSpec((tq, dv), lambda h, kj, qi: (qcap(kj, qi), h))]
    operands = [qa, ka, va, do]
    if softmax:
        in_specs += [pl.BlockSpec((tq, dv), lambda h, kj, qi: (qcap(kj, qi), h)),
                     pl.BlockSpec((None, tq, 1), lambda h, kj, qi: (h, qcap(kj, qi), 0))]
        operands += [o, lse]
    else:
        in_specs.append(pl.BlockSpec((None, 8, LANES), lambda h, kj, qi: (h, 0, 0)))
        operands.append(log_gamma)
    chunks = link.take(est_us) if link is not None else ()
    return _call(
        body, grid=(heads, nk, nq), in_specs=in_specs,
        out_specs=(pl.BlockSpec((Sq, dqk), lambda h, kj, qi: (0, h)),
                   pl.BlockSpec((tk, dqk), lambda h, kj, qi: (kj, h)),
                   pl.BlockSpec((tk, dv), lambda h, kj, qi: (kj, h))),
        out_shape=(jax.ShapeDtypeStruct((Sq, heads * dqk), F32), jax.ShapeDtypeStruct((Sk, heads * dqk), F32),
                   jax.ShapeDtypeStruct((Sk, heads * dv), F32)),
        operands=operands, name=name, chunks=chunks, bufs=link.bufs if link is not None else None)


def _sigmoid(x):
    return 1.0 / (1.0 + jnp.exp(-x))


def ret_gate_fwd(o, proj, gate_col, gn, heads, name):
    S, W = o.shape
    dv = W // heads
    tm = _tile(S, ROW_TILE // 2, 8)

    def body(o_ref, g_ref, gn_ref, y_ref):
        for hi in range(heads):
            cols = slice(hi * dv, (hi + 1) * dv)
            oh = o_ref[:, cols]
            mu = jnp.mean(oh, axis=-1, keepdims=True)
            oc = oh - mu
            rstd = lax.rsqrt(jnp.mean(oc * oc, axis=-1, keepdims=True) + LN_EPS)
            gt = g_ref[:, cols]
            y_ref[:, cols] = (gt * _sigmoid(gt) * (oc * rstd) * gn_ref[:, cols]).astype(y_ref.dtype)

    row = pl.BlockSpec((tm, W), lambda i: (i, 0))
    return pl.pallas_call(
        body, out_shape=jax.ShapeDtypeStruct((S, W), MXU_DTYPE), grid=(S // tm,),
        in_specs=[row, pl.BlockSpec((tm, W), lambda i: (i, gate_col)), pl.BlockSpec((1, W), lambda i: (0, 0))],
        out_specs=row, compiler_params=_params(1), name=name)(o, proj, gn)


def ret_gate_bwd(dy, o, proj, gate_col, gn, heads, name):
    S, W = o.shape
    dv = W // heads
    tm = _tile(S, ROW_TILE // 2, 8)

    def body(dy_ref, o_ref, g_ref, gn_ref, do_ref, dgt_ref, dgn_ref):
        @pl.when(pl.program_id(0) == 0)
        def _():
            dgn_ref[...] = jnp.zeros_like(dgn_ref)

        for hi in range(heads):
            cols = slice(hi * dv, (hi + 1) * dv)
            oh = o_ref[:, cols]
            mu = jnp.mean(oh, axis=-1, keepdims=True)
            oc = oh - mu
            rstd = lax.rsqrt(jnp.mean(oc * oc, axis=-1, keepdims=True) + LN_EPS)
            xhat = oc * rstd
            gt = g_ref[:, cols]
            sg = _sigmoid(gt)
            gain = gn_ref[:, cols]
            dy = dy_ref[:, cols]
            dgt_ref[:, cols] = (dy * xhat * gain * (sg * (1.0 + gt * (1.0 - sg)))).astype(dgt_ref.dtype)
            dn = dy * (gt * sg)
            dgn_ref[:, cols] += jnp.sum(dn * xhat, axis=0, keepdims=True)
            dxh = dn * gain
            m1 = jnp.mean(dxh, axis=-1, keepdims=True)
            m2 = jnp.mean(dxh * xhat, axis=-1, keepdims=True)
            do_ref[:, cols] = (rstd * (dxh - m1 - xhat * m2)).astype(do_ref.dtype)

    row = pl.BlockSpec((tm, W), lambda i: (i, 0))
    vec = pl.BlockSpec((1, W), lambda i: (0, 0))
    return pl.pallas_call(
        body,
        out_shape=(jax.ShapeDtypeStruct((S, W), MXU_DTYPE), jax.ShapeDtypeStruct((S, W), MXU_DTYPE),
                   jax.ShapeDtypeStruct((1, W), F32)),
        grid=(S // tm,), in_specs=[row, row, pl.BlockSpec((tm, W), lambda i: (i, gate_col)), vec],
        out_specs=(row, row, vec), compiler_params=_params(1), name=name)(dy, o, proj, gn)


def _shift_down(x, s):
    rows = lax.broadcasted_iota(jnp.int32, x.shape, 0)
    return jnp.where(rows >= s, pltpu.roll(x, s, 0), 0.0)


def _shift_up(x, s):
    n = x.shape[0]
    rows = lax.broadcasted_iota(jnp.int32, x.shape, 0)
    return jnp.where(rows < n - s, pltpu.roll(x, n - s, 0), 0.0)


def _conv3(x, w_ref, b_ref):
    return (w_ref[2:3, :] * x + w_ref[1:2, :] * _shift_down(x, 1) + w_ref[0:1, :] * _shift_down(x, 2)
            + b_ref[...])


def conv_glu_fwd(hup, w, b, name):
    S, W2 = hup.shape
    F = W2 // 2
    tc = _tile(F, LANES)
    nb = F // tc

    def body(g_ref, v_ref, wg_ref, wv_ref, bg_ref, bv_ref, u_ref):
        cg = _conv3(g_ref[...], wg_ref, bg_ref)
        cv = _conv3(v_ref[...], wv_ref, bv_ref)
        u_ref[...] = (cg * _sigmoid(cg) * cv).astype(u_ref.dtype)

    def col(rows, off):
        return pl.BlockSpec((rows, tc), lambda j: (0, j + off))

    return pl.pallas_call(
        body, out_shape=jax.ShapeDtypeStruct((S, F), MXU_DTYPE), grid=(nb,),
        in_specs=[col(S, 0), col(S, nb), col(3, 0), col(3, nb), col(1, 0), col(1, nb)],
        out_specs=col(S, 0), compiler_params=_params(1), name=name)(hup, hup, w, w, b, b)


def conv_glu_bwd(du, hup, w, b, name):
    S, W2 = hup.shape
    F = W2 // 2
    tc = _tile(F, LANES)
    nb = F // tc

    def back(dc, x, w_ref, dh_ref, dw_ref, db_ref):
        dw_ref[2:3, :] = jnp.sum(dc * x, axis=0, keepdims=True)
        dw_ref[1:2, :] = jnp.sum(dc * _shift_down(x, 1), axis=0, keepdims=True)
        dw_ref[0:1, :] = jnp.sum(dc * _shift_down(x, 2), axis=0, keepdims=True)
        db_ref[...] = jnp.sum(dc, axis=0, keepdims=True)
        dh_ref[...] = (w_ref[2:3, :] * dc + w_ref[1:2, :] * _shift_up(dc, 1)
                       + w_ref[0:1, :] * _shift_up(dc, 2)).astype(dh_ref.dtype)

    def body(du_ref, g_ref, v_ref, wg_ref, wv_ref, bg_ref, bv_ref,
             dhg_ref, dhv_ref, dwg_ref, dwv_ref, dbg_ref, dbv_ref):
        xg, xv = g_ref[...], v_ref[...]
        cg = _conv3(xg, wg_ref, bg_ref)
        cv = _conv3(xv, wv_ref, bv_ref)
        sg = _sigmoid(cg)
        du = du_ref[...]
        back(du * cv * (sg * (1.0 + cg * (1.0 - sg))), xg, wg_ref, dhg_ref, dwg_ref, dbg_ref)
        back(du * (cg * sg), xv, wv_ref, dhv_ref, dwv_ref, dbv_ref)

    def col(rows, off):
        return pl.BlockSpec((rows, tc), lambda j: (0, j + off))

    return pl.pallas_call(
        body,
        out_shape=(jax.ShapeDtypeStruct((S, F), MXU_DTYPE), jax.ShapeDtypeStruct((S, F), MXU_DTYPE),
                   jax.ShapeDtypeStruct((3, F), F32), jax.ShapeDtypeStruct((3, F), F32),
                   jax.ShapeDtypeStruct((1, F), F32), jax.ShapeDtypeStruct((1, F), F32)),
        grid=(nb,),
        in_specs=[col(S, 0), col(S, 0), col(S, nb), col(3, 0), col(3, nb), col(1, 0), col(1, nb)],
        out_specs=(col(S, 0), col(S, 0), col(3, 0), col(3, 0), col(1, 0), col(1, 0)),
        compiler_params=_params(1), name=name)(du, hup, hup, w, w, b, b)


def loss_head(y, target, name):
    S, D = y.shape
    tm = _tile(S, ROW_TILE, 8)

    def body(y_ref, t_ref, dy_ref, loss_ref):
        @pl.when(pl.program_id(0) == 0)
        def _():
            loss_ref[...] = jnp.zeros_like(loss_ref)

        e = y_ref[...] - t_ref[...]
        dy_ref[...] = e * (1.0 / D)
        part = jnp.sum(jnp.sum(e * e, axis=-1, keepdims=True), axis=0, keepdims=True) * (0.5 / D)
        loss_ref[...] += jnp.broadcast_to(part, loss_ref.shape)

    row = pl.BlockSpec((tm, D), lambda i: (i, 0))
    return pl.pallas_call(
        body, out_shape=(jax.ShapeDtypeStruct((S, D), F32), jax.ShapeDtypeStruct((8, LANES), F32)),
        grid=(S // tm,), in_specs=[row, row], out_specs=(row, pl.BlockSpec((8, LANES), lambda i: (0, 0))),
        compiler_params=_params(1), name=name)(y, target)


def adam_update(parts, w, m, v, name):
    n_layers, n_parts, R, C = parts.shape
    tr = _tile(R, max(8, (1 << 19) // C), 8)
    c1 = 1.0 - ADAM_B1 ** ADAM_STEP
    c2 = 1.0 - ADAM_B2 ** ADAM_STEP

    def body(p_ref, w_ref, m_ref, v_ref, g_ref, d_ref, nm_ref, nv_ref):
        g = p_ref[0].astype(F32)
        for pi in range(1, n_parts):
            g = g + p_ref[pi].astype(F32)
        nm = ADAM_B1 * m_ref[...] + (1.0 - ADAM_B1) * g
        nv = ADAM_B2 * v_ref[...] + (1.0 - ADAM_B2) * (g * g)
        g_ref[...] = g
        nm_ref[...] = nm
        nv_ref[...] = nv
        d_ref[...] = -ADAM_LR * ((nm / c1) / (jnp.sqrt(nv / c2) + ADAM_EPS) + ADAM_WD * w_ref[...])

    row = pl.BlockSpec((None, tr, C), lambda l, i: (l, i, 0))
    out = jax.ShapeDtypeStruct((n_layers, R, C), F32)
    return pl.pallas_call(
        body, out_shape=(out, out, out, out), grid=(n_layers, R // tr),
        in_specs=[pl.BlockSpec((None, n_parts, tr, C), lambda l, i: (l, 0, i, 0)), row, row, row],
        out_specs=(row, row, row, row), compiler_params=_params(2), name=name)(parts, w, m, v)


def pair_sum(own, got, core, name):
    _, R, C = own.shape
    tr = _tile(R, max(16, (1 << 19) // C), 16)

    def body(core_ref, a_ref, b_ref, o_ref):
        o_ref[...] = (a_ref[...].astype(F32) + b_ref[...].astype(F32)).astype(o_ref.dtype)

    grid_spec = pltpu.PrefetchScalarGridSpec(
        num_scalar_prefetch=1, grid=(N_CHIP, R // tr),
        in_specs=[pl.BlockSpec((None, tr, C), lambda j, i, core_ref: (2 * j + core_ref[0], i, 0)),
                  pl.BlockSpec((None, tr, C), lambda j, i, core_ref: (j, i, 0))],
        out_specs=pl.BlockSpec((None, tr, C), lambda j, i, core_ref: (j, i, 0)))
    return pl.pallas_call(
        body, out_shape=jax.ShapeDtypeStruct((N_CHIP, R, C), own.dtype), grid_spec=grid_spec,
        compiler_params=_params(2), name=name)(core, own, got)


def gather_small(block, name):
    m_per, n = block.shape

    def body(x_ref, out_ref, send_sems, recv_sems, local_sem):
        x, y, c = _place()
        me, sibling = (x, y, c), (x, y, 1 - c)
        chips = _other_chips(x, y)

        def rows(px, py, pc):
            return out_ref.at[pl.ds((4 * px + 2 * py + pc) * m_per, m_per), :]

        def copy(k, blk, to, src=None):
            return pltpu.make_async_remote_copy(
                src_ref=rows(*blk) if src is None else src, dst_ref=rows(*blk),
                send_sem=send_sems.at[k], recv_sem=recv_sems.at[k], device_id=to, device_id_type=MESH)

        mine = pltpu.make_async_copy(x_ref, rows(*me), local_sem)
        mine.start()
        first = [copy(0, me, sibling, src=x_ref)]
        first += [copy(1 + j, me, (*chip, c), src=x_ref) for j, chip in enumerate(chips)]
        for cp in first:
            cp.start()
        passed = [copy(4 + j, (*chip, c), sibling) for j, chip in enumerate(chips)]
        for j, chip in enumerate(chips):
            copy(1 + j, (*chip, c), me).wait_recv()
            passed[j].start()
        copy(0, sibling, me).wait_recv()
        for j, chip in enumerate(chips):
            copy(4 + j, (*chip, 1 - c), me).wait_recv()
        for cp in first + passed:
            cp.wait_send()
        mine.wait()

    return pl.pallas_call(
        body, out_shape=jax.ShapeDtypeStruct((N_DEV * m_per, n), block.dtype),
        in_specs=[pl.BlockSpec(memory_space=pltpu.VMEM)], out_specs=pl.BlockSpec(memory_space=pltpu.VMEM),
        scratch_shapes=[pltpu.SemaphoreType.DMA((7,)), pltpu.SemaphoreType.DMA((7,)), pltpu.SemaphoreType.DMA],
        compiler_params=pltpu.CompilerParams(vmem_limit_bytes=VMEM_LIMIT_BYTES), name=name)(block)


def swap_with_sibling(slabs, name):
    _, R, C = slabs.shape

    def body(g_ref, got_ref, send_sems, recv_sems):
        x, y, c = _place()
        copies = [pltpu.make_async_remote_copy(
            src_ref=g_ref.at[2 * j + (1 - c)], dst_ref=got_ref.at[j], send_sem=send_sems.at[j],
            recv_sem=recv_sems.at[j], device_id=(x, y, 1 - c), device_id_type=MESH) for j in range(N_CHIP)]
        for cp in copies:
            cp.start()
        for cp in copies:
            cp.wait()

    return pl.pallas_call(
        body, out_shape=jax.ShapeDtypeStruct((N_CHIP, R, C), slabs.dtype),
        in_specs=[pl.BlockSpec(memory_space=pl.ANY)], out_specs=pl.BlockSpec(memory_space=pl.ANY),
        scratch_shapes=[pltpu.SemaphoreType.DMA((N_CHIP,)), pltpu.SemaphoreType.DMA((N_CHIP,))],
        name=name)(slabs)


def _rope_tables(positions, d):
    inv_freq = ROPE_BASE ** (-jnp.arange(0, d, 2, dtype=F32) / d)
    ang = positions.astype(F32)[:, None] * inv_freq
    return jnp.cos(ang), jnp.sin(ang)


def _mla_pad(nope, rope):
    S, H, _ = nope.shape
    half = MLA_ROPE // 2
    z = jnp.zeros((S, H, LANES // 2 - half), nope.dtype)
    return jnp.concatenate([nope, rope[..., :half], z, rope[..., half:], z], axis=2).reshape(S, H * MLA_PAD)


def _mla_unpad(x, H):
    S = x.shape[0]
    half = MLA_ROPE // 2
    x3 = x.reshape(S, H, MLA_PAD)
    rope = jnp.concatenate([x3[..., MLA_NOPE:MLA_NOPE + half],
                            x3[..., MLA_NOPE + LANES // 2:MLA_NOPE + LANES // 2 + half]], axis=2)
    return x3[..., :MLA_NOPE], rope


def kernel(x, mem, positions, ret_w_in, ret_gn_g, ret_w_out, mla_w_in, mla_q_norm_g, mla_w_uq, mla_kv_norm_g, mla_w_ukv, mla_w_out, xa_w_q, xa_w_kv, xa_w_out, ffn_w_up, ffn_conv_w, ffn_conv_b, ffn_w_down, ln_mix_g, ln_mix_b, ln_mem_g, ln_mem_b, ln_ffn_g, ln_ffn_b, loss_target, m_ret_w_in, m_ret_gn_g, m_ret_w_out, m_mla_w_in, m_mla_q_norm_g, m_mla_w_uq, m_mla_kv_norm_g, m_mla_w_ukv, m_mla_w_out, m_xa_w_q, m_xa_w_kv, m_xa_w_out, m_ffn_w_up, m_ffn_conv_w, m_ffn_conv_b, m_ffn_w_down, m_ln_mix_g, m_ln_mix_b, m_ln_mem_g, m_ln_mem_b, m_ln_ffn_g, m_ln_ffn_b, v_ret_w_in, v_ret_gn_g, v_ret_w_out, v_mla_w_in, v_mla_q_norm_g, v_mla_w_uq, v_mla_kv_norm_g, v_mla_w_ukv, v_mla_w_out, v_xa_w_q, v_xa_w_kv, v_xa_w_out, v_ffn_w_up, v_ffn_conv_w, v_ffn_conv_b, v_ffn_w_down, v_ln_mix_g, v_ln_mix_b, v_ln_mem_g, v_ln_mem_b, v_ln_ffn_g, v_ln_ffn_b):
    weights = dict(ret_w_in=ret_w_in, ret_gn_g=ret_gn_g, ret_w_out=ret_w_out, mla_w_in=mla_w_in,
                   mla_q_norm_g=mla_q_norm_g, mla_w_uq=mla_w_uq, mla_kv_norm_g=mla_kv_norm_g,
                   mla_w_ukv=mla_w_ukv, mla_w_out=mla_w_out, xa_w_q=xa_w_q, xa_w_kv=xa_w_kv, xa_w_out=xa_w_out,
                   ffn_w_up=ffn_w_up, ffn_conv_w=ffn_conv_w, ffn_conv_b=ffn_conv_b, ffn_w_down=ffn_w_down,
                   ln_mix_g=ln_mix_g, ln_mix_b=ln_mix_b, ln_mem_g=ln_mem_g, ln_mem_b=ln_mem_b,
                   ln_ffn_g=ln_ffn_g, ln_ffn_b=ln_ffn_b)
    mom_m = dict(ret_w_in=m_ret_w_in, ret_gn_g=m_ret_gn_g, ret_w_out=m_ret_w_out, mla_w_in=m_mla_w_in,
                 mla_q_norm_g=m_mla_q_norm_g, mla_w_uq=m_mla_w_uq, mla_kv_norm_g=m_mla_kv_norm_g,
                 mla_w_ukv=m_mla_w_ukv, mla_w_out=m_mla_w_out, xa_w_q=m_xa_w_q, xa_w_kv=m_xa_w_kv,
                 xa_w_out=m_xa_w_out, ffn_w_up=m_ffn_w_up, ffn_conv_w=m_ffn_conv_w, ffn_conv_b=m_ffn_conv_b,
                 ffn_w_down=m_ffn_w_down, ln_mix_g=m_ln_mix_g, ln_mix_b=m_ln_mix_b, ln_mem_g=m_ln_mem_g,
                 ln_mem_b=m_ln_mem_b, ln_ffn_g=m_ln_ffn_g, ln_ffn_b=m_ln_ffn_b)
    mom_v = dict(ret_w_in=v_ret_w_in, ret_gn_g=v_ret_gn_g, ret_w_out=v_ret_w_out, mla_w_in=v_mla_w_in,
                 mla_q_norm_g=v_mla_q_norm_g, mla_w_uq=v_mla_w_uq, mla_kv_norm_g=v_mla_kv_norm_g,
                 mla_w_ukv=v_mla_w_ukv, mla_w_out=v_mla_w_out, xa_w_q=v_xa_w_q, xa_w_kv=v_xa_w_kv,
                 xa_w_out=v_xa_w_out, ffn_w_up=v_ffn_w_up, ffn_conv_w=v_ffn_conv_w, ffn_conv_b=v_ffn_conv_b,
                 ffn_w_down=v_ffn_w_down, ln_mix_g=v_ln_mix_g, ln_mix_b=v_ln_mix_b, ln_mem_g=v_ln_mem_g,
                 ln_mem_b=v_ln_mem_b, ln_ffn_g=v_ln_ffn_g, ln_ffn_b=v_ln_ffn_b)
    order = list(weights)
    BIG = dict(ret_w_in=2, ret_w_out=1, mla_w_in=1, mla_w_uq=2, mla_w_ukv=2, mla_w_out=1,
               xa_w_q=1, xa_w_kv=2, xa_w_out=1, ffn_w_up=2, ffn_w_down=1)
    SMALL_CUT = ("ffn_conv_w", "mla_q_norm_g", "mla_kv_norm_g")
    REPLICATED = ("ret_gn_g", "ffn_conv_b", "ln_mix_g", "ln_mix_b", "ln_mem_g", "ln_mem_b", "ln_ffn_g", "ln_ffn_b")

    x = x[0]
    mem = mem[0]
    positions = positions[0]
    target = loss_target[0]
    S, D = x.shape
    depth = ln_mix_g.shape[0]
    alpha = (2 * depth) ** 0.25
    ret_dk = D // RET_HEADS
    ret_dv = 2 * D // RET_HEADS
    ret_qkw = RET_HEADS * ret_dk
    ret_vw = RET_HEADS * ret_dv
    xa_d = D // XA_HEADS
    assert 2 * ret_qkw == ret_vw and MLA_NOPE == LANES and MLA_V == LANES and MLA_ROPE == LANES // 2
    assert MLA_Q_RANK == MLA_KV_RANK and CHUNK & (CHUNK - 1) == 0
    core = lax.axis_index("c").astype(jnp.int32).reshape(1)
    dev = 4 * lax.axis_index("x") + 2 * lax.axis_index("y") + lax.axis_index("c")

    gather, scatter = _Link(), _Link()
    shard_b = {n: weights[n].astype(PAY_DTYPE) for n in BIG}
    staged = set()

    def units(layer):
        mixer = ("ret_w_in", "ret_w_out") if layer % 2 == 0 else ("mla_w_in", "mla_w_uq", "mla_w_ukv", "mla_w_out")
        return [(n, layer // 2) for n in mixer] + [(n, layer) for n in
                                                    ("xa_w_q", "xa_w_kv", "xa_w_out", "ffn_w_up", "ffn_w_down")]

    def geometry(n):
        _, k, nn = weights[n].shape
        return BIG[n], k, nn

    for layer in range(depth):
        for n, j in units(layer):
            axis, k, nn = geometry(n)
            full = (N_DEV * k, nn) if axis == 1 else (k, N_DEV * nn)
            gather.push("gather", shard_b[n], (n, j), full, j, k, nn, axis=axis, k=k, n=nn)

    def weight(n, j):
        if (n, j) not in staged:
            gather.flush((n, j), name="gather_rest")
            axis, k, nn = geometry(n)
            gather.bufs[(n, j)] = forward_to_sibling(gather.bufs[(n, j)], axis, k, nn, name="gather_level2")
            staged.add((n, j))
        return gather.bufs[(n, j)]

    def pack(arrays):
        flat = jnp.concatenate([a.reshape(-1) for a in arrays])
        rows = -(-flat.shape[0] // (8 * LANES)) * 8
        return jnp.pad(flat, (0, rows * LANES - flat.shape[0])).reshape(rows, LANES)

    def unpack(flat, like):
        out, at = [], 0
        for a in like:
            out.append(flat[at:at + a.size].reshape(a.shape))
            at += a.size
        return out

    small_local = [weights[n] for n in SMALL_CUT]
    blk = pack(small_local)
    allsmall = gather_small(blk, name="gather_small_weights").reshape(N_DEV, -1)
    per_dev = [unpack(allsmall[d], small_local) for d in range(N_DEV)]
    conv_w_full, qg_full, kvg_full = [jnp.concatenate([per_dev[d][i] for d in range(N_DEV)], axis=-1)
                                      for i in range(len(SMALL_CUT))]

    rcos, rsin = _rope_tables(positions, ret_dk)
    mcos, msin = _rope_tables(positions, MLA_ROPE)
    zq = jnp.zeros((S, LANES // 2 - MLA_ROPE // 2), F32)
    mla_ta = jnp.concatenate([mcos, zq, mcos, zq], axis=1)
    mla_tb = jnp.concatenate([-msin, zq, msin, zq], axis=1)
    log_gamma = jnp.log(1.0 - 2.0 ** (-5.0 - jnp.arange(RET_HEADS, dtype=F32)))
    log_gamma = jnp.broadcast_to(log_gamma[:, None, None], (RET_HEADS, 8, LANES))
    mla_scale = (MLA_NOPE + MLA_ROPE) ** -0.5
    xa_scale = xa_d ** -0.5
    mem_b = mem.astype(MXU_DTYPE)

    def vec(a, l):
        return a[l][None, :]

    saved = []
    h = x
    hb = x.astype(MXU_DTYPE)
    for layer in range(depth):
        j = layer // 2
        sv = {}
        sv["hb_mix"] = hb
        if layer % 2 == 0:
            proj = mm(hb, weight("ret_w_in", j), link=gather, name="ret_in")
            qk = ret_rope_fwd(proj, rcos, rsin, 2 * RET_HEADS, ret_dk // 2, RET_HEADS, ret_dk ** -0.5, name="ret_rope")
            (o,) = attn_fwd((qk, ret_dk, 0, 1), (qk, ret_dk, RET_HEADS, 1), (proj, ret_dv, RET_HEADS, 1),
                            heads=RET_HEADS, softmax=False, masked=True, log_gamma=log_gamma, link=gather,
                            est_us=ATTN_US["ret"][0], name="ret_attn")
            yb = ret_gate_fwd(o, proj, 2, vec(ret_gn_g, j), RET_HEADS, name="ret_gate")
            mix = mm(yb, weight("ret_w_out", j), link=gather, name="ret_out")
            sv.update(proj=proj, qk=qk, o=o, yb=yb)
        else:
            proj = mm(hb, weight("mla_w_in", j), link=gather, name="mla_in")
            cq = rms_fwd(proj, 0, MLA_Q_RANK, vec(qg_full, j), name="mla_q_norm")
            ckv = rms_fwd(proj, 1, MLA_KV_RANK, vec(kvg_full, j), name="mla_kv_norm")
            qf = mm(cq, weight("mla_w_uq", j), link=gather, name="mla_uq")
            kvf = mm(ckv, weight("mla_w_ukv", j), link=gather, name="mla_ukv")
            q3 = qf.reshape(S, MLA_HEADS, MLA_NOPE + MLA_ROPE)
            kv3 = kvf.reshape(S, MLA_HEADS, MLA_NOPE + MLA_V)
            k_rope = jnp.broadcast_to(proj[:, None, MLA_Q_RANK + MLA_KV_RANK:], (S, MLA_HEADS, MLA_ROPE))
            q_pad = _mla_pad(q3[..., :MLA_NOPE], q3[..., MLA_NOPE:])
            k_pad = _mla_pad(kv3[..., :MLA_NOPE], k_rope)
            (qr,) = mla_rope(q_pad, mla_ta, mla_tb, MLA_HEADS, backward=False, head_sum=False,
                             out_dtype=MXU_DTYPE, name="mla_rope_q")
            (kr,) = mla_rope(k_pad, mla_ta, mla_tb, MLA_HEADS, backward=False, head_sum=False,
                             out_dtype=MXU_DTYPE, name="mla_rope_k")
            o, lse = attn_fwd((qr, MLA_PAD, 0, 1), (kr, MLA_PAD, 0, 1), (kvf, MLA_V, 1, 2), heads=MLA_HEADS,
                              softmax=True, masked=True, scale=mla_scale, link=gather, est_us=ATTN_US["mla"][0],
                              name="mla_attn")
            ob = o.astype(MXU_DTYPE)
            mix = mm(ob, weight("mla_w_out", j), link=gather, name="mla_out")
            sv.update(proj=proj, cq=cq, ckv=ckv, kvf=kvf, qr=qr, kr=kr, o=o, ob=ob, lse=lse)
        h, hb, sv["xhat_mix"], sv["rstd_mix"] = ln_fwd(h, mix, vec(ln_mix_g, layer), vec(ln_mix_b, layer), alpha,
                                                       name="ln_mix")
        sv["hb_mem"] = hb
        q = mm(hb, weight("xa_w_q", layer), link=gather, name="xa_q")
        kvm = mm(mem_b, weight("xa_w_kv", layer), link=gather, name="xa_kv")
        o, lse = attn_fwd((q, xa_d, 0, 1), (kvm, xa_d, 0, 1), (kvm, xa_d, XA_HEADS, 1), heads=XA_HEADS,
                          softmax=True, masked=False, scale=xa_scale, link=gather, est_us=ATTN_US["xa"][0],
                          name="xa_attn")
        ob = o.astype(MXU_DTYPE)
        mix = mm(ob, weight("xa_w_out", layer), link=gather, name="xa_out")
        sv.update(xa_q=q, xa_kvm=kvm, xa_o=o, xa_ob=ob, xa_lse=lse)
        h, hb, sv["xhat_mem"], sv["rstd_mem"] = ln_fwd(h, mix, vec(ln_mem_g, layer), vec(ln_mem_b, layer), alpha,
                                                       name="ln_mem")
        sv["hb_ffn"] = hb
        hup = mm(hb, weight("ffn_w_up", layer), link=gather, name="ffn_up")
        u = conv_glu_fwd(hup, conv_w_full[layer], vec(ffn_conv_b, layer), name="ffn_conv")
        mix = mm(u, weight("ffn_w_down", layer), link=gather, name="ffn_down")
        sv.update(hup=hup, u=u)
        h, hb, sv["xhat_ffn"], sv["rstd_ffn"] = ln_fwd(h, mix, vec(ln_ffn_g, layer), vec(ln_ffn_b, layer), alpha,
                                                       name="ln_ffn")
        saved.append(sv)

    dh, loss_blk = loss_head(h, target, name="loss_head")
    loss = lax.psum(loss_blk[0, 0], ("x", "y", "c"))

    small = {n: [None] * weights[n].shape[0] for n in REPLICATED + SMALL_CUT}
    W = gather.bufs

    def wgrad(n, a, d, l, tag):
        axis, k, nn = geometry(n)
        if axis == 2:
            slabs = mm(a, d, mode="tn", out_dtype=PAY_DTYPE, slab_width=nn, link=scatter, name=tag)
        else:
            slabs = mm(a, d, mode="tn", out_dtype=PAY_DTYPE, link=scatter, name=tag).reshape(N_DEV, k, nn)
        got = swap_with_sibling(slabs, name="swap_" + n)
        sums = pair_sum(slabs, got, core, name="pairsum_" + n)
        n_layers = weights[n].shape[0]
        scatter.push("scatter", sums, n, (n_layers, N_CHIP, k, nn), l, k, nn)

    for layer in reversed(range(depth)):
        j = layer // 2
        sv = saved[layer]
        dz, dzb, dg, db = ln_bwd(dh, sv["xhat_ffn"], sv["rstd_ffn"], vec(ln_ffn_g, layer), name="ln_ffn_bwd")
        small["ln_ffn_g"][layer], small["ln_ffn_b"][layer] = dg[0], db[0]
        wgrad("ffn_w_down", sv["u"], dzb, layer, "ffn_down_dw")
        du = mm(dzb, W[("ffn_w_down", layer)], mode="nt", link=scatter, name="ffn_down_dx")
        dhg, dhv, dwg, dwv, dbg, dbv = conv_glu_bwd(du, sv["hup"], conv_w_full[layer], vec(ffn_conv_b, layer),
                                                    name="ffn_conv_bwd")
        small["ffn_conv_w"][layer] = jnp.concatenate([dwg, dwv], axis=1)
        small["ffn_conv_b"][layer] = jnp.concatenate([dbg, dbv], axis=1)[0]
        dhup = jnp.concatenate([dhg, dhv], axis=1)
        wgrad("ffn_w_up", sv["hb_ffn"], dhup, layer, "ffn_up_dw")
        dh = mm(dhup, W[("ffn_w_up", layer)], mode="nt", add=dz, add_scale=alpha, link=scatter, name="ffn_up_dx")
        dz, dzb, dg, db = ln_bwd(dh, sv["xhat_mem"], sv["rstd_mem"], vec(ln_mem_g, layer), name="ln_mem_bwd")
        small["ln_mem_g"][layer], small["ln_mem_b"][layer] = dg[0], db[0]
        wgrad("xa_w_out", sv["xa_ob"], dzb, layer, "xa_out_dw")
        do = mm(dzb, W[("xa_w_out", layer)], mode="nt", link=scatter, name="xa_out_dx")
        dq, dk, dv = attn_bwd((sv["xa_q"], xa_d, 0, 1), (sv["xa_kvm"], xa_d, 0, 1), (sv["xa_kvm"], xa_d, XA_HEADS, 1),
                              do, heads=XA_HEADS, softmax=True, masked=False, scale=xa_scale, o=sv["xa_o"],
                              lse=sv["xa_lse"], link=scatter, est_us=ATTN_US["xa"][1], name="xa_attn_bwd")
        dqb = dq.astype(MXU_DTYPE)
        wgrad("xa_w_kv", mem_b, jnp.concatenate([dk, dv], axis=1).astype(MXU_DTYPE), layer, "xa_kv_dw")
        wgrad("xa_w_q", sv["hb_mem"], dqb, layer, "xa_q_dw")
        dh = mm(dqb, W[("xa_w_q", layer)], mode="nt", add=dz, add_scale=alpha, link=scatter, name="xa_q_dx")
        dz, dzb, dg, db = ln_bwd(dh, sv["xhat_mix"], sv["rstd_mix"], vec(ln_mix_g, layer), name="ln_mix_bwd")
        small["ln_mix_g"][layer], small["ln_mix_b"][layer] = dg[0], db[0]
        if layer % 2 == 0:
            wgrad("ret_w_out", sv["yb"], dzb, j, "ret_out_dw")
            dy = mm(dzb, W[("ret_w_out", j)], mode="nt", link=scatter, name="ret_out_dx")
            do, dgate, dgn = ret_gate_bwd(dy, sv["o"], sv["proj"], 2, vec(ret_gn_g, j), RET_HEADS, name="ret_gate_bwd")
            small["ret_gn_g"][j] = dgn[0]
            dq, dk, dv = attn_bwd((sv["qk"], ret_dk, 0, 1), (sv["qk"], ret_dk, RET_HEADS, 1),
                                  (sv["proj"], ret_dv, RET_HEADS, 1), do, heads=RET_HEADS, softmax=False,
                                  masked=True, log_gamma=log_gamma, link=scatter, est_us=ATTN_US["ret"][1],
                                  name="ret_attn_bwd")
            dqk = ret_rope_bwd(dq, dk, rcos, rsin, ret_dk // 2, ret_dk ** -0.5, name="ret_rope_bwd")
            dproj = jnp.concatenate([dqk, dv.astype(MXU_DTYPE), dgate], axis=1)
            wgrad("ret_w_in", sv["hb_mix"], dproj, j, "ret_in_dw")
            dh = mm(dproj, W[("ret_w_in", j)], mode="nt", add=dz, add_scale=alpha, link=scatter, name="ret_in_dx")
        else:
            wgrad("mla_w_out", sv["ob"], dzb, j, "mla_out_dw")
            do = mm(dzb, W[("mla_w_out", j)], mode="nt", link=scatter, name="mla_out_dx")
            dq, dk, dv = attn_bwd((sv["qr"], MLA_PAD, 0, 1), (sv["kr"], MLA_PAD, 0, 1), (sv["kvf"], MLA_V, 1, 2), do,
                                  heads=MLA_HEADS, softmax=True, masked=True, scale=mla_scale, o=sv["o"],
                                  lse=sv["lse"], link=scatter, est_us=ATTN_US["mla"][1], name="mla_attn_bwd")
            (dq_un,) = mla_rope(dq, mla_ta, mla_tb, MLA_HEADS, backward=True, head_sum=False, out_dtype=F32,
                                name="mla_rope_q_bwd")
            dk_un, dk_rope_sum = mla_rope(dk, mla_ta, mla_tb, MLA_HEADS, backward=True, head_sum=True, out_dtype=F32,
                                          name="mla_rope_k_bwd")
            dq_nope, dq_rope = _mla_unpad(dq_un, MLA_HEADS)
            dqf = jnp.concatenate([dq_nope, dq_rope], axis=2).reshape(S, -1).astype(MXU_DTYPE)
            dk_nope, _ = _mla_unpad(dk_un, MLA_HEADS)
            dkvf = jnp.concatenate([dk_nope, dv.reshape(S, MLA_HEADS, MLA_V)], axis=2).reshape(S, -1).astype(MXU_DTYPE)
            half = MLA_ROPE // 2
            dk_rope = jnp.concatenate([dk_rope_sum[:, :half], dk_rope_sum[:, LANES // 2:LANES // 2 + half]], axis=1)
            wgrad("mla_w_uq", sv["cq"], dqf, j, "mla_uq_dw")
            dcq = mm(dqf, W[("mla_w_uq", j)], mode="nt", link=scatter, name="mla_uq_dx")
            wgrad("mla_w_ukv", sv["ckv"], dkvf, j, "mla_ukv_dw")
            dckv = mm(dkvf, W[("mla_w_ukv", j)], mode="nt", link=scatter, name="mla_ukv_dx")
            dcq_in, dqg = rms_bwd(dcq, sv["proj"], 0, MLA_Q_RANK, vec(qg_full, j), name="mla_q_norm_bwd")
            dckv_in, dkvg = rms_bwd(dckv, sv["proj"], 1, MLA_KV_RANK, vec(kvg_full, j), name="mla_kv_norm_bwd")
            small["mla_q_norm_g"][j], small["mla_kv_norm_g"][j] = dqg[0], dkvg[0]
            dproj = jnp.concatenate([dcq_in, dckv_in, dk_rope], axis=1).astype(MXU_DTYPE)
            wgrad("mla_w_in", sv["hb_mix"], dproj, j, "mla_in_dw")
            dh = mm(dproj, W[("mla_w_in", j)], mode="nt", add=dz, add_scale=alpha, link=scatter, name="mla_in_dx")
    grad_x = dh[None]
    scatter.flush(None, name="scatter_rest")

    grads, deltas, new_m, new_v = {}, {}, {}, {}
    for n in BIG:
        grads[n], deltas[n], new_m[n], new_v[n] = adam_update(scatter.bufs[n], weights[n], mom_m[n], mom_v[n],
                                                              name="adam_" + n)

    small_names = list(REPLICATED + SMALL_CUT)
    partial = [jnp.stack(small[n]) for n in small_names]
    allpart = gather_small(pack(partial), name="gather_small_grads")
    rows = allpart.shape[0] // N_DEV
    allpart = allpart.reshape(N_DEV, rows, LANES)

    rep_names = list(REPLICATED)
    rep_w = pack([weights[n] for n in rep_names])
    rep_m = pack([mom_m[n] for n in rep_names])
    rep_v = pack([mom_v[n] for n in rep_names])
    rep_rows = rep_w.shape[0]
    rep_size = sum(weights[n].size for n in rep_names)
    flat_parts = allpart.reshape(N_DEV, rows * LANES)
    rep_parts = jnp.pad(flat_parts[:, :rep_size], ((0, 0), (0, rep_rows * LANES - rep_size)))
    outs = adam_update(rep_parts.reshape(1, N_DEV, rep_rows, LANES), rep_w[None], rep_m[None], rep_v[None],
                       name="adam_replicated")
    for o_, dst in zip(outs, (grads, deltas, new_m, new_v)):
        for n, a in zip(rep_names, unpack(o_.reshape(-1), [weights[n] for n in rep_names])):
            dst[n] = a

    cut_names = list(SMALL_CUT)
    cut_full = [jnp.stack(small[n]) for n in cut_names]
    cut_parts = []
    at = rep_size
    for n, a in zip(cut_names, cut_full):
        whole = flat_parts[:, at:at + a.size].reshape((N_DEV,) + a.shape)
        at += a.size
        width = weights[n].shape[-1]
        start = (0,) * (whole.ndim - 1) + (dev * width,)
        mine = lax.dynamic_slice(whole, start, whole.shape[:-1] + (width,))
        cut_parts.append(mine.reshape(N_DEV, -1))
    cut_parts = jnp.concatenate(cut_parts, axis=1)
    cut_w = pack([weights[n] for n in cut_names])
    cut_m = pack([mom_m[n] for n in cut_names])
    cut_v = pack([mom_v[n] for n in cut_names])
    cut_rows = cut_w.shape[0]
    cut_parts = jnp.pad(cut_parts, ((0, 0), (0, cut_rows * LANES - cut_parts.shape[1])))
    outs = adam_update(cut_parts.reshape(1, N_DEV, cut_rows, LANES), cut_w[None], cut_m[None], cut_v[None],
                       name="adam_small_cut")
    for o_, dst in zip(outs, (grads, deltas, new_m, new_v)):
        for n, a in zip(cut_names, unpack(o_.reshape(-1), [weights[n] for n in cut_names])):
            dst[n] = a

    return (loss, grad_x, *[grads[n] for n in order], *[deltas[n] for n in order],
            *[new_m[n] for n in order], *[new_v[n] for n in order])
```

```python
import functools
import math

import jax
import jax.numpy as jnp
from jax import lax
from jax.experimental import pallas as pl
from jax.experimental.pallas import tpu as pltpu

F32 = jnp.float32
MXU_DTYPE = jnp.bfloat16
PAY_DTYPE = jnp.bfloat16
MESH = pl.DeviceIdType.MESH
N_DEV = 8
N_CHIP = 4

DEPTH = 4
CHUNK = 64
RET_HEADS = 8
MLA_HEADS = 16
MLA_Q_RANK = 512
MLA_KV_RANK = 512
MLA_NOPE = 128
MLA_ROPE = 64
MLA_V = 128
MLA_PAD = 256
XA_HEADS = 4
ROPE_BASE = 10000.0
LN_EPS = 1e-5
RMS_EPS = 1e-6
NEG_INF = -1e30
ADAM_LR = 0.001
ADAM_B1 = 0.9
ADAM_B2 = 0.999
ADAM_EPS = 1e-08
ADAM_WD = 0.01
ADAM_STEP = 10

LANES = 128
VMEM_LIMIT_BYTES = 48 * 1024 * 1024
TILE_M = 1024
TILE_N = 1024
TILE_K = 2048
SLAB_TILE_N = 1536
ATTN_TILE = 512
ROW_TILE = 256

MXU_FLOPS_PER_US = 7.0e8
LINK_BYTES_PER_US = 2.2e4
CHUNK_BYTES = 768 * 1024
HOST_CHUNKS = 6
SIDE_CHUNKS = 4
ATTN_US = {"ret": (125.0, 160.0), "mla": (320.0, 270.0), "xa": (35.0, 35.0)}
ROWWISE_US = {"ln": (25.0, 22.0), "conv": (65.0, 110.0)}


def _tile(n, cap, mult=LANES):
    best = None
    for t in range(mult, min(n, cap) + 1, mult):
        if n % t == 0:
            best = t
    return n if best is None else best


def _params(n_axes):
    return pltpu.CompilerParams(dimension_semantics=("arbitrary",) * n_axes,
                                vmem_limit_bytes=VMEM_LIMIT_BYTES)


def _place():
    return lax.axis_index("x"), lax.axis_index("y"), lax.axis_index("c")


def _other_chips(x, y):
    return [(1 - x, y), (x, 1 - y), (1 - x, 1 - y)]


def _remote(src, dst, send, recv, to):
    return pltpu.make_async_remote_copy(src_ref=src, dst_ref=dst, send_sem=send, recv_sem=recv,
                                        device_id=to, device_id_type=MESH)


class _Chunk:
    def __init__(self, kind, src, key, dst_shape, layer, r0, rc, cols, axis=None, k=None, n=None):
        self.kind, self.src, self.key, self.dst_shape = kind, src, key, dst_shape
        self.layer, self.r0, self.rc, self.axis, self.k, self.n = layer, r0, rc, axis, k, n
        self.nbytes = rc * cols * jnp.dtype(PAY_DTYPE).itemsize

    def copies(self, src_ref, dst_ref, send, recv, loc, i):
        x, y, c = _place()
        chips = _other_chips(x, y)
        rows = pl.ds(self.r0, self.rc)
        if self.kind == "gather":
            peers = [(x, y, 1 - c)] + [(px, py, c) for px, py in chips]
            src = src_ref.at[self.layer, rows, :]

            def slab(px, py, pc):
                idx = 4 * px + 2 * py + pc
                if self.axis == 1:
                    return dst_ref.at[pl.ds(idx * self.k + self.r0, self.rc), :]
                return dst_ref.at[rows, pl.ds(idx * self.n, self.n)]

            sends = [_remote(src, slab(x, y, c), send.at[4 * i + t], recv.at[4 * i + t], p)
                     for t, p in enumerate(peers)]
            recvs = [_remote(src, slab(*p), send.at[4 * i + t], recv.at[4 * i + t], p)
                     for t, p in enumerate(peers)]
            return sends, recvs, pltpu.make_async_copy(src, slab(x, y, c), loc.at[i])
        sibling = (x, y, 1 - c)
        if self.kind == "gather2":
            def slab(px, py, pc):
                idx = 4 * px + 2 * py + pc
                if self.axis == 1:
                    return dst_ref.at[pl.ds(idx * self.k, self.k), :]
                return dst_ref.at[:, pl.ds(idx * self.n, self.n)]

            sends = [_remote(slab(px, py, c), slab(px, py, c), send.at[4 * i + t], recv.at[4 * i + t], sibling)
                     for t, (px, py) in enumerate(chips)]
            recvs = [_remote(slab(px, py, 1 - c), slab(px, py, 1 - c), send.at[4 * i + t], recv.at[4 * i + t], sibling)
                     for t, (px, py) in enumerate(chips)]
            return sends, recvs, None
        if self.kind == "swap":
            sends = [_remote(src_ref.at[2 * j + (1 - c)], dst_ref.at[j], send.at[4 * i + j], recv.at[4 * i + j], sibling)
                     for j in range(N_CHIP)]
            recvs = [_remote(src_ref.at[2 * j + c], dst_ref.at[j], send.at[4 * i + j], recv.at[4 * i + j], sibling)
                     for j in range(N_CHIP)]
            return sends, recvs, None
        mine = 2 * x + y
        own = src_ref.at[mine, rows, :]
        sends = [_remote(src_ref.at[2 * px + py, rows, :], dst_ref.at[self.layer, mine, rows, :],
                         send.at[4 * i + t], recv.at[4 * i + t], (px, py, c)) for t, (px, py) in enumerate(chips)]
        recvs = [_remote(own, dst_ref.at[self.layer, 2 * px + py, rows, :],
                         send.at[4 * i + t], recv.at[4 * i + t], (px, py, c)) for t, (px, py) in enumerate(chips)]
        return sends, recvs, pltpu.make_async_copy(own, dst_ref.at[self.layer, mine, rows, :], loc.at[i])


def _call(body, *, grid, in_specs, out_specs, out_shape, operands, scratch=(), name, chunks=(), bufs=None):
    n_in, n_out, n_scr = len(operands), len(out_shape), len(scratch)
    params = _params(len(grid)) if grid else pltpu.CompilerParams(vmem_limit_bytes=VMEM_LIMIT_BYTES)
    if not chunks:
        return pl.pallas_call(
            body, out_shape=tuple(out_shape), grid=grid, in_specs=list(in_specs), out_specs=tuple(out_specs),
            scratch_shapes=list(scratch), compiler_params=params, name=name)(*operands)
    srcs, keys = [], []
    for ch in chunks:
        if ch.src is not None and not any(ch.src is s for s in srcs):
            srcs.append(ch.src)
        if ch.key not in keys:
            keys.append(ch.key)
    shape_of = {ch.key: ch.dst_shape for ch in chunks}
    held = [k for k in keys if bufs.get(k) is not None]
    extra = srcs + [bufs[k] for k in held]
    aliases = {n_in + len(srcs) + i: n_out + keys.index(k) for i, k in enumerate(held)}
    anywhere = pl.BlockSpec(memory_space=pl.ANY)
    n_extra, n_keys, n_ch = len(extra), len(keys), len(chunks)

    def hosted(*refs):
        src_refs = refs[n_in:n_in + len(srcs)]
        o0 = n_in + n_extra
        dst_refs = refs[o0 + n_out:o0 + n_out + n_keys]
        s0 = o0 + n_out + n_keys
        send, recv, loc = refs[s0 + n_scr:]
        sends, recvs, locs = [], [], []
        for i, ch in enumerate(chunks):
            src_ref = None if ch.src is None else src_refs[[ch.src is s for s in srcs].index(True)]
            s_, r_, l_ = ch.copies(src_ref, dst_refs[keys.index(ch.key)], send, recv, loc, i)
            sends += s_
            recvs += r_
            if l_ is not None:
                locs.append(l_)

        def start():
            for cp in locs + sends:
                cp.start()

        def finish():
            for cp in recvs:
                cp.wait_recv()
            for cp in sends:
                cp.wait_send()
            for cp in locs:
                cp.wait()

        if grid:
            first = functools.reduce(jnp.logical_and, [pl.program_id(a) == 0 for a in range(len(grid))])
            last = functools.reduce(jnp.logical_and, [pl.program_id(a) == grid[a] - 1 for a in range(len(grid))])
            pl.when(first)(start)
            body(*refs[:n_in], *refs[o0:o0 + n_out], *refs[s0:s0 + n_scr])
            pl.when(last)(finish)
        else:
            start()
            body(*refs[:n_in], *refs[o0:o0 + n_out], *refs[s0:s0 + n_scr])
            finish()

    res = pl.pallas_call(
        hosted, out_shape=tuple(out_shape) + tuple(jax.ShapeDtypeStruct(shape_of[k], PAY_DTYPE) for k in keys),
        grid=grid, in_specs=list(in_specs) + [anywhere] * n_extra, out_specs=tuple(out_specs) + (anywhere,) * n_keys,
        scratch_shapes=list(scratch) + [pltpu.SemaphoreType.DMA((4 * n_ch,)), pltpu.SemaphoreType.DMA((4 * n_ch,)),
                                        pltpu.SemaphoreType.DMA((n_ch,))],
        input_output_aliases=aliases, compiler_params=params, name=name)(*operands, *extra)
    for k, arr in zip(keys, res[n_out:]):
        bufs[k] = arr
    return res[:n_out]


class _Link:
    def __init__(self):
        self.queue, self.bufs = [], {}
        self.side = []
        self.after = []
        self.left = {}
        self.on_done = {}

    def push(self, kind, src, key, dst_shape, layer, rows, cols, done=None, **geometry):
        fits = [r for r in range(16, rows + 1, 16) if rows % r == 0 and r * cols * 2 <= CHUNK_BYTES]
        rc = max(fits) if fits else min(r for r in range(16, rows + 1, 16) if rows % r == 0)
        for r0 in range(0, rows, rc):
            self.queue.append(_Chunk(kind, src, key, dst_shape, layer, r0, rc, cols, **geometry))
        self.left[key] = self.left.get(key, 0) + rows // rc
        if done is not None:
            self.on_done[key] = done

    def settle(self):
        todo, self.after = self.after, []
        for f in todo:
            f()

    def _pop(self, count):
        out = []
        for ch in self.queue[:count]:
            self.left[ch.key] -= 1
            if self.left[ch.key] == 0 and ch.key in self.on_done:
                self.after.append(self.on_done.pop(ch.key))
            out.append(ch)
        self.queue = self.queue[count:]
        return out

    def _side(self):
        riders, self.side = self.side[:SIDE_CHUNKS], self.side[SIDE_CHUNKS:]
        self.after += [then for _, then in riders if then is not None]
        return [ch for ch, _ in riders]

    def take(self, est_us):
        self.settle()
        out = self._side()
        budget, count = est_us * LINK_BYTES_PER_US, 0
        while count < min(len(self.queue), HOST_CHUNKS) and self.queue[count].nbytes <= 2 * budget:
            budget -= self.queue[count].nbytes
            count += 1
        return out + self._pop(count)

    def flush(self, key, name):
        self.settle()
        if key is None:
            n = len(self.queue)
        else:
            n = max([i + 1 for i, ch in enumerate(self.queue) if ch.key == key], default=0)
        while n > 0:
            group = self._side() + self._pop(min(n, HOST_CHUNKS))
            n -= min(n, HOST_CHUNKS)
            _call(lambda: None, grid=(), in_specs=[], out_specs=(), out_shape=(), operands=[], name=name,
                  chunks=group, bufs=self.bufs)
            self.settle()

    def drain(self, name):
        self.settle()
        while self.queue or self.side:
            group = self._side() + self._pop(min(len(self.queue), HOST_CHUNKS))
            _call(lambda: None, grid=(), in_specs=[], out_specs=(), out_shape=(), operands=[], name=name,
                  chunks=group, bufs=self.bufs)
            self.settle()


def forward_to_sibling(w, axis, k, n, name):
    def body(w_in, w_ref, send, recv):
        x, y, c = _place()
        sibling = (x, y, 1 - c)

        def slab(px, py, pc):
            idx = 4 * px + 2 * py + pc
            return w_ref.at[pl.ds(idx * k, k), :] if axis == 1 else w_ref.at[:, pl.ds(idx * n, n)]

        chips = _other_chips(x, y)
        sends = [_remote(slab(px, py, c), slab(px, py, c), send.at[t], recv.at[t], sibling)
                 for t, (px, py) in enumerate(chips)]
        recvs = [_remote(slab(px, py, 1 - c), slab(px, py, 1 - c), send.at[t], recv.at[t], sibling)
                 for t, (px, py) in enumerate(chips)]
        for cp in sends:
            cp.start()
        for cp in recvs:
            cp.wait_recv()
        for cp in sends:
            cp.wait_send()

    return pl.pallas_call(
        body, out_shape=jax.ShapeDtypeStruct(w.shape, w.dtype),
        in_specs=[pl.BlockSpec(memory_space=pl.ANY)], out_specs=pl.BlockSpec(memory_space=pl.ANY),
        scratch_shapes=[pltpu.SemaphoreType.DMA((3,)), pltpu.SemaphoreType.DMA((3,))],
        input_output_aliases={0: 0}, name=name)(w)


def mm(a, b, *, mode="nn", out_dtype=F32, add=None, add_scale=1.0, slab_width=None, link=None, name):
    if mode == "nn":
        (M, K), N = a.shape, b.shape[-1]
    elif mode == "nt":
        (M, K), N = a.shape, b.shape[-2]
    else:
        (K, M), N = a.shape, b.shape[-1]
    tm, tk = _tile(M, TILE_M), _tile(K, TILE_K)
    tn = _tile(N, TILE_N) if slab_width is None else _tile(slab_width, SLAB_TILE_N)
    nk = K // tk
    dims = {"nn": (((1,), (0,)), ((), ())), "nt": (((1,), (1,)), ((), ())),
            "tn": (((0,), (0,)), ((), ()))}[mode]

    a_spec = (pl.BlockSpec((tk, tm), lambda i, j, k: (k, i)) if mode == "tn"
              else pl.BlockSpec((tm, tk), lambda i, j, k: (i, k)))
    b_spec = (pl.BlockSpec((tn, tk), lambda i, j, k: (j, k)) if mode == "nt"
              else pl.BlockSpec((tk, tn), lambda i, j, k: (k, j)))
    in_specs, operands = [a_spec, b_spec], [a, b]
    if add is not None:
        in_specs.append(pl.BlockSpec((tm, tn), lambda i, j, k: (i, j)))
        operands.append(add)
    if slab_width is None:
        out_shape = jax.ShapeDtypeStruct((M, N), out_dtype)
        out_spec = pl.BlockSpec((tm, tn), lambda i, j, k: (i, j))
    else:
        per = slab_width // tn
        out_shape = jax.ShapeDtypeStruct((N // slab_width, M, slab_width), out_dtype)
        out_spec = pl.BlockSpec((None, tm, tn), lambda i, j, k: (j // per, i, j % per))
    n_in = len(operands)

    def body(*refs):
        a_ref, b_ref = refs[0], refs[1]
        add_ref = refs[2] if add is not None else None
        o_ref = refs[n_in]
        k = pl.program_id(2)
        part = lax.dot_general(a_ref[...].astype(MXU_DTYPE), b_ref[...].astype(MXU_DTYPE),
                               dims, preferred_element_type=F32)

        def finish(r):
            if add_ref is not None:
                r = r + add_scale * add_ref[...].astype(F32)
            o_ref[...] = r.astype(o_ref.dtype)

        if nk == 1:
            finish(part)
        else:
            acc_ref = refs[n_in + 1]

            @pl.when(k == 0)
            def _():
                acc_ref[...] = part

            @pl.when(jnp.logical_and(k > 0, k < nk - 1))
            def _():
                acc_ref[...] += part

            @pl.when(k == nk - 1)
            def _():
                finish(acc_ref[...] + part)

    chunks = link.take(2.0 * M * N * K / MXU_FLOPS_PER_US) if link is not None else ()
    (out,) = _call(body, grid=(M // tm, N // tn, nk), in_specs=in_specs, out_specs=(out_spec,), out_shape=(out_shape,),
                   operands=operands, scratch=[pltpu.VMEM((tm, tn), F32)] if nk > 1 else [], name=name,
                   chunks=chunks, bufs=link.bufs if link is not None else None)
    return out


def ln_fwd(h, f, g, b, alpha, link, name):
    S, D = h.shape
    tm = _tile(S, ROW_TILE, 8)

    def body(h_ref, f_ref, g_ref, b_ref, y_ref, yb_ref, xhat_ref, rstd_ref):
        z = alpha * h_ref[...] + f_ref[...]
        mu = jnp.mean(z, axis=-1, keepdims=True)
        zc = z - mu
        var = jnp.mean(zc * zc, axis=-1, keepdims=True)
        rstd = lax.rsqrt(var + LN_EPS)
        xhat = zc * rstd
        y = xhat * g_ref[...] + b_ref[...]
        y_ref[...] = y
        yb_ref[...] = y.astype(yb_ref.dtype)
        xhat_ref[...] = xhat
        rstd_ref[...] = rstd

    row = pl.BlockSpec((tm, D), lambda i: (i, 0))
    vec = pl.BlockSpec((1, D), lambda i: (0, 0))
    return _call(
        body, grid=(S // tm,), in_specs=[row, row, vec, vec],
        out_specs=(row, row, row, pl.BlockSpec((tm, 1), lambda i: (i, 0))),
        out_shape=(jax.ShapeDtypeStruct((S, D), F32), jax.ShapeDtypeStruct((S, D), MXU_DTYPE),
                   jax.ShapeDtypeStruct((S, D), F32), jax.ShapeDtypeStruct((S, 1), F32)),
        operands=[h, f, g, b], name=name, chunks=link.take(ROWWISE_US["ln"][0]), bufs=link.bufs)


def ln_bwd(dy, xhat, rstd, g, link, name):
    S, D = dy.shape
    tm = _tile(S, ROW_TILE, 8)

    def body(dy_ref, xhat_ref, rstd_ref, g_ref, dz_ref, dzb_ref, dg_ref, db_ref):
        @pl.when(pl.program_id(0) == 0)
        def _():
            dg_ref[...] = jnp.zeros_like(dg_ref)
            db_ref[...] = jnp.zeros_like(db_ref)

        dy = dy_ref[...]
        xhat = xhat_ref[...]
        dxh = dy * g_ref[...]
        m1 = jnp.mean(dxh, axis=-1, keepdims=True)
        m2 = jnp.mean(dxh * xhat, axis=-1, keepdims=True)
        dz = rstd_ref[...] * (dxh - m1 - xhat * m2)
        dz_ref[...] = dz
        dzb_ref[...] = dz.astype(dzb_ref.dtype)
        dg_ref[...] += jnp.sum(dy * xhat, axis=0, keepdims=True)
        db_ref[...] += jnp.sum(dy, axis=0, keepdims=True)

    row = pl.BlockSpec((tm, D), lambda i: (i, 0))
    vec = pl.BlockSpec((1, D), lambda i: (0, 0))
    return _call(
        body, grid=(S // tm,), in_specs=[row, row, pl.BlockSpec((tm, 1), lambda i: (i, 0)), vec],
        out_specs=(row, row, vec, vec),
        out_shape=(jax.ShapeDtypeStruct((S, D), F32), jax.ShapeDtypeStruct((S, D), MXU_DTYPE),
                   jax.ShapeDtypeStruct((1, D), F32), jax.ShapeDtypeStruct((1, D), F32)),
        operands=[dy, xhat, rstd, g], name=name, chunks=link.take(ROWWISE_US["ln"][1]), bufs=link.bufs)


def rms_fwd(x, col, width, g, name):
    S = x.shape[0]
    tm = _tile(S, ROW_TILE, 8)

    def body(x_ref, g_ref, y_ref):
        xv = x_ref[...]
        r = lax.rsqrt(jnp.mean(xv * xv, axis=-1, keepdims=True) + RMS_EPS)
        y_ref[...] = (xv * r * g_ref[...]).astype(y_ref.dtype)

    return pl.pallas_call(
        body, out_shape=jax.ShapeDtypeStruct((S, width), MXU_DTYPE), grid=(S // tm,),
        in_specs=[pl.BlockSpec((tm, width), lambda i: (i, col)), pl.BlockSpec((1, width), lambda i: (0, 0))],
        out_specs=pl.BlockSpec((tm, width), lambda i: (i, 0)), compiler_params=_params(1), name=name)(x, g)


def rms_bwd(dy, x, col, width, g, name):
    S = x.shape[0]
    tm = _tile(S, ROW_TILE, 8)

    def body(dy_ref, x_ref, g_ref, dx_ref, dg_ref):
        @pl.when(pl.program_id(0) == 0)
        def _():
            dg_ref[...] = jnp.zeros_like(dg_ref)

        xv = x_ref[...]
        dy = dy_ref[...]
        r = lax.rsqrt(jnp.mean(xv * xv, axis=-1, keepdims=True) + RMS_EPS)
        dxn = dy * g_ref[...]
        m = jnp.mean(dxn * xv, axis=-1, keepdims=True)
        dx_ref[...] = r * (dxn - xv * (r * r * m))
        dg_ref[...] += jnp.sum(dy * xv * r, axis=0, keepdims=True)

    return pl.pallas_call(
        body, out_shape=(jax.ShapeDtypeStruct((S, width), F32), jax.ShapeDtypeStruct((1, width), F32)),
        grid=(S // tm,),
        in_specs=[pl.BlockSpec((tm, width), lambda i: (i, 0)), pl.BlockSpec((tm, width), lambda i: (i, col)),
                  pl.BlockSpec((1, width), lambda i: (0, 0))],
        out_specs=(pl.BlockSpec((tm, width), lambda i: (i, 0)), pl.BlockSpec((1, width), lambda i: (0, 0))),
        compiler_params=_params(1), name=name)(dy, x, g)


def ret_rope_fwd(proj, cos, sin, n_groups, half, k_from, k_scale, name):
    S = proj.shape[0]
    W = n_groups * 2 * half
    tm = _tile(S, ROW_TILE, 8)

    def body(x_ref, c_ref, s_ref, o_ref):
        c, s = c_ref[...], s_ref[...]
        for gi in range(n_groups):
            lo = gi * 2 * half
            x1 = x_ref[:, lo:lo + half]
            x2 = x_ref[:, lo + half:lo + 2 * half]
            sc = k_scale if gi >= k_from else 1.0
            o_ref[:, lo:lo + half] = ((x1 * c - x2 * s) * sc).astype(o_ref.dtype)
            o_ref[:, lo + half:lo + 2 * half] = ((x2 * c + x1 * s) * sc).astype(o_ref.dtype)

    tab = pl.BlockSpec((tm, half), lambda i: (i, 0))
    return pl.pallas_call(
        body, out_shape=jax.ShapeDtypeStruct((S, W), MXU_DTYPE), grid=(S // tm,),
        in_specs=[pl.BlockSpec((tm, W), lambda i: (i, 0)), tab, tab],
        out_specs=pl.BlockSpec((tm, W), lambda i: (i, 0)), compiler_params=_params(1), name=name)(proj, cos, sin)


def ret_rope_bwd(dq, dk, cos, sin, half, k_scale, name):
    S, Wq = dq.shape
    n_heads = Wq // (2 * half)
    tm = _tile(S, ROW_TILE, 8)

    def body(dq_ref, dk_ref, c_ref, s_ref, o_ref):
        c, s = c_ref[...], s_ref[...]
        for part, (d_ref, sc) in enumerate(((dq_ref, 1.0), (dk_ref, k_scale))):
            for hi in range(n_heads):
                lo = hi * 2 * half
                d1 = d_ref[:, lo:lo + half]
                d2 = d_ref[:, lo + half:lo + 2 * half]
                base = part * Wq + lo
                o_ref[:, base:base + half] = ((d1 * c + d2 * s) * sc).astype(o_ref.dtype)
                o_ref[:, base + half:base + 2 * half] = ((d2 * c - d1 * s) * sc).astype(o_ref.dtype)

    tab = pl.BlockSpec((tm, half), lambda i: (i, 0))
    row = pl.BlockSpec((tm, Wq), lambda i: (i, 0))
    return pl.pallas_call(
        body, out_shape=jax.ShapeDtypeStruct((S, 2 * Wq), MXU_DTYPE), grid=(S // tm,),
        in_specs=[row, row, tab, tab], out_specs=pl.BlockSpec((tm, 2 * Wq), lambda i: (i, 0)),
        compiler_params=_params(1), name=name)(dq, dk, cos, sin)


def mla_rope(x, ta, tb, n_heads, *, backward, head_sum, out_dtype, name):
    S = x.shape[0]
    W = n_heads * MLA_PAD
    tm = _tile(S, ROW_TILE, 8)

    def body(x_ref, a_ref, b_ref, o_ref, *rest):
        a, b = a_ref[...], b_ref[...]
        total = jnp.zeros((tm, LANES), F32)
        for hi in range(n_heads):
            lo = hi * MLA_PAD
            o_ref[:, lo:lo + MLA_NOPE] = x_ref[:, lo:lo + MLA_NOPE].astype(o_ref.dtype)
            t = x_ref[:, lo + MLA_NOPE:lo + MLA_PAD].astype(F32)
            if backward:
                r = t * a + pltpu.roll(t * b, LANES // 2, 1)
            else:
                r = t * a + pltpu.roll(t, LANES // 2, 1) * b
            o_ref[:, lo + MLA_NOPE:lo + MLA_PAD] = r.astype(o_ref.dtype)
            total = total + r
        if head_sum:
            rest[0][...] = total

    tab = pl.BlockSpec((tm, LANES), lambda i: (i, 0))
    row = pl.BlockSpec((tm, W), lambda i: (i, 0))
    out_shape = [jax.ShapeDtypeStruct((S, W), out_dtype)]
    out_specs = [row]
    if head_sum:
        out_shape.append(jax.ShapeDtypeStruct((S, LANES), F32))
        out_specs.append(tab)
    return pl.pallas_call(
        body, out_shape=tuple(out_shape), grid=(S // tm,), in_specs=[row, tab, tab],
        out_specs=tuple(out_specs), compiler_params=_params(1), name=name)(x, ta, tb)


def _visible(qi, kj, tq, tk):
    n = qi * tq + lax.broadcasted_iota(jnp.int32, (tq, tk), 0)
    m = kj * tk + lax.broadcasted_iota(jnp.int32, (tq, tk), 1)
    shift = CHUNK.bit_length() - 1
    vis = lax.shift_right_logical(m, shift) <= lax.shift_right_logical(n, shift)
    return vis, jnp.abs(n - m).astype(F32)


def attn_fwd(q, k, v, *, heads, softmax, masked, scale=1.0, log_gamma=None, link=None, est_us=0.0, name):
    (qa, dqk, q0, qs), (ka, _, k0, ks), (va, dv, v0, vs) = q, k, v
    Sq, Sk = qa.shape[0], ka.shape[0]
    tq, tk = _tile(Sq, ATTN_TILE, 8), _tile(Sk, ATTN_TILE, 8)
    nq, nk = Sq // tq, Sk // tk
    assert not masked or tq == tk

    def body(*refs):
        if softmax:
            q_ref, k_ref, v_ref, o_ref, lse_ref, m_ref, l_ref, acc_ref = refs
        else:
            q_ref, k_ref, v_ref, lg_ref, o_ref, acc_ref = refs
        qi, kj = pl.program_id(1), pl.program_id(2)

        @pl.when(kj == 0)
        def _():
            acc_ref[...] = jnp.zeros_like(acc_ref)
            if softmax:
                m_ref[...] = jnp.full_like(m_ref, NEG_INF)
                l_ref[...] = jnp.zeros_like(l_ref)

        def step(diagonal):
            s = lax.dot_general(q_ref[...].astype(MXU_DTYPE), k_ref[...].astype(MXU_DTYPE),
                                (((1,), (1,)), ((), ())), preferred_element_type=F32)
            vb = v_ref[...].astype(MXU_DTYPE)
            if diagonal or not softmax:
                vis, dist = _visible(qi, kj, tq, tk)
            if softmax:
                s = s * scale
                if diagonal:
                    s = jnp.where(vis, s, NEG_INF)
                m_old = m_ref[...]
                m_new = jnp.maximum(m_old, jnp.max(s, axis=-1, keepdims=True))
                p = jnp.exp(s - m_new)
                corr = jnp.exp(m_old - m_new)
                l_ref[...] = corr * l_ref[...] + jnp.sum(p, axis=-1, keepdims=True)
                acc_ref[...] = corr * acc_ref[...] + jnp.dot(p.astype(MXU_DTYPE), vb, preferred_element_type=F32)
                m_ref[...] = m_new
            else:
                decay = jnp.exp(lg_ref[0:1, 0:1] * dist)
                if diagonal:
                    decay = jnp.where(vis, decay, 0.0)
                p = s * decay
                acc_ref[...] += jnp.dot(p.astype(MXU_DTYPE), vb, preferred_element_type=F32)

        if masked:
            pl.when(kj < qi)(functools.partial(step, False))
            pl.when(kj == qi)(functools.partial(step, True))
        else:
            step(False)

        @pl.when(kj == nk - 1)
        def _():
            if softmax:
                o_ref[...] = (acc_ref[...] / l_ref[...]).astype(o_ref.dtype)
                lse_ref[...] = m_ref[...] + jnp.log(l_ref[...])
            else:
                o_ref[...] = acc_ref[...].astype(o_ref.dtype)

    kcap = (lambda qi, kj: jnp.minimum(kj, qi)) if masked else (lambda qi, kj: kj)
    in_specs = [pl.BlockSpec((tq, dqk), lambda h, qi, kj: (qi, q0 + qs * h)),
                pl.BlockSpec((tk, dqk), lambda h, qi, kj: (kcap(qi, kj), k0 + ks * h)),
                pl.BlockSpec((tk, dv), lambda h, qi, kj: (kcap(qi, kj), v0 + vs * h))]
    operands = [qa, ka, va]
    out_shape = [jax.ShapeDtypeStruct((Sq, heads * dv), F32)]
    out_specs = [pl.BlockSpec((tq, dv), lambda h, qi, kj: (qi, h))]
    scratch = []
    if softmax:
        out_shape.append(jax.ShapeDtypeStruct((heads, Sq, 1), F32))
        out_specs.append(pl.BlockSpec((None, tq, 1), lambda h, qi, kj: (h, qi, 0)))
        scratch += [pltpu.VMEM((tq, 1), F32), pltpu.VMEM((tq, 1), F32)]
    else:
        in_specs.append(pl.BlockSpec((None, 8, LANES), lambda h, qi, kj: (h, 0, 0)))
        operands.append(log_gamma)
    scratch.append(pltpu.VMEM((tq, dv), F32))
    chunks = link.take(est_us) if link is not None else ()
    return _call(body, grid=(heads, nq, nk), in_specs=in_specs, out_specs=out_specs, out_shape=out_shape,
                 operands=operands, scratch=scratch, name=name, chunks=chunks,
                 bufs=link.bufs if link is not None else None)


def attn_bwd(q, k, v, do, *, heads, softmax, masked, scale=1.0, log_gamma=None, o=None, lse=None,
             link=None, est_us=0.0, name):
    (qa, dqk, q0, qs), (ka, _, k0, ks), (va, dv, v0, vs) = q, k, v
    Sq, Sk = qa.shape[0], ka.shape[0]
    tq, tk = _tile(Sq, ATTN_TILE, 8), _tile(Sk, ATTN_TILE, 8)
    nq, nk = Sq // tq, Sk // tk
    assert not masked or tq == tk
    contract0 = (((0,), (0,)), ((), ()))

    def body(*refs):
        if softmax:
            q_ref, k_ref, v_ref, do_ref, o_ref, lse_ref, dq_ref, dk_ref, dv_ref = refs
        else:
            q_ref, k_ref, v_ref, do_ref, lg_ref, dq_ref, dk_ref, dv_ref = refs
        kj, qi = pl.program_id(1), pl.program_id(2)

        @pl.when(jnp.logical_and(kj == 0, qi == 0))
        def _():
            dq_ref[...] = jnp.zeros_like(dq_ref)

        @pl.when(qi == 0)
        def _():
            dk_ref[...] = jnp.zeros_like(dk_ref)
            dv_ref[...] = jnp.zeros_like(dv_ref)

        def step(diagonal):
            qb = q_ref[...].astype(MXU_DTYPE)
            kb = k_ref[...].astype(MXU_DTYPE)
            vb = v_ref[...].astype(MXU_DTYPE)
            dob = do_ref[...].astype(MXU_DTYPE)
            s = lax.dot_general(qb, kb, (((1,), (1,)), ((), ())), preferred_element_type=F32)
            dp = lax.dot_general(dob, vb, (((1,), (1,)), ((), ())), preferred_element_type=F32)
            if diagonal or not softmax:
                vis, dist = _visible(qi, kj, tq, tk)
            if softmax:
                s = s * scale
                if diagonal:
                    s = jnp.where(vis, s, NEG_INF)
                p = jnp.exp(s - lse_ref[...])
                delta = jnp.sum(do_ref[...].astype(F32) * o_ref[...], axis=-1, keepdims=True)
                ds = p * (dp - delta) * scale
            else:
                decay = jnp.exp(lg_ref[0:1, 0:1] * dist)
                if diagonal:
                    decay = jnp.where(vis, decay, 0.0)
                p = s * decay
                ds = dp * decay
            pb = p.astype(MXU_DTYPE)
            dsb = ds.astype(MXU_DTYPE)
            dv_ref[...] += lax.dot_general(pb, dob, contract0, preferred_element_type=F32)
            dk_ref[...] += lax.dot_general(dsb, qb, contract0, preferred_element_type=F32)
            rows = pl.ds(pl.multiple_of(qi * tq, tq), tq)
            dq_ref[rows, :] += jnp.dot(dsb, kb, preferred_element_type=F32)

        if masked:
            pl.when(qi > kj)(functools.partial(step, False))
            pl.when(qi == kj)(functools.partial(step, True))
        else:
            step(False)

    qcap = (lambda kj, qi: jnp.maximum(qi, kj)) if masked else (lambda kj, qi: qi)
    in_specs = [pl.BlockSpec((tq, dqk), lambda h, kj, qi: (qcap(kj, qi), q0 + qs * h)),
                pl.BlockSpec((tk, dqk), lambda h, kj, qi: (kj, k0 + ks * h)),
                pl.BlockSpec((tk, dv), lambda h, kj, qi: (kj, v0 + vs * h)),
                pl.BlockSpec((tq, dv), lambda h, kj, qi: (qcap(kj, qi), h))]
    operands = [qa, ka, va, do]
    if softmax:
        in_specs += [pl.BlockSpec((tq, dv), lambda h, kj, qi: (qcap(kj, qi), h)),
                     pl.BlockSpec((None, tq, 1), lambda h, kj, qi: (h, qcap(kj, qi), 0))]
        operands += [o, lse]
    else:
        in_specs.append(pl.BlockSpec((None, 8, LANES), lambda h, kj, qi: (h, 0, 0)))
        operands.append(log_gamma)
    chunks = link.take(est_us) if link is not None else ()
    return _call(
        body, grid=(heads, nk, nq), in_specs=in_specs,
        out_specs=(pl.BlockSpec((Sq, dqk), lambda h, kj, qi: (0, h)),
                   pl.BlockSpec((tk, dqk), lambda h, kj, qi: (kj, h)),
                   pl.BlockSpec((tk, dv), lambda h, kj, qi: (kj, h))),
        out_shape=(jax.ShapeDtypeStruct((Sq, heads * dqk), F32), jax.ShapeDtypeStruct((Sk, heads * dqk), F32),
                   jax.ShapeDtypeStruct((Sk, heads * dv), F32)),
        operands=operands, name=name, chunks=chunks, bufs=link.bufs if link is not None else None)


def _sigmoid(x):
    return 1.0 / (1.0 + jnp.exp(-x))


def ret_gate_fwd(o, proj, gate_col, gn, heads, name):
    S, W = o.shape
    dv = W // heads
    tm = _tile(S, ROW_TILE // 2, 8)

    def body(o_ref, g_ref, gn_ref, y_ref):
        for hi in range(heads):
            cols = slice(hi * dv, (hi + 1) * dv)
            oh = o_ref[:, cols]
            mu = jnp.mean(oh, axis=-1, keepdims=True)
            oc = oh - mu
            rstd = lax.rsqrt(jnp.mean(oc * oc, axis=-1, keepdims=True) + LN_EPS)
            gt = g_ref[:, cols]
            y_ref[:, cols] = (gt * _sigmoid(gt) * (oc * rstd) * gn_ref[:, cols]).astype(y_ref.dtype)

    row = pl.BlockSpec((tm, W), lambda i: (i, 0))
    return pl.pallas_call(
        body, out_shape=jax.ShapeDtypeStruct((S, W), MXU_DTYPE), grid=(S // tm,),
        in_specs=[row, pl.BlockSpec((tm, W), lambda i: (i, gate_col)), pl.BlockSpec((1, W), lambda i: (0, 0))],
        out_specs=row, compiler_params=_params(1), name=name)(o, proj, gn)


def ret_gate_bwd(dy, o, proj, gate_col, gn, heads, name):
    S, W = o.shape
    dv = W // heads
    tm = _tile(S, ROW_TILE // 2, 8)

    def body(dy_ref, o_ref, g_ref, gn_ref, do_ref, dgt_ref, dgn_ref):
        @pl.when(pl.program_id(0) == 0)
        def _():
            dgn_ref[...] = jnp.zeros_like(dgn_ref)

        for hi in range(heads):
            cols = slice(hi * dv, (hi + 1) * dv)
            oh = o_ref[:, cols]
            mu = jnp.mean(oh, axis=-1, keepdims=True)
            oc = oh - mu
            rstd = lax.rsqrt(jnp.mean(oc * oc, axis=-1, keepdims=True) + LN_EPS)
            xhat = oc * rstd
            gt = g_ref[:, cols]
            sg = _sigmoid(gt)
            gain = gn_ref[:, cols]
            dy = dy_ref[:, cols]
            dgt_ref[:, cols] = (dy * xhat * gain * (sg * (1.0 + gt * (1.0 - sg)))).astype(dgt_ref.dtype)
            dn = dy * (gt * sg)
            dgn_ref[:, cols] += jnp.sum(dn * xhat, axis=0, keepdims=True)
            dxh = dn * gain
            m1 = jnp.mean(dxh, axis=-1, keepdims=True)
            m2 = jnp.mean(dxh * xhat, axis=-1, keepdims=True)
            do_ref[:, cols] = (rstd * (dxh - m1 - xhat * m2)).astype(do_ref.dtype)

    row = pl.BlockSpec((tm, W), lambda i: (i, 0))
    vec = pl.BlockSpec((1, W), lambda i: (0, 0))
    return pl.pallas_call(
        body,
        out_shape=(jax.ShapeDtypeStruct((S, W), MXU_DTYPE), jax.ShapeDtypeStruct((S, W), MXU_DTYPE),
                   jax.ShapeDtypeStruct((1, W), F32)),
        grid=(S // tm,), in_specs=[row, row, pl.BlockSpec((tm, W), lambda i: (i, gate_col)), vec],
        out_specs=(row, row, vec), compiler_params=_params(1), name=name)(dy, o, proj, gn)


def _shift_down(x, s):
    rows = lax.broadcasted_iota(jnp.int32, x.shape, 0)
    return jnp.where(rows >= s, pltpu.roll(x, s, 0), 0.0)


def _shift_up(x, s):
    n = x.shape[0]
    rows = lax.broadcasted_iota(jnp.int32, x.shape, 0)
    return jnp.where(rows < n - s, pltpu.roll(x, n - s, 0), 0.0)


def _conv3(x, w_ref, b_ref):
    return (w_ref[2:3, :] * x + w_ref[1:2, :] * _shift_down(x, 1) + w_ref[0:1, :] * _shift_down(x, 2)
            + b_ref[...])


def conv_glu_fwd(hup, w, b, link, name):
    S, W2 = hup.shape
    F = W2 // 2
    tc = _tile(F, LANES)
    nb = F // tc

    def body(g_ref, v_ref, wg_ref, wv_ref, bg_ref, bv_ref, u_ref):
        cg = _conv3(g_ref[...], wg_ref, bg_ref)
        cv = _conv3(v_ref[...], wv_ref, bv_ref)
        u_ref[...] = (cg * _sigmoid(cg) * cv).astype(u_ref.dtype)

    def col(rows, off):
        return pl.BlockSpec((rows, tc), lambda j: (0, j + off))

    (u,) = _call(
        body, grid=(nb,), in_specs=[col(S, 0), col(S, nb), col(3, 0), col(3, nb), col(1, 0), col(1, nb)],
        out_specs=(col(S, 0),), out_shape=(jax.ShapeDtypeStruct((S, F), MXU_DTYPE),),
        operands=[hup, hup, w, w, b, b], name=name, chunks=link.take(ROWWISE_US["conv"][0]), bufs=link.bufs)
    return u


def conv_glu_bwd(du, hup, w, b, link, name):
    S, W2 = hup.shape
    F = W2 // 2
    tc = _tile(F, LANES)
    nb = F // tc

    def back(dc, x, w_ref, dh_ref, dw_ref, db_ref):
        dw_ref[2:3, :] = jnp.sum(dc * x, axis=0, keepdims=True)
        dw_ref[1:2, :] = jnp.sum(dc * _shift_down(x, 1), axis=0, keepdims=True)
        dw_ref[0:1, :] = jnp.sum(dc * _shift_down(x, 2), axis=0, keepdims=True)
        db_ref[...] = jnp.sum(dc, axis=0, keepdims=True)
        dh_ref[...] = (w_ref[2:3, :] * dc + w_ref[1:2, :] * _shift_up(dc, 1)
                       + w_ref[0:1, :] * _shift_up(dc, 2)).astype(dh_ref.dtype)

    def body(du_ref, g_ref, v_ref, wg_ref, wv_ref, bg_ref, bv_ref,
             dhg_ref, dhv_ref, dwg_ref, dwv_ref, dbg_ref, dbv_ref):
        xg, xv = g_ref[...], v_ref[...]
        cg = _conv3(xg, wg_ref, bg_ref)
        cv = _conv3(xv, wv_ref, bv_ref)
        sg = _sigmoid(cg)
        du = du_ref[...]
        back(du * cv * (sg * (1.0 + cg * (1.0 - sg))), xg, wg_ref, dhg_ref, dwg_ref, dbg_ref)
        back(du * (cg * sg), xv, wv_ref, dhv_ref, dwv_ref, dbv_ref)

    def col(rows, off):
        return pl.BlockSpec((rows, tc), lambda j: (0, j + off))

    return _call(
        body, grid=(nb,),
        in_specs=[col(S, 0), col(S, 0), col(S, nb), col(3, 0), col(3, nb), col(1, 0), col(1, nb)],
        out_specs=(col(S, 0), col(S, 0), col(3, 0), col(3, 0), col(1, 0), col(1, 0)),
        out_shape=(jax.ShapeDtypeStruct((S, F), MXU_DTYPE), jax.ShapeDtypeStruct((S, F), MXU_DTYPE),
                   jax.ShapeDtypeStruct((3, F), F32), jax.ShapeDtypeStruct((3, F), F32),
                   jax.ShapeDtypeStruct((1, F), F32), jax.ShapeDtypeStruct((1, F), F32)),
        operands=[du, hup, hup, w, w, b, b], name=name, chunks=link.take(ROWWISE_US["conv"][1]), bufs=link.bufs)


def loss_head(y, target, name):
    S, D = y.shape
    tm = _tile(S, ROW_TILE, 8)

    def body(y_ref, t_ref, dy_ref, loss_ref):
        @pl.when(pl.program_id(0) == 0)
        def _():
            loss_ref[...] = jnp.zeros_like(loss_ref)

        e = y_ref[...] - t_ref[...]
        dy_ref[...] = e * (1.0 / D)
        part = jnp.sum(jnp.sum(e * e, axis=-1, keepdims=True), axis=0, keepdims=True) * (0.5 / D)
        loss_ref[...] += jnp.broadcast_to(part, loss_ref.shape)

    row = pl.BlockSpec((tm, D), lambda i: (i, 0))
    return pl.pallas_call(
        body, out_shape=(jax.ShapeDtypeStruct((S, D), F32), jax.ShapeDtypeStruct((8, LANES), F32)),
        grid=(S // tm,), in_specs=[row, row], out_specs=(row, pl.BlockSpec((8, LANES), lambda i: (0, 0))),
        compiler_params=_params(1), name=name)(y, target)


def adam_update(parts, w, m, v, name):
    n_layers, n_parts, R, C = parts.shape
    tr = _tile(R, max(8, (1 << 19) // C), 8)
    c1 = 1.0 - ADAM_B1 ** ADAM_STEP
    c2 = 1.0 - ADAM_B2 ** ADAM_STEP

    def body(p_ref, w_ref, m_ref, v_ref, g_ref, d_ref, nm_ref, nv_ref):
        g = p_ref[0].astype(F32)
        for pi in range(1, n_parts):
            g = g + p_ref[pi].astype(F32)
        nm = ADAM_B1 * m_ref[...] + (1.0 - ADAM_B1) * g
        nv = ADAM_B2 * v_ref[...] + (1.0 - ADAM_B2) * (g * g)
        g_ref[...] = g
        nm_ref[...] = nm
        nv_ref[...] = nv
        d_ref[...] = -ADAM_LR * ((nm / c1) / (jnp.sqrt(nv / c2) + ADAM_EPS) + ADAM_WD * w_ref[...])

    row = pl.BlockSpec((None, tr, C), lambda l, i: (l, i, 0))
    out = jax.ShapeDtypeStruct((n_layers, R, C), F32)
    return pl.pallas_call(
        body, out_shape=(out, out, out, out), grid=(n_layers, R // tr),
        in_specs=[pl.BlockSpec((None, n_parts, tr, C), lambda l, i: (l, 0, i, 0)), row, row, row],
        out_specs=(row, row, row, row), compiler_params=_params(2), name=name)(parts, w, m, v)


def pair_sum(own, got, core, name):
    _, R, C = own.shape
    tr = _tile(R, max(16, (1 << 19) // C), 16)

    def body(core_ref, a_ref, b_ref, o_ref):
        o_ref[...] = (a_ref[...].astype(F32) + b_ref[...].astype(F32)).astype(o_ref.dtype)

    grid_spec = pltpu.PrefetchScalarGridSpec(
        num_scalar_prefetch=1, grid=(N_CHIP, R // tr),
        in_specs=[pl.BlockSpec((None, tr, C), lambda j, i, core_ref: (2 * j + core_ref[0], i, 0)),
                  pl.BlockSpec((None, tr, C), lambda j, i, core_ref: (j, i, 0))],
        out_specs=pl.BlockSpec((None, tr, C), lambda j, i, core_ref: (j, i, 0)))
    return pl.pallas_call(
        body, out_shape=jax.ShapeDtypeStruct((N_CHIP, R, C), own.dtype), grid_spec=grid_spec,
        compiler_params=_params(2), name=name)(core, own, got)


def gather_small(block, name):
    m_per, n = block.shape

    def body(x_ref, out_ref, send_sems, recv_sems, local_sem):
        x, y, c = _place()
        me, sibling = (x, y, c), (x, y, 1 - c)
        chips = _other_chips(x, y)

        def rows(px, py, pc):
            return out_ref.at[pl.ds((4 * px + 2 * py + pc) * m_per, m_per), :]

        def copy(k, blk, to, src=None):
            return pltpu.make_async_remote_copy(
                src_ref=rows(*blk) if src is None else src, dst_ref=rows(*blk),
                send_sem=send_sems.at[k], recv_sem=recv_sems.at[k], device_id=to, device_id_type=MESH)

        mine = pltpu.make_async_copy(x_ref, rows(*me), local_sem)
        mine.start()
        first = [copy(0, me, sibling, src=x_ref)]
        first += [copy(1 + j, me, (*chip, c), src=x_ref) for j, chip in enumerate(chips)]
        for cp in first:
            cp.start()
        passed = [copy(4 + j, (*chip, c), sibling) for j, chip in enumerate(chips)]
        for j, chip in enumerate(chips):
            copy(1 + j, (*chip, c), me).wait_recv()
            passed[j].start()
        copy(0, sibling, me).wait_recv()
        for j, chip in enumerate(chips):
            copy(4 + j, (*chip, 1 - c), me).wait_recv()
        for cp in first + passed:
            cp.wait_send()
        mine.wait()

    return pl.pallas_call(
        body, out_shape=jax.ShapeDtypeStruct((N_DEV * m_per, n), block.dtype),
        in_specs=[pl.BlockSpec(memory_space=pltpu.VMEM)], out_specs=pl.BlockSpec(memory_space=pltpu.VMEM),
        scratch_shapes=[pltpu.SemaphoreType.DMA((7,)), pltpu.SemaphoreType.DMA((7,)), pltpu.SemaphoreType.DMA],
        compiler_params=pltpu.CompilerParams(vmem_limit_bytes=VMEM_LIMIT_BYTES), name=name)(block)


def _rope_tables(positions, d):
    inv_freq = ROPE_BASE ** (-jnp.arange(0, d, 2, dtype=F32) / d)
    ang = positions.astype(F32)[:, None] * inv_freq
    return jnp.cos(ang), jnp.sin(ang)


def _mla_pad(nope, rope):
    S, H, _ = nope.shape
    half = MLA_ROPE // 2
    z = jnp.zeros((S, H, LANES // 2 - half), nope.dtype)
    return jnp.concatenate([nope, rope[..., :half], z, rope[..., half:], z], axis=2).reshape(S, H * MLA_PAD)


def _mla_unpad(x, H):
    S = x.shape[0]
    half = MLA_ROPE // 2
    x3 = x.reshape(S, H, MLA_PAD)
    rope = jnp.concatenate([x3[..., MLA_NOPE:MLA_NOPE + half],
                            x3[..., MLA_NOPE + LANES // 2:MLA_NOPE + LANES // 2 + half]], axis=2)
    return x3[..., :MLA_NOPE], rope


def kernel(x, mem, positions, ret_w_in, ret_gn_g, ret_w_out, mla_w_in, mla_q_norm_g, mla_w_uq, mla_kv_norm_g, mla_w_ukv, mla_w_out, xa_w_q, xa_w_kv, xa_w_out, ffn_w_up, ffn_conv_w, ffn_conv_b, ffn_w_down, ln_mix_g, ln_mix_b, ln_mem_g, ln_mem_b, ln_ffn_g, ln_ffn_b, loss_target, m_ret_w_in, m_ret_gn_g, m_ret_w_out, m_mla_w_in, m_mla_q_norm_g, m_mla_w_uq, m_mla_kv_norm_g, m_mla_w_ukv, m_mla_w_out, m_xa_w_q, m_xa_w_kv, m_xa_w_out, m_ffn_w_up, m_ffn_conv_w, m_ffn_conv_b, m_ffn_w_down, m_ln_mix_g, m_ln_mix_b, m_ln_mem_g, m_ln_mem_b, m_ln_ffn_g, m_ln_ffn_b, v_ret_w_in, v_ret_gn_g, v_ret_w_out, v_mla_w_in, v_mla_q_norm_g, v_mla_w_uq, v_mla_kv_norm_g, v_mla_w_ukv, v_mla_w_out, v_xa_w_q, v_xa_w_kv, v_xa_w_out, v_ffn_w_up, v_ffn_conv_w, v_ffn_conv_b, v_ffn_w_down, v_ln_mix_g, v_ln_mix_b, v_ln_mem_g, v_ln_mem_b, v_ln_ffn_g, v_ln_ffn_b):
    weights = dict(ret_w_in=ret_w_in, ret_gn_g=ret_gn_g, ret_w_out=ret_w_out, mla_w_in=mla_w_in,
                   mla_q_norm_g=mla_q_norm_g, mla_w_uq=mla_w_uq, mla_kv_norm_g=mla_kv_norm_g,
                   mla_w_ukv=mla_w_ukv, mla_w_out=mla_w_out, xa_w_q=xa_w_q, xa_w_kv=xa_w_kv, xa_w_out=xa_w_out,
                   ffn_w_up=ffn_w_up, ffn_conv_w=ffn_conv_w, ffn_conv_b=ffn_conv_b, ffn_w_down=ffn_w_down,
                   ln_mix_g=ln_mix_g, ln_mix_b=ln_mix_b, ln_mem_g=ln_mem_g, ln_mem_b=ln_mem_b,
                   ln_ffn_g=ln_ffn_g, ln_ffn_b=ln_ffn_b)
    mom_m = dict(ret_w_in=m_ret_w_in, ret_gn_g=m_ret_gn_g, ret_w_out=m_ret_w_out, mla_w_in=m_mla_w_in,
                 mla_q_norm_g=m_mla_q_norm_g, mla_w_uq=m_mla_w_uq, mla_kv_norm_g=m_mla_kv_norm_g,
                 mla_w_ukv=m_mla_w_ukv, mla_w_out=m_mla_w_out, xa_w_q=m_xa_w_q, xa_w_kv=m_xa_w_kv,
                 xa_w_out=m_xa_w_out, ffn_w_up=m_ffn_w_up, ffn_conv_w=m_ffn_conv_w, ffn_conv_b=m_ffn_conv_b,
                 ffn_w_down=m_ffn_w_down, ln_mix_g=m_ln_mix_g, ln_mix_b=m_ln_mix_b, ln_mem_g=m_ln_mem_g,
                 ln_mem_b=m_ln_mem_b, ln_ffn_g=m_ln_ffn_g, ln_ffn_b=m_ln_ffn_b)
    mom_v = dict(ret_w_in=v_ret_w_in, ret_gn_g=v_ret_gn_g, ret_w_out=v_ret_w_out, mla_w_in=v_mla_w_in,
                 mla_q_norm_g=v_mla_q_norm_g, mla_w_uq=v_mla_w_uq, mla_kv_norm_g=v_mla_kv_norm_g,
                 mla_w_ukv=v_mla_w_ukv, mla_w_out=v_mla_w_out, xa_w_q=v_xa_w_q, xa_w_kv=v_xa_w_kv,
                 xa_w_out=v_xa_w_out, ffn_w_up=v_ffn_w_up, ffn_conv_w=v_ffn_conv_w, ffn_conv_b=v_ffn_conv_b,
                 ffn_w_down=v_ffn_w_down, ln_mix_g=v_ln_mix_g, ln_mix_b=v_ln_mix_b, ln_mem_g=v_ln_mem_g,
                 ln_mem_b=v_ln_mem_b, ln_ffn_g=v_ln_ffn_g, ln_ffn_b=v_ln_ffn_b)
    order = list(weights)
    BIG = dict(ret_w_in=2, ret_w_out=1, mla_w_in=1, mla_w_uq=2, mla_w_ukv=2, mla_w_out=1,
               xa_w_q=1, xa_w_kv=2, xa_w_out=1, ffn_w_up=2, ffn_w_down=1)
    SMALL_CUT = ("ffn_conv_w", "mla_q_norm_g", "mla_kv_norm_g")
    REPLICATED = ("ret_gn_g", "ffn_conv_b", "ln_mix_g", "ln_mix_b", "ln_mem_g", "ln_mem_b", "ln_ffn_g", "ln_ffn_b")

    x = x[0]
    mem = mem[0]
    positions = positions[0]
    target = loss_target[0]
    S, D = x.shape
    depth = ln_mix_g.shape[0]
    alpha = (2 * depth) ** 0.25
    ret_dk = D // RET_HEADS
    ret_dv = 2 * D // RET_HEADS
    ret_qkw = RET_HEADS * ret_dk
    ret_vw = RET_HEADS * ret_dv
    xa_d = D // XA_HEADS
    assert 2 * ret_qkw == ret_vw and MLA_NOPE == LANES and MLA_V == LANES and MLA_ROPE == LANES // 2
    assert MLA_Q_RANK == MLA_KV_RANK and CHUNK & (CHUNK - 1) == 0
    core = lax.axis_index("c").astype(jnp.int32).reshape(1)
    dev = 4 * lax.axis_index("x") + 2 * lax.axis_index("y") + lax.axis_index("c")

    gather, scatter = _Link(), _Link()
    shard_b = {n: weights[n].astype(PAY_DTYPE) for n in BIG}
    staged = set()

    def units(layer):
        mixer = ("ret_w_in", "ret_w_out") if layer % 2 == 0 else ("mla_w_in", "mla_w_uq", "mla_w_ukv", "mla_w_out")
        return [(n, layer // 2) for n in mixer] + [(n, layer) for n in
                                                    ("xa_w_q", "xa_w_kv", "xa_w_out", "ffn_w_up", "ffn_w_down")]

    def geometry(n):
        _, k, nn = weights[n].shape
        return BIG[n], k, nn

    def second_level(key, full, axis, k, nn):
        def go():
            chunk = _Chunk("gather2", None, key, full, 0, 0, k, nn, axis=axis, k=k, n=nn)
            gather.side.append((chunk, lambda: staged.add(key)))
        return go

    for layer in range(depth):
        for n, j in units(layer):
            axis, k, nn = geometry(n)
            full = (N_DEV * k, nn) if axis == 1 else (k, N_DEV * nn)
            gather.push("gather", shard_b[n], (n, j), full, j, k, nn, done=second_level((n, j), full, axis, k, nn),
                        axis=axis, k=k, n=nn)

    def weight(n, j):
        key = (n, j)
        if key not in staged:
            gather.flush(key, name="gather_rest")
            gather.settle()
        if key not in staged:
            gather.side = [(ch, then) for ch, then in gather.side if ch.key != key]
            axis, k, nn = geometry(n)
            gather.bufs[key] = forward_to_sibling(gather.bufs[key], axis, k, nn, name="gather_level2")
            staged.add(key)
        return gather.bufs[key]

    def pack(arrays):
        flat = jnp.concatenate([a.reshape(-1) for a in arrays])
        rows = -(-flat.shape[0] // (8 * LANES)) * 8
        return jnp.pad(flat, (0, rows * LANES - flat.shape[0])).reshape(rows, LANES)

    def unpack(flat, like):
        out, at = [], 0
        for a in like:
            out.append(flat[at:at + a.size].reshape(a.shape))
            at += a.size
        return out

    small_local = [weights[n] for n in SMALL_CUT]
    blk = pack(small_local)
    allsmall = gather_small(blk, name="gather_small_weights").reshape(N_DEV, -1)
    per_dev = [unpack(allsmall[d], small_local) for d in range(N_DEV)]
    conv_w_full, qg_full, kvg_full = [jnp.concatenate([per_dev[d][i] for d in range(N_DEV)], axis=-1)
                                      for i in range(len(SMALL_CUT))]

    rcos, rsin = _rope_tables(positions, ret_dk)
    mcos, msin = _rope_tables(positions, MLA_ROPE)
    zq = jnp.zeros((S, LANES // 2 - MLA_ROPE // 2), F32)
    mla_ta = jnp.concatenate([mcos, zq, mcos, zq], axis=1)
    mla_tb = jnp.concatenate([-msin, zq, msin, zq], axis=1)
    log_gamma = jnp.log(1.0 - 2.0 ** (-5.0 - jnp.arange(RET_HEADS, dtype=F32)))
    log_gamma = jnp.broadcast_to(log_gamma[:, None, None], (RET_HEADS, 8, LANES))
    mla_scale = (MLA_NOPE + MLA_ROPE) ** -0.5
    xa_scale = xa_d ** -0.5
    mem_b = mem.astype(MXU_DTYPE)

    def vec(a, l):
        return a[l][None, :]

    saved = []
    h = x
    hb = x.astype(MXU_DTYPE)
    for layer in range(depth):
        j = layer // 2
        sv = {}
        sv["hb_mix"] = hb
        if layer % 2 == 0:
            proj = mm(hb, weight("ret_w_in", j), link=gather, name="ret_in")
            qk = ret_rope_fwd(proj, rcos, rsin, 2 * RET_HEADS, ret_dk // 2, RET_HEADS, ret_dk ** -0.5, name="ret_rope")
            (o,) = attn_fwd((qk, ret_dk, 0, 1), (qk, ret_dk, RET_HEADS, 1), (proj, ret_dv, RET_HEADS, 1),
                            heads=RET_HEADS, softmax=False, masked=True, log_gamma=log_gamma, link=gather,
                            est_us=ATTN_US["ret"][0], name="ret_attn")
            yb = ret_gate_fwd(o, proj, 2, vec(ret_gn_g, j), RET_HEADS, name="ret_gate")
            mix = mm(yb, weight("ret_w_out", j), link=gather, name="ret_out")
            sv.update(proj=proj, qk=qk, o=o, yb=yb)
        else:
            proj = mm(hb, weight("mla_w_in", j), link=gather, name="mla_in")
            cq = rms_fwd(proj, 0, MLA_Q_RANK, vec(qg_full, j), name="mla_q_norm")
            ckv = rms_fwd(proj, 1, MLA_KV_RANK, vec(kvg_full, j), name="mla_kv_norm")
            qf = mm(cq, weight("mla_w_uq", j), link=gather, name="mla_uq")
            kvf = mm(ckv, weight("mla_w_ukv", j), link=gather, name="mla_ukv")
            q3 = qf.reshape(S, MLA_HEADS, MLA_NOPE + MLA_ROPE)
            kv3 = kvf.reshape(S, MLA_HEADS, MLA_NOPE + MLA_V)
            k_rope = jnp.broadcast_to(proj[:, None, MLA_Q_RANK + MLA_KV_RANK:], (S, MLA_HEADS, MLA_ROPE))
            q_pad = _mla_pad(q3[..., :MLA_NOPE], q3[..., MLA_NOPE:])
            k_pad = _mla_pad(kv3[..., :MLA_NOPE], k_rope)
            (qr,) = mla_rope(q_pad, mla_ta, mla_tb, MLA_HEADS, backward=False, head_sum=False,
                             out_dtype=MXU_DTYPE, name="mla_rope_q")
            (kr,) = mla_rope(k_pad, mla_ta, mla_tb, MLA_HEADS, backward=False, head_sum=False,
                             out_dtype=MXU_DTYPE, name="mla_rope_k")
            o, lse = attn_fwd((qr, MLA_PAD, 0, 1), (kr, MLA_PAD, 0, 1), (kvf, MLA_V, 1, 2), heads=MLA_HEADS,
                              softmax=True, masked=True, scale=mla_scale, link=gather, est_us=ATTN_US["mla"][0],
                              name="mla_attn")
            ob = o.astype(MXU_DTYPE)
            mix = mm(ob, weight("mla_w_out", j), link=gather, name="mla_out")
            sv.update(proj=proj, cq=cq, ckv=ckv, kvf=kvf, qr=qr, kr=kr, o=o, ob=ob, lse=lse)
        h, hb, sv["xhat_mix"], sv["rstd_mix"] = ln_fwd(h, mix, vec(ln_mix_g, layer), vec(ln_mix_b, layer), alpha,
                                                       gather, name="ln_mix")
        sv["hb_mem"] = hb
        q = mm(hb, weight("xa_w_q", layer), link=gather, name="xa_q")
        kvm = mm(mem_b, weight("xa_w_kv", layer), link=gather, name="xa_kv")
        o, lse = attn_fwd((q, xa_d, 0, 1), (kvm, xa_d, 0, 1), (kvm, xa_d, XA_HEADS, 1), heads=XA_HEADS,
                          softmax=True, masked=False, scale=xa_scale, link=gather, est_us=ATTN_US["xa"][0],
                          name="xa_attn")
        ob = o.astype(MXU_DTYPE)
        mix = mm(ob, weight("xa_w_out", layer), link=gather, name="xa_out")
        sv.update(xa_q=q, xa_kvm=kvm, xa_o=o, xa_ob=ob, xa_lse=lse)
        h, hb, sv["xhat_mem"], sv["rstd_mem"] = ln_fwd(h, mix, vec(ln_mem_g, layer), vec(ln_mem_b, layer), alpha,
                                                       gather, name="ln_mem")
        sv["hb_ffn"] = hb
        hup = mm(hb, weight("ffn_w_up", layer), link=gather, name="ffn_up")
        u = conv_glu_fwd(hup, conv_w_full[layer], vec(ffn_conv_b, layer), gather, name="ffn_conv")
        mix = mm(u, weight("ffn_w_down", layer), link=gather, name="ffn_down")
        sv.update(hup=hup, u=u)
        h, hb, sv["xhat_ffn"], sv["rstd_ffn"] = ln_fwd(h, mix, vec(ln_ffn_g, layer), vec(ln_ffn_b, layer), alpha,
                                                       gather, name="ln_ffn")
        saved.append(sv)

    dh, loss_blk = loss_head(h, target, name="loss_head")
    loss = lax.psum(loss_blk[0, 0], ("x", "y", "c"))

    small = {n: [None] * weights[n].shape[0] for n in REPLICATED + SMALL_CUT}
    W = gather.bufs

    def wgrad(n, a, d, l, tag):
        axis, k, nn = geometry(n)
        if axis == 2:
            slabs = mm(a, d, mode="tn", out_dtype=PAY_DTYPE, slab_width=nn, link=scatter, name=tag)
        else:
            slabs = mm(a, d, mode="tn", out_dtype=PAY_DTYPE, link=scatter, name=tag).reshape(N_DEV, k, nn)
        n_layers = weights[n].shape[0]
        pair = ("pair", n, l)

        def then():
            sums = pair_sum(slabs, scatter.bufs.pop(pair), core, name="pairsum_" + n)
            scatter.push("scatter", sums, n, (n_layers, N_CHIP, k, nn), l, k, nn)

        scatter.side.append((_Chunk("swap", slabs, pair, (N_CHIP, k, nn), 0, 0, k, nn), then))

    for layer in reversed(range(depth)):
        j = layer // 2
        sv = saved[layer]
        dz, dzb, dg, db = ln_bwd(dh, sv["xhat_ffn"], sv["rstd_ffn"], vec(ln_ffn_g, layer), scatter, name="ln_ffn_bwd")
        small["ln_ffn_g"][layer], small["ln_ffn_b"][layer] = dg[0], db[0]
        wgrad("ffn_w_down", sv["u"], dzb, layer, "ffn_down_dw")
        du = mm(dzb, W[("ffn_w_down", layer)], mode="nt", link=scatter, name="ffn_down_dx")
        dhg, dhv, dwg, dwv, dbg, dbv = conv_glu_bwd(du, sv["hup"], conv_w_full[layer], vec(ffn_conv_b, layer),
                                                    scatter, name="ffn_conv_bwd")
        small["ffn_conv_w"][layer] = jnp.concatenate([dwg, dwv], axis=1)
        small["ffn_conv_b"][layer] = jnp.concatenate([dbg, dbv], axis=1)[0]
        dhup = jnp.concatenate([dhg, dhv], axis=1)
        wgrad("ffn_w_up", sv["hb_ffn"], dhup, layer, "ffn_up_dw")
        dh = mm(dhup, W[("ffn_w_up", layer)], mode="nt", add=dz, add_scale=alpha, link=scatter, name="ffn_up_dx")
        dz, dzb, dg, db = ln_bwd(dh, sv["xhat_mem"], sv["rstd_mem"], vec(ln_mem_g, layer), scatter, name="ln_mem_bwd")
        small["ln_mem_g"][layer], small["ln_mem_b"][layer] = dg[0], db[0]
        wgrad("xa_w_out", sv["xa_ob"], dzb, layer, "xa_out_dw")
        do = mm(dzb, W[("xa_w_out", layer)], mode="nt", link=scatter, name="xa_out_dx")
        dq, dk, dv = attn_bwd((sv["xa_q"], xa_d, 0, 1), (sv["xa_kvm"], xa_d, 0, 1), (sv["xa_kvm"], xa_d, XA_HEADS, 1),
                              do, heads=XA_HEADS, softmax=True, masked=False, scale=xa_scale, o=sv["xa_o"],
                              lse=sv["xa_lse"], link=scatter, est_us=ATTN_US["xa"][1], name="xa_attn_bwd")
        dqb = dq.astype(MXU_DTYPE)
        wgrad("xa_w_kv", mem_b, jnp.concatenate([dk, dv], axis=1).astype(MXU_DTYPE), layer, "xa_kv_dw")
        wgrad("xa_w_q", sv["hb_mem"], dqb, layer, "xa_q_dw")
        dh = mm(dqb, W[("xa_w_q", layer)], mode="nt", add=dz, add_scale=alpha, link=scatter, name="xa_q_dx")
        dz, dzb, dg, db = ln_bwd(dh, sv["xhat_mix"], sv["rstd_mix"], vec(ln_mix_g, layer), scatter, name="ln_mix_bwd")
        small["ln_mix_g"][layer], small["ln_mix_b"][layer] = dg[0], db[0]
        if layer % 2 == 0:
            wgrad("ret_w_out", sv["yb"], dzb, j, "ret_out_dw")
            dy = mm(dzb, W[("ret_w_out", j)], mode="nt", link=scatter, name="ret_out_dx")
            do, dgate, dgn = ret_gate_bwd(dy, sv["o"], sv["proj"], 2, vec(ret_gn_g, j), RET_HEADS, name="ret_gate_bwd")
            small["ret_gn_g"][j] = dgn[0]
            dq, dk, dv = attn_bwd((sv["qk"], ret_dk, 0, 1), (sv["qk"], ret_dk, RET_HEADS, 1),
                                  (sv["proj"], ret_dv, RET_HEADS, 1), do, heads=RET_HEADS, softmax=False,
                                  masked=True, log_gamma=log_gamma, link=scatter, est_us=ATTN_US["ret"][1],
                                  name="ret_attn_bwd")
            dqk = ret_rope_bwd(dq, dk, rcos, rsin, ret_dk // 2, ret_dk ** -0.5, name="ret_rope_bwd")
            dproj = jnp.concatenate([dqk, dv.astype(MXU_DTYPE), dgate], axis=1)
            wgrad("ret_w_in", sv["hb_mix"], dproj, j, "ret_in_dw")
            dh = mm(dproj, W[("ret_w_in", j)], mode="nt", add=dz, add_scale=alpha, link=scatter, name="ret_in_dx")
        else:
            wgrad("mla_w_out", sv["ob"], dzb, j, "mla_out_dw")
            do = mm(dzb, W[("mla_w_out", j)], mode="nt", link=scatter, name="mla_out_dx")
            dq, dk, dv = attn_bwd((sv["qr"], MLA_PAD, 0, 1), (sv["kr"], MLA_PAD, 0, 1), (sv["kvf"], MLA_V, 1, 2), do,
                                  heads=MLA_HEADS, softmax=True, masked=True, scale=mla_scale, o=sv["o"],
                                  lse=sv["lse"], link=scatter, est_us=ATTN_US["mla"][1], name="mla_attn_bwd")
            (dq_un,) = mla_rope(dq, mla_ta, mla_tb, MLA_HEADS, backward=True, head_sum=False, out_dtype=F32,
                                name="mla_rope_q_bwd")
            dk_un, dk_rope_sum = mla_rope(dk, mla_ta, mla_tb, MLA_HEADS, backward=True, head_sum=True, out_dtype=F32,
                                          name="mla_rope_k_bwd")
            dq_nope, dq_rope = _mla_unpad(dq_un, MLA_HEADS)
            dqf = jnp.concatenate([dq_nope, dq_rope], axis=2).reshape(S, -1).astype(MXU_DTYPE)
            dk_nope, _ = _mla_unpad(dk_un, MLA_HEADS)
            dkvf = jnp.concatenate([dk_nope, dv.reshape(S, MLA_HEADS, MLA_V)], axis=2).reshape(S, -1).astype(MXU_DTYPE)
            half = MLA_ROPE // 2
            dk_rope = jnp.concatenate([dk_rope_sum[:, :half], dk_rope_sum[:, LANES // 2:LANES // 2 + half]], axis=1)
            wgrad("mla_w_uq", sv["cq"], dqf, j, "mla_uq_dw")
            dcq = mm(dqf, W[("mla_w_uq", j)], mode="nt", link=scatter, name="mla_uq_dx")
            wgrad("mla_w_ukv", sv["ckv"], dkvf, j, "mla_ukv_dw")
            dckv = mm(dkvf, W[("mla_w_ukv", j)], mode="nt", link=scatter, name="mla_ukv_dx")
            dcq_in, dqg = rms_bwd(dcq, sv["proj"], 0, MLA_Q_RANK, vec(qg_full, j), name="mla_q_norm_bwd")
            dckv_in, dkvg = rms_bwd(dckv, sv["proj"], 1, MLA_KV_RANK, vec(kvg_full, j), name="mla_kv_norm_bwd")
            small["mla_q_norm_g"][j], small["mla_kv_norm_g"][j] = dqg[0], dkvg[0]
            dproj = jnp.concatenate([dcq_in, dckv_in, dk_rope], axis=1).astype(MXU_DTYPE)
            wgrad("mla_w_in", sv["hb_mix"], dproj, j, "mla_in_dw")
            dh = mm(dproj, W[("mla_w_in", j)], mode="nt", add=dz, add_scale=alpha, link=scatter, name="mla_in_dx")
    grad_x = dh[None]
    scatter.drain(name="scatter_rest")

    grads, deltas, new_m, new_v = {}, {}, {}, {}
    for n in BIG:
        grads[n], deltas[n], new_m[n], new_v[n] = adam_update(scatter.bufs[n], weights[n], mom_m[n], mom_v[n],
                                                              name="adam_" + n)

    small_names = list(REPLICATED + SMALL_CUT)
    partial = [jnp.stack(small[n]) for n in small_names]
    allpart = gather_small(pack(partial), name="gather_small_grads")
    rows = allpart.shape[0] // N_DEV
    allpart = allpart.reshape(N_DEV, rows, LANES)

    rep_names = list(REPLICATED)
    rep_w = pack([weights[n] for n in rep_names])
    rep_m = pack([mom_m[n] for n in rep_names])
    rep_v = pack([mom_v[n] for n in rep_names])
    rep_rows = rep_w.shape[0]
    rep_size = sum(weights[n].size for n in rep_names)
    flat_parts = allpart.reshape(N_DEV, rows * LANES)
    rep_parts = jnp.pad(flat_parts[:, :rep_size], ((0, 0), (0, rep_rows * LANES - rep_size)))
    outs = adam_update(rep_parts.reshape(1, N_DEV, rep_rows, LANES), rep_w[None], rep_m[None], rep_v[None],
                       name="adam_replicated")
    for o_, dst in zip(outs, (grads, deltas, new_m, new_v)):
        for n, a in zip(rep_names, unpack(o_.reshape(-1), [weights[n] for n in rep_names])):
            dst[n] = a

    cut_names = list(SMALL_CUT)
    cut_full = [jnp.stack(small[n]) for n in cut_names]
    cut_parts = []
    at = rep_size
    for n, a in zip(cut_names, cut_full):
        whole = flat_parts[:, at:at + a.size].reshape((N_DEV,) + a.shape)
        at += a.size
        width = weights[n].shape[-1]
        start = (0,) * (whole.ndim - 1) + (dev * width,)
        mine = lax.dynamic_slice(whole, start, whole.shape[:-1] + (width,))
        cut_parts.append(mine.reshape(N_DEV, -1))
    cut_parts = jnp.concatenate(cut_parts, axis=1)
    cut_w = pack([weights[n] for n in cut_names])
    cut_m = pack([mom_m[n] for n in cut_names])
    cut_v = pack([mom_v[n] for n in cut_names])
    cut_rows = cut_w.shape[0]
    cut_parts = jnp.pad(cut_parts, ((0, 0), (0, cut_rows * LANES - cut_parts.shape[1])))
    outs = adam_update(cut_parts.reshape(1, N_DEV, cut_rows, LANES), cut_w[None], cut_m[None], cut_v[None],
                       name="adam_small_cut")
    for o_, dst in zip(outs, (grads, deltas, new_m, new_v)):
        for n, a in zip(cut_names, unpack(o_.reshape(-1), [weights[n] for n in cut_names])):
            dst[n] = a

    return (loss, grad_x, *[grads[n] for n in order], *[deltas[n] for n in order],
            *[new_m[n] for n in order], *[new_v[n] for n in order])
```

```python
import functools
import math

import jax
import jax.numpy as jnp
from jax import lax
from jax.experimental import pallas as pl
from jax.experimental.pallas import tpu as pltpu

F32 = jnp.float32
MXU_DTYPE = jnp.bfloat16
PAY_DTYPE = jnp.bfloat16
MESH = pl.DeviceIdType.MESH
N_DEV = 8
N_CHIP = 4

DEPTH = 4
CHUNK = 64
RET_HEADS = 8
MLA_HEADS = 16
MLA_Q_RANK = 512
MLA_KV_RANK = 512
MLA_NOPE = 128
MLA_ROPE = 64
MLA_V = 128
MLA_PAD = 256
XA_HEADS = 4
ROPE_BASE = 10000.0
LN_EPS = 1e-5
RMS_EPS = 1e-6
NEG_INF = -1e30
ADAM_LR = 0.001
ADAM_B1 = 0.9
ADAM_B2 = 0.999
ADAM_EPS = 1e-08
ADAM_WD = 0.01
ADAM_STEP = 10

LANES = 128
VMEM_LIMIT_BYTES = 48 * 1024 * 1024
TILE_M = 1024
TILE_N = 1024
TILE_K = 2048
SLAB_TILE_N = 1536
ATTN_TILE = 512
ROW_TILE = 256

MXU_FLOPS_PER_US = 7.0e8
GATHER_BYTES_PER_US = 4.4e4
SCATTER_BYTES_PER_US = 2.2e4
CHUNK_BYTES = 768 * 1024
HOST_CHUNKS = 6
SIDE_CHUNKS = 4
ATTN_US = {"ret": (125.0, 160.0), "mla": (320.0, 270.0), "xa": (35.0, 35.0)}
ROWWISE_US = {"ln": (25.0, 22.0), "conv": (65.0, 110.0)}
HBM_BYTES_PER_US = 3.0e6
ADAM_BYTES_PER_ELEMENT = 36


def _tile(n, cap, mult=LANES):
    best = None
    for t in range(mult, min(n, cap) + 1, mult):
        if n % t == 0:
            best = t
    return n if best is None else best


def _params(n_axes):
    return pltpu.CompilerParams(dimension_semantics=("arbitrary",) * n_axes,
                                vmem_limit_bytes=VMEM_LIMIT_BYTES)


def _place():
    return lax.axis_index("x"), lax.axis_index("y"), lax.axis_index("c")


def _other_chips(x, y):
    return [(1 - x, y), (x, 1 - y), (1 - x, 1 - y)]


def _remote(src, dst, send, recv, to):
    return pltpu.make_async_remote_copy(src_ref=src, dst_ref=dst, send_sem=send, recv_sem=recv,
                                        device_id=to, device_id_type=MESH)


class _Chunk:
    def __init__(self, kind, src, key, dst_shape, layer, r0, rc, cols, axis=None, k=None, n=None, half=None):
        self.kind, self.src, self.key, self.dst_shape = kind, src, key, dst_shape
        self.layer, self.r0, self.rc, self.axis, self.k, self.n, self.half = layer, r0, rc, axis, k, n, half
        self.nbytes = rc * cols * jnp.dtype(PAY_DTYPE).itemsize

    def copies(self, src_ref, dst_ref, send, recv, loc, i):
        x, y, c = _place()
        chips = _other_chips(x, y)
        rows = pl.ds(self.r0, self.rc)
        sibling, x_nbr, y_nbr, diagonal = (x, y, 1 - c), (1 - x, y, c), (x, 1 - y, c), (1 - x, 1 - y, c)
        if self.kind == "gather":
            peers = [sibling, x_nbr, y_nbr]
            src = src_ref.at[self.layer, rows, :]

            def slab(px, py, pc):
                idx = 4 * px + 2 * py + pc
                if self.axis == 1:
                    return dst_ref.at[pl.ds(idx * self.k + self.r0, self.rc), :]
                return dst_ref.at[rows, pl.ds(idx * self.n, self.n)]

            sends = [_remote(src, slab(x, y, c), send.at[4 * i + t], recv.at[4 * i + t], p)
                     for t, p in enumerate(peers)]
            recvs = [_remote(src, slab(*p), send.at[4 * i + t], recv.at[4 * i + t], p)
                     for t, p in enumerate(peers)]
            return sends, recvs, pltpu.make_async_copy(src, slab(x, y, c), loc.at[i])
        if self.kind == "relay":
            def piece(p, r):
                idx = 4 * p[0] + 2 * p[1] + p[2]
                if self.axis == 1:
                    return dst_ref.at[pl.ds(idx * self.k + r, self.rc), :]
                return dst_ref.at[pl.ds(r, self.rc), pl.ds(idx * self.n, self.n)]

            top, bottom = self.r0, self.half + self.r0
            sends = [_remote(piece(y_nbr, top), piece(y_nbr, top), send.at[4 * i], recv.at[4 * i], x_nbr),
                     _remote(piece(x_nbr, bottom), piece(x_nbr, bottom), send.at[4 * i + 1], recv.at[4 * i + 1], y_nbr)]
            recvs = [_remote(piece(diagonal, top), piece(diagonal, top), send.at[4 * i], recv.at[4 * i], x_nbr),
                     _remote(piece(diagonal, bottom), piece(diagonal, bottom), send.at[4 * i + 1], recv.at[4 * i + 1],
                             y_nbr)]
            return sends, recvs, None
        if self.kind == "gather2":
            def slab(px, py, pc):
                idx = 4 * px + 2 * py + pc
                if self.axis == 1:
                    return dst_ref.at[pl.ds(idx * self.k, self.k), :]
                return dst_ref.at[:, pl.ds(idx * self.n, self.n)]

            sends = [_remote(slab(px, py, c), slab(px, py, c), send.at[4 * i + t], recv.at[4 * i + t], sibling)
                     for t, (px, py) in enumerate(chips)]
            recvs = [_remote(slab(px, py, 1 - c), slab(px, py, 1 - c), send.at[4 * i + t], recv.at[4 * i + t], sibling)
                     for t, (px, py) in enumerate(chips)]
            return sends, recvs, None
        if self.kind == "swap":
            sends = [_remote(src_ref.at[2 * j + (1 - c)], dst_ref.at[j], send.at[4 * i + j], recv.at[4 * i + j], sibling)
                     for j in range(N_CHIP)]
            recvs = [_remote(src_ref.at[2 * j + c], dst_ref.at[j], send.at[4 * i + j], recv.at[4 * i + j], sibling)
                     for j in range(N_CHIP)]
            return sends, recvs, None
        mine = 2 * x + y
        own = src_ref.at[mine, rows, :]
        sends = [_remote(src_ref.at[2 * px + py, rows, :], dst_ref.at[self.layer, mine, rows, :],
                         send.at[4 * i + t], recv.at[4 * i + t], (px, py, c)) for t, (px, py) in enumerate(chips)]
        recvs = [_remote(own, dst_ref.at[self.layer, 2 * px + py, rows, :],
                         send.at[4 * i + t], recv.at[4 * i + t], (px, py, c)) for t, (px, py) in enumerate(chips)]
        return sends, recvs, pltpu.make_async_copy(own, dst_ref.at[self.layer, mine, rows, :], loc.at[i])


def _call(body, *, grid, in_specs, out_specs, out_shape, operands, scratch=(), name, chunks=(), bufs=None):
    n_in, n_out, n_scr = len(operands), len(out_shape), len(scratch)
    params = _params(len(grid)) if grid else pltpu.CompilerParams(vmem_limit_bytes=VMEM_LIMIT_BYTES)
    if not chunks:
        return pl.pallas_call(
            body, out_shape=tuple(out_shape), grid=grid, in_specs=list(in_specs), out_specs=tuple(out_specs),
            scratch_shapes=list(scratch), compiler_params=params, name=name)(*operands)
    srcs, keys = [], []
    for ch in chunks:
        if ch.src is not None and not any(ch.src is s for s in srcs):
            srcs.append(ch.src)
        if ch.key not in keys:
            keys.append(ch.key)
    shape_of = {ch.key: ch.dst_shape for ch in chunks}
    held = [k for k in keys if bufs.get(k) is not None]
    extra = srcs + [bufs[k] for k in held]
    aliases = {n_in + len(srcs) + i: n_out + keys.index(k) for i, k in enumerate(held)}
    anywhere = pl.BlockSpec(memory_space=pl.ANY)
    n_extra, n_keys, n_ch = len(extra), len(keys), len(chunks)

    def hosted(*refs):
        src_refs = refs[n_in:n_in + len(srcs)]
        o0 = n_in + n_extra
        dst_refs = refs[o0 + n_out:o0 + n_out + n_keys]
        s0 = o0 + n_out + n_keys
        send, recv, loc = refs[s0 + n_scr:]
        sends, recvs, locs = [], [], []
        for i, ch in enumerate(chunks):
            src_ref = None if ch.src is None else src_refs[[ch.src is s for s in srcs].index(True)]
            s_, r_, l_ = ch.copies(src_ref, dst_refs[keys.index(ch.key)], send, recv, loc, i)
            sends += s_
            recvs += r_
            if l_ is not None:
                locs.append(l_)

        def start():
            for cp in locs + sends:
                cp.start()

        def finish():
            for cp in recvs:
                cp.wait_recv()
            for cp in sends:
                cp.wait_send()
            for cp in locs:
                cp.wait()

        if grid:
            first = functools.reduce(jnp.logical_and, [pl.program_id(a) == 0 for a in range(len(grid))])
            last = functools.reduce(jnp.logical_and, [pl.program_id(a) == grid[a] - 1 for a in range(len(grid))])
            pl.when(first)(start)
            body(*refs[:n_in], *refs[o0:o0 + n_out], *refs[s0:s0 + n_scr])
            pl.when(last)(finish)
        else:
            start()
            body(*refs[:n_in], *refs[o0:o0 + n_out], *refs[s0:s0 + n_scr])
            finish()

    res = pl.pallas_call(
        hosted, out_shape=tuple(out_shape) + tuple(jax.ShapeDtypeStruct(shape_of[k], PAY_DTYPE) for k in keys),
        grid=grid, in_specs=list(in_specs) + [anywhere] * n_extra, out_specs=tuple(out_specs) + (anywhere,) * n_keys,
        scratch_shapes=list(scratch) + [pltpu.SemaphoreType.DMA((4 * n_ch,)), pltpu.SemaphoreType.DMA((4 * n_ch,)),
                                        pltpu.SemaphoreType.DMA((n_ch,))],
        input_output_aliases=aliases, compiler_params=params, name=name)(*operands, *extra)
    for k, arr in zip(keys, res[n_out:]):
        bufs[k] = arr
    return res[:n_out]


class _Link:
    def __init__(self, bytes_per_us):
        self.rate = bytes_per_us
        self.queue, self.bufs = [], {}
        self.side = []
        self.after = []
        self.left = {}
        self.on_done = {}

    def push(self, kind, src, key, dst_shape, layer, rows, cols, done=None, front=False, **geometry):
        fits = [r for r in range(16, rows + 1, 16) if rows % r == 0 and r * cols * 2 <= CHUNK_BYTES]
        rc = max(fits) if fits else min(r for r in range(16, rows + 1, 16) if rows % r == 0)
        new = [_Chunk(kind, src, key, dst_shape, layer, r0, rc, cols, **geometry) for r0 in range(0, rows, rc)]
        self.queue = new + self.queue if front else self.queue + new
        self.left[key] = self.left.get(key, 0) + rows // rc
        if done is not None:
            self.on_done[key] = done

    def settle(self):
        todo, self.after = self.after, []
        for f in todo:
            f()

    def _pop(self, count):
        out = []
        for ch in self.queue[:count]:
            self.left[ch.key] -= 1
            if self.left[ch.key] == 0 and ch.key in self.on_done:
                self.after.append(self.on_done.pop(ch.key))
            out.append(ch)
        self.queue = self.queue[count:]
        return out

    def _side(self):
        riders, self.side = self.side[:SIDE_CHUNKS], self.side[SIDE_CHUNKS:]
        self.after += [then for _, then in riders if then is not None]
        return [ch for ch, _ in riders]

    def take(self, est_us):
        self.settle()
        out = self._side()
        budget, count = est_us * self.rate, 0
        while count < min(len(self.queue), HOST_CHUNKS) and self.queue[count].nbytes <= 2 * budget:
            budget -= self.queue[count].nbytes
            count += 1
        return out + self._pop(count)

    def flush(self, key, name):
        self.settle()
        if key is None:
            n = len(self.queue)
        else:
            n = max([i + 1 for i, ch in enumerate(self.queue) if ch.key == key], default=0)
        while n > 0:
            group = self._side() + self._pop(min(n, HOST_CHUNKS))
            n -= min(n, HOST_CHUNKS)
            _call(lambda: None, grid=(), in_specs=[], out_specs=(), out_shape=(), operands=[], name=name,
                  chunks=group, bufs=self.bufs)
            self.settle()

    def ride_out(self, name):
        self.settle()
        while self.side:
            _call(lambda: None, grid=(), in_specs=[], out_specs=(), out_shape=(), operands=[], name=name,
                  chunks=self._side(), bufs=self.bufs)
            self.settle()


def forward_to_sibling(w, axis, k, n, name):
    def body(w_in, w_ref, send, recv):
        x, y, c = _place()
        sibling = (x, y, 1 - c)

        def slab(px, py, pc):
            idx = 4 * px + 2 * py + pc
            return w_ref.at[pl.ds(idx * k, k), :] if axis == 1 else w_ref.at[:, pl.ds(idx * n, n)]

        chips = _other_chips(x, y)
        sends = [_remote(slab(px, py, c), slab(px, py, c), send.at[t], recv.at[t], sibling)
                 for t, (px, py) in enumerate(chips)]
        recvs = [_remote(slab(px, py, 1 - c), slab(px, py, 1 - c), send.at[t], recv.at[t], sibling)
                 for t, (px, py) in enumerate(chips)]
        for cp in sends:
            cp.start()
        for cp in recvs:
            cp.wait_recv()
        for cp in sends:
            cp.wait_send()

    return pl.pallas_call(
        body, out_shape=jax.ShapeDtypeStruct(w.shape, w.dtype),
        in_specs=[pl.BlockSpec(memory_space=pl.ANY)], out_specs=pl.BlockSpec(memory_space=pl.ANY),
        scratch_shapes=[pltpu.SemaphoreType.DMA((3,)), pltpu.SemaphoreType.DMA((3,))],
        input_output_aliases={0: 0}, name=name)(w)


def mm(a, b, *, mode="nn", out_dtype=F32, add=None, add_scale=1.0, slab_width=None, link=None, name):
    if mode == "nn":
        (M, K), N = a.shape, b.shape[-1]
    elif mode == "nt":
        (M, K), N = a.shape, b.shape[-2]
    else:
        (K, M), N = a.shape, b.shape[-1]
    tm, tk = _tile(M, TILE_M), _tile(K, TILE_K)
    tn = _tile(N, TILE_N) if slab_width is None else _tile(slab_width, SLAB_TILE_N)
    nk = K // tk
    dims = {"nn": (((1,), (0,)), ((), ())), "nt": (((1,), (1,)), ((), ())),
            "tn": (((0,), (0,)), ((), ()))}[mode]

    a_spec = (pl.BlockSpec((tk, tm), lambda i, j, k: (k, i)) if mode == "tn"
              else pl.BlockSpec((tm, tk), lambda i, j, k: (i, k)))
    b_spec = (pl.BlockSpec((tn, tk), lambda i, j, k: (j, k)) if mode == "nt"
              else pl.BlockSpec((tk, tn), lambda i, j, k: (k, j)))
    in_specs, operands = [a_spec, b_spec], [a, b]
    if add is not None:
        in_specs.append(pl.BlockSpec((tm, tn), lambda i, j, k: (i, j)))
        operands.append(add)
    if slab_width is None:
        out_shape = jax.ShapeDtypeStruct((M, N), out_dtype)
        out_spec = pl.BlockSpec((tm, tn), lambda i, j, k: (i, j))
    else:
        per = slab_width // tn
        out_shape = jax.ShapeDtypeStruct((N // slab_width, M, slab_width), out_dtype)
        out_spec = pl.BlockSpec((None, tm, tn), lambda i, j, k: (j // per, i, j % per))
    n_in = len(operands)

    def body(*refs):
        a_ref, b_ref = refs[0], refs[1]
        add_ref = refs[2] if add is not None else None
        o_ref = refs[n_in]
        k = pl.program_id(2)
        part = lax.dot_general(a_ref[...].astype(MXU_DTYPE), b_ref[...].astype(MXU_DTYPE),
                               dims, preferred_element_type=F32)

        def finish(r):
            if add_ref is not None:
                r = r + add_scale * add_ref[...].astype(F32)
            o_ref[...] = r.astype(o_ref.dtype)

        if nk == 1:
            finish(part)
        else:
            acc_ref = refs[n_in + 1]

            @pl.when(k == 0)
            def _():
                acc_ref[...] = part

            @pl.when(jnp.logical_and(k > 0, k < nk - 1))
            def _():
                acc_ref[...] += part

            @pl.when(k == nk - 1)
            def _():
                finish(acc_ref[...] + part)

    chunks = link.take(2.0 * M * N * K / MXU_FLOPS_PER_US) if link is not None else ()
    (out,) = _call(body, grid=(M // tm, N // tn, nk), in_specs=in_specs, out_specs=(out_spec,), out_shape=(out_shape,),
                   operands=operands, scratch=[pltpu.VMEM((tm, tn), F32)] if nk > 1 else [], name=name,
                   chunks=chunks, bufs=link.bufs if link is not None else None)
    return out


def ln_fwd(h, f, g, b, alpha, link, name):
    S, D = h.shape
    tm = _tile(S, ROW_TILE, 8)

    def body(h_ref, f_ref, g_ref, b_ref, y_ref, yb_ref, xhat_ref, rstd_ref):
        z = alpha * h_ref[...] + f_ref[...]
        mu = jnp.mean(z, axis=-1, keepdims=True)
        zc = z - mu
        var = jnp.mean(zc * zc, axis=-1, keepdims=True)
        rstd = lax.rsqrt(var + LN_EPS)
        xhat = zc * rstd
        y = xhat * g_ref[...] + b_ref[...]
        y_ref[...] = y
        yb_ref[...] = y.astype(yb_ref.dtype)
        xhat_ref[...] = xhat
        rstd_ref[...] = rstd

    row = pl.BlockSpec((tm, D), lambda i: (i, 0))
    vec = pl.BlockSpec((1, D), lambda i: (0, 0))
    return _call(
        body, grid=(S // tm,), in_specs=[row, row, vec, vec],
        out_specs=(row, row, row, pl.BlockSpec((tm, 1), lambda i: (i, 0))),
        out_shape=(jax.ShapeDtypeStruct((S, D), F32), jax.ShapeDtypeStruct((S, D), MXU_DTYPE),
                   jax.ShapeDtypeStruct((S, D), F32), jax.ShapeDtypeStruct((S, 1), F32)),
        operands=[h, f, g, b], name=name, chunks=link.take(ROWWISE_US["ln"][0]), bufs=link.bufs)


def ln_bwd(dy, xhat, rstd, g, link, name):
    S, D = dy.shape
    tm = _tile(S, ROW_TILE, 8)

    def body(dy_ref, xhat_ref, rstd_ref, g_ref, dz_ref, dzb_ref, dg_ref, db_ref):
        @pl.when(pl.program_id(0) == 0)
        def _():
            dg_ref[...] = jnp.zeros_like(dg_ref)
            db_ref[...] = jnp.zeros_like(db_ref)

        dy = dy_ref[...]
        xhat = xhat_ref[...]
        dxh = dy * g_ref[...]
        m1 = jnp.mean(dxh, axis=-1, keepdims=True)
        m2 = jnp.mean(dxh * xhat, axis=-1, keepdims=True)
        dz = rstd_ref[...] * (dxh - m1 - xhat * m2)
        dz_ref[...] = dz
        dzb_ref[...] = dz.astype(dzb_ref.dtype)
        dg_ref[...] += jnp.sum(dy * xhat, axis=0, keepdims=True)
        db_ref[...] += jnp.sum(dy, axis=0, keepdims=True)

    row = pl.BlockSpec((tm, D), lambda i: (i, 0))
    vec = pl.BlockSpec((1, D), lambda i: (0, 0))
    return _call(
        body, grid=(S // tm,), in_specs=[row, row, pl.BlockSpec((tm, 1), lambda i: (i, 0)), vec],
        out_specs=(row, row, vec, vec),
        out_shape=(jax.ShapeDtypeStruct((S, D), F32), jax.ShapeDtypeStruct((S, D), MXU_DTYPE),
                   jax.ShapeDtypeStruct((1, D), F32), jax.ShapeDtypeStruct((1, D), F32)),
        operands=[dy, xhat, rstd, g], name=name, chunks=link.take(ROWWISE_US["ln"][1]), bufs=link.bufs)


def rms_fwd(x, col, width, g, name):
    S = x.shape[0]
    tm = _tile(S, ROW_TILE, 8)

    def body(x_ref, g_ref, y_ref):
        xv = x_ref[...]
        r = lax.rsqrt(jnp.mean(xv * xv, axis=-1, keepdims=True) + RMS_EPS)
        y_ref[...] = (xv * r * g_ref[...]).astype(y_ref.dtype)

    return pl.pallas_call(
        body, out_shape=jax.ShapeDtypeStruct((S, width), MXU_DTYPE), grid=(S // tm,),
        in_specs=[pl.BlockSpec((tm, width), lambda i: (i, col)), pl.BlockSpec((1, width), lambda i: (0, 0))],
        out_specs=pl.BlockSpec((tm, width), lambda i: (i, 0)), compiler_params=_params(1), name=name)(x, g)


def rms_bwd(dy, x, col, width, g, name):
    S = x.shape[0]
    tm = _tile(S, ROW_TILE, 8)

    def body(dy_ref, x_ref, g_ref, dx_ref, dg_ref):
        @pl.when(pl.program_id(0) == 0)
        def _():
            dg_ref[...] = jnp.zeros_like(dg_ref)

        xv = x_ref[...]
        dy = dy_ref[...]
        r = lax.rsqrt(jnp.mean(xv * xv, axis=-1, keepdims=True) + RMS_EPS)
        dxn = dy * g_ref[...]
        m = jnp.mean(dxn * xv, axis=-1, keepdims=True)
        dx_ref[...] = r * (dxn - xv * (r * r * m))
        dg_ref[...] += jnp.sum(dy * xv * r, axis=0, keepdims=True)

    return pl.pallas_call(
        body, out_shape=(jax.ShapeDtypeStruct((S, width), F32), jax.ShapeDtypeStruct((1, width), F32)),
        grid=(S // tm,),
        in_specs=[pl.BlockSpec((tm, width), lambda i: (i, 0)), pl.BlockSpec((tm, width), lambda i: (i, col)),
                  pl.BlockSpec((1, width), lambda i: (0, 0))],
        out_specs=(pl.BlockSpec((tm, width), lambda i: (i, 0)), pl.BlockSpec((1, width), lambda i: (0, 0))),
        compiler_params=_params(1), name=name)(dy, x, g)


def ret_rope_fwd(proj, cos, sin, n_groups, half, k_from, k_scale, name):
    S = proj.shape[0]
    W = n_groups * 2 * half
    tm = _tile(S, ROW_TILE, 8)

    def body(x_ref, c_ref, s_ref, o_ref):
        c, s = c_ref[...], s_ref[...]
        for gi in range(n_groups):
            lo = gi * 2 * half
            x1 = x_ref[:, lo:lo + half]
            x2 = x_ref[:, lo + half:lo + 2 * half]
            sc = k_scale if gi >= k_from else 1.0
            o_ref[:, lo:lo + half] = ((x1 * c - x2 * s) * sc).astype(o_ref.dtype)
            o_ref[:, lo + half:lo + 2 * half] = ((x2 * c + x1 * s) * sc).astype(o_ref.dtype)

    tab = pl.BlockSpec((tm, half), lambda i: (i, 0))
    return pl.pallas_call(
        body, out_shape=jax.ShapeDtypeStruct((S, W), MXU_DTYPE), grid=(S // tm,),
        in_specs=[pl.BlockSpec((tm, W), lambda i: (i, 0)), tab, tab],
        out_specs=pl.BlockSpec((tm, W), lambda i: (i, 0)), compiler_params=_params(1), name=name)(proj, cos, sin)


def ret_rope_bwd(dq, dk, cos, sin, half, k_scale, name):
    S, Wq = dq.shape
    n_heads = Wq // (2 * half)
    tm = _tile(S, ROW_TILE, 8)

    def body(dq_ref, dk_ref, c_ref, s_ref, o_ref):
        c, s = c_ref[...], s_ref[...]
        for part, (d_ref, sc) in enumerate(((dq_ref, 1.0), (dk_ref, k_scale))):
            for hi in range(n_heads):
                lo = hi * 2 * half
                d1 = d_ref[:, lo:lo + half]
                d2 = d_ref[:, lo + half:lo + 2 * half]
                base = part * Wq + lo
                o_ref[:, base:base + half] = ((d1 * c + d2 * s) * sc).astype(o_ref.dtype)
                o_ref[:, base + half:base + 2 * half] = ((d2 * c - d1 * s) * sc).astype(o_ref.dtype)

    tab = pl.BlockSpec((tm, half), lambda i: (i, 0))
    row = pl.BlockSpec((tm, Wq), lambda i: (i, 0))
    return pl.pallas_call(
        body, out_shape=jax.ShapeDtypeStruct((S, 2 * Wq), MXU_DTYPE), grid=(S // tm,),
        in_specs=[row, row, tab, tab], out_specs=pl.BlockSpec((tm, 2 * Wq), lambda i: (i, 0)),
        compiler_params=_params(1), name=name)(dq, dk, cos, sin)


def mla_rope(x, ta, tb, n_heads, *, backward, head_sum, out_dtype, name):
    S = x.shape[0]
    W = n_heads * MLA_PAD
    tm = _tile(S, ROW_TILE, 8)

    def body(x_ref, a_ref, b_ref, o_ref, *rest):
        a, b = a_ref[...], b_ref[...]
        total = jnp.zeros((tm, LANES), F32)
        for hi in range(n_heads):
            lo = hi * MLA_PAD
            o_ref[:, lo:lo + MLA_NOPE] = x_ref[:, lo:lo + MLA_NOPE].astype(o_ref.dtype)
            t = x_ref[:, lo + MLA_NOPE:lo + MLA_PAD].astype(F32)
            if backward:
                r = t * a + pltpu.roll(t * b, LANES // 2, 1)
            else:
                r = t * a + pltpu.roll(t, LANES // 2, 1) * b
            o_ref[:, lo + MLA_NOPE:lo + MLA_PAD] = r.astype(o_ref.dtype)
            total = total + r
        if head_sum:
            rest[0][...] = total

    tab = pl.BlockSpec((tm, LANES), lambda i: (i, 0))
    row = pl.BlockSpec((tm, W), lambda i: (i, 0))
    out_shape = [jax.ShapeDtypeStruct((S, W), out_dtype)]
    out_specs = [row]
    if head_sum:
        out_shape.append(jax.ShapeDtypeStruct((S, LANES), F32))
        out_specs.append(tab)
    return pl.pallas_call(
        body, out_shape=tuple(out_shape), grid=(S // tm,), in_specs=[row, tab, tab],
        out_specs=tuple(out_specs), compiler_params=_params(1), name=name)(x, ta, tb)


def _visible(qi, kj, tq, tk):
    n = qi * tq + lax.broadcasted_iota(jnp.int32, (tq, tk), 0)
    m = kj * tk + lax.broadcasted_iota(jnp.int32, (tq, tk), 1)
    shift = CHUNK.bit_length() - 1
    vis = lax.shift_right_logical(m, shift) <= lax.shift_right_logical(n, shift)
    return vis, jnp.abs(n - m).astype(F32)


def attn_fwd(q, k, v, *, heads, softmax, masked, scale=1.0, log_gamma=None, link=None, est_us=0.0, name):
    (qa, dqk, q0, qs), (ka, _, k0, ks), (va, dv, v0, vs) = q, k, v
    Sq, Sk = qa.shape[0], ka.shape[0]
    tq, tk = _tile(Sq, ATTN_TILE, 8), _tile(Sk, ATTN_TILE, 8)
    nq, nk = Sq // tq, Sk // tk
    assert not masked or tq == tk

    def body(*refs):
        if softmax:
            q_ref, k_ref, v_ref, o_ref, lse_ref, m_ref, l_ref, acc_ref = refs
        else:
            q_ref, k_ref, v_ref, lg_ref, o_ref, acc_ref = refs
        qi, kj = pl.program_id(1), pl.program_id(2)

        @pl.when(kj == 0)
        def _():
            acc_ref[...] = jnp.zeros_like(acc_ref)
            if softmax:
                m_ref[...] = jnp.full_like(m_ref, NEG_INF)
                l_ref[...] = jnp.zeros_like(l_ref)

        def step(diagonal):
            s = lax.dot_general(q_ref[...].astype(MXU_DTYPE), k_ref[...].astype(MXU_DTYPE),
                                (((1,), (1,)), ((), ())), preferred_element_type=F32)
            vb = v_ref[...].astype(MXU_DTYPE)
            if diagonal or not softmax:
                vis, dist = _visible(qi, kj, tq, tk)
            if softmax:
                s = s * scale
                if diagonal:
                    s = jnp.where(vis, s, NEG_INF)
                m_old = m_ref[...]
                m_new = jnp.maximum(m_old, jnp.max(s, axis=-1, keepdims=True))
                p = jnp.exp(s - m_new)
                corr = jnp.exp(m_old - m_new)
                l_ref[...] = corr * l_ref[...] + jnp.sum(p, axis=-1, keepdims=True)
                acc_ref[...] = corr * acc_ref[...] + jnp.dot(p.astype(MXU_DTYPE), vb, preferred_element_type=F32)
                m_ref[...] = m_new
            else:
                decay = jnp.exp(lg_ref[0:1, 0:1] * dist)
                if diagonal:
                    decay = jnp.where(vis, decay, 0.0)
                p = s * decay
                acc_ref[...] += jnp.dot(p.astype(MXU_DTYPE), vb, preferred_element_type=F32)

        if masked:
            pl.when(kj < qi)(functools.partial(step, False))
            pl.when(kj == qi)(functools.partial(step, True))
        else:
            step(False)

        @pl.when(kj == nk - 1)
        def _():
            if softmax:
                o_ref[...] = (acc_ref[...] / l_ref[...]).astype(o_ref.dtype)
                lse_ref[...] = m_ref[...] + jnp.log(l_ref[...])
            else:
                o_ref[...] = acc_ref[...].astype(o_ref.dtype)

    kcap = (lambda qi, kj: jnp.minimum(kj, qi)) if masked else (lambda qi, kj: kj)
    in_specs = [pl.BlockSpec((tq, dqk), lambda h, qi, kj: (qi, q0 + qs * h)),
                pl.BlockSpec((tk, dqk), lambda h, qi, kj: (kcap(qi, kj), k0 + ks * h)),
                pl.BlockSpec((tk, dv), lambda h, qi, kj: (kcap(qi, kj), v0 + vs * h))]
    operands = [qa, ka, va]
    out_shape = [jax.ShapeDtypeStruct((Sq, heads * dv), F32)]
    out_specs = [pl.BlockSpec((tq, dv), lambda h, qi, kj: (qi, h))]
    scratch = []
    if softmax:
        out_shape.append(jax.ShapeDtypeStruct((heads, Sq, 1), F32))
        out_specs.append(pl.BlockSpec((None, tq, 1), lambda h, qi, kj: (h, qi, 0)))
        scratch += [pltpu.VMEM((tq, 1), F32), pltpu.VMEM((tq, 1), F32)]
    else:
        in_specs.append(pl.BlockSpec((None, 8, LANES), lambda h, qi, kj: (h, 0, 0)))
        operands.append(log_gamma)
    scratch.append(pltpu.VMEM((tq, dv), F32))
    chunks = link.take(est_us) if link is not None else ()
    return _call(body, grid=(heads, nq, nk), in_specs=in_specs, out_specs=out_specs, out_shape=out_shape,
                 operands=operands, scratch=scratch, name=name, chunks=chunks,
                 bufs=link.bufs if link is not None else None)


def attn_bwd(q, k, v, do, *, heads, softmax, masked, scale=1.0, log_gamma=None, o=None, lse=None,
             link=None, est_us=0.0, name):
    (qa, dqk, q0, qs), (ka, _, k0, ks), (va, dv, v0, vs) = q, k, v
    Sq, Sk = qa.shape[0], ka.shape[0]
    tq, tk = _tile(Sq, ATTN_TILE, 8), _tile(Sk, ATTN_TILE, 8)
    nq, nk = Sq // tq, Sk // tk
    assert not masked or tq == tk
    contract0 = (((0,), (0,)), ((), ()))

    def body(*refs):
        if softmax:
            q_ref, k_ref, v_ref, do_ref, o_ref, lse_ref, dq_ref, dk_ref, dv_ref = refs
        else:
            q_ref, k_ref, v_ref, do_ref, lg_ref, dq_ref, dk_ref, dv_ref = refs
        kj, qi = pl.program_id(1), pl.program_id(2)

        @pl.when(jnp.logical_and(kj == 0, qi == 0))
        def _():
            dq_ref[...] = jnp.zeros_like(dq_ref)

        @pl.when(qi == 0)
        def _():
            dk_ref[...] = jnp.zeros_like(dk_ref)
            dv_ref[...] = jnp.zeros_like(dv_ref)

        def step(diagonal):
            qb = q_ref[...].astype(MXU_DTYPE)
            kb = k_ref[...].astype(MXU_DTYPE)
            vb = v_ref[...].astype(MXU_DTYPE)
            dob = do_ref[...].astype(MXU_DTYPE)
            s = lax.dot_general(qb, kb, (((1,), (1,)), ((), ())), preferred_element_type=F32)
            dp = lax.dot_general(dob, vb, (((1,), (1,)), ((), ())), preferred_element_type=F32)
            if diagonal or not softmax:
                vis, dist = _visible(qi, kj, tq, tk)
            if softmax:
                s = s * scale
                if diagonal:
                    s = jnp.where(vis, s, NEG_INF)
                p = jnp.exp(s - lse_ref[...])
                delta = jnp.sum(do_ref[...].astype(F32) * o_ref[...], axis=-1, keepdims=True)
                ds = p * (dp - delta) * scale
            else:
                decay = jnp.exp(lg_ref[0:1, 0:1] * dist)
                if diagonal:
                    decay = jnp.where(vis, decay, 0.0)
                p = s * decay
                ds = dp * decay
            pb = p.astype(MXU_DTYPE)
            dsb = ds.astype(MXU_DTYPE)
            dv_ref[...] += lax.dot_general(pb, dob, contract0, preferred_element_type=F32)
            dk_ref[...] += lax.dot_general(dsb, qb, contract0, preferred_element_type=F32)
            rows = pl.ds(pl.multiple_of(qi * tq, tq), tq)
            dq_ref[rows, :] += jnp.dot(dsb, kb, preferred_element_type=F32)

        if masked:
            pl.when(qi > kj)(functools.partial(step, False))
            pl.when(qi == kj)(functools.partial(step, True))
        else:
            step(False)

    qcap = (lambda kj, qi: jnp.maximum(qi, kj)) if masked else (lambda kj, qi: qi)
    in_specs = [pl.BlockSpec((tq, dqk), lambda h, kj, qi: (qcap(kj, qi), q0 + qs * h)),
                pl.BlockSpec((tk, dqk), lambda h, kj, qi: (kj, k0 + ks * h)),
                pl.BlockSpec((tk, dv), lambda h, kj, qi: (kj, v0 + vs * h)),
                pl.BlockSpec((tq, dv), lambda h, kj, qi: (qcap(kj, qi), h))]
    operands = [qa, ka, va, do]
    if softmax:
        in_specs += [pl.BlockSpec((tq, dv), lambda h, kj, qi: (qcap(kj, qi), h)),
                     pl.BlockSpec((None, tq, 1), lambda h, kj, qi: (h, qcap(kj, qi), 0))]
        operands += [o, lse]
    else:
        in_specs.append(pl.BlockSpec((None, 8, LANES), lambda h, kj, qi: (h, 0, 0)))
        operands.append(log_gamma)
    chunks = link.take(est_us) if link is not None else ()
    return _call(
        body, grid=(heads, nk, nq), in_specs=in_specs,
        out_specs=(pl.BlockSpec((Sq, dqk), lambda h, kj, qi: (0, h)),
                   pl.BlockSpec((tk, dqk), lambda h, kj, qi: (kj, h)),
                   pl.BlockSpec((tk, dv), lambda h, kj, qi: (kj, h))),
        out_shape=(jax.ShapeDtypeStruct((Sq, heads * dqk), F32), jax.ShapeDtypeStruct((Sk, heads * dqk), F32),
                   jax.ShapeDtypeStruct((Sk, heads * dv), F32)),
        operands=operands, name=name, chunks=chunks, bufs=link.bufs if link is not None else None)


def _sigmoid(x):
    return 0.5 * jnp.tanh(0.5 * x) + 0.5


def ret_gate_fwd(o, proj, gate_col, gn, heads, name):
    S, W = o.shape
    dv = W // heads
    tm = _tile(S, ROW_TILE // 2, 8)

    def body(o_ref, g_ref, gn_ref, y_ref):
        for hi in range(heads):
            cols = slice(hi * dv, (hi + 1) * dv)
            oh = o_ref[:, cols]
            mu = jnp.mean(oh, axis=-1, keepdims=True)
            oc = oh - mu
            rstd = lax.rsqrt(jnp.mean(oc * oc, axis=-1, keepdims=True) + LN_EPS)
            gt = g_ref[:, cols]
            y_ref[:, cols] = (gt * _sigmoid(gt) * (oc * rstd) * gn_ref[:, cols]).astype(y_ref.dtype)

    row = pl.BlockSpec((tm, W), lambda i: (i, 0))
    return pl.pallas_call(
        body, out_shape=jax.ShapeDtypeStruct((S, W), MXU_DTYPE), grid=(S // tm,),
        in_specs=[row, pl.BlockSpec((tm, W), lambda i: (i, gate_col)), pl.BlockSpec((1, W), lambda i: (0, 0))],
        out_specs=row, compiler_params=_params(1), name=name)(o, proj, gn)


def ret_gate_bwd(dy, o, proj, gate_col, gn, heads, name):
    S, W = o.shape
    dv = W // heads
    tm = _tile(S, ROW_TILE // 2, 8)

    def body(dy_ref, o_ref, g_ref, gn_ref, do_ref, dgt_ref, dgn_ref):
        @pl.when(pl.program_id(0) == 0)
        def _():
            dgn_ref[...] = jnp.zeros_like(dgn_ref)

        for hi in range(heads):
            cols = slice(hi * dv, (hi + 1) * dv)
            oh = o_ref[:, cols]
            mu = jnp.mean(oh, axis=-1, keepdims=True)
            oc = oh - mu
            rstd = lax.rsqrt(jnp.mean(oc * oc, axis=-1, keepdims=True) + LN_EPS)
            xhat = oc * rstd
            gt = g_ref[:, cols]
            sg = _sigmoid(gt)
            gain = gn_ref[:, cols]
            dy = dy_ref[:, cols]
            dgt_ref[:, cols] = (dy * xhat * gain * (sg * (1.0 + gt * (1.0 - sg)))).astype(dgt_ref.dtype)
            dn = dy * (gt * sg)
            dgn_ref[:, cols] += jnp.sum(dn * xhat, axis=0, keepdims=True)
            dxh = dn * gain
            m1 = jnp.mean(dxh, axis=-1, keepdims=True)
            m2 = jnp.mean(dxh * xhat, axis=-1, keepdims=True)
            do_ref[:, cols] = (rstd * (dxh - m1 - xhat * m2)).astype(do_ref.dtype)

    row = pl.BlockSpec((tm, W), lambda i: (i, 0))
    vec = pl.BlockSpec((1, W), lambda i: (0, 0))
    return pl.pallas_call(
        body,
        out_shape=(jax.ShapeDtypeStruct((S, W), MXU_DTYPE), jax.ShapeDtypeStruct((S, W), MXU_DTYPE),
                   jax.ShapeDtypeStruct((1, W), F32)),
        grid=(S // tm,), in_specs=[row, row, pl.BlockSpec((tm, W), lambda i: (i, gate_col)), vec],
        out_specs=(row, row, vec), compiler_params=_params(1), name=name)(dy, o, proj, gn)


def _shift_down(x, s):
    rows = lax.broadcasted_iota(jnp.int32, x.shape, 0)
    return jnp.where(rows >= s, pltpu.roll(x, s, 0), 0.0)


def _shift_up(x, s):
    n = x.shape[0]
    rows = lax.broadcasted_iota(jnp.int32, x.shape, 0)
    return jnp.where(rows < n - s, pltpu.roll(x, n - s, 0), 0.0)


def _conv3(x, w_ref, b_ref):
    return (w_ref[2:3, :] * x + w_ref[1:2, :] * _shift_down(x, 1) + w_ref[0:1, :] * _shift_down(x, 2)
            + b_ref[...])


def conv_glu_fwd(hup, w, b, link, name):
    S, W2 = hup.shape
    F = W2 // 2
    tc = _tile(F, LANES)
    nb = F // tc

    def body(g_ref, v_ref, wg_ref, wv_ref, bg_ref, bv_ref, u_ref):
        cg = _conv3(g_ref[...], wg_ref, bg_ref)
        cv = _conv3(v_ref[...], wv_ref, bv_ref)
        u_ref[...] = (cg * _sigmoid(cg) * cv).astype(u_ref.dtype)

    def col(rows, off):
        return pl.BlockSpec((rows, tc), lambda j: (0, j + off))

    (u,) = _call(
        body, grid=(nb,), in_specs=[col(S, 0), col(S, nb), col(3, 0), col(3, nb), col(1, 0), col(1, nb)],
        out_specs=(col(S, 0),), out_shape=(jax.ShapeDtypeStruct((S, F), MXU_DTYPE),),
        operands=[hup, hup, w, w, b, b], name=name, chunks=link.take(ROWWISE_US["conv"][0]), bufs=link.bufs)
    return u


def conv_glu_bwd(du, hup, w, b, link, name):
    S, W2 = hup.shape
    F = W2 // 2
    tc = _tile(F, LANES)
    nb = F // tc

    def back(dc, x, w_ref, dh_ref, dw_ref, db_ref):
        dw_ref[2:3, :] = jnp.sum(dc * x, axis=0, keepdims=True)
        dw_ref[1:2, :] = jnp.sum(dc * _shift_down(x, 1), axis=0, keepdims=True)
        dw_ref[0:1, :] = jnp.sum(dc * _shift_down(x, 2), axis=0, keepdims=True)
        db_ref[...] = jnp.sum(dc, axis=0, keepdims=True)
        dh_ref[...] = (w_ref[2:3, :] * dc + w_ref[1:2, :] * _shift_up(dc, 1)
                       + w_ref[0:1, :] * _shift_up(dc, 2)).astype(dh_ref.dtype)

    def body(du_ref, g_ref, v_ref, wg_ref, wv_ref, bg_ref, bv_ref,
             dhg_ref, dhv_ref, dwg_ref, dwv_ref, dbg_ref, dbv_ref):
        xg, xv = g_ref[...], v_ref[...]
        cg = _conv3(xg, wg_ref, bg_ref)
        cv = _conv3(xv, wv_ref, bv_ref)
        sg = _sigmoid(cg)
        du = du_ref[...]
        back(du * cv * (sg * (1.0 + cg * (1.0 - sg))), xg, wg_ref, dhg_ref, dwg_ref, dbg_ref)
        back(du * (cg * sg), xv, wv_ref, dhv_ref, dwv_ref, dbv_ref)

    def col(rows, off):
        return pl.BlockSpec((rows, tc), lambda j: (0, j + off))

    return _call(
        body, grid=(nb,),
        in_specs=[col(S, 0), col(S, 0), col(S, nb), col(3, 0), col(3, nb), col(1, 0), col(1, nb)],
        out_specs=(col(S, 0), col(S, 0), col(3, 0), col(3, 0), col(1, 0), col(1, 0)),
        out_shape=(jax.ShapeDtypeStruct((S, F), MXU_DTYPE), jax.ShapeDtypeStruct((S, F), MXU_DTYPE),
                   jax.ShapeDtypeStruct((3, F), F32), jax.ShapeDtypeStruct((3, F), F32),
                   jax.ShapeDtypeStruct((1, F), F32), jax.ShapeDtypeStruct((1, F), F32)),
        operands=[du, hup, hup, w, w, b, b], name=name, chunks=link.take(ROWWISE_US["conv"][1]), bufs=link.bufs)


def loss_head(y, target, name):
    S, D = y.shape
    tm = _tile(S, ROW_TILE, 8)

    def body(y_ref, t_ref, dy_ref, loss_ref):
        @pl.when(pl.program_id(0) == 0)
        def _():
            loss_ref[...] = jnp.zeros_like(loss_ref)

        e = y_ref[...] - t_ref[...]
        dy_ref[...] = e * (1.0 / D)
        part = jnp.sum(jnp.sum(e * e, axis=-1, keepdims=True), axis=0, keepdims=True) * (0.5 / D)
        loss_ref[...] += jnp.broadcast_to(part, loss_ref.shape)

    row = pl.BlockSpec((tm, D), lambda i: (i, 0))
    return pl.pallas_call(
        body, out_shape=(jax.ShapeDtypeStruct((S, D), F32), jax.ShapeDtypeStruct((8, LANES), F32)),
        grid=(S // tm,), in_specs=[row, row], out_specs=(row, pl.BlockSpec((8, LANES), lambda i: (0, 0))),
        compiler_params=_params(1), name=name)(y, target)


def adam_update(parts, w, m, v, name, link=None):
    n_layers, n_parts, R, C = parts.shape
    tr = _tile(R, max(8, (1 << 19) // C), 8)
    c1 = 1.0 - ADAM_B1 ** ADAM_STEP
    c2 = 1.0 - ADAM_B2 ** ADAM_STEP

    def body(p_ref, w_ref, m_ref, v_ref, g_ref, d_ref, nm_ref, nv_ref):
        g = p_ref[0].astype(F32)
        for pi in range(1, n_parts):
            g = g + p_ref[pi].astype(F32)
        nm = ADAM_B1 * m_ref[...] + (1.0 - ADAM_B1) * g
        nv = ADAM_B2 * v_ref[...] + (1.0 - ADAM_B2) * (g * g)
        g_ref[...] = g
        nm_ref[...] = nm
        nv_ref[...] = nv
        d_ref[...] = -ADAM_LR * ((nm / c1) / (jnp.sqrt(nv / c2) + ADAM_EPS) + ADAM_WD * w_ref[...])

    row = pl.BlockSpec((None, tr, C), lambda l, i: (l, i, 0))
    out = jax.ShapeDtypeStruct((n_layers, R, C), F32)
    chunks = link.take(w.size * ADAM_BYTES_PER_ELEMENT / HBM_BYTES_PER_US) if link is not None else ()
    return _call(
        body, grid=(n_layers, R // tr),
        in_specs=[pl.BlockSpec((None, n_parts, tr, C), lambda l, i: (l, 0, i, 0)), row, row, row],
        out_specs=(row, row, row, row), out_shape=(out, out, out, out), operands=[parts, w, m, v], name=name,
        chunks=chunks, bufs=link.bufs if link is not None else None)


def pair_sum(own, got, core, name):
    _, R, C = own.shape
    tr = _tile(R, max(16, (1 << 19) // C), 16)

    def body(core_ref, a_ref, b_ref, o_ref):
        o_ref[...] = (a_ref[...].astype(F32) + b_ref[...].astype(F32)).astype(o_ref.dtype)

    grid_spec = pltpu.PrefetchScalarGridSpec(
        num_scalar_prefetch=1, grid=(N_CHIP, R // tr),
        in_specs=[pl.BlockSpec((None, tr, C), lambda j, i, core_ref: (2 * j + core_ref[0], i, 0)),
                  pl.BlockSpec((None, tr, C), lambda j, i, core_ref: (j, i, 0))],
        out_specs=pl.BlockSpec((None, tr, C), lambda j, i, core_ref: (j, i, 0)))
    return pl.pallas_call(
        body, out_shape=jax.ShapeDtypeStruct((N_CHIP, R, C), own.dtype), grid_spec=grid_spec,
        compiler_params=_params(2), name=name)(core, own, got)


def gather_small(block, name):
    m_per, n = block.shape

    def body(x_ref, out_ref, send_sems, recv_sems, local_sem):
        x, y, c = _place()
        me, sibling = (x, y, c), (x, y, 1 - c)
        chips = _other_chips(x, y)

        def rows(px, py, pc):
            return out_ref.at[pl.ds((4 * px + 2 * py + pc) * m_per, m_per), :]

        def copy(k, blk, to, src=None):
            return pltpu.make_async_remote_copy(
                src_ref=rows(*blk) if src is None else src, dst_ref=rows(*blk),
                send_sem=send_sems.at[k], recv_sem=recv_sems.at[k], device_id=to, device_id_type=MESH)

        mine = pltpu.make_async_copy(x_ref, rows(*me), local_sem)
        mine.start()
        first = [copy(0, me, sibling, src=x_ref)]
        first += [copy(1 + j, me, (*chip, c), src=x_ref) for j, chip in enumerate(chips)]
        for cp in first:
            cp.start()
        passed = [copy(4 + j, (*chip, c), sibling) for j, chip in enumerate(chips)]
        for j, chip in enumerate(chips):
            copy(1 + j, (*chip, c), me).wait_recv()
            passed[j].start()
        copy(0, sibling, me).wait_recv()
        for j, chip in enumerate(chips):
            copy(4 + j, (*chip, 1 - c), me).wait_recv()
        for cp in first + passed:
            cp.wait_send()
        mine.wait()

    return pl.pallas_call(
        body, out_shape=jax.ShapeDtypeStruct((N_DEV * m_per, n), block.dtype),
        in_specs=[pl.BlockSpec(memory_space=pltpu.VMEM)], out_specs=pl.BlockSpec(memory_space=pltpu.VMEM),
        scratch_shapes=[pltpu.SemaphoreType.DMA((7,)), pltpu.SemaphoreType.DMA((7,)), pltpu.SemaphoreType.DMA],
        compiler_params=pltpu.CompilerParams(vmem_limit_bytes=VMEM_LIMIT_BYTES), name=name)(block)


def _rope_tables(positions, d):
    inv_freq = ROPE_BASE ** (-jnp.arange(0, d, 2, dtype=F32) / d)
    ang = positions.astype(F32)[:, None] * inv_freq
    return jnp.cos(ang), jnp.sin(ang)


def _mla_pad(nope, rope):
    S, H, _ = nope.shape
    half = MLA_ROPE // 2
    z = jnp.zeros((S, H, LANES // 2 - half), nope.dtype)
    return jnp.concatenate([nope, rope[..., :half], z, rope[..., half:], z], axis=2).reshape(S, H * MLA_PAD)


def _mla_unpad(x, H):
    S = x.shape[0]
    half = MLA_ROPE // 2
    x3 = x.reshape(S, H, MLA_PAD)
    rope = jnp.concatenate([x3[..., MLA_NOPE:MLA_NOPE + half],
                            x3[..., MLA_NOPE + LANES // 2:MLA_NOPE + LANES // 2 + half]], axis=2)
    return x3[..., :MLA_NOPE], rope


def kernel(x, mem, positions, ret_w_in, ret_gn_g, ret_w_out, mla_w_in, mla_q_norm_g, mla_w_uq, mla_kv_norm_g, mla_w_ukv, mla_w_out, xa_w_q, xa_w_kv, xa_w_out, ffn_w_up, ffn_conv_w, ffn_conv_b, ffn_w_down, ln_mix_g, ln_mix_b, ln_mem_g, ln_mem_b, ln_ffn_g, ln_ffn_b, loss_target, m_ret_w_in, m_ret_gn_g, m_ret_w_out, m_mla_w_in, m_mla_q_norm_g, m_mla_w_uq, m_mla_kv_norm_g, m_mla_w_ukv, m_mla_w_out, m_xa_w_q, m_xa_w_kv, m_xa_w_out, m_ffn_w_up, m_ffn_conv_w, m_ffn_conv_b, m_ffn_w_down, m_ln_mix_g, m_ln_mix_b, m_ln_mem_g, m_ln_mem_b, m_ln_ffn_g, m_ln_ffn_b, v_ret_w_in, v_ret_gn_g, v_ret_w_out, v_mla_w_in, v_mla_q_norm_g, v_mla_w_uq, v_mla_kv_norm_g, v_mla_w_ukv, v_mla_w_out, v_xa_w_q, v_xa_w_kv, v_xa_w_out, v_ffn_w_up, v_ffn_conv_w, v_ffn_conv_b, v_ffn_w_down, v_ln_mix_g, v_ln_mix_b, v_ln_mem_g, v_ln_mem_b, v_ln_ffn_g, v_ln_ffn_b):
    weights = dict(ret_w_in=ret_w_in, ret_gn_g=ret_gn_g, ret_w_out=ret_w_out, mla_w_in=mla_w_in,
                   mla_q_norm_g=mla_q_norm_g, mla_w_uq=mla_w_uq, mla_kv_norm_g=mla_kv_norm_g,
                   mla_w_ukv=mla_w_ukv, mla_w_out=mla_w_out, xa_w_q=xa_w_q, xa_w_kv=xa_w_kv, xa_w_out=xa_w_out,
                   ffn_w_up=ffn_w_up, ffn_conv_w=ffn_conv_w, ffn_conv_b=ffn_conv_b, ffn_w_down=ffn_w_down,
                   ln_mix_g=ln_mix_g, ln_mix_b=ln_mix_b, ln_mem_g=ln_mem_g, ln_mem_b=ln_mem_b,
                   ln_ffn_g=ln_ffn_g, ln_ffn_b=ln_ffn_b)
    mom_m = dict(ret_w_in=m_ret_w_in, ret_gn_g=m_ret_gn_g, ret_w_out=m_ret_w_out, mla_w_in=m_mla_w_in,
                 mla_q_norm_g=m_mla_q_norm_g, mla_w_uq=m_mla_w_uq, mla_kv_norm_g=m_mla_kv_norm_g,
                 mla_w_ukv=m_mla_w_ukv, mla_w_out=m_mla_w_out, xa_w_q=m_xa_w_q, xa_w_kv=m_xa_w_kv,
                 xa_w_out=m_xa_w_out, ffn_w_up=m_ffn_w_up, ffn_conv_w=m_ffn_conv_w, ffn_conv_b=m_ffn_conv_b,
                 ffn_w_down=m_ffn_w_down, ln_mix_g=m_ln_mix_g, ln_mix_b=m_ln_mix_b, ln_mem_g=m_ln_mem_g,
                 ln_mem_b=m_ln_mem_b, ln_ffn_g=m_ln_ffn_g, ln_ffn_b=m_ln_ffn_b)
    mom_v = dict(ret_w_in=v_ret_w_in, ret_gn_g=v_ret_gn_g, ret_w_out=v_ret_w_out, mla_w_in=v_mla_w_in,
                 mla_q_norm_g=v_mla_q_norm_g, mla_w_uq=v_mla_w_uq, mla_kv_norm_g=v_mla_kv_norm_g,
                 mla_w_ukv=v_mla_w_ukv, mla_w_out=v_mla_w_out, xa_w_q=v_xa_w_q, xa_w_kv=v_xa_w_kv,
                 xa_w_out=v_xa_w_out, ffn_w_up=v_ffn_w_up, ffn_conv_w=v_ffn_conv_w, ffn_conv_b=v_ffn_conv_b,
                 ffn_w_down=v_ffn_w_down, ln_mix_g=v_ln_mix_g, ln_mix_b=v_ln_mix_b, ln_mem_g=v_ln_mem_g,
                 ln_mem_b=v_ln_mem_b, ln_ffn_g=v_ln_ffn_g, ln_ffn_b=v_ln_ffn_b)
    order = list(weights)
    BIG = dict(ret_w_in=2, ret_w_out=1, mla_w_in=1, mla_w_uq=2, mla_w_ukv=2, mla_w_out=1,
               xa_w_q=1, xa_w_kv=2, xa_w_out=1, ffn_w_up=2, ffn_w_down=1)
    SMALL_CUT = ("ffn_conv_w", "mla_q_norm_g", "mla_kv_norm_g")
    REPLICATED = ("ret_gn_g", "ffn_conv_b", "ln_mix_g", "ln_mix_b", "ln_mem_g", "ln_mem_b", "ln_ffn_g", "ln_ffn_b")

    x = x[0]
    mem = mem[0]
    positions = positions[0]
    target = loss_target[0]
    S, D = x.shape
    depth = ln_mix_g.shape[0]
    alpha = (2 * depth) ** 0.25
    ret_dk = D // RET_HEADS
    ret_dv = 2 * D // RET_HEADS
    ret_qkw = RET_HEADS * ret_dk
    ret_vw = RET_HEADS * ret_dv
    xa_d = D // XA_HEADS
    assert 2 * ret_qkw == ret_vw and MLA_NOPE == LANES and MLA_V == LANES and MLA_ROPE == LANES // 2
    assert MLA_Q_RANK == MLA_KV_RANK and CHUNK & (CHUNK - 1) == 0
    core = lax.axis_index("c").astype(jnp.int32).reshape(1)
    dev = 4 * lax.axis_index("x") + 2 * lax.axis_index("y") + lax.axis_index("c")

    gather, scatter = _Link(GATHER_BYTES_PER_US), _Link(SCATTER_BYTES_PER_US)
    shard_b = {n: weights[n].astype(PAY_DTYPE) for n in BIG}
    staged = set()

    def units(layer):
        mixer = ("ret_w_in", "ret_w_out") if layer % 2 == 0 else ("mla_w_in", "mla_w_uq", "mla_w_ukv", "mla_w_out")
        return [(n, layer // 2) for n in mixer] + [(n, layer) for n in
                                                    ("xa_w_q", "xa_w_kv", "xa_w_out", "ffn_w_up", "ffn_w_down")]

    def geometry(n):
        _, k, nn = weights[n].shape
        return BIG[n], k, nn

    def after_first_level(key, full, axis, k, nn):
        def to_sibling():
            chunk = _Chunk("gather2", None, key, full, 0, 0, k, nn, axis=axis, k=k, n=nn)
            gather.side.append((chunk, lambda: staged.add(key)))

        def relay():
            gather.push("relay", None, key, full, 0, k // 2, nn, done=to_sibling, front=True,
                        axis=axis, k=k, n=nn, half=k // 2)
        return relay

    for layer in range(depth):
        for n, j in units(layer):
            axis, k, nn = geometry(n)
            full = (N_DEV * k, nn) if axis == 1 else (k, N_DEV * nn)
            gather.push("gather", shard_b[n], (n, j), full, j, k, nn,
                        done=after_first_level((n, j), full, axis, k, nn), axis=axis, k=k, n=nn)

    def weight(n, j):
        key = (n, j)
        while key not in staged:
            gather.flush(key, name="gather_rest")
            gather.settle()
            if not any(ch.key == key for ch in gather.queue):
                break
        if key not in staged:
            gather.side = [(ch, then) for ch, then in gather.side if ch.key != key]
            axis, k, nn = geometry(n)
            gather.bufs[key] = forward_to_sibling(gather.bufs[key], axis, k, nn, name="gather_level2")
            staged.add(key)
        return gather.bufs[key]

    def pack(arrays):
        flat = jnp.concatenate([a.reshape(-1) for a in arrays])
        rows = -(-flat.shape[0] // (8 * LANES)) * 8
        return jnp.pad(flat, (0, rows * LANES - flat.shape[0])).reshape(rows, LANES)

    def unpack(flat, like):
        out, at = [], 0
        for a in like:
            out.append(flat[at:at + a.size].reshape(a.shape))
            at += a.size
        return out

    small_local = [weights[n] for n in SMALL_CUT]
    blk = pack(small_local)
    allsmall = gather_small(blk, name="gather_small_weights").reshape(N_DEV, -1)
    per_dev = [unpack(allsmall[d], small_local) for d in range(N_DEV)]
    conv_w_full, qg_full, kvg_full = [jnp.concatenate([per_dev[d][i] for d in range(N_DEV)], axis=-1)
                                      for i in range(len(SMALL_CUT))]

    rcos, rsin = _rope_tables(positions, ret_dk)
    mcos, msin = _rope_tables(positions, MLA_ROPE)
    zq = jnp.zeros((S, LANES // 2 - MLA_ROPE // 2), F32)
    mla_ta = jnp.concatenate([mcos, zq, mcos, zq], axis=1)
    mla_tb = jnp.concatenate([-msin, zq, msin, zq], axis=1)
    log_gamma = jnp.log(1.0 - 2.0 ** (-5.0 - jnp.arange(RET_HEADS, dtype=F32)))
    log_gamma = jnp.broadcast_to(log_gamma[:, None, None], (RET_HEADS, 8, LANES))
    mla_scale = (MLA_NOPE + MLA_ROPE) ** -0.5
    xa_scale = xa_d ** -0.5
    mem_b = mem.astype(MXU_DTYPE)

    def vec(a, l):
        return a[l][None, :]

    saved = []
    h = x
    hb = x.astype(MXU_DTYPE)
    for layer in range(depth):
        j = layer // 2
        sv = {}
        sv["hb_mix"] = hb
        if layer % 2 == 0:
            proj = mm(hb, weight("ret_w_in", j), link=gather, name="ret_in")
            qk = ret_rope_fwd(proj, rcos, rsin, 2 * RET_HEADS, ret_dk // 2, RET_HEADS, ret_dk ** -0.5, name="ret_rope")
            (o,) = attn_fwd((qk, ret_dk, 0, 1), (qk, ret_dk, RET_HEADS, 1), (proj, ret_dv, RET_HEADS, 1),
                            heads=RET_HEADS, softmax=False, masked=True, log_gamma=log_gamma, link=gather,
                            est_us=ATTN_US["ret"][0], name="ret_attn")
            yb = ret_gate_fwd(o, proj, 2, vec(ret_gn_g, j), RET_HEADS, name="ret_gate")
            mix = mm(yb, weight("ret_w_out", j), link=gather, name="ret_out")
            sv.update(proj=proj, qk=qk, o=o, yb=yb)
        else:
            proj = mm(hb, weight("mla_w_in", j), link=gather, name="mla_in")
            cq = rms_fwd(proj, 0, MLA_Q_RANK, vec(qg_full, j), name="mla_q_norm")
            ckv = rms_fwd(proj, 1, MLA_KV_RANK, vec(kvg_full, j), name="mla_kv_norm")
            qf = mm(cq, weight("mla_w_uq", j), link=gather, name="mla_uq")
            kvf = mm(ckv, weight("mla_w_ukv", j), link=gather, name="mla_ukv")
            q3 = qf.reshape(S, MLA_HEADS, MLA_NOPE + MLA_ROPE)
            kv3 = kvf.reshape(S, MLA_HEADS, MLA_NOPE + MLA_V)
            k_rope = jnp.broadcast_to(proj[:, None, MLA_Q_RANK + MLA_KV_RANK:], (S, MLA_HEADS, MLA_ROPE))
            q_pad = _mla_pad(q3[..., :MLA_NOPE], q3[..., MLA_NOPE:])
            k_pad = _mla_pad(kv3[..., :MLA_NOPE], k_rope)
            (qr,) = mla_rope(q_pad, mla_ta, mla_tb, MLA_HEADS, backward=False, head_sum=False,
                             out_dtype=MXU_DTYPE, name="mla_rope_q")
            (kr,) = mla_rope(k_pad, mla_ta, mla_tb, MLA_HEADS, backward=False, head_sum=False,
                             out_dtype=MXU_DTYPE, name="mla_rope_k")
            o, lse = attn_fwd((qr, MLA_PAD, 0, 1), (kr, MLA_PAD, 0, 1), (kvf, MLA_V, 1, 2), heads=MLA_HEADS,
                              softmax=True, masked=True, scale=mla_scale, link=gather, est_us=ATTN_US["mla"][0],
                              name="mla_attn")
            ob = o.astype(MXU_DTYPE)
            mix = mm(ob, weight("mla_w_out", j), link=gather, name="mla_out")
            sv.update(proj=proj, cq=cq, ckv=ckv, kvf=kvf, qr=qr, kr=kr, o=o, ob=ob, lse=lse)
        h, hb, sv["xhat_mix"], sv["rstd_mix"] = ln_fwd(h, mix, vec(ln_mix_g, layer), vec(ln_mix_b, layer), alpha,
                                                       gather, name="ln_mix")
        sv["hb_mem"] = hb
        q = mm(hb, weight("xa_w_q", layer), link=gather, name="xa_q")
        kvm = mm(mem_b, weight("xa_w_kv", layer), link=gather, name="xa_kv")
        o, lse = attn_fwd((q, xa_d, 0, 1), (kvm, xa_d, 0, 1), (kvm, xa_d, XA_HEADS, 1), heads=XA_HEADS,
                          softmax=True, masked=False, scale=xa_scale, link=gather, est_us=ATTN_US["xa"][0],
                          name="xa_attn")
        ob = o.astype(MXU_DTYPE)
        mix = mm(ob, weight("xa_w_out", layer), link=gather, name="xa_out")
        sv.update(xa_q=q, xa_kvm=kvm, xa_o=o, xa_ob=ob, xa_lse=lse)
        h, hb, sv["xhat_mem"], sv["rstd_mem"] = ln_fwd(h, mix, vec(ln_mem_g, layer), vec(ln_mem_b, layer), alpha,
                                                       gather, name="ln_mem")
        sv["hb_ffn"] = hb
        hup = mm(hb, weight("ffn_w_up", layer), link=gather, name="ffn_up")
        u = conv_glu_fwd(hup, conv_w_full[layer], vec(ffn_conv_b, layer), gather, name="ffn_conv")
        mix = mm(u, weight("ffn_w_down", layer), link=gather, name="ffn_down")
        sv.update(hup=hup, u=u)
        h, hb, sv["xhat_ffn"], sv["rstd_ffn"] = ln_fwd(h, mix, vec(ln_ffn_g, layer), vec(ln_ffn_b, layer), alpha,
                                                       gather, name="ln_ffn")
        saved.append(sv)

    dh, loss_blk = loss_head(h, target, name="loss_head")
    loss = lax.psum(loss_blk[0, 0], ("x", "y", "c"))

    small = {n: [None] * weights[n].shape[0] for n in REPLICATED + SMALL_CUT}
    W = gather.bufs

    def wgrad(n, a, d, l, tag):
        axis, k, nn = geometry(n)
        if axis == 2:
            slabs = mm(a, d, mode="tn", out_dtype=PAY_DTYPE, slab_width=nn, link=scatter, name=tag)
        else:
            slabs = mm(a, d, mode="tn", out_dtype=PAY_DTYPE, link=scatter, name=tag).reshape(N_DEV, k, nn)
        n_layers = weights[n].shape[0]
        pair = ("pair", n, l)

        def then():
            sums = pair_sum(slabs, scatter.bufs.pop(pair), core, name="pairsum_" + n)
            scatter.push("scatter", sums, n, (n_layers, N_CHIP, k, nn), l, k, nn)

        scatter.side.append((_Chunk("swap", slabs, pair, (N_CHIP, k, nn), 0, 0, k, nn), then))

    for layer in reversed(range(depth)):
        j = layer // 2
        sv = saved[layer]
        dz, dzb, dg, db = ln_bwd(dh, sv["xhat_ffn"], sv["rstd_ffn"], vec(ln_ffn_g, layer), scatter, name="ln_ffn_bwd")
        small["ln_ffn_g"][layer], small["ln_ffn_b"][layer] = dg[0], db[0]
        wgrad("ffn_w_down", sv["u"], dzb, layer, "ffn_down_dw")
        du = mm(dzb, W[("ffn_w_down", layer)], mode="nt", link=scatter, name="ffn_down_dx")
        dhg, dhv, dwg, dwv, dbg, dbv = conv_glu_bwd(du, sv["hup"], conv_w_full[layer], vec(ffn_conv_b, layer),
                                                    scatter, name="ffn_conv_bwd")
        small["ffn_conv_w"][layer] = jnp.concatenate([dwg, dwv], axis=1)
        small["ffn_conv_b"][layer] = jnp.concatenate([dbg, dbv], axis=1)[0]
        dhup = jnp.concatenate([dhg, dhv], axis=1)
        wgrad("ffn_w_up", sv["hb_ffn"], dhup, layer, "ffn_up_dw")
        dh = mm(dhup, W[("ffn_w_up", layer)], mode="nt", add=dz, add_scale=alpha, link=scatter, name="ffn_up_dx")
        dz, dzb, dg, db = ln_bwd(dh, sv["xhat_mem"], sv["rstd_mem"], vec(ln_mem_g, layer), scatter, name="ln_mem_bwd")
        small["ln_mem_g"][layer], small["ln_mem_b"][layer] = dg[0], db[0]
        wgrad("xa_w_out", sv["xa_ob"], dzb, layer, "xa_out_dw")
        do = mm(dzb, W[("xa_w_out", layer)], mode="nt", link=scatter, name="xa_out_dx")
        dq, dk, dv = attn_bwd((sv["xa_q"], xa_d, 0, 1), (sv["xa_kvm"], xa_d, 0, 1), (sv["xa_kvm"], xa_d, XA_HEADS, 1),
                              do, heads=XA_HEADS, softmax=True, masked=False, scale=xa_scale, o=sv["xa_o"],
                              lse=sv["xa_lse"], link=scatter, est_us=ATTN_US["xa"][1], name="xa_attn_bwd")
        dqb = dq.astype(MXU_DTYPE)
        wgrad("xa_w_kv", mem_b, jnp.concatenate([dk, dv], axis=1).astype(MXU_DTYPE), layer, "xa_kv_dw")
        wgrad("xa_w_q", sv["hb_mem"], dqb, layer, "xa_q_dw")
        dh = mm(dqb, W[("xa_w_q", layer)], mode="nt", add=dz, add_scale=alpha, link=scatter, name="xa_q_dx")
        dz, dzb, dg, db = ln_bwd(dh, sv["xhat_mix"], sv["rstd_mix"], vec(ln_mix_g, layer), scatter, name="ln_mix_bwd")
        small["ln_mix_g"][layer], small["ln_mix_b"][layer] = dg[0], db[0]
        if layer % 2 == 0:
            wgrad("ret_w_out", sv["yb"], dzb, j, "ret_out_dw")
            dy = mm(dzb, W[("ret_w_out", j)], mode="nt", link=scatter, name="ret_out_dx")
            do, dgate, dgn = ret_gate_bwd(dy, sv["o"], sv["proj"], 2, vec(ret_gn_g, j), RET_HEADS, name="ret_gate_bwd")
            small["ret_gn_g"][j] = dgn[0]
            dq, dk, dv = attn_bwd((sv["qk"], ret_dk, 0, 1), (sv["qk"], ret_dk, RET_HEADS, 1),
                                  (sv["proj"], ret_dv, RET_HEADS, 1), do, heads=RET_HEADS, softmax=False,
                                  masked=True, log_gamma=log_gamma, link=scatter, est_us=ATTN_US["ret"][1],
                                  name="ret_attn_bwd")
            dqk = ret_rope_bwd(dq, dk, rcos, rsin, ret_dk // 2, ret_dk ** -0.5, name="ret_rope_bwd")
            dproj = jnp.concatenate([dqk, dv.astype(MXU_DTYPE), dgate], axis=1)
            wgrad("ret_w_in", sv["hb_mix"], dproj, j, "ret_in_dw")
            dh = mm(dproj, W[("ret_w_in", j)], mode="nt", add=dz, add_scale=alpha, link=scatter, name="ret_in_dx")
        else:
            wgrad("mla_w_out", sv["ob"], dzb, j, "mla_out_dw")
            do = mm(dzb, W[("mla_w_out", j)], mode="nt", link=scatter, name="mla_out_dx")
            dq, dk, dv = attn_bwd((sv["qr"], MLA_PAD, 0, 1), (sv["kr"], MLA_PAD, 0, 1), (sv["kvf"], MLA_V, 1, 2), do,
                                  heads=MLA_HEADS, softmax=True, masked=True, scale=mla_scale, o=sv["o"],
                                  lse=sv["lse"], link=scatter, est_us=ATTN_US["mla"][1], name="mla_attn_bwd")
            (dq_un,) = mla_rope(dq, mla_ta, mla_tb, MLA_HEADS, backward=True, head_sum=False, out_dtype=F32,
                                name="mla_rope_q_bwd")
            dk_un, dk_rope_sum = mla_rope(dk, mla_ta, mla_tb, MLA_HEADS, backward=True, head_sum=True, out_dtype=F32,
                                          name="mla_rope_k_bwd")
            dq_nope, dq_rope = _mla_unpad(dq_un, MLA_HEADS)
            dqf = jnp.concatenate([dq_nope, dq_rope], axis=2).reshape(S, -1).astype(MXU_DTYPE)
            dk_nope, _ = _mla_unpad(dk_un, MLA_HEADS)
            dkvf = jnp.concatenate([dk_nope, dv.reshape(S, MLA_HEADS, MLA_V)], axis=2).reshape(S, -1).astype(MXU_DTYPE)
            half = MLA_ROPE // 2
            dk_rope = jnp.concatenate([dk_rope_sum[:, :half], dk_rope_sum[:, LANES // 2:LANES // 2 + half]], axis=1)
            wgrad("mla_w_uq", sv["cq"], dqf, j, "mla_uq_dw")
            dcq = mm(dqf, W[("mla_w_uq", j)], mode="nt", link=scatter, name="mla_uq_dx")
            wgrad("mla_w_ukv", sv["ckv"], dkvf, j, "mla_ukv_dw")
            dckv = mm(dkvf, W[("mla_w_ukv", j)], mode="nt", link=scatter, name="mla_ukv_dx")
            dcq_in, dqg = rms_bwd(dcq, sv["proj"], 0, MLA_Q_RANK, vec(qg_full, j), name="mla_q_norm_bwd")
            dckv_in, dkvg = rms_bwd(dckv, sv["proj"], 1, MLA_KV_RANK, vec(kvg_full, j), name="mla_kv_norm_bwd")
            small["mla_q_norm_g"][j], small["mla_kv_norm_g"][j] = dqg[0], dkvg[0]
            dproj = jnp.concatenate([dcq_in, dckv_in, dk_rope], axis=1).astype(MXU_DTYPE)
            wgrad("mla_w_in", sv["hb_mix"], dproj, j, "mla_in_dw")
            dh = mm(dproj, W[("mla_w_in", j)], mode="nt", add=dz, add_scale=alpha, link=scatter, name="mla_in_dx")
    grad_x = dh[None]

    grads, deltas, new_m, new_v = {}, {}, {}, {}
    by_readiness = ("mla_w_out", "mla_w_ukv", "mla_w_uq", "mla_w_in", "ffn_w_down", "ffn_w_up", "xa_w_out", "xa_w_kv",
                    "xa_w_q", "ret_w_out", "ret_w_in")
    assert sorted(by_readiness) == sorted(BIG)
    scatter.ride_out(name="scatter_rest")
    for n in by_readiness:
        scatter.flush(n, name="scatter_rest")
        grads[n], deltas[n], new_m[n], new_v[n] = adam_update(scatter.bufs[n], weights[n], mom_m[n], mom_v[n],
                                                              name="adam_" + n, link=scatter)
    assert not scatter.queue and not scatter.side

    small_names = list(REPLICATED + SMALL_CUT)
    partial = [jnp.stack(small[n]) for n in small_names]
    allpart = gather_small(pack(partial), name="gather_small_grads")
    rows = allpart.shape[0] // N_DEV
    allpart = allpart.reshape(N_DEV, rows, LANES)

    rep_names = list(REPLICATED)
    rep_w = pack([weights[n] for n in rep_names])
    rep_m = pack([mom_m[n] for n in rep_names])
    rep_v = pack([mom_v[n] for n in rep_names])
    rep_rows = rep_w.shape[0]
    rep_size = sum(weights[n].size for n in rep_names)
    flat_parts = allpart.reshape(N_DEV, rows * LANES)
    rep_parts = jnp.pad(flat_parts[:, :rep_size], ((0, 0), (0, rep_rows * LANES - rep_size)))
    outs = adam_update(rep_parts.reshape(1, N_DEV, rep_rows, LANES), rep_w[None], rep_m[None], rep_v[None],
                       name="adam_replicated")
    for o_, dst in zip(outs, (grads, deltas, new_m, new_v)):
        for n, a in zip(rep_names, unpack(o_.reshape(-1), [weights[n] for n in rep_names])):
            dst[n] = a

    cut_names = list(SMALL_CUT)
    cut_full = [jnp.stack(small[n]) for n in cut_names]
    cut_parts = []
    at = rep_size
    for n, a in zip(cut_names, cut_full):
        whole = flat_parts[:, at:at + a.size].reshape((N_DEV,) + a.shape)
        at += a.size
        width = weights[n].shape[-1]
        start = (0,) * (whole.ndim - 1) + (dev * width,)
        mine = lax.dynamic_slice(whole, start, whole.shape[:-1] + (width,))
        cut_parts.append(mine.reshape(N_DEV, -1))
    cut_parts = jnp.concatenate(cut_parts, axis=1)
    cut_w = pack([weights[n] for n in cut_names])
    cut_m = pack([mom_m[n] for n in cut_names])
    cut_v = pack([mom_v[n] for n in cut_names])
    cut_rows = cut_w.shape[0]
    cut_parts = jnp.pad(cut_parts, ((0, 0), (0, cut_rows * LANES - cut_parts.shape[1])))
    outs = adam_update(cut_parts.reshape(1, N_DEV, cut_rows, LANES), cut_w[None], cut_m[None], cut_v[None],
                       name="adam_small_cut")
    for o_, dst in zip(outs, (grads, deltas, new_m, new_v)):
        for n, a in zip(cut_names, unpack(o_.reshape(-1), [weights[n] for n in cut_names])):
            dst[n] = a

    return (loss, grad_x, *[grads[n] for n in order], *[deltas[n] for n in order],
            *[new_m[n] for n in order], *[new_v[n] for n in order])
```

```python
import functools
import math

import jax
import jax.numpy as jnp
from jax import lax
from jax.experimental import pallas as pl
from jax.experimental.pallas import tpu as pltpu

F32 = jnp.float32
MXU_DTYPE = jnp.bfloat16
PAY_DTYPE = jnp.bfloat16
MESH = pl.DeviceIdType.MESH
N_DEV = 8
N_CHIP = 4

DEPTH = 4
CHUNK = 64
RET_HEADS = 8
MLA_HEADS = 16
MLA_Q_RANK = 512
MLA_KV_RANK = 512
MLA_NOPE = 128
MLA_ROPE = 64
MLA_V = 128
MLA_PAD = 256
XA_HEADS = 4
ROPE_BASE = 10000.0
LN_EPS = 1e-5
RMS_EPS = 1e-6
NEG_INF = -1e30
ADAM_LR = 0.001
ADAM_B1 = 0.9
ADAM_B2 = 0.999
ADAM_EPS = 1e-08
ADAM_WD = 0.01
ADAM_STEP = 10

LANES = 128
VMEM_LIMIT_BYTES = 48 * 1024 * 1024
TILE_M = 1024
TILE_N = 1024
TILE_K = 2048
SLAB_TILE_N = 1536
ATTN_TILE = 512
ROW_TILE = 256

MXU_FLOPS_PER_US = 7.0e8
GATHER_BYTES_PER_US = 5.5e4
SCATTER_BYTES_PER_US = 2.75e4
CHUNK_BYTES = 1280 * 1024
HOST_CHUNKS = 12
LEVEL2_US = 30.0
SIDE_CHUNKS = 4
ATTN_US = {"ret": (125.0, 160.0), "mla": (320.0, 270.0), "xa": (35.0, 35.0)}
ROWWISE_US = {"ln": (25.0, 22.0), "conv": (65.0, 110.0)}


def _tile(n, cap, mult=LANES):
    best = None
    for t in range(mult, min(n, cap) + 1, mult):
        if n % t == 0:
            best = t
    return n if best is None else best


def _params(n_axes):
    return pltpu.CompilerParams(dimension_semantics=("arbitrary",) * n_axes,
                                vmem_limit_bytes=VMEM_LIMIT_BYTES)


def _place():
    return lax.axis_index("x"), lax.axis_index("y"), lax.axis_index("c")


def _other_chips(x, y):
    return [(1 - x, y), (x, 1 - y), (1 - x, 1 - y)]


def _remote(src, dst, send, recv, to):
    return pltpu.make_async_remote_copy(src_ref=src, dst_ref=dst, send_sem=send, recv_sem=recv,
                                        device_id=to, device_id_type=MESH)


class _Chunk:
    def __init__(self, kind, src, key, dst_shape, layer, r0, rc, cols, axis=None, k=None, n=None, half=None):
        self.kind, self.src, self.key, self.dst_shape = kind, src, key, dst_shape
        self.layer, self.r0, self.rc, self.axis, self.k, self.n, self.half = layer, r0, rc, axis, k, n, half
        self.nbytes = rc * cols * jnp.dtype(PAY_DTYPE).itemsize

    def copies(self, src_ref, dst_ref, send, recv, loc, i):
        x, y, c = _place()
        chips = _other_chips(x, y)
        rows = pl.ds(self.r0, self.rc)
        sibling, x_nbr, y_nbr, diagonal = (x, y, 1 - c), (1 - x, y, c), (x, 1 - y, c), (1 - x, 1 - y, c)
        if self.kind == "gather":
            peers = [sibling, x_nbr, y_nbr]
            src = src_ref.at[self.layer, rows, :]

            def slab(px, py, pc):
                idx = 4 * px + 2 * py + pc
                if self.axis == 1:
                    return dst_ref.at[pl.ds(idx * self.k + self.r0, self.rc), :]
                return dst_ref.at[rows, pl.ds(idx * self.n, self.n)]

            sends = [_remote(src, slab(x, y, c), send.at[4 * i + t], recv.at[4 * i + t], p)
                     for t, p in enumerate(peers)]
            recvs = [_remote(src, slab(*p), send.at[4 * i + t], recv.at[4 * i + t], p)
                     for t, p in enumerate(peers)]
            return sends, recvs, pltpu.make_async_copy(src, slab(x, y, c), loc.at[i])
        if self.kind == "relay":
            def piece(p, r):
                idx = 4 * p[0] + 2 * p[1] + p[2]
                if self.axis == 1:
                    return dst_ref.at[pl.ds(idx * self.k + r, self.rc), :]
                return dst_ref.at[pl.ds(r, self.rc), pl.ds(idx * self.n, self.n)]

            top, bottom = self.r0, self.half + self.r0
            sends = [_remote(piece(y_nbr, top), piece(y_nbr, top), send.at[4 * i], recv.at[4 * i], x_nbr),
                     _remote(piece(x_nbr, bottom), piece(x_nbr, bottom), send.at[4 * i + 1], recv.at[4 * i + 1], y_nbr)]
            recvs = [_remote(piece(diagonal, top), piece(diagonal, top), send.at[4 * i], recv.at[4 * i], x_nbr),
                     _remote(piece(diagonal, bottom), piece(diagonal, bottom), send.at[4 * i + 1], recv.at[4 * i + 1],
                             y_nbr)]
            return sends, recvs, None
        if self.kind == "gather2":
            def slab(px, py, pc):
                idx = 4 * px + 2 * py + pc
                if self.axis == 1:
                    return dst_ref.at[pl.ds(idx * self.k, self.k), :]
                return dst_ref.at[:, pl.ds(idx * self.n, self.n)]

            sends = [_remote(slab(px, py, c), slab(px, py, c), send.at[4 * i + t], recv.at[4 * i + t], sibling)
                     for t, (px, py) in enumerate(chips)]
            recvs = [_remote(slab(px, py, 1 - c), slab(px, py, 1 - c), send.at[4 * i + t], recv.at[4 * i + t], sibling)
                     for t, (px, py) in enumerate(chips)]
            return sends, recvs, None
        if self.kind == "swap":
            sends = [_remote(src_ref.at[2 * j + (1 - c)], dst_ref.at[j], send.at[4 * i + j], recv.at[4 * i + j], sibling)
                     for j in range(N_CHIP)]
            recvs = [_remote(src_ref.at[2 * j + c], dst_ref.at[j], send.at[4 * i + j], recv.at[4 * i + j], sibling)
                     for j in range(N_CHIP)]
            return sends, recvs, None
        mine = 2 * x + y
        own = src_ref.at[mine, rows, :]
        sends = [_remote(src_ref.at[2 * px + py, rows, :], dst_ref.at[self.layer, mine, rows, :],
                         send.at[4 * i + t], recv.at[4 * i + t], (px, py, c)) for t, (px, py) in enumerate(chips)]
        recvs = [_remote(own, dst_ref.at[self.layer, 2 * px + py, rows, :],
                         send.at[4 * i + t], recv.at[4 * i + t], (px, py, c)) for t, (px, py) in enumerate(chips)]
        return sends, recvs, pltpu.make_async_copy(own, dst_ref.at[self.layer, mine, rows, :], loc.at[i])


def _call(body, *, grid, in_specs, out_specs, out_shape, operands, scratch=(), name, chunks=(), bufs=None):
    n_in, n_out, n_scr = len(operands), len(out_shape), len(scratch)
    params = _params(len(grid)) if grid else pltpu.CompilerParams(vmem_limit_bytes=VMEM_LIMIT_BYTES)
    if not chunks:
        return pl.pallas_call(
            body, out_shape=tuple(out_shape), grid=grid, in_specs=list(in_specs), out_specs=tuple(out_specs),
            scratch_shapes=list(scratch), compiler_params=params, name=name)(*operands)
    srcs, keys = [], []
    for ch in chunks:
        if ch.src is not None and not any(ch.src is s for s in srcs):
            srcs.append(ch.src)
        if ch.key not in keys:
            keys.append(ch.key)
    shape_of = {ch.key: ch.dst_shape for ch in chunks}
    held = [k for k in keys if bufs.get(k) is not None]
    extra = srcs + [bufs[k] for k in held]
    aliases = {n_in + len(srcs) + i: n_out + keys.index(k) for i, k in enumerate(held)}
    anywhere = pl.BlockSpec(memory_space=pl.ANY)
    n_extra, n_keys, n_ch = len(extra), len(keys), len(chunks)

    def hosted(*refs):
        src_refs = refs[n_in:n_in + len(srcs)]
        o0 = n_in + n_extra
        dst_refs = refs[o0 + n_out:o0 + n_out + n_keys]
        s0 = o0 + n_out + n_keys
        send, recv, loc = refs[s0 + n_scr:]
        sends, recvs, locs = [], [], []
        for i, ch in enumerate(chunks):
            src_ref = None if ch.src is None else src_refs[[ch.src is s for s in srcs].index(True)]
            s_, r_, l_ = ch.copies(src_ref, dst_refs[keys.index(ch.key)], send, recv, loc, i)
            sends += s_
            recvs += r_
            if l_ is not None:
                locs.append(l_)

        def start():
            for cp in locs + sends:
                cp.start()

        def finish():
            for cp in recvs:
                cp.wait_recv()
            for cp in sends:
                cp.wait_send()
            for cp in locs:
                cp.wait()

        if grid:
            first = functools.reduce(jnp.logical_and, [pl.program_id(a) == 0 for a in range(len(grid))])
            last = functools.reduce(jnp.logical_and, [pl.program_id(a) == grid[a] - 1 for a in range(len(grid))])
            pl.when(first)(start)
            body(*refs[:n_in], *refs[o0:o0 + n_out], *refs[s0:s0 + n_scr])
            pl.when(last)(finish)
        else:
            start()
            body(*refs[:n_in], *refs[o0:o0 + n_out], *refs[s0:s0 + n_scr])
            finish()

    res = pl.pallas_call(
        hosted, out_shape=tuple(out_shape) + tuple(jax.ShapeDtypeStruct(shape_of[k], PAY_DTYPE) for k in keys),
        grid=grid, in_specs=list(in_specs) + [anywhere] * n_extra, out_specs=tuple(out_specs) + (anywhere,) * n_keys,
        scratch_shapes=list(scratch) + [pltpu.SemaphoreType.DMA((4 * n_ch,)), pltpu.SemaphoreType.DMA((4 * n_ch,)),
                                        pltpu.SemaphoreType.DMA((n_ch,))],
        input_output_aliases=aliases, compiler_params=params, name=name)(*operands, *extra)
    for k, arr in zip(keys, res[n_out:]):
        bufs[k] = arr
    return res[:n_out]


class _Link:
    def __init__(self, bytes_per_us):
        self.rate = bytes_per_us
        self.queue, self.bufs = [], {}
        self.side = []
        self.after = []
        self.left = {}
        self.on_done = {}

    def push(self, kind, src, key, dst_shape, layer, rows, cols, done=None, front=False, **geometry):
        fits = [r for r in range(16, rows + 1, 16) if rows % r == 0 and r * cols * 2 <= CHUNK_BYTES]
        rc = max(fits) if fits else min(r for r in range(16, rows + 1, 16) if rows % r == 0)
        new = [_Chunk(kind, src, key, dst_shape, layer, r0, rc, cols, **geometry) for r0 in range(0, rows, rc)]
        self.queue = new + self.queue if front else self.queue + new
        self.left[key] = self.left.get(key, 0) + rows // rc
        if done is not None:
            self.on_done[key] = done

    def settle(self):
        todo, self.after = self.after, []
        for f in todo:
            f()

    def _pop(self, count):
        out = []
        for ch in self.queue[:count]:
            self.left[ch.key] -= 1
            if self.left[ch.key] == 0 and ch.key in self.on_done:
                self.after.append(self.on_done.pop(ch.key))
            out.append(ch)
        self.queue = self.queue[count:]
        return out

    def _side(self):
        riders, self.side = self.side[:SIDE_CHUNKS], self.side[SIDE_CHUNKS:]
        self.after += [then for _, then in riders if then is not None]
        return [ch for ch, _ in riders]

    def take(self, est_us):
        self.settle()
        out = self._side()
        budget, count = est_us * self.rate, 0
        while count < min(len(self.queue), HOST_CHUNKS) and self.queue[count].nbytes <= 2 * budget:
            budget -= self.queue[count].nbytes
            count += 1
        return out + self._pop(count)

    def flush(self, key, name):
        self.settle()
        if key is None:
            n = len(self.queue)
        else:
            n = max([i + 1 for i, ch in enumerate(self.queue) if ch.key == key], default=0)
        while n > 0:
            group = self._side() + self._pop(min(n, HOST_CHUNKS))
            n -= min(n, HOST_CHUNKS)
            _call(lambda: None, grid=(), in_specs=[], out_specs=(), out_shape=(), operands=[], name=name,
                  chunks=group, bufs=self.bufs)
            self.settle()

    def carry(self, est_us, name):
        _call(lambda: None, grid=(), in_specs=[], out_specs=(), out_shape=(), operands=[], name=name,
              chunks=self.take(est_us), bufs=self.bufs)

    def ride_out(self, name):
        self.settle()
        while self.side:
            _call(lambda: None, grid=(), in_specs=[], out_specs=(), out_shape=(), operands=[], name=name,
                  chunks=self._side(), bufs=self.bufs)
            self.settle()


def mm(a, b, *, mode="nn", out_dtype=F32, add=None, add_scale=1.0, slab_width=None, link=None, name):
    if mode == "nn":
        (M, K), N = a.shape, b.shape[-1]
    elif mode == "nt":
        (M, K), N = a.shape, b.shape[-2]
    else:
        (K, M), N = a.shape, b.shape[-1]
    tm, tk = _tile(M, TILE_M), _tile(K, TILE_K)
    tn = _tile(N, TILE_N) if slab_width is None else _tile(slab_width, SLAB_TILE_N)
    nk = K // tk
    dims = {"nn": (((1,), (0,)), ((), ())), "nt": (((1,), (1,)), ((), ())),
            "tn": (((0,), (0,)), ((), ()))}[mode]

    a_spec = (pl.BlockSpec((tk, tm), lambda i, j, k: (k, i)) if mode == "tn"
              else pl.BlockSpec((tm, tk), lambda i, j, k: (i, k)))
    b_spec = (pl.BlockSpec((tn, tk), lambda i, j, k: (j, k)) if mode == "nt"
              else pl.BlockSpec((tk, tn), lambda i, j, k: (k, j)))
    in_specs, operands = [a_spec, b_spec], [a, b]
    if add is not None:
        in_specs.append(pl.BlockSpec((tm, tn), lambda i, j, k: (i, j)))
        operands.append(add)
    if slab_width is None:
        out_shape = jax.ShapeDtypeStruct((M, N), out_dtype)
        out_spec = pl.BlockSpec((tm, tn), lambda i, j, k: (i, j))
    else:
        per = slab_width // tn
        out_shape = jax.ShapeDtypeStruct((N // slab_width, M, slab_width), out_dtype)
        out_spec = pl.BlockSpec((None, tm, tn), lambda i, j, k: (j // per, i, j % per))
    n_in = len(operands)

    def body(*refs):
        a_ref, b_ref = refs[0], refs[1]
        add_ref = refs[2] if add is not None else None
        o_ref = refs[n_in]
        k = pl.program_id(2)
        part = lax.dot_general(a_ref[...].astype(MXU_DTYPE), b_ref[...].astype(MXU_DTYPE),
                               dims, preferred_element_type=F32)

        def finish(r):
            if add_ref is not None:
                r = r + add_scale * add_ref[...].astype(F32)
            o_ref[...] = r.astype(o_ref.dtype)

        if nk == 1:
            finish(part)
        else:
            acc_ref = refs[n_in + 1]

            @pl.when(k == 0)
            def _():
                acc_ref[...] = part

            @pl.when(jnp.logical_and(k > 0, k < nk - 1))
            def _():
                acc_ref[...] += part

            @pl.when(k == nk - 1)
            def _():
                finish(acc_ref[...] + part)

    chunks = link.take(2.0 * M * N * K / MXU_FLOPS_PER_US) if link is not None else ()
    (out,) = _call(body, grid=(M // tm, N // tn, nk), in_specs=in_specs, out_specs=(out_spec,), out_shape=(out_shape,),
                   operands=operands, scratch=[pltpu.VMEM((tm, tn), F32)] if nk > 1 else [], name=name,
                   chunks=chunks, bufs=link.bufs if link is not None else None)
    return out


def ln_fwd(h, f, g, b, alpha, link, name):
    S, D = h.shape
    tm = _tile(S, ROW_TILE, 8)

    def body(h_ref, f_ref, g_ref, b_ref, y_ref, yb_ref, xhat_ref, rstd_ref):
        z = alpha * h_ref[...] + f_ref[...]
        mu = jnp.mean(z, axis=-1, keepdims=True)
        zc = z - mu
        var = jnp.mean(zc * zc, axis=-1, keepdims=True)
        rstd = lax.rsqrt(var + LN_EPS)
        xhat = zc * rstd
        y = xhat * g_ref[...] + b_ref[...]
        y_ref[...] = y
        yb_ref[...] = y.astype(yb_ref.dtype)
        xhat_ref[...] = xhat
        rstd_ref[...] = rstd

    row = pl.BlockSpec((tm, D), lambda i: (i, 0))
    vec = pl.BlockSpec((1, D), lambda i: (0, 0))
    return _call(
        body, grid=(S // tm,), in_specs=[row, row, vec, vec],
        out_specs=(row, row, row, pl.BlockSpec((tm, 1), lambda i: (i, 0))),
        out_shape=(jax.ShapeDtypeStruct((S, D), F32), jax.ShapeDtypeStruct((S, D), MXU_DTYPE),
                   jax.ShapeDtypeStruct((S, D), F32), jax.ShapeDtypeStruct((S, 1), F32)),
        operands=[h, f, g, b], name=name, chunks=link.take(ROWWISE_US["ln"][0]), bufs=link.bufs)


def ln_bwd(dy, xhat, rstd, g, link, name):
    S, D = dy.shape
    tm = _tile(S, ROW_TILE, 8)

    def body(dy_ref, xhat_ref, rstd_ref, g_ref, dz_ref, dzb_ref, dg_ref, db_ref):
        @pl.when(pl.program_id(0) == 0)
        def _():
            dg_ref[...] = jnp.zeros_like(dg_ref)
            db_ref[...] = jnp.zeros_like(db_ref)

        dy = dy_ref[...]
        xhat = xhat_ref[...]
        dxh = dy * g_ref[...]
        m1 = jnp.mean(dxh, axis=-1, keepdims=True)
        m2 = jnp.mean(dxh * xhat, axis=-1, keepdims=True)
        dz = rstd_ref[...] * (dxh - m1 - xhat * m2)
        dz_ref[...] = dz
        dzb_ref[...] = dz.astype(dzb_ref.dtype)
        dg_ref[...] += jnp.sum(dy * xhat, axis=0, keepdims=True)
        db_ref[...] += jnp.sum(dy, axis=0, keepdims=True)

    row = pl.BlockSpec((tm, D), lambda i: (i, 0))
    vec = pl.BlockSpec((1, D), lambda i: (0, 0))
    return _call(
        body, grid=(S // tm,), in_specs=[row, row, pl.BlockSpec((tm, 1), lambda i: (i, 0)), vec],
        out_specs=(row, row, vec, vec),
        out_shape=(jax.ShapeDtypeStruct((S, D), F32), jax.ShapeDtypeStruct((S, D), MXU_DTYPE),
                   jax.ShapeDtypeStruct((1, D), F32), jax.ShapeDtypeStruct((1, D), F32)),
        operands=[dy, xhat, rstd, g], name=name, chunks=link.take(ROWWISE_US["ln"][1]), bufs=link.bufs)


def rms_fwd(x, col, width, g, name):
    S = x.shape[0]
    tm = _tile(S, ROW_TILE, 8)

    def body(x_ref, g_ref, y_ref):
        xv = x_ref[...]
        r = lax.rsqrt(jnp.mean(xv * xv, axis=-1, keepdims=True) + RMS_EPS)
        y_ref[...] = (xv * r * g_ref[...]).astype(y_ref.dtype)

    return pl.pallas_call(
        body, out_shape=jax.ShapeDtypeStruct((S, width), MXU_DTYPE), grid=(S // tm,),
        in_specs=[pl.BlockSpec((tm, width), lambda i: (i, col)), pl.BlockSpec((1, width), lambda i: (0, 0))],
        out_specs=pl.BlockSpec((tm, width), lambda i: (i, 0)), compiler_params=_params(1), name=name)(x, g)


def rms_bwd(dy, x, col, width, g, name):
    S = x.shape[0]
    tm = _tile(S, ROW_TILE, 8)

    def body(dy_ref, x_ref, g_ref, dx_ref, dg_ref):
        @pl.when(pl.program_id(0) == 0)
        def _():
            dg_ref[...] = jnp.zeros_like(dg_ref)

        xv = x_ref[...]
        dy = dy_ref[...]
        r = lax.rsqrt(jnp.mean(xv * xv, axis=-1, keepdims=True) + RMS_EPS)
        dxn = dy * g_ref[...]
        m = jnp.mean(dxn * xv, axis=-1, keepdims=True)
        dx_ref[...] = r * (dxn - xv * (r * r * m))
        dg_ref[...] += jnp.sum(dy * xv * r, axis=0, keepdims=True)

    return pl.pallas_call(
        body, out_shape=(jax.ShapeDtypeStruct((S, width), F32), jax.ShapeDtypeStruct((1, width), F32)),
        grid=(S // tm,),
        in_specs=[pl.BlockSpec((tm, width), lambda i: (i, 0)), pl.BlockSpec((tm, width), lambda i: (i, col)),
                  pl.BlockSpec((1, width), lambda i: (0, 0))],
        out_specs=(pl.BlockSpec((tm, width), lambda i: (i, 0)), pl.BlockSpec((1, width), lambda i: (0, 0))),
        compiler_params=_params(1), name=name)(dy, x, g)


def ret_rope_fwd(proj, cos, sin, n_groups, half, k_from, k_scale, name):
    S = proj.shape[0]
    W = n_groups * 2 * half
    tm = _tile(S, ROW_TILE, 8)

    def body(x_ref, c_ref, s_ref, o_ref):
        c, s = c_ref[...], s_ref[...]
        for gi in range(n_groups):
            lo = gi * 2 * half
            x1 = x_ref[:, lo:lo + half]
            x2 = x_ref[:, lo + half:lo + 2 * half]
            sc = k_scale if gi >= k_from else 1.0
            o_ref[:, lo:lo + half] = ((x1 * c - x2 * s) * sc).astype(o_ref.dtype)
            o_ref[:, lo + half:lo + 2 * half] = ((x2 * c + x1 * s) * sc).astype(o_ref.dtype)

    tab = pl.BlockSpec((tm, half), lambda i: (i, 0))
    return pl.pallas_call(
        body, out_shape=jax.ShapeDtypeStruct((S, W), MXU_DTYPE), grid=(S // tm,),
        in_specs=[pl.BlockSpec((tm, W), lambda i: (i, 0)), tab, tab],
        out_specs=pl.BlockSpec((tm, W), lambda i: (i, 0)), compiler_params=_params(1), name=name)(proj, cos, sin)


def ret_rope_bwd(dq, dk, cos, sin, half, k_scale, name):
    S, Wq = dq.shape
    n_heads = Wq // (2 * half)
    tm = _tile(S, ROW_TILE, 8)

    def body(dq_ref, dk_ref, c_ref, s_ref, o_ref):
        c, s = c_ref[...], s_ref[...]
        for part, (d_ref, sc) in enumerate(((dq_ref, 1.0), (dk_ref, k_scale))):
            for hi in range(n_heads):
                lo = hi * 2 * half
                d1 = d_ref[:, lo:lo + half]
                d2 = d_ref[:, lo + half:lo + 2 * half]
                base = part * Wq + lo
                o_ref[:, base:base + half] = ((d1 * c + d2 * s) * sc).astype(o_ref.dtype)
                o_ref[:, base + half:base + 2 * half] = ((d2 * c - d1 * s) * sc).astype(o_ref.dtype)

    tab = pl.BlockSpec((tm, half), lambda i: (i, 0))
    row = pl.BlockSpec((tm, Wq), lambda i: (i, 0))
    return pl.pallas_call(
        body, out_shape=jax.ShapeDtypeStruct((S, 2 * Wq), MXU_DTYPE), grid=(S // tm,),
        in_specs=[row, row, tab, tab], out_specs=pl.BlockSpec((tm, 2 * Wq), lambda i: (i, 0)),
        compiler_params=_params(1), name=name)(dq, dk, cos, sin)


def mla_rope(x, ta, tb, n_heads, *, backward, head_sum, out_dtype, name):
    S = x.shape[0]
    W = n_heads * MLA_PAD
    tm = _tile(S, ROW_TILE, 8)

    def body(x_ref, a_ref, b_ref, o_ref, *rest):
        a, b = a_ref[...], b_ref[...]
        total = jnp.zeros((tm, LANES), F32)
        for hi in range(n_heads):
            lo = hi * MLA_PAD
            o_ref[:, lo:lo + MLA_NOPE] = x_ref[:, lo:lo + MLA_NOPE].astype(o_ref.dtype)
            t = x_ref[:, lo + MLA_NOPE:lo + MLA_PAD].astype(F32)
            if backward:
                r = t * a + pltpu.roll(t * b, LANES // 2, 1)
            else:
                r = t * a + pltpu.roll(t, LANES // 2, 1) * b
            o_ref[:, lo + MLA_NOPE:lo + MLA_PAD] = r.astype(o_ref.dtype)
            total = total + r
        if head_sum:
            rest[0][...] = total

    tab = pl.BlockSpec((tm, LANES), lambda i: (i, 0))
    row = pl.BlockSpec((tm, W), lambda i: (i, 0))
    out_shape = [jax.ShapeDtypeStruct((S, W), out_dtype)]
    out_specs = [row]
    if head_sum:
        out_shape.append(jax.ShapeDtypeStruct((S, LANES), F32))
        out_specs.append(tab)
    return pl.pallas_call(
        body, out_shape=tuple(out_shape), grid=(S // tm,), in_specs=[row, tab, tab],
        out_specs=tuple(out_specs), compiler_params=_params(1), name=name)(x, ta, tb)


def _visible(qi, kj, tq, tk):
    n = qi * tq + lax.broadcasted_iota(jnp.int32, (tq, tk), 0)
    m = kj * tk + lax.broadcasted_iota(jnp.int32, (tq, tk), 1)
    shift = CHUNK.bit_length() - 1
    vis = lax.shift_right_logical(m, shift) <= lax.shift_right_logical(n, shift)
    return vis, jnp.abs(n - m).astype(F32)


def attn_fwd(q, k, v, *, heads, softmax, masked, scale=1.0, log_gamma=None, link=None, est_us=0.0, name):
    (qa, dqk, q0, qs), (ka, _, k0, ks), (va, dv, v0, vs) = q, k, v
    Sq, Sk = qa.shape[0], ka.shape[0]
    tq, tk = _tile(Sq, ATTN_TILE, 8), _tile(Sk, ATTN_TILE, 8)
    nq, nk = Sq // tq, Sk // tk
    assert not masked or tq == tk

    def body(*refs):
        if softmax:
            q_ref, k_ref, v_ref, o_ref, ob_ref, lse_ref, m_ref, l_ref, acc_ref = refs
        else:
            q_ref, k_ref, v_ref, lg_ref, o_ref, acc_ref = refs
        qi, kj = pl.program_id(1), pl.program_id(2)

        @pl.when(kj == 0)
        def _():
            acc_ref[...] = jnp.zeros_like(acc_ref)
            if softmax:
                m_ref[...] = jnp.full_like(m_ref, NEG_INF)
                l_ref[...] = jnp.zeros_like(l_ref)

        def step(diagonal):
            s = lax.dot_general(q_ref[...].astype(MXU_DTYPE), k_ref[...].astype(MXU_DTYPE),
                                (((1,), (1,)), ((), ())), preferred_element_type=F32)
            vb = v_ref[...].astype(MXU_DTYPE)
            if diagonal or not softmax:
                vis, dist = _visible(qi, kj, tq, tk)
            if softmax:
                s = s * scale
                if diagonal:
                    s = jnp.where(vis, s, NEG_INF)
                m_old = m_ref[...]
                m_new = jnp.maximum(m_old, jnp.max(s, axis=-1, keepdims=True))
                p = jnp.exp(s - m_new)
                corr = jnp.exp(m_old - m_new)
                l_ref[...] = corr * l_ref[...] + jnp.sum(p, axis=-1, keepdims=True)
                acc_ref[...] = corr * acc_ref[...] + jnp.dot(p.astype(MXU_DTYPE), vb, preferred_element_type=F32)
                m_ref[...] = m_new
            else:
                decay = jnp.exp(lg_ref[0:1, 0:1] * dist)
                if diagonal:
                    decay = jnp.where(vis, decay, 0.0)
                p = s * decay
                acc_ref[...] += jnp.dot(p.astype(MXU_DTYPE), vb, preferred_element_type=F32)

        if masked:
            pl.when(kj < qi)(functools.partial(step, False))
            pl.when(kj == qi)(functools.partial(step, True))
        else:
            step(False)

        @pl.when(kj == nk - 1)
        def _():
            if softmax:
                out = acc_ref[...] / l_ref[...]
                o_ref[...] = out
                ob_ref[...] = out.astype(ob_ref.dtype)
                lse_ref[...] = m_ref[...] + jnp.log(l_ref[...])
            else:
                o_ref[...] = acc_ref[...].astype(o_ref.dtype)

    kcap = (lambda qi, kj: jnp.minimum(kj, qi)) if masked else (lambda qi, kj: kj)
    in_specs = [pl.BlockSpec((tq, dqk), lambda h, qi, kj: (qi, q0 + qs * h)),
                pl.BlockSpec((tk, dqk), lambda h, qi, kj: (kcap(qi, kj), k0 + ks * h)),
                pl.BlockSpec((tk, dv), lambda h, qi, kj: (kcap(qi, kj), v0 + vs * h))]
    operands = [qa, ka, va]
    out_shape = [jax.ShapeDtypeStruct((Sq, heads * dv), F32)]
    out_specs = [pl.BlockSpec((tq, dv), lambda h, qi, kj: (qi, h))]
    scratch = []
    if softmax:
        out_shape += [jax.ShapeDtypeStruct((Sq, heads * dv), MXU_DTYPE), jax.ShapeDtypeStruct((heads, Sq, 1), F32)]
        out_specs += [pl.BlockSpec((tq, dv), lambda h, qi, kj: (qi, h)),
                      pl.BlockSpec((None, tq, 1), lambda h, qi, kj: (h, qi, 0))]
        scratch += [pltpu.VMEM((tq, 1), F32), pltpu.VMEM((tq, 1), F32)]
    else:
        in_specs.append(pl.BlockSpec((None, 8, LANES), lambda h, qi, kj: (h, 0, 0)))
        operands.append(log_gamma)
    scratch.append(pltpu.VMEM((tq, dv), F32))
    chunks = link.take(est_us) if link is not None else ()
    return _call(body, grid=(heads, nq, nk), in_specs=in_specs, out_specs=out_specs, out_shape=out_shape,
                 operands=operands, scratch=scratch, name=name, chunks=chunks,
                 bufs=link.bufs if link is not None else None)


def attn_bwd(q, k, v, do, *, heads, softmax, masked, scale=1.0, log_gamma=None, o=None, lse=None,
             link=None, est_us=0.0, name):
    (qa, dqk, q0, qs), (ka, _, k0, ks), (va, dv, v0, vs) = q, k, v
    Sq, Sk = qa.shape[0], ka.shape[0]
    tq, tk = _tile(Sq, ATTN_TILE, 8), _tile(Sk, ATTN_TILE, 8)
    nq, nk = Sq // tq, Sk // tk
    assert not masked or tq == tk
    contract0 = (((0,), (0,)), ((), ()))

    def body(*refs):
        if softmax:
            q_ref, k_ref, v_ref, do_ref, o_ref, lse_ref, dq_ref, dk_ref, dv_ref = refs
        else:
            q_ref, k_ref, v_ref, do_ref, lg_ref, dq_ref, dk_ref, dv_ref = refs
        kj, qi = pl.program_id(1), pl.program_id(2)

        @pl.when(jnp.logical_and(kj == 0, qi == 0))
        def _():
            dq_ref[...] = jnp.zeros_like(dq_ref)

        @pl.when(qi == 0)
        def _():
            dk_ref[...] = jnp.zeros_like(dk_ref)
            dv_ref[...] = jnp.zeros_like(dv_ref)

        def step(diagonal):
            qb = q_ref[...].astype(MXU_DTYPE)
            kb = k_ref[...].astype(MXU_DTYPE)
            vb = v_ref[...].astype(MXU_DTYPE)
            dob = do_ref[...].astype(MXU_DTYPE)
            s = lax.dot_general(qb, kb, (((1,), (1,)), ((), ())), preferred_element_type=F32)
            dp = lax.dot_general(dob, vb, (((1,), (1,)), ((), ())), preferred_element_type=F32)
            if diagonal or not softmax:
                vis, dist = _visible(qi, kj, tq, tk)
            if softmax:
                s = s * scale
                if diagonal:
                    s = jnp.where(vis, s, NEG_INF)
                p = jnp.exp(s - lse_ref[...])
                delta = jnp.sum(do_ref[...].astype(F32) * o_ref[...], axis=-1, keepdims=True)
                ds = p * (dp - delta) * scale
            else:
                decay = jnp.exp(lg_ref[0:1, 0:1] * dist)
                if diagonal:
                    decay = jnp.where(vis, decay, 0.0)
                p = s * decay
                ds = dp * decay
            pb = p.astype(MXU_DTYPE)
            dsb = ds.astype(MXU_DTYPE)
            dv_ref[...] += lax.dot_general(pb, dob, contract0, preferred_element_type=F32)
            dk_ref[...] += lax.dot_general(dsb, qb, contract0, preferred_element_type=F32)
            rows = pl.ds(pl.multiple_of(qi * tq, tq), tq)
            dq_ref[rows, :] += jnp.dot(dsb, kb, preferred_element_type=F32)

        if masked:
            pl.when(qi > kj)(functools.partial(step, False))
            pl.when(qi == kj)(functools.partial(step, True))
        else:
            step(False)

    qcap = (lambda kj, qi: jnp.maximum(qi, kj)) if masked else (lambda kj, qi: qi)
    in_specs = [pl.BlockSpec((tq, dqk), lambda h, kj, qi: (qcap(kj, qi), q0 + qs * h)),
                pl.BlockSpec((tk, dqk), lambda h, kj, qi: (kj, k0 + ks * h)),
                pl.BlockSpec((tk, dv), lambda h, kj, qi: (kj, v0 + vs * h)),
                pl.BlockSpec((tq, dv), lambda h, kj, qi: (qcap(kj, qi), h))]
    operands = [qa, ka, va, do]
    if softmax:
        in_specs += [pl.BlockSpec((tq, dv), lambda h, kj, qi: (qcap(kj, qi), h)),
                     pl.BlockSpec((None, tq, 1), lambda h, kj, qi: (h, qcap(kj, qi), 0))]
        operands += [o, lse]
    else:
        in_specs.append(pl.BlockSpec((None, 8, LANES), lambda h, kj, qi: (h, 0, 0)))
        operands.append(log_gamma)
    chunks = link.take(est_us) if link is not None else ()
    return _call(
        body, grid=(heads, nk, nq), in_specs=in_specs,
        out_specs=(pl.BlockSpec((Sq, dqk), lambda h, kj, qi: (0, h)),
                   pl.BlockSpec((tk, dqk), lambda h, kj, qi: (kj, h)),
                   pl.BlockSpec((tk, dv), lambda h, kj, qi: (kj, h))),
        out_shape=(jax.ShapeDtypeStruct((Sq, heads * dqk), F32), jax.ShapeDtypeStruct((Sk, heads * dqk), F32),
                   jax.ShapeDtypeStruct((Sk, heads * dv), F32)),
        operands=operands, name=name, chunks=chunks, bufs=link.bufs if link is not None else None)


def _sigmoid(x):
    return 0.5 * jnp.tanh(0.5 * x) + 0.5


def ret_gate_fwd(o, proj, gate_col, gn, heads, name):
    S, W = o.shape
    dv = W // heads
    tm = _tile(S, ROW_TILE // 2, 8)

    def body(o_ref, g_ref, gn_ref, y_ref):
        for hi in range(heads):
            cols = slice(hi * dv, (hi + 1) * dv)
            oh = o_ref[:, cols]
            mu = jnp.mean(oh, axis=-1, keepdims=True)
            oc = oh - mu
            rstd = lax.rsqrt(jnp.mean(oc * oc, axis=-1, keepdims=True) + LN_EPS)
            gt = g_ref[:, cols]
            y_ref[:, cols] = (gt * _sigmoid(gt) * (oc * rstd) * gn_ref[:, cols]).astype(y_ref.dtype)

    row = pl.BlockSpec((tm, W), lambda i: (i, 0))
    return pl.pallas_call(
        body, out_shape=jax.ShapeDtypeStruct((S, W), MXU_DTYPE), grid=(S // tm,),
        in_specs=[row, pl.BlockSpec((tm, W), lambda i: (i, gate_col)), pl.BlockSpec((1, W), lambda i: (0, 0))],
        out_specs=row, compiler_params=_params(1), name=name)(o, proj, gn)


def ret_gate_bwd(dy, o, proj, gate_col, gn, heads, name):
    S, W = o.shape
    dv = W // heads
    tm = _tile(S, ROW_TILE // 2, 8)

    def body(dy_ref, o_ref, g_ref, gn_ref, do_ref, dgt_ref, dgn_ref):
        @pl.when(pl.program_id(0) == 0)
        def _():
            dgn_ref[...] = jnp.zeros_like(dgn_ref)

        for hi in range(heads):
            cols = slice(hi * dv, (hi + 1) * dv)
            oh = o_ref[:, cols]
            mu = jnp.mean(oh, axis=-1, keepdims=True)
            oc = oh - mu
            rstd = lax.rsqrt(jnp.mean(oc * oc, axis=-1, keepdims=True) + LN_EPS)
            xhat = oc * rstd
            gt = g_ref[:, cols]
            sg = _sigmoid(gt)
            gain = gn_ref[:, cols]
            dy = dy_ref[:, cols]
            dgt_ref[:, cols] = (dy * xhat * gain * (sg * (1.0 + gt * (1.0 - sg)))).astype(dgt_ref.dtype)
            dn = dy * (gt * sg)
            dgn_ref[:, cols] += jnp.sum(dn * xhat, axis=0, keepdims=True)
            dxh = dn * gain
            m1 = jnp.mean(dxh, axis=-1, keepdims=True)
            m2 = jnp.mean(dxh * xhat, axis=-1, keepdims=True)
            do_ref[:, cols] = (rstd * (dxh - m1 - xhat * m2)).astype(do_ref.dtype)

    row = pl.BlockSpec((tm, W), lambda i: (i, 0))
    vec = pl.BlockSpec((1, W), lambda i: (0, 0))
    return pl.pallas_call(
        body,
        out_shape=(jax.ShapeDtypeStruct((S, W), MXU_DTYPE), jax.ShapeDtypeStruct((S, W), MXU_DTYPE),
                   jax.ShapeDtypeStruct((1, W), F32)),
        grid=(S // tm,), in_specs=[row, row, pl.BlockSpec((tm, W), lambda i: (i, gate_col)), vec],
        out_specs=(row, row, vec), compiler_params=_params(1), name=name)(dy, o, proj, gn)


def _shift_down(x, s):
    rows = lax.broadcasted_iota(jnp.int32, x.shape, 0)
    return jnp.where(rows >= s, pltpu.roll(x, s, 0), 0.0)


def _shift_up(x, s):
    n = x.shape[0]
    rows = lax.broadcasted_iota(jnp.int32, x.shape, 0)
    return jnp.where(rows < n - s, pltpu.roll(x, n - s, 0), 0.0)


def _conv3(x, w_ref, b_ref):
    return (w_ref[2:3, :] * x + w_ref[1:2, :] * _shift_down(x, 1) + w_ref[0:1, :] * _shift_down(x, 2)
            + b_ref[...])


def conv_glu_fwd(hup, w, b, link, name):
    S, W2 = hup.shape
    F = W2 // 2
    tc = _tile(F, LANES)
    nb = F // tc

    def body(g_ref, v_ref, wg_ref, wv_ref, bg_ref, bv_ref, u_ref):
        cg = _conv3(g_ref[...], wg_ref, bg_ref)
        cv = _conv3(v_ref[...], wv_ref, bv_ref)
        u_ref[...] = (cg * _sigmoid(cg) * cv).astype(u_ref.dtype)

    def col(rows, off):
        return pl.BlockSpec((rows, tc), lambda j: (0, j + off))

    (u,) = _call(
        body, grid=(nb,), in_specs=[col(S, 0), col(S, nb), col(3, 0), col(3, nb), col(1, 0), col(1, nb)],
        out_specs=(col(S, 0),), out_shape=(jax.ShapeDtypeStruct((S, F), MXU_DTYPE),),
        operands=[hup, hup, w, w, b, b], name=name, chunks=link.take(ROWWISE_US["conv"][0]), bufs=link.bufs)
    return u


def conv_glu_bwd(du, hup, w, b, link, name):
    S, W2 = hup.shape
    F = W2 // 2
    tc = _tile(F, LANES)
    nb = F // tc

    def back(dc, x, w_ref, dh_ref, dw_ref, db_ref):
        dw_ref[2:3, :] = jnp.sum(dc * x, axis=0, keepdims=True)
        dw_ref[1:2, :] = jnp.sum(dc * _shift_down(x, 1), axis=0, keepdims=True)
        dw_ref[0:1, :] = jnp.sum(dc * _shift_down(x, 2), axis=0, keepdims=True)
        db_ref[...] = jnp.sum(dc, axis=0, keepdims=True)
        dh_ref[...] = (w_ref[2:3, :] * dc + w_ref[1:2, :] * _shift_up(dc, 1)
                       + w_ref[0:1, :] * _shift_up(dc, 2)).astype(dh_ref.dtype)

    def body(du_ref, g_ref, v_ref, wg_ref, wv_ref, bg_ref, bv_ref,
             dhg_ref, dhv_ref, dwg_ref, dwv_ref, dbg_ref, dbv_ref):
        xg, xv = g_ref[...], v_ref[...]
        cg = _conv3(xg, wg_ref, bg_ref)
        cv = _conv3(xv, wv_ref, bv_ref)
        sg = _sigmoid(cg)
        du = du_ref[...]
        back(du * cv * (sg * (1.0 + cg * (1.0 - sg))), xg, wg_ref, dhg_ref, dwg_ref, dbg_ref)
        back(du * (cg * sg), xv, wv_ref, dhv_ref, dwv_ref, dbv_ref)

    def col(rows, off):
        return pl.BlockSpec((rows, tc), lambda j: (0, j + off))

    return _call(
        body, grid=(nb,),
        in_specs=[col(S, 0), col(S, 0), col(S, nb), col(3, 0), col(3, nb), col(1, 0), col(1, nb)],
        out_specs=(col(S, 0), col(S, 0), col(3, 0), col(3, 0), col(1, 0), col(1, 0)),
        out_shape=(jax.ShapeDtypeStruct((S, F), MXU_DTYPE), jax.ShapeDtypeStruct((S, F), MXU_DTYPE),
                   jax.ShapeDtypeStruct((3, F), F32), jax.ShapeDtypeStruct((3, F), F32),
                   jax.ShapeDtypeStruct((1, F), F32), jax.ShapeDtypeStruct((1, F), F32)),
        operands=[du, hup, hup, w, w, b, b], name=name, chunks=link.take(ROWWISE_US["conv"][1]), bufs=link.bufs)


def loss_head(y, target, name):
    S, D = y.shape
    tm = _tile(S, ROW_TILE, 8)

    def body(y_ref, t_ref, dy_ref, loss_ref):
        @pl.when(pl.program_id(0) == 0)
        def _():
            loss_ref[...] = jnp.zeros_like(loss_ref)

        e = y_ref[...] - t_ref[...]
        dy_ref[...] = e * (1.0 / D)
        part = jnp.sum(jnp.sum(e * e, axis=-1, keepdims=True), axis=0, keepdims=True) * (0.5 / D)
        loss_ref[...] += jnp.broadcast_to(part, loss_ref.shape)

    row = pl.BlockSpec((tm, D), lambda i: (i, 0))
    return pl.pallas_call(
        body, out_shape=(jax.ShapeDtypeStruct((S, D), F32), jax.ShapeDtypeStruct((8, LANES), F32)),
        grid=(S // tm,), in_specs=[row, row], out_specs=(row, pl.BlockSpec((8, LANES), lambda i: (0, 0))),
        compiler_params=_params(1), name=name)(y, target)


def adam_update(parts, w, m, v, name):
    n_layers, n_parts, R, C = parts.shape
    tr = _tile(R, max(8, (1 << 19) // C), 8)
    c1 = 1.0 - ADAM_B1 ** ADAM_STEP
    c2 = 1.0 - ADAM_B2 ** ADAM_STEP

    def body(p_ref, w_ref, m_ref, v_ref, g_ref, d_ref, nm_ref, nv_ref):
        g = p_ref[0].astype(F32)
        for pi in range(1, n_parts):
            g = g + p_ref[pi].astype(F32)
        nm = ADAM_B1 * m_ref[...] + (1.0 - ADAM_B1) * g
        nv = ADAM_B2 * v_ref[...] + (1.0 - ADAM_B2) * (g * g)
        g_ref[...] = g
        nm_ref[...] = nm
        nv_ref[...] = nv
        d_ref[...] = -ADAM_LR * ((nm / c1) / (jnp.sqrt(nv / c2) + ADAM_EPS) + ADAM_WD * w_ref[...])

    row = pl.BlockSpec((None, tr, C), lambda l, i: (l, i, 0))
    out = jax.ShapeDtypeStruct((n_layers, R, C), F32)
    return _call(
        body, grid=(n_layers, R // tr),
        in_specs=[pl.BlockSpec((None, n_parts, tr, C), lambda l, i: (l, 0, i, 0)), row, row, row],
        out_specs=(row, row, row, row), out_shape=(out, out, out, out), operands=[parts, w, m, v], name=name)


def pair_sum(own, got, core, name):
    _, R, C = own.shape
    tr = _tile(R, max(16, (1 << 19) // C), 16)

    def body(core_ref, a_ref, b_ref, o_ref):
        o_ref[...] = (a_ref[...].astype(F32) + b_ref[...].astype(F32)).astype(o_ref.dtype)

    grid_spec = pltpu.PrefetchScalarGridSpec(
        num_scalar_prefetch=1, grid=(N_CHIP, R // tr),
        in_specs=[pl.BlockSpec((None, tr, C), lambda j, i, core_ref: (2 * j + core_ref[0], i, 0)),
                  pl.BlockSpec((None, tr, C), lambda j, i, core_ref: (j, i, 0))],
        out_specs=pl.BlockSpec((None, tr, C), lambda j, i, core_ref: (j, i, 0)))
    return pl.pallas_call(
        body, out_shape=jax.ShapeDtypeStruct((N_CHIP, R, C), own.dtype), grid_spec=grid_spec,
        compiler_params=_params(2), name=name)(core, own, got)


def gather_small(block, name):
    m_per, n = block.shape

    def body(x_ref, out_ref, send_sems, recv_sems, local_sem):
        x, y, c = _place()
        me, sibling = (x, y, c), (x, y, 1 - c)
        chips = _other_chips(x, y)

        def rows(px, py, pc):
            return out_ref.at[pl.ds((4 * px + 2 * py + pc) * m_per, m_per), :]

        def copy(k, blk, to, src=None):
            return pltpu.make_async_remote_copy(
                src_ref=rows(*blk) if src is None else src, dst_ref=rows(*blk),
                send_sem=send_sems.at[k], recv_sem=recv_sems.at[k], device_id=to, device_id_type=MESH)

        mine = pltpu.make_async_copy(x_ref, rows(*me), local_sem)
        mine.start()
        first = [copy(0, me, sibling, src=x_ref)]
        first += [copy(1 + j, me, (*chip, c), src=x_ref) for j, chip in enumerate(chips)]
        for cp in first:
            cp.start()
        passed = [copy(4 + j, (*chip, c), sibling) for j, chip in enumerate(chips)]
        for j, chip in enumerate(chips):
            copy(1 + j, (*chip, c), me).wait_recv()
            passed[j].start()
        copy(0, sibling, me).wait_recv()
        for j, chip in enumerate(chips):
            copy(4 + j, (*chip, 1 - c), me).wait_recv()
        for cp in first + passed:
            cp.wait_send()
        mine.wait()

    return pl.pallas_call(
        body, out_shape=jax.ShapeDtypeStruct((N_DEV * m_per, n), block.dtype),
        in_specs=[pl.BlockSpec(memory_space=pltpu.VMEM)], out_specs=pl.BlockSpec(memory_space=pltpu.VMEM),
        scratch_shapes=[pltpu.SemaphoreType.DMA((7,)), pltpu.SemaphoreType.DMA((7,)), pltpu.SemaphoreType.DMA],
        compiler_params=pltpu.CompilerParams(vmem_limit_bytes=VMEM_LIMIT_BYTES), name=name)(block)


def _rope_tables(positions, d):
    inv_freq = ROPE_BASE ** (-jnp.arange(0, d, 2, dtype=F32) / d)
    ang = positions.astype(F32)[:, None] * inv_freq
    return jnp.cos(ang), jnp.sin(ang)


def _mla_pad(nope, rope):
    S, H, _ = nope.shape
    half = MLA_ROPE // 2
    z = jnp.zeros((S, H, LANES // 2 - half), nope.dtype)
    return jnp.concatenate([nope, rope[..., :half], z, rope[..., half:], z], axis=2).reshape(S, H * MLA_PAD)


def _mla_unpad(x, H):
    S = x.shape[0]
    half = MLA_ROPE // 2
    x3 = x.reshape(S, H, MLA_PAD)
    rope = jnp.concatenate([x3[..., MLA_NOPE:MLA_NOPE + half],
                            x3[..., MLA_NOPE + LANES // 2:MLA_NOPE + LANES // 2 + half]], axis=2)
    return x3[..., :MLA_NOPE], rope


def kernel(x, mem, positions, ret_w_in, ret_gn_g, ret_w_out, mla_w_in, mla_q_norm_g, mla_w_uq, mla_kv_norm_g, mla_w_ukv, mla_w_out, xa_w_q, xa_w_kv, xa_w_out, ffn_w_up, ffn_conv_w, ffn_conv_b, ffn_w_down, ln_mix_g, ln_mix_b, ln_mem_g, ln_mem_b, ln_ffn_g, ln_ffn_b, loss_target, m_ret_w_in, m_ret_gn_g, m_ret_w_out, m_mla_w_in, m_mla_q_norm_g, m_mla_w_uq, m_mla_kv_norm_g, m_mla_w_ukv, m_mla_w_out, m_xa_w_q, m_xa_w_kv, m_xa_w_out, m_ffn_w_up, m_ffn_conv_w, m_ffn_conv_b, m_ffn_w_down, m_ln_mix_g, m_ln_mix_b, m_ln_mem_g, m_ln_mem_b, m_ln_ffn_g, m_ln_ffn_b, v_ret_w_in, v_ret_gn_g, v_ret_w_out, v_mla_w_in, v_mla_q_norm_g, v_mla_w_uq, v_mla_kv_norm_g, v_mla_w_ukv, v_mla_w_out, v_xa_w_q, v_xa_w_kv, v_xa_w_out, v_ffn_w_up, v_ffn_conv_w, v_ffn_conv_b, v_ffn_w_down, v_ln_mix_g, v_ln_mix_b, v_ln_mem_g, v_ln_mem_b, v_ln_ffn_g, v_ln_ffn_b):
    weights = dict(ret_w_in=ret_w_in, ret_gn_g=ret_gn_g, ret_w_out=ret_w_out, mla_w_in=mla_w_in,
                   mla_q_norm_g=mla_q_norm_g, mla_w_uq=mla_w_uq, mla_kv_norm_g=mla_kv_norm_g,
                   mla_w_ukv=mla_w_ukv, mla_w_out=mla_w_out, xa_w_q=xa_w_q, xa_w_kv=xa_w_kv, xa_w_out=xa_w_out,
                   ffn_w_up=ffn_w_up, ffn_conv_w=ffn_conv_w, ffn_conv_b=ffn_conv_b, ffn_w_down=ffn_w_down,
                   ln_mix_g=ln_mix_g, ln_mix_b=ln_mix_b, ln_mem_g=ln_mem_g, ln_mem_b=ln_mem_b,
                   ln_ffn_g=ln_ffn_g, ln_ffn_b=ln_ffn_b)
    mom_m = dict(ret_w_in=m_ret_w_in, ret_gn_g=m_ret_gn_g, ret_w_out=m_ret_w_out, mla_w_in=m_mla_w_in,
                 mla_q_norm_g=m_mla_q_norm_g, mla_w_uq=m_mla_w_uq, mla_kv_norm_g=m_mla_kv_norm_g,
                 mla_w_ukv=m_mla_w_ukv, mla_w_out=m_mla_w_out, xa_w_q=m_xa_w_q, xa_w_kv=m_xa_w_kv,
                 xa_w_out=m_xa_w_out, ffn_w_up=m_ffn_w_up, ffn_conv_w=m_ffn_conv_w, ffn_conv_b=m_ffn_conv_b,
                 ffn_w_down=m_ffn_w_down, ln_mix_g=m_ln_mix_g, ln_mix_b=m_ln_mix_b, ln_mem_g=m_ln_mem_g,
                 ln_mem_b=m_ln_mem_b, ln_ffn_g=m_ln_ffn_g, ln_ffn_b=m_ln_ffn_b)
    mom_v = dict(ret_w_in=v_ret_w_in, ret_gn_g=v_ret_gn_g, ret_w_out=v_ret_w_out, mla_w_in=v_mla_w_in,
                 mla_q_norm_g=v_mla_q_norm_g, mla_w_uq=v_mla_w_uq, mla_kv_norm_g=v_mla_kv_norm_g,
                 mla_w_ukv=v_mla_w_ukv, mla_w_out=v_mla_w_out, xa_w_q=v_xa_w_q, xa_w_kv=v_xa_w_kv,
                 xa_w_out=v_xa_w_out, ffn_w_up=v_ffn_w_up, ffn_conv_w=v_ffn_conv_w, ffn_conv_b=v_ffn_conv_b,
                 ffn_w_down=v_ffn_w_down, ln_mix_g=v_ln_mix_g, ln_mix_b=v_ln_mix_b, ln_mem_g=v_ln_mem_g,
                 ln_mem_b=v_ln_mem_b, ln_ffn_g=v_ln_ffn_g, ln_ffn_b=v_ln_ffn_b)
    order = list(weights)
    BIG = dict(ret_w_in=2, ret_w_out=1, mla_w_in=1, mla_w_uq=2, mla_w_ukv=2, mla_w_out=1,
               xa_w_q=1, xa_w_kv=2, xa_w_out=1, ffn_w_up=2, ffn_w_down=1)
    SMALL_CUT = ("ffn_conv_w", "mla_q_norm_g", "mla_kv_norm_g")
    REPLICATED = ("ret_gn_g", "ffn_conv_b", "ln_mix_g", "ln_mix_b", "ln_mem_g", "ln_mem_b", "ln_ffn_g", "ln_ffn_b")

    x = x[0]
    mem = mem[0]
    positions = positions[0]
    target = loss_target[0]
    S, D = x.shape
    depth = ln_mix_g.shape[0]
    alpha = (2 * depth) ** 0.25
    ret_dk = D // RET_HEADS
    ret_dv = 2 * D // RET_HEADS
    ret_qkw = RET_HEADS * ret_dk
    ret_vw = RET_HEADS * ret_dv
    xa_d = D // XA_HEADS
    assert 2 * ret_qkw == ret_vw and MLA_NOPE == LANES and MLA_V == LANES and MLA_ROPE == LANES // 2
    assert MLA_Q_RANK == MLA_KV_RANK and CHUNK & (CHUNK - 1) == 0
    core = lax.axis_index("c").astype(jnp.int32).reshape(1)
    dev = 4 * lax.axis_index("x") + 2 * lax.axis_index("y") + lax.axis_index("c")

    gather, scatter = _Link(GATHER_BYTES_PER_US), _Link(SCATTER_BYTES_PER_US)
    shard_b = {n: weights[n].astype(PAY_DTYPE) for n in BIG}
    staged = set()

    def units(layer):
        mixer = ("ret_w_in", "ret_w_out") if layer % 2 == 0 else ("mla_w_in", "mla_w_uq", "mla_w_ukv", "mla_w_out")
        return [(n, layer // 2) for n in mixer] + [(n, layer) for n in
                                                    ("xa_w_q", "xa_w_kv", "xa_w_out", "ffn_w_up", "ffn_w_down")]

    def geometry(n):
        _, k, nn = weights[n].shape
        return BIG[n], k, nn

    def after_first_level(key, full, axis, k, nn):
        def to_sibling():
            chunk = _Chunk("gather2", None, key, full, 0, 0, k, nn, axis=axis, k=k, n=nn)
            gather.side.append((chunk, lambda: staged.add(key)))

        def relay():
            gather.push("relay", None, key, full, 0, k // 2, nn, done=to_sibling, front=True,
                        axis=axis, k=k, n=nn, half=k // 2)
        return relay

    for layer in range(depth):
        for n, j in units(layer):
            axis, k, nn = geometry(n)
            full = (N_DEV * k, nn) if axis == 1 else (k, N_DEV * nn)
            gather.push("gather", shard_b[n], (n, j), full, j, k, nn,
                        done=after_first_level((n, j), full, axis, k, nn), axis=axis, k=k, n=nn)

    def weight(n, j):
        key = (n, j)
        while key not in staged:
            gather.flush(key, name="gather_rest")
            gather.settle()
            if not any(ch.key == key for ch in gather.queue):
                break
        while key not in staged:
            assert gather.side
            gather.carry(LEVEL2_US, name="gather_level2")
            gather.settle()
        return gather.bufs[key]

    def pack(arrays):
        flat = jnp.concatenate([a.reshape(-1) for a in arrays])
        rows = -(-flat.shape[0] // (8 * LANES)) * 8
        return jnp.pad(flat, (0, rows * LANES - flat.shape[0])).reshape(rows, LANES)

    def unpack(flat, like):
        out, at = [], 0
        for a in like:
            out.append(flat[at:at + a.size].reshape(a.shape))
            at += a.size
        return out

    small_local = [weights[n] for n in SMALL_CUT]
    blk = pack(small_local)
    allsmall = gather_small(blk, name="gather_small_weights").reshape(N_DEV, -1)
    per_dev = [unpack(allsmall[d], small_local) for d in range(N_DEV)]
    conv_w_full, qg_full, kvg_full = [jnp.concatenate([per_dev[d][i] for d in range(N_DEV)], axis=-1)
                                      for i in range(len(SMALL_CUT))]

    rcos, rsin = _rope_tables(positions, ret_dk)
    mcos, msin = _rope_tables(positions, MLA_ROPE)
    zq = jnp.zeros((S, LANES // 2 - MLA_ROPE // 2), F32)
    mla_ta = jnp.concatenate([mcos, zq, mcos, zq], axis=1)
    mla_tb = jnp.concatenate([-msin, zq, msin, zq], axis=1)
    log_gamma = jnp.log(1.0 - 2.0 ** (-5.0 - jnp.arange(RET_HEADS, dtype=F32)))
    log_gamma = jnp.broadcast_to(log_gamma[:, None, None], (RET_HEADS, 8, LANES))
    mla_scale = (MLA_NOPE + MLA_ROPE) ** -0.5
    xa_scale = xa_d ** -0.5
    mem_b = mem.astype(MXU_DTYPE)

    def vec(a, l):
        return a[l][None, :]

    saved = []
    h = x
    hb = x.astype(MXU_DTYPE)
    for layer in range(depth):
        j = layer // 2
        sv = {}
        sv["hb_mix"] = hb
        if layer % 2 == 0:
            proj = mm(hb, weight("ret_w_in", j), link=gather, name="ret_in")
            qk = ret_rope_fwd(proj, rcos, rsin, 2 * RET_HEADS, ret_dk // 2, RET_HEADS, ret_dk ** -0.5, name="ret_rope")
            (o,) = attn_fwd((qk, ret_dk, 0, 1), (qk, ret_dk, RET_HEADS, 1), (proj, ret_dv, RET_HEADS, 1),
                            heads=RET_HEADS, softmax=False, masked=True, log_gamma=log_gamma, link=gather,
                            est_us=ATTN_US["ret"][0], name="ret_attn")
            yb = ret_gate_fwd(o, proj, 2, vec(ret_gn_g, j), RET_HEADS, name="ret_gate")
            mix = mm(yb, weight("ret_w_out", j), link=gather, name="ret_out")
            sv.update(proj=proj, qk=qk, o=o, yb=yb)
        else:
            proj = mm(hb, weight("mla_w_in", j), link=gather, name="mla_in")
            cq = rms_fwd(proj, 0, MLA_Q_RANK, vec(qg_full, j), name="mla_q_norm")
            ckv = rms_fwd(proj, 1, MLA_KV_RANK, vec(kvg_full, j), name="mla_kv_norm")
            qf = mm(cq, weight("mla_w_uq", j), link=gather, name="mla_uq")
            kvf = mm(ckv, weight("mla_w_ukv", j), link=gather, name="mla_ukv")
            q3 = qf.reshape(S, MLA_HEADS, MLA_NOPE + MLA_ROPE)
            kv3 = kvf.reshape(S, MLA_HEADS, MLA_NOPE + MLA_V)
            k_rope = jnp.broadcast_to(proj[:, None, MLA_Q_RANK + MLA_KV_RANK:], (S, MLA_HEADS, MLA_ROPE))
            q_pad = _mla_pad(q3[..., :MLA_NOPE], q3[..., MLA_NOPE:])
            k_pad = _mla_pad(kv3[..., :MLA_NOPE], k_rope)
            (qr,) = mla_rope(q_pad, mla_ta, mla_tb, MLA_HEADS, backward=False, head_sum=False,
                             out_dtype=MXU_DTYPE, name="mla_rope_q")
            (kr,) = mla_rope(k_pad, mla_ta, mla_tb, MLA_HEADS, backward=False, head_sum=False,
                             out_dtype=MXU_DTYPE, name="mla_rope_k")
            o, ob, lse = attn_fwd((qr, MLA_PAD, 0, 1), (kr, MLA_PAD, 0, 1), (kvf, MLA_V, 1, 2), heads=MLA_HEADS,
                              softmax=True, masked=True, scale=mla_scale, link=gather, est_us=ATTN_US["mla"][0],
                              name="mla_attn")
            mix = mm(ob, weight("mla_w_out", j), link=gather, name="mla_out")
            sv.update(proj=proj, cq=cq, ckv=ckv, kvf=kvf, qr=qr, kr=kr, o=o, ob=ob, lse=lse)
        h, hb, sv["xhat_mix"], sv["rstd_mix"] = ln_fwd(h, mix, vec(ln_mix_g, layer), vec(ln_mix_b, layer), alpha,
                                                       gather, name="ln_mix")
        sv["hb_mem"] = hb
        q = mm(hb, weight("xa_w_q", layer), link=gather, name="xa_q")
        kvm = mm(mem_b, weight("xa_w_kv", layer), link=gather, name="xa_kv")
        o, ob, lse = attn_fwd((q, xa_d, 0, 1), (kvm, xa_d, 0, 1), (kvm, xa_d, XA_HEADS, 1), heads=XA_HEADS,
                          softmax=True, masked=False, scale=xa_scale, link=gather, est_us=ATTN_US["xa"][0],
                          name="xa_attn")
        mix = mm(ob, weight("xa_w_out", layer), link=gather, name="xa_out")
        sv.update(xa_q=q, xa_kvm=kvm, xa_o=o, xa_ob=ob, xa_lse=lse)
        h, hb, sv["xhat_mem"], sv["rstd_mem"] = ln_fwd(h, mix, vec(ln_mem_g, layer), vec(ln_mem_b, layer), alpha,
                                                       gather, name="ln_mem")
        sv["hb_ffn"] = hb
        hup = mm(hb, weight("ffn_w_up", layer), link=gather, name="ffn_up")
        u = conv_glu_fwd(hup, conv_w_full[layer], vec(ffn_conv_b, layer), gather, name="ffn_conv")
        mix = mm(u, weight("ffn_w_down", layer), link=gather, name="ffn_down")
        sv.update(hup=hup, u=u)
        h, hb, sv["xhat_ffn"], sv["rstd_ffn"] = ln_fwd(h, mix, vec(ln_ffn_g, layer), vec(ln_ffn_b, layer), alpha,
                                                       gather, name="ln_ffn")
        saved.append(sv)

    dh, loss_blk = loss_head(h, target, name="loss_head")
    loss = lax.psum(loss_blk[0, 0], ("x", "y", "c"))

    small = {n: [None] * weights[n].shape[0] for n in REPLICATED + SMALL_CUT}
    W = gather.bufs

    def wgrad(n, a, d, l, tag):
        axis, k, nn = geometry(n)
        if axis == 2:
            slabs = mm(a, d, mode="tn", out_dtype=PAY_DTYPE, slab_width=nn, link=scatter, name=tag)
        else:
            slabs = mm(a, d, mode="tn", out_dtype=PAY_DTYPE, link=scatter, name=tag).reshape(N_DEV, k, nn)
        n_layers = weights[n].shape[0]
        pair = ("pair", n, l)

        def then():
            sums = pair_sum(slabs, scatter.bufs.pop(pair), core, name="pairsum_" + n)
            scatter.push("scatter", sums, n, (n_layers, N_CHIP, k, nn), l, k, nn)

        scatter.side.append((_Chunk("swap", slabs, pair, (N_CHIP, k, nn), 0, 0, k, nn), then))

    for layer in reversed(range(depth)):
        j = layer // 2
        sv = saved[layer]
        dz, dzb, dg, db = ln_bwd(dh, sv["xhat_ffn"], sv["rstd_ffn"], vec(ln_ffn_g, layer), scatter, name="ln_ffn_bwd")
        small["ln_ffn_g"][layer], small["ln_ffn_b"][layer] = dg[0], db[0]
        wgrad("ffn_w_down", sv["u"], dzb, layer, "ffn_down_dw")
        du = mm(dzb, W[("ffn_w_down", layer)], mode="nt", link=scatter, name="ffn_down_dx")
        dhg, dhv, dwg, dwv, dbg, dbv = conv_glu_bwd(du, sv["hup"], conv_w_full[layer], vec(ffn_conv_b, layer),
                                                    scatter, name="ffn_conv_bwd")
        small["ffn_conv_w"][layer] = jnp.concatenate([dwg, dwv], axis=1)
        small["ffn_conv_b"][layer] = jnp.concatenate([dbg, dbv], axis=1)[0]
        dhup = jnp.concatenate([dhg, dhv], axis=1)
        wgrad("ffn_w_up", sv["hb_ffn"], dhup, layer, "ffn_up_dw")
        dh = mm(dhup, W[("ffn_w_up", layer)], mode="nt", add=dz, add_scale=alpha, link=scatter, name="ffn_up_dx")
        dz, dzb, dg, db = ln_bwd(dh, sv["xhat_mem"], sv["rstd_mem"], vec(ln_mem_g, layer), scatter, name="ln_mem_bwd")
        small["ln_mem_g"][layer], small["ln_mem_b"][layer] = dg[0], db[0]
        wgrad("xa_w_out", sv["xa_ob"], dzb, layer, "xa_out_dw")
        do = mm(dzb, W[("xa_w_out", layer)], mode="nt", link=scatter, name="xa_out_dx")
        dq, dk, dv = attn_bwd((sv["xa_q"], xa_d, 0, 1), (sv["xa_kvm"], xa_d, 0, 1), (sv["xa_kvm"], xa_d, XA_HEADS, 1),
                              do, heads=XA_HEADS, softmax=True, masked=False, scale=xa_scale, o=sv["xa_o"],
                              lse=sv["xa_lse"], link=scatter, est_us=ATTN_US["xa"][1], name="xa_attn_bwd")
        dqb = dq.astype(MXU_DTYPE)
        wgrad("xa_w_kv", mem_b, jnp.concatenate([dk, dv], axis=1).astype(MXU_DTYPE), layer, "xa_kv_dw")
        wgrad("xa_w_q", sv["hb_mem"], dqb, layer, "xa_q_dw")
        dh = mm(dqb, W[("xa_w_q", layer)], mode="nt", add=dz, add_scale=alpha, link=scatter, name="xa_q_dx")
        dz, dzb, dg, db = ln_bwd(dh, sv["xhat_mix"], sv["rstd_mix"], vec(ln_mix_g, layer), scatter, name="ln_mix_bwd")
        small["ln_mix_g"][layer], small["ln_mix_b"][layer] = dg[0], db[0]
        if layer % 2 == 0:
            wgrad("ret_w_out", sv["yb"], dzb, j, "ret_out_dw")
            dy = mm(dzb, W[("ret_w_out", j)], mode="nt", link=scatter, name="ret_out_dx")
            do, dgate, dgn = ret_gate_bwd(dy, sv["o"], sv["proj"], 2, vec(ret_gn_g, j), RET_HEADS, name="ret_gate_bwd")
            small["ret_gn_g"][j] = dgn[0]
            dq, dk, dv = attn_bwd((sv["qk"], ret_dk, 0, 1), (sv["qk"], ret_dk, RET_HEADS, 1),
                                  (sv["proj"], ret_dv, RET_HEADS, 1), do, heads=RET_HEADS, softmax=False,
                                  masked=True, log_gamma=log_gamma, link=scatter, est_us=ATTN_US["ret"][1],
                                  name="ret_attn_bwd")
            dqk = ret_rope_bwd(dq, dk, rcos, rsin, ret_dk // 2, ret_dk ** -0.5, name="ret_rope_bwd")
            dproj = jnp.concatenate([dqk, dv.astype(MXU_DTYPE), dgate], axis=1)
            wgrad("ret_w_in", sv["hb_mix"], dproj, j, "ret_in_dw")
            dh = mm(dproj, W[("ret_w_in", j)], mode="nt", add=dz, add_scale=alpha, link=scatter, name="ret_in_dx")
        else:
            wgrad("mla_w_out", sv["ob"], dzb, j, "mla_out_dw")
            do = mm(dzb, W[("mla_w_out", j)], mode="nt", link=scatter, name="mla_out_dx")
            dq, dk, dv = attn_bwd((sv["qr"], MLA_PAD, 0, 1), (sv["kr"], MLA_PAD, 0, 1), (sv["kvf"], MLA_V, 1, 2), do,
                                  heads=MLA_HEADS, softmax=True, masked=True, scale=mla_scale, o=sv["o"],
                                  lse=sv["lse"], link=scatter, est_us=ATTN_US["mla"][1], name="mla_attn_bwd")
            (dq_un,) = mla_rope(dq, mla_ta, mla_tb, MLA_HEADS, backward=True, head_sum=False, out_dtype=F32,
                                name="mla_rope_q_bwd")
            dk_un, dk_rope_sum = mla_rope(dk, mla_ta, mla_tb, MLA_HEADS, backward=True, head_sum=True, out_dtype=F32,
                                          name="mla_rope_k_bwd")
            dq_nope, dq_rope = _mla_unpad(dq_un, MLA_HEADS)
            dqf = jnp.concatenate([dq_nope, dq_rope], axis=2).reshape(S, -1).astype(MXU_DTYPE)
            dk_nope, _ = _mla_unpad(dk_un, MLA_HEADS)
            dkvf = jnp.concatenate([dk_nope, dv.reshape(S, MLA_HEADS, MLA_V)], axis=2).reshape(S, -1).astype(MXU_DTYPE)
            half = MLA_ROPE // 2
            dk_rope = jnp.concatenate([dk_rope_sum[:, :half], dk_rope_sum[:, LANES // 2:LANES // 2 + half]], axis=1)
            wgrad("mla_w_uq", sv["cq"], dqf, j, "mla_uq_dw")
            dcq = mm(dqf, W[("mla_w_uq", j)], mode="nt", link=scatter, name="mla_uq_dx")
            wgrad("mla_w_ukv", sv["ckv"], dkvf, j, "mla_ukv_dw")
            dckv = mm(dkvf, W[("mla_w_ukv", j)], mode="nt", link=scatter, name="mla_ukv_dx")
            dcq_in, dqg = rms_bwd(dcq, sv["proj"], 0, MLA_Q_RANK, vec(qg_full, j), name="mla_q_norm_bwd")
            dckv_in, dkvg = rms_bwd(dckv, sv["proj"], 1, MLA_KV_RANK, vec(kvg_full, j), name="mla_kv_norm_bwd")
            small["mla_q_norm_g"][j], small["mla_kv_norm_g"][j] = dqg[0], dkvg[0]
            dproj = jnp.concatenate([dcq_in, dckv_in, dk_rope], axis=1).astype(MXU_DTYPE)
            wgrad("mla_w_in", sv["hb_mix"], dproj, j, "mla_in_dw")
            dh = mm(dproj, W[("mla_w_in", j)], mode="nt", add=dz, add_scale=alpha, link=scatter, name="mla_in_dx")
    grad_x = dh[None]

    grads, deltas, new_m, new_v = {}, {}, {}, {}
    scatter.ride_out(name="scatter_rest")
    scatter.flush(None, name="scatter_rest")
    assert not scatter.queue and not scatter.side
    for n in BIG:
        grads[n], deltas[n], new_m[n], new_v[n] = adam_update(scatter.bufs[n], weights[n], mom_m[n], mom_v[n],
                                                              name="adam_" + n)

    small_names = list(REPLICATED + SMALL_CUT)
    partial = [jnp.stack(small[n]) for n in small_names]
    allpart = gather_small(pack(partial), name="gather_small_grads")
    rows = allpart.shape[0] // N_DEV
    allpart = allpart.reshape(N_DEV, rows, LANES)

    rep_names = list(REPLICATED)
    rep_w = pack([weights[n] for n in rep_names])
    rep_m = pack([mom_m[n] for n in rep_names])
    rep_v = pack([mom_v[n] for n in rep_names])
    rep_rows = rep_w.shape[0]
    rep_size = sum(weights[n].size for n in rep_names)
    flat_parts = allpart.reshape(N_DEV, rows * LANES)
    rep_parts = jnp.pad(flat_parts[:, :rep_size], ((0, 0), (0, rep_rows * LANES - rep_size)))
    outs = adam_update(rep_parts.reshape(1, N_DEV, rep_rows, LANES), rep_w[None], rep_m[None], rep_v[None],
                       name="adam_replicated")
    for o_, dst in zip(outs, (grads, deltas, new_m, new_v)):
        for n, a in zip(rep_names, unpack(o_.reshape(-1), [weights[n] for n in rep_names])):
            dst[n] = a

    cut_names = list(SMALL_CUT)
    cut_full = [jnp.stack(small[n]) for n in cut_names]
    cut_parts = []
    at = rep_size
    for n, a in zip(cut_names, cut_full):
        whole = flat_parts[:, at:at + a.size].reshape((N_DEV,) + a.shape)
        at += a.size
        width = weights[n].shape[-1]
        start = (0,) * (whole.ndim - 1) + (dev * width,)
        mine = lax.dynamic_slice(whole, start, whole.shape[:-1] + (width,))
        cut_parts.append(mine.reshape(N_DEV, -1))
    cut_parts = jnp.concatenate(cut_parts, axis=1)
    cut_w = pack([weights[n] for n in cut_names])
    cut_m = pack([mom_m[n] for n in cut_names])
    cut_v = pack([mom_v[n] for n in cut_names])
    cut_rows = cut_w.shape[0]
    cut_parts = jnp.pad(cut_parts, ((0, 0), (0, cut_rows * LANES - cut_parts.shape[1])))
    outs = adam_update(cut_parts.reshape(1, N_DEV, cut_rows, LANES), cut_w[None], cut_m[None], cut_v[None],
                       name="adam_small_cut")
    for o_, dst in zip(outs, (grads, deltas, new_m, new_v)):
        for n, a in zip(cut_names, unpack(o_.reshape(-1), [weights[n] for n in cut_names])):
            dst[n] = a

    return (loss, grad_x, *[grads[n] for n in order], *[deltas[n] for n in order],
            *[new_m[n] for n in order], *[new_v[n] for n in order])
```

```python
import functools
import math

import jax
import jax.numpy as jnp
from jax import lax
from jax.experimental import pallas as pl
from jax.experimental.pallas import tpu as pltpu

F32 = jnp.float32
MXU_DTYPE = jnp.bfloat16
PAY_DTYPE = jnp.bfloat16
MESH = pl.DeviceIdType.MESH
N_DEV = 8
N_CHIP = 4

DEPTH = 4
CHUNK = 64
RET_HEADS = 8
MLA_HEADS = 16
MLA_Q_RANK = 512
MLA_KV_RANK = 512
MLA_NOPE = 128
MLA_ROPE = 64
MLA_V = 128
MLA_PAD = 256
XA_HEADS = 4
ROPE_BASE = 10000.0
LN_EPS = 1e-5
RMS_EPS = 1e-6
NEG_INF = -1e30
ADAM_LR = 0.001
ADAM_B1 = 0.9
ADAM_B2 = 0.999
ADAM_EPS = 1e-08
ADAM_WD = 0.01
ADAM_STEP = 10

LANES = 128
VMEM_LIMIT_BYTES = 48 * 1024 * 1024
TILE_M = 1024
TILE_N = 1024
TILE_K = 2048
SLAB_TILE_N = 1536
ATTN_TILE = 512
ATTN_PARTS = 2
ROW_TILE = 256

MXU_FLOPS_PER_US = 7.0e8
GATHER_BYTES_PER_US = 5.5e4
SCATTER_BYTES_PER_US = 1.5e4
CHUNK_BYTES = 1280 * 1024
HOST_CHUNKS = 12
LEVEL2_US = 30.0
SIDE_CHUNKS = 4
ATTN_US = {"ret": (125.0, 160.0), "mla": (320.0, 270.0), "xa": (35.0, 35.0)}
ROWWISE_US = {"ln": (25.0, 22.0), "conv": (65.0, 110.0)}


def _tile(n, cap, mult=LANES):
    best = None
    for t in range(mult, min(n, cap) + 1, mult):
        if n % t == 0:
            best = t
    return n if best is None else best


def _params(n_axes):
    return pltpu.CompilerParams(dimension_semantics=("arbitrary",) * n_axes,
                                vmem_limit_bytes=VMEM_LIMIT_BYTES)


def _place():
    return lax.axis_index("x"), lax.axis_index("y"), lax.axis_index("c")


def _other_chips(x, y):
    return [(1 - x, y), (x, 1 - y), (1 - x, 1 - y)]


def _remote(src, dst, send, recv, to):
    return pltpu.make_async_remote_copy(src_ref=src, dst_ref=dst, send_sem=send, recv_sem=recv,
                                        device_id=to, device_id_type=MESH)


class _Chunk:
    def __init__(self, kind, src, key, dst_shape, layer, r0, rc, cols, axis=None, k=None, n=None, half=None):
        self.kind, self.src, self.key, self.dst_shape = kind, src, key, dst_shape
        self.layer, self.r0, self.rc, self.axis, self.k, self.n, self.half = layer, r0, rc, axis, k, n, half
        self.nbytes = rc * cols * jnp.dtype(PAY_DTYPE).itemsize

    def copies(self, src_ref, dst_ref, send, recv, loc, i):
        x, y, c = _place()
        chips = _other_chips(x, y)
        rows = pl.ds(self.r0, self.rc)
        sibling, x_nbr, y_nbr, diagonal = (x, y, 1 - c), (1 - x, y, c), (x, 1 - y, c), (1 - x, 1 - y, c)
        if self.kind == "gather":
            peers = [sibling, x_nbr, y_nbr]
            src = src_ref.at[self.layer, rows, :]

            def slab(px, py, pc):
                idx = 4 * px + 2 * py + pc
                if self.axis == 1:
                    return dst_ref.at[pl.ds(idx * self.k + self.r0, self.rc), :]
                return dst_ref.at[rows, pl.ds(idx * self.n, self.n)]

            sends = [_remote(src, slab(x, y, c), send.at[4 * i + t], recv.at[4 * i + t], p)
                     for t, p in enumerate(peers)]
            recvs = [_remote(src, slab(*p), send.at[4 * i + t], recv.at[4 * i + t], p)
                     for t, p in enumerate(peers)]
            return sends, recvs, pltpu.make_async_copy(src, slab(x, y, c), loc.at[i])
        if self.kind == "relay":
            def piece(p, r):
                idx = 4 * p[0] + 2 * p[1] + p[2]
                if self.axis == 1:
                    return dst_ref.at[pl.ds(idx * self.k + r, self.rc), :]
                return dst_ref.at[pl.ds(r, self.rc), pl.ds(idx * self.n, self.n)]

            top, bottom = self.r0, self.half + self.r0
            sends = [_remote(piece(y_nbr, top), piece(y_nbr, top), send.at[4 * i], recv.at[4 * i], x_nbr),
                     _remote(piece(x_nbr, bottom), piece(x_nbr, bottom), send.at[4 * i + 1], recv.at[4 * i + 1], y_nbr)]
            recvs = [_remote(piece(diagonal, top), piece(diagonal, top), send.at[4 * i], recv.at[4 * i], x_nbr),
                     _remote(piece(diagonal, bottom), piece(diagonal, bottom), send.at[4 * i + 1], recv.at[4 * i + 1],
                             y_nbr)]
            return sends, recvs, None
        if self.kind == "gather2":
            def slab(px, py, pc):
                idx = 4 * px + 2 * py + pc
                if self.axis == 1:
                    return dst_ref.at[pl.ds(idx * self.k, self.k), :]
                return dst_ref.at[:, pl.ds(idx * self.n, self.n)]

            sends = [_remote(slab(px, py, c), slab(px, py, c), send.at[4 * i + t], recv.at[4 * i + t], sibling)
                     for t, (px, py) in enumerate(chips)]
            recvs = [_remote(slab(px, py, 1 - c), slab(px, py, 1 - c), send.at[4 * i + t], recv.at[4 * i + t], sibling)
                     for t, (px, py) in enumerate(chips)]
            return sends, recvs, None
        if self.kind == "swap":
            sends = [_remote(src_ref.at[2 * j + (1 - c)], dst_ref.at[j], send.at[4 * i + j], recv.at[4 * i + j], sibling)
                     for j in range(N_CHIP)]
            recvs = [_remote(src_ref.at[2 * j + c], dst_ref.at[j], send.at[4 * i + j], recv.at[4 * i + j], sibling)
                     for j in range(N_CHIP)]
            return sends, recvs, None
        mine = 2 * x + y
        own = src_ref.at[mine, rows, :]
        sends = [_remote(src_ref.at[2 * px + py, rows, :], dst_ref.at[self.layer, mine, rows, :],
                         send.at[4 * i + t], recv.at[4 * i + t], (px, py, c)) for t, (px, py) in enumerate(chips)]
        recvs = [_remote(own, dst_ref.at[self.layer, 2 * px + py, rows, :],
                         send.at[4 * i + t], recv.at[4 * i + t], (px, py, c)) for t, (px, py) in enumerate(chips)]
        return sends, recvs, pltpu.make_async_copy(own, dst_ref.at[self.layer, mine, rows, :], loc.at[i])


def _call(body, *, grid, in_specs, out_specs, out_shape, operands, scratch=(), name, chunks=(), bufs=None):
    n_in, n_out, n_scr = len(operands), len(out_shape), len(scratch)
    params = _params(len(grid)) if grid else pltpu.CompilerParams(vmem_limit_bytes=VMEM_LIMIT_BYTES)
    if not chunks:
        return pl.pallas_call(
            body, out_shape=tuple(out_shape), grid=grid, in_specs=list(in_specs), out_specs=tuple(out_specs),
            scratch_shapes=list(scratch), compiler_params=params, name=name)(*operands)
    srcs, keys = [], []
    for ch in chunks:
        if ch.src is not None and not any(ch.src is s for s in srcs):
            srcs.append(ch.src)
        if ch.key not in keys:
            keys.append(ch.key)
    shape_of = {ch.key: ch.dst_shape for ch in chunks}
    held = [k for k in keys if bufs.get(k) is not None]
    extra = srcs + [bufs[k] for k in held]
    aliases = {n_in + len(srcs) + i: n_out + keys.index(k) for i, k in enumerate(held)}
    anywhere = pl.BlockSpec(memory_space=pl.ANY)
    n_extra, n_keys, n_ch = len(extra), len(keys), len(chunks)

    def hosted(*refs):
        src_refs = refs[n_in:n_in + len(srcs)]
        o0 = n_in + n_extra
        dst_refs = refs[o0 + n_out:o0 + n_out + n_keys]
        s0 = o0 + n_out + n_keys
        send, recv, loc = refs[s0 + n_scr:]
        sends, recvs, locs = [], [], []
        for i, ch in enumerate(chunks):
            src_ref = None if ch.src is None else src_refs[[ch.src is s for s in srcs].index(True)]
            s_, r_, l_ = ch.copies(src_ref, dst_refs[keys.index(ch.key)], send, recv, loc, i)
            sends += s_
            recvs += r_
            if l_ is not None:
                locs.append(l_)

        def start():
            for cp in locs + sends:
                cp.start()

        def finish():
            for cp in recvs:
                cp.wait_recv()
            for cp in sends:
                cp.wait_send()
            for cp in locs:
                cp.wait()

        if grid:
            first = functools.reduce(jnp.logical_and, [pl.program_id(a) == 0 for a in range(len(grid))])
            last = functools.reduce(jnp.logical_and, [pl.program_id(a) == grid[a] - 1 for a in range(len(grid))])
            pl.when(first)(start)
            body(*refs[:n_in], *refs[o0:o0 + n_out], *refs[s0:s0 + n_scr])
            pl.when(last)(finish)
        else:
            start()
            body(*refs[:n_in], *refs[o0:o0 + n_out], *refs[s0:s0 + n_scr])
            finish()

    res = pl.pallas_call(
        hosted, out_shape=tuple(out_shape) + tuple(jax.ShapeDtypeStruct(shape_of[k], PAY_DTYPE) for k in keys),
        grid=grid, in_specs=list(in_specs) + [anywhere] * n_extra, out_specs=tuple(out_specs) + (anywhere,) * n_keys,
        scratch_shapes=list(scratch) + [pltpu.SemaphoreType.DMA((4 * n_ch,)), pltpu.SemaphoreType.DMA((4 * n_ch,)),
                                        pltpu.SemaphoreType.DMA((n_ch,))],
        input_output_aliases=aliases, compiler_params=params, name=name)(*operands, *extra)
    for k, arr in zip(keys, res[n_out:]):
        bufs[k] = arr
    return res[:n_out]


class _Link:
    def __init__(self, bytes_per_us):
        self.rate = bytes_per_us
        self.queue, self.bufs = [], {}
        self.side = []
        self.after = []
        self.left = {}
        self.on_done = {}

    def push(self, kind, src, key, dst_shape, layer, rows, cols, done=None, front=False, **geometry):
        fits = [r for r in range(16, rows + 1, 16) if rows % r == 0 and r * cols * 2 <= CHUNK_BYTES]
        rc = max(fits) if fits else min(r for r in range(16, rows + 1, 16) if rows % r == 0)
        new = [_Chunk(kind, src, key, dst_shape, layer, r0, rc, cols, **geometry) for r0 in range(0, rows, rc)]
        self.queue = new + self.queue if front else self.queue + new
        self.left[key] = self.left.get(key, 0) + rows // rc
        if done is not None:
            self.on_done[key] = done

    def settle(self):
        todo, self.after = self.after, []
        for f in todo:
            f()

    def _pop(self, count):
        out = []
        for ch in self.queue[:count]:
            self.left[ch.key] -= 1
            if self.left[ch.key] == 0 and ch.key in self.on_done:
                self.after.append(self.on_done.pop(ch.key))
            out.append(ch)
        self.queue = self.queue[count:]
        return out

    def _side(self):
        riders, self.side = self.side[:SIDE_CHUNKS], self.side[SIDE_CHUNKS:]
        self.after += [then for _, then in riders if then is not None]
        return [ch for ch, _ in riders]

    def take(self, est_us):
        self.settle()
        out = self._side()
        budget, count = est_us * self.rate, 0
        while count < min(len(self.queue), HOST_CHUNKS) and self.queue[count].nbytes <= 2 * budget:
            budget -= self.queue[count].nbytes
            count += 1
        return out + self._pop(count)

    def flush(self, key, name):
        self.settle()
        if key is None:
            n = len(self.queue)
        else:
            n = max([i + 1 for i, ch in enumerate(self.queue) if ch.key == key], default=0)
        while n > 0:
            group = self._side() + self._pop(min(n, HOST_CHUNKS))
            n -= min(n, HOST_CHUNKS)
            _call(lambda: None, grid=(), in_specs=[], out_specs=(), out_shape=(), operands=[], name=name,
                  chunks=group, bufs=self.bufs)
            self.settle()

    def carry(self, est_us, name):
        _call(lambda: None, grid=(), in_specs=[], out_specs=(), out_shape=(), operands=[], name=name,
              chunks=self.take(est_us), bufs=self.bufs)

    def ride_out(self, name):
        self.settle()
        while self.side:
            _call(lambda: None, grid=(), in_specs=[], out_specs=(), out_shape=(), operands=[], name=name,
                  chunks=self._side(), bufs=self.bufs)
            self.settle()


def mm(a, b, *, mode="nn", out_dtype=F32, add=None, add_scale=1.0, slab_width=None, link=None, name):
    if mode == "nn":
        (M, K), N = a.shape, b.shape[-1]
    elif mode == "nt":
        (M, K), N = a.shape, b.shape[-2]
    else:
        (K, M), N = a.shape, b.shape[-1]
    tm, tk = _tile(M, TILE_M), _tile(K, TILE_K)
    tn = _tile(N, TILE_N) if slab_width is None else _tile(slab_width, SLAB_TILE_N)
    nk = K // tk
    dims = {"nn": (((1,), (0,)), ((), ())), "nt": (((1,), (1,)), ((), ())),
            "tn": (((0,), (0,)), ((), ()))}[mode]

    a_spec = (pl.BlockSpec((tk, tm), lambda i, j, k: (k, i)) if mode == "tn"
              else pl.BlockSpec((tm, tk), lambda i, j, k: (i, k)))
    b_spec = (pl.BlockSpec((tn, tk), lambda i, j, k: (j, k)) if mode == "nt"
              else pl.BlockSpec((tk, tn), lambda i, j, k: (k, j)))
    in_specs, operands = [a_spec, b_spec], [a, b]
    if add is not None:
        in_specs.append(pl.BlockSpec((tm, tn), lambda i, j, k: (i, j)))
        operands.append(add)
    if slab_width is None:
        out_shape = jax.ShapeDtypeStruct((M, N), out_dtype)
        out_spec = pl.BlockSpec((tm, tn), lambda i, j, k: (i, j))
    else:
        per = slab_width // tn
        out_shape = jax.ShapeDtypeStruct((N // slab_width, M, slab_width), out_dtype)
        out_spec = pl.BlockSpec((None, tm, tn), lambda i, j, k: (j // per, i, j % per))
    n_in = len(operands)

    def body(*refs):
        a_ref, b_ref = refs[0], refs[1]
        add_ref = refs[2] if add is not None else None
        o_ref = refs[n_in]
        k = pl.program_id(2)
        part = lax.dot_general(a_ref[...].astype(MXU_DTYPE), b_ref[...].astype(MXU_DTYPE),
                               dims, preferred_element_type=F32)

        def finish(r):
            if add_ref is not None:
                r = r + add_scale * add_ref[...].astype(F32)
            o_ref[...] = r.astype(o_ref.dtype)

        if nk == 1:
            finish(part)
        else:
            acc_ref = refs[n_in + 1]

            @pl.when(k == 0)
            def _():
                acc_ref[...] = part

            @pl.when(jnp.logical_and(k > 0, k < nk - 1))
            def _():
                acc_ref[...] += part

            @pl.when(k == nk - 1)
            def _():
                finish(acc_ref[...] + part)

    chunks = link.take(2.0 * M * N * K / MXU_FLOPS_PER_US) if link is not None else ()
    (out,) = _call(body, grid=(M // tm, N // tn, nk), in_specs=in_specs, out_specs=(out_spec,), out_shape=(out_shape,),
                   operands=operands, scratch=[pltpu.VMEM((tm, tn), F32)] if nk > 1 else [], name=name,
                   chunks=chunks, bufs=link.bufs if link is not None else None)
    return out


def ln_fwd(h, f, g, b, alpha, link, name):
    S, D = h.shape
    tm = _tile(S, ROW_TILE, 8)

    def body(h_ref, f_ref, g_ref, b_ref, y_ref, yb_ref, xhat_ref, rstd_ref):
        z = alpha * h_ref[...] + f_ref[...]
        mu = jnp.mean(z, axis=-1, keepdims=True)
        zc = z - mu
        var = jnp.mean(zc * zc, axis=-1, keepdims=True)
        rstd = lax.rsqrt(var + LN_EPS)
        xhat = zc * rstd
        y = xhat * g_ref[...] + b_ref[...]
        y_ref[...] = y
        yb_ref[...] = y.astype(yb_ref.dtype)
        xhat_ref[...] = xhat
        rstd_ref[...] = rstd

    row = pl.BlockSpec((tm, D), lambda i: (i, 0))
    vec = pl.BlockSpec((1, D), lambda i: (0, 0))
    return _call(
        body, grid=(S // tm,), in_specs=[row, row, vec, vec],
        out_specs=(row, row, row, pl.BlockSpec((tm, 1), lambda i: (i, 0))),
        out_shape=(jax.ShapeDtypeStruct((S, D), F32), jax.ShapeDtypeStruct((S, D), MXU_DTYPE),
                   jax.ShapeDtypeStruct((S, D), F32), jax.ShapeDtypeStruct((S, 1), F32)),
        operands=[h, f, g, b], name=name, chunks=link.take(ROWWISE_US["ln"][0]), bufs=link.bufs)


def ln_bwd(dy, xhat, rstd, g, link, name):
    S, D = dy.shape
    tm = _tile(S, ROW_TILE, 8)

    def body(dy_ref, xhat_ref, rstd_ref, g_ref, dz_ref, dzb_ref, dg_ref, db_ref):
        @pl.when(pl.program_id(0) == 0)
        def _():
            dg_ref[...] = jnp.zeros_like(dg_ref)
            db_ref[...] = jnp.zeros_like(db_ref)

        dy = dy_ref[...]
        xhat = xhat_ref[...]
        dxh = dy * g_ref[...]
        m1 = jnp.mean(dxh, axis=-1, keepdims=True)
        m2 = jnp.mean(dxh * xhat, axis=-1, keepdims=True)
        dz = rstd_ref[...] * (dxh - m1 - xhat * m2)
        dz_ref[...] = dz
        dzb_ref[...] = dz.astype(dzb_ref.dtype)
        dg_ref[...] += jnp.sum(dy * xhat, axis=0, keepdims=True)
        db_ref[...] += jnp.sum(dy, axis=0, keepdims=True)

    row = pl.BlockSpec((tm, D), lambda i: (i, 0))
    vec = pl.BlockSpec((1, D), lambda i: (0, 0))
    return _call(
        body, grid=(S // tm,), in_specs=[row, row, pl.BlockSpec((tm, 1), lambda i: (i, 0)), vec],
        out_specs=(row, row, vec, vec),
        out_shape=(jax.ShapeDtypeStruct((S, D), F32), jax.ShapeDtypeStruct((S, D), MXU_DTYPE),
                   jax.ShapeDtypeStruct((1, D), F32), jax.ShapeDtypeStruct((1, D), F32)),
        operands=[dy, xhat, rstd, g], name=name, chunks=link.take(ROWWISE_US["ln"][1]), bufs=link.bufs)


def rms_fwd(x, col, width, g, name):
    S = x.shape[0]
    tm = _tile(S, ROW_TILE, 8)

    def body(x_ref, g_ref, y_ref):
        xv = x_ref[...]
        r = lax.rsqrt(jnp.mean(xv * xv, axis=-1, keepdims=True) + RMS_EPS)
        y_ref[...] = (xv * r * g_ref[...]).astype(y_ref.dtype)

    return pl.pallas_call(
        body, out_shape=jax.ShapeDtypeStruct((S, width), MXU_DTYPE), grid=(S // tm,),
        in_specs=[pl.BlockSpec((tm, width), lambda i: (i, col)), pl.BlockSpec((1, width), lambda i: (0, 0))],
        out_specs=pl.BlockSpec((tm, width), lambda i: (i, 0)), compiler_params=_params(1), name=name)(x, g)


def rms_bwd(dy, x, col, width, g, name):
    S = x.shape[0]
    tm = _tile(S, ROW_TILE, 8)

    def body(dy_ref, x_ref, g_ref, dx_ref, dg_ref):
        @pl.when(pl.program_id(0) == 0)
        def _():
            dg_ref[...] = jnp.zeros_like(dg_ref)

        xv = x_ref[...]
        dy = dy_ref[...]
        r = lax.rsqrt(jnp.mean(xv * xv, axis=-1, keepdims=True) + RMS_EPS)
        dxn = dy * g_ref[...]
        m = jnp.mean(dxn * xv, axis=-1, keepdims=True)
        dx_ref[...] = r * (dxn - xv * (r * r * m))
        dg_ref[...] += jnp.sum(dy * xv * r, axis=0, keepdims=True)

    return pl.pallas_call(
        body, out_shape=(jax.ShapeDtypeStruct((S, width), F32), jax.ShapeDtypeStruct((1, width), F32)),
        grid=(S // tm,),
        in_specs=[pl.BlockSpec((tm, width), lambda i: (i, 0)), pl.BlockSpec((tm, width), lambda i: (i, col)),
                  pl.BlockSpec((1, width), lambda i: (0, 0))],
        out_specs=(pl.BlockSpec((tm, width), lambda i: (i, 0)), pl.BlockSpec((1, width), lambda i: (0, 0))),
        compiler_params=_params(1), name=name)(dy, x, g)


def ret_rope_fwd(proj, cos, sin, n_groups, half, k_from, k_scale, name):
    S = proj.shape[0]
    W = n_groups * 2 * half
    tm = _tile(S, ROW_TILE, 8)

    def body(x_ref, c_ref, s_ref, o_ref):
        c, s = c_ref[...], s_ref[...]
        for gi in range(n_groups):
            lo = gi * 2 * half
            x1 = x_ref[:, lo:lo + half]
            x2 = x_ref[:, lo + half:lo + 2 * half]
            sc = k_scale if gi >= k_from else 1.0
            o_ref[:, lo:lo + half] = ((x1 * c - x2 * s) * sc).astype(o_ref.dtype)
            o_ref[:, lo + half:lo + 2 * half] = ((x2 * c + x1 * s) * sc).astype(o_ref.dtype)

    tab = pl.BlockSpec((tm, half), lambda i: (i, 0))
    return pl.pallas_call(
        body, out_shape=jax.ShapeDtypeStruct((S, W), MXU_DTYPE), grid=(S // tm,),
        in_specs=[pl.BlockSpec((tm, W), lambda i: (i, 0)), tab, tab],
        out_specs=pl.BlockSpec((tm, W), lambda i: (i, 0)), compiler_params=_params(1), name=name)(proj, cos, sin)


def ret_rope_bwd(dq, dk, cos, sin, half, k_scale, name):
    S, Wq = dq.shape
    n_heads = Wq // (2 * half)
    tm = _tile(S, ROW_TILE, 8)

    def body(dq_ref, dk_ref, c_ref, s_ref, o_ref):
        c, s = c_ref[...], s_ref[...]
        for part, (d_ref, sc) in enumerate(((dq_ref, 1.0), (dk_ref, k_scale))):
            for hi in range(n_heads):
                lo = hi * 2 * half
                d1 = d_ref[:, lo:lo + half]
                d2 = d_ref[:, lo + half:lo + 2 * half]
                base = part * Wq + lo
                o_ref[:, base:base + half] = ((d1 * c + d2 * s) * sc).astype(o_ref.dtype)
                o_ref[:, base + half:base + 2 * half] = ((d2 * c - d1 * s) * sc).astype(o_ref.dtype)

    tab = pl.BlockSpec((tm, half), lambda i: (i, 0))
    row = pl.BlockSpec((tm, Wq), lambda i: (i, 0))
    return pl.pallas_call(
        body, out_shape=jax.ShapeDtypeStruct((S, 2 * Wq), MXU_DTYPE), grid=(S // tm,),
        in_specs=[row, row, tab, tab], out_specs=pl.BlockSpec((tm, 2 * Wq), lambda i: (i, 0)),
        compiler_params=_params(1), name=name)(dq, dk, cos, sin)


def mla_rope(x, ta, tb, n_heads, *, backward, head_sum, out_dtype, name):
    S = x.shape[0]
    W = n_heads * MLA_PAD
    tm = _tile(S, ROW_TILE, 8)

    def body(x_ref, a_ref, b_ref, o_ref, *rest):
        a, b = a_ref[...], b_ref[...]
        total = jnp.zeros((tm, LANES), F32)
        for hi in range(n_heads):
            lo = hi * MLA_PAD
            o_ref[:, lo:lo + MLA_NOPE] = x_ref[:, lo:lo + MLA_NOPE].astype(o_ref.dtype)
            t = x_ref[:, lo + MLA_NOPE:lo + MLA_PAD].astype(F32)
            if backward:
                r = t * a + pltpu.roll(t * b, LANES // 2, 1)
            else:
                r = t * a + pltpu.roll(t, LANES // 2, 1) * b
            o_ref[:, lo + MLA_NOPE:lo + MLA_PAD] = r.astype(o_ref.dtype)
            total = total + r
        if head_sum:
            rest[0][...] = total

    tab = pl.BlockSpec((tm, LANES), lambda i: (i, 0))
    row = pl.BlockSpec((tm, W), lambda i: (i, 0))
    out_shape = [jax.ShapeDtypeStruct((S, W), out_dtype)]
    out_specs = [row]
    if head_sum:
        out_shape.append(jax.ShapeDtypeStruct((S, LANES), F32))
        out_specs.append(tab)
    return pl.pallas_call(
        body, out_shape=tuple(out_shape), grid=(S // tm,), in_specs=[row, tab, tab],
        out_specs=tuple(out_specs), compiler_params=_params(1), name=name)(x, ta, tb)


def _visible(qi, kj, tq, tk, r0, rows):
    n = qi * tq + r0 + lax.broadcasted_iota(jnp.int32, (rows, tk), 0)
    m = kj * tk + lax.broadcasted_iota(jnp.int32, (rows, tk), 1)
    shift = CHUNK.bit_length() - 1
    vis = lax.shift_right_logical(m, shift) <= lax.shift_right_logical(n, shift)
    return vis, jnp.abs(n - m).astype(F32)


def attn_fwd(q, k, v, *, heads, softmax, masked, scale=1.0, log_gamma=None, link=None, est_us=0.0, name):
    (qa, dqk, q0, qs), (ka, _, k0, ks), (va, dv, v0, vs) = q, k, v
    Sq, Sk = qa.shape[0], ka.shape[0]
    tq, tk = _tile(Sq, ATTN_TILE, 8), _tile(Sk, ATTN_TILE, 8)
    nq, nk = Sq // tq, Sk // tk
    assert not masked or tq == tk
    sub = tq // ATTN_PARTS if tq % (16 * ATTN_PARTS) == 0 else tq

    def body(*refs):
        if softmax:
            q_ref, k_ref, v_ref, o_ref, ob_ref, lse_ref, m_ref, l_ref, acc_ref = refs
        else:
            q_ref, k_ref, v_ref, lg_ref, o_ref, acc_ref = refs
        qi, kj = pl.program_id(1), pl.program_id(2)

        @pl.when(kj == 0)
        def _():
            acc_ref[...] = jnp.zeros_like(acc_ref)
            if softmax:
                m_ref[...] = jnp.full_like(m_ref, NEG_INF)
                l_ref[...] = jnp.zeros_like(l_ref)

        def step(diagonal):
            kb = k_ref[...].astype(MXU_DTYPE)
            vb = v_ref[...].astype(MXU_DTYPE)
            parts = [pl.ds(r * sub, sub) for r in range(tq // sub)]
            scores = [lax.dot_general(q_ref[rows, :].astype(MXU_DTYPE), kb, (((1,), (1,)), ((), ())),
                                      preferred_element_type=F32) for rows in parts]
            for r, rows in enumerate(parts):
                s = scores[r]
                if diagonal or not softmax:
                    vis, dist = _visible(qi, kj, tq, tk, r * sub, sub)
                if softmax:
                    s = s * scale
                    if diagonal:
                        s = jnp.where(vis, s, NEG_INF)
                    m_old = m_ref[rows, :]
                    m_new = jnp.maximum(m_old, jnp.max(s, axis=-1, keepdims=True))
                    p = jnp.exp(s - m_new)
                    corr = jnp.exp(m_old - m_new)
                    l_ref[rows, :] = corr * l_ref[rows, :] + jnp.sum(p, axis=-1, keepdims=True)
                    acc_ref[rows, :] = (corr * acc_ref[rows, :]
                                        + jnp.dot(p.astype(MXU_DTYPE), vb, preferred_element_type=F32))
                    m_ref[rows, :] = m_new
                else:
                    decay = jnp.exp(lg_ref[0:1, 0:1] * dist)
                    if diagonal:
                        decay = jnp.where(vis, decay, 0.0)
                    acc_ref[rows, :] += jnp.dot((s * decay).astype(MXU_DTYPE), vb, preferred_element_type=F32)

        if masked:
            pl.when(kj < qi)(functools.partial(step, False))
            pl.when(kj == qi)(functools.partial(step, True))
        else:
            step(False)

        @pl.when(kj == nk - 1)
        def _():
            if softmax:
                out = acc_ref[...] / l_ref[...]
                o_ref[...] = out
                ob_ref[...] = out.astype(ob_ref.dtype)
                lse_ref[...] = m_ref[...] + jnp.log(l_ref[...])
            else:
                o_ref[...] = acc_ref[...].astype(o_ref.dtype)

    kcap = (lambda qi, kj: jnp.minimum(kj, qi)) if masked else (lambda qi, kj: kj)
    in_specs = [pl.BlockSpec((tq, dqk), lambda h, qi, kj: (qi, q0 + qs * h)),
                pl.BlockSpec((tk, dqk), lambda h, qi, kj: (kcap(qi, kj), k0 + ks * h)),
                pl.BlockSpec((tk, dv), lambda h, qi, kj: (kcap(qi, kj), v0 + vs * h))]
    operands = [qa, ka, va]
    out_shape = [jax.ShapeDtypeStruct((Sq, heads * dv), F32)]
    out_specs = [pl.BlockSpec((tq, dv), lambda h, qi, kj: (qi, h))]
    scratch = []
    if softmax:
        out_shape += [jax.ShapeDtypeStruct((Sq, heads * dv), MXU_DTYPE), jax.ShapeDtypeStruct((heads, Sq, 1), F32)]
        out_specs += [pl.BlockSpec((tq, dv), lambda h, qi, kj: (qi, h)),
                      pl.BlockSpec((None, tq, 1), lambda h, qi, kj: (h, qi, 0))]
        scratch += [pltpu.VMEM((tq, 1), F32), pltpu.VMEM((tq, 1), F32)]
    else:
        in_specs.append(pl.BlockSpec((None, 8, LANES), lambda h, qi, kj: (h, 0, 0)))
        operands.append(log_gamma)
    scratch.append(pltpu.VMEM((tq, dv), F32))
    chunks = link.take(est_us) if link is not None else ()
    return _call(body, grid=(heads, nq, nk), in_specs=in_specs, out_specs=out_specs, out_shape=out_shape,
                 operands=operands, scratch=scratch, name=name, chunks=chunks,
                 bufs=link.bufs if link is not None else None)


def attn_bwd(q, k, v, do, *, heads, softmax, masked, scale=1.0, log_gamma=None, o=None, lse=None,
             link=None, est_us=0.0, name):
    (qa, dqk, q0, qs), (ka, _, k0, ks), (va, dv, v0, vs) = q, k, v
    Sq, Sk = qa.shape[0], ka.shape[0]
    tq, tk = _tile(Sq, ATTN_TILE, 8), _tile(Sk, ATTN_TILE, 8)
    nq, nk = Sq // tq, Sk // tk
    assert not masked or tq == tk
    sub = tq // ATTN_PARTS if tq % (16 * ATTN_PARTS) == 0 else tq
    contract0 = (((0,), (0,)), ((), ()))

    def body(*refs):
        if softmax:
            q_ref, k_ref, v_ref, do_ref, o_ref, lse_ref, dq_ref, dk_ref, dv_ref = refs
        else:
            q_ref, k_ref, v_ref, do_ref, lg_ref, dq_ref, dk_ref, dv_ref = refs
        kj, qi = pl.program_id(1), pl.program_id(2)

        @pl.when(jnp.logical_and(kj == 0, qi == 0))
        def _():
            dq_ref[...] = jnp.zeros_like(dq_ref)

        @pl.when(qi == 0)
        def _():
            dk_ref[...] = jnp.zeros_like(dk_ref)
            dv_ref[...] = jnp.zeros_like(dv_ref)

        def step(diagonal):
            kb = k_ref[...].astype(MXU_DTYPE)
            vb = v_ref[...].astype(MXU_DTYPE)
            parts = [pl.ds(r * sub, sub) for r in range(tq // sub)]
            nt = (((1,), (1,)), ((), ()))
            qbs = [q_ref[rows, :].astype(MXU_DTYPE) for rows in parts]
            dobs = [do_ref[rows, :].astype(MXU_DTYPE) for rows in parts]
            scores = [lax.dot_general(qb, kb, nt, preferred_element_type=F32) for qb in qbs]
            dps = [lax.dot_general(dob, vb, nt, preferred_element_type=F32) for dob in dobs]
            dv_new, dk_new = dv_ref[...], dk_ref[...]
            for r, rows in enumerate(parts):
                s, dp, qb, dob = scores[r], dps[r], qbs[r], dobs[r]
                if diagonal or not softmax:
                    vis, dist = _visible(qi, kj, tq, tk, r * sub, sub)
                if softmax:
                    s = s * scale
                    if diagonal:
                        s = jnp.where(vis, s, NEG_INF)
                    p = jnp.exp(s - lse_ref[rows, :])
                    delta = jnp.sum(do_ref[rows, :].astype(F32) * o_ref[rows, :], axis=-1, keepdims=True)
                    ds = p * (dp - delta) * scale
                else:
                    decay = jnp.exp(lg_ref[0:1, 0:1] * dist)
                    if diagonal:
                        decay = jnp.where(vis, decay, 0.0)
                    p = s * decay
                    ds = dp * decay
                pb = p.astype(MXU_DTYPE)
                dsb = ds.astype(MXU_DTYPE)
                dv_new = dv_new + lax.dot_general(pb, dob, contract0, preferred_element_type=F32)
                dk_new = dk_new + lax.dot_general(dsb, qb, contract0, preferred_element_type=F32)
                out_rows = pl.ds(pl.multiple_of(qi * tq + r * sub, sub), sub)
                dq_ref[out_rows, :] += jnp.dot(dsb, kb, preferred_element_type=F32)
            dv_ref[...] = dv_new
            dk_ref[...] = dk_new

        if masked:
            pl.when(qi > kj)(functools.partial(step, False))
            pl.when(qi == kj)(functools.partial(step, True))
        else:
            step(False)

    qcap = (lambda kj, qi: jnp.maximum(qi, kj)) if masked else (lambda kj, qi: qi)
    in_specs = [pl.BlockSpec((tq, dqk), lambda h, kj, qi: (qcap(kj, qi), q0 + qs * h)),
                pl.BlockSpec((tk, dqk), lambda h, kj, qi: (kj, k0 + ks * h)),
                pl.BlockSpec((tk, dv), lambda h, kj, qi: (kj, v0 + vs * h)),
                pl.BlockSpec((tq, dv), lambda h, kj, qi: (qcap(kj, qi), h))]
    operands = [qa, ka, va, do]
    if softmax:
        in_specs += [pl.BlockSpec((tq, dv), lambda h, kj, qi: (qcap(kj, qi), h)),
                     pl.BlockSpec((None, tq, 1), lambda h, kj, qi: (h, qcap(kj, qi), 0))]
        operands += [o, lse]
    else:
        in_specs.append(pl.BlockSpec((None, 8, LANES), lambda h, kj, qi: (h, 0, 0)))
        operands.append(log_gamma)
    chunks = link.take(est_us) if link is not None else ()
    return _call(
        body, grid=(heads, nk, nq), in_specs=in_specs,
        out_specs=(pl.BlockSpec((Sq, dqk), lambda h, kj, qi: (0, h)),
                   pl.BlockSpec((tk, dqk), lambda h, kj, qi: (kj, h)),
                   pl.BlockSpec((tk, dv), lambda h, kj, qi: (kj, h))),
        out_shape=(jax.ShapeDtypeStruct((Sq, heads * dqk), F32), jax.ShapeDtypeStruct((Sk, heads * dqk), F32),
                   jax.ShapeDtypeStruct((Sk, heads * dv), F32)),
        operands=operands, name=name, chunks=chunks, bufs=link.bufs if link is not None else None)


def _sigmoid(x):
    return 0.5 * jnp.tanh(0.5 * x) + 0.5


def ret_gate_fwd(o, proj, gate_col, gn, heads, name):
    S, W = o.shape
    dv = W // heads
    tm = _tile(S, ROW_TILE // 2, 8)

    def body(o_ref, g_ref, gn_ref, y_ref):
        for hi in range(heads):
            cols = slice(hi * dv, (hi + 1) * dv)
            oh = o_ref[:, cols]
            mu = jnp.mean(oh, axis=-1, keepdims=True)
            oc = oh - mu
            rstd = lax.rsqrt(jnp.mean(oc * oc, axis=-1, keepdims=True) + LN_EPS)
            gt = g_ref[:, cols]
            y_ref[:, cols] = (gt * _sigmoid(gt) * (oc * rstd) * gn_ref[:, cols]).astype(y_ref.dtype)

    row = pl.BlockSpec((tm, W), lambda i: (i, 0))
    return pl.pallas_call(
        body, out_shape=jax.ShapeDtypeStruct((S, W), MXU_DTYPE), grid=(S // tm,),
        in_specs=[row, pl.BlockSpec((tm, W), lambda i: (i, gate_col)), pl.BlockSpec((1, W), lambda i: (0, 0))],
        out_specs=row, compiler_params=_params(1), name=name)(o, proj, gn)


def ret_gate_bwd(dy, o, proj, gate_col, gn, heads, name):
    S, W = o.shape
    dv = W // heads
    tm = _tile(S, ROW_TILE // 2, 8)

    def body(dy_ref, o_ref, g_ref, gn_ref, do_ref, dgt_ref, dgn_ref):
        @pl.when(pl.program_id(0) == 0)
        def _():
            dgn_ref[...] = jnp.zeros_like(dgn_ref)

        for hi in range(heads):
            cols = slice(hi * dv, (hi + 1) * dv)
            oh = o_ref[:, cols]
            mu = jnp.mean(oh, axis=-1, keepdims=True)
            oc = oh - mu
            rstd = lax.rsqrt(jnp.mean(oc * oc, axis=-1, keepdims=True) + LN_EPS)
            xhat = oc * rstd
            gt = g_ref[:, cols]
            sg = _sigmoid(gt)
            gain = gn_ref[:, cols]
            dy = dy_ref[:, cols]
            dgt_ref[:, cols] = (dy * xhat * gain * (sg * (1.0 + gt * (1.0 - sg)))).astype(dgt_ref.dtype)
            dn = dy * (gt * sg)
            dgn_ref[:, cols] += jnp.sum(dn * xhat, axis=0, keepdims=True)
            dxh = dn * gain
            m1 = jnp.mean(dxh, axis=-1, keepdims=True)
            m2 = jnp.mean(dxh * xhat, axis=-1, keepdims=True)
            do_ref[:, cols] = (rstd * (dxh - m1 - xhat * m2)).astype(do_ref.dtype)

    row = pl.BlockSpec((tm, W), lambda i: (i, 0))
    vec = pl.BlockSpec((1, W), lambda i: (0, 0))
    return pl.pallas_call(
        body,
        out_shape=(jax.ShapeDtypeStruct((S, W), MXU_DTYPE), jax.ShapeDtypeStruct((S, W), MXU_DTYPE),
                   jax.ShapeDtypeStruct((1, W), F32)),
        grid=(S // tm,), in_specs=[row, row, pl.BlockSpec((tm, W), lambda i: (i, gate_col)), vec],
        out_specs=(row, row, vec), compiler_params=_params(1), name=name)(dy, o, proj, gn)


def _shift_down(x, s):
    rows = lax.broadcasted_iota(jnp.int32, x.shape, 0)
    return jnp.where(rows >= s, pltpu.roll(x, s, 0), 0.0)


def _shift_up(x, s):
    n = x.shape[0]
    rows = lax.broadcasted_iota(jnp.int32, x.shape, 0)
    return jnp.where(rows < n - s, pltpu.roll(x, n - s, 0), 0.0)


def _conv3(x, w_ref, b_ref):
    return (w_ref[2:3, :] * x + w_ref[1:2, :] * _shift_down(x, 1) + w_ref[0:1, :] * _shift_down(x, 2)
            + b_ref[...])


def conv_glu_fwd(hup, w, b, link, name):
    S, W2 = hup.shape
    F = W2 // 2
    tc = _tile(F, LANES)
    nb = F // tc

    def body(g_ref, v_ref, wg_ref, wv_ref, bg_ref, bv_ref, u_ref):
        cg = _conv3(g_ref[...], wg_ref, bg_ref)
        cv = _conv3(v_ref[...], wv_ref, bv_ref)
        u_ref[...] = (cg * _sigmoid(cg) * cv).astype(u_ref.dtype)

    def col(rows, off):
        return pl.BlockSpec((rows, tc), lambda j: (0, j + off))

    (u,) = _call(
        body, grid=(nb,), in_specs=[col(S, 0), col(S, nb), col(3, 0), col(3, nb), col(1, 0), col(1, nb)],
        out_specs=(col(S, 0),), out_shape=(jax.ShapeDtypeStruct((S, F), MXU_DTYPE),),
        operands=[hup, hup, w, w, b, b], name=name, chunks=link.take(ROWWISE_US["conv"][0]), bufs=link.bufs)
    return u


def conv_glu_bwd(du, hup, w, b, link, name):
    S, W2 = hup.shape
    F = W2 // 2
    tc = _tile(F, LANES)
    nb = F // tc

    def back(dc, x, w_ref, dh_ref, dw_ref, db_ref):
        dw_ref[2:3, :] = jnp.sum(dc * x, axis=0, keepdims=True)
        dw_ref[1:2, :] = jnp.sum(dc * _shift_down(x, 1), axis=0, keepdims=True)
        dw_ref[0:1, :] = jnp.sum(dc * _shift_down(x, 2), axis=0, keepdims=True)
        db_ref[...] = jnp.sum(dc, axis=0, keepdims=True)
        dh_ref[...] = (w_ref[2:3, :] * dc + w_ref[1:2, :] * _shift_up(dc, 1)
                       + w_ref[0:1, :] * _shift_up(dc, 2)).astype(dh_ref.dtype)

    def body(du_ref, g_ref, v_ref, wg_ref, wv_ref, bg_ref, bv_ref,
             dhg_ref, dhv_ref, dwg_ref, dwv_ref, dbg_ref, dbv_ref):
        xg, xv = g_ref[...], v_ref[...]
        cg = _conv3(xg, wg_ref, bg_ref)
        cv = _conv3(xv, wv_ref, bv_ref)
        sg = _sigmoid(cg)
        du = du_ref[...]
        back(du * cv * (sg * (1.0 + cg * (1.0 - sg))), xg, wg_ref, dhg_ref, dwg_ref, dbg_ref)
        back(du * (cg * sg), xv, wv_ref, dhv_ref, dwv_ref, dbv_ref)

    def col(rows, off):
        return pl.BlockSpec((rows, tc), lambda j: (0, j + off))

    return _call(
        body, grid=(nb,),
        in_specs=[col(S, 0), col(S, 0), col(S, nb), col(3, 0), col(3, nb), col(1, 0), col(1, nb)],
        out_specs=(col(S, 0), col(S, 0), col(3, 0), col(3, 0), col(1, 0), col(1, 0)),
        out_shape=(jax.ShapeDtypeStruct((S, F), MXU_DTYPE), jax.ShapeDtypeStruct((S, F), MXU_DTYPE),
                   jax.ShapeDtypeStruct((3, F), F32), jax.ShapeDtypeStruct((3, F), F32),
                   jax.ShapeDtypeStruct((1, F), F32), jax.ShapeDtypeStruct((1, F), F32)),
        operands=[du, hup, hup, w, w, b, b], name=name, chunks=link.take(ROWWISE_US["conv"][1]), bufs=link.bufs)


def loss_head(y, target, name):
    S, D = y.shape
    tm = _tile(S, ROW_TILE, 8)

    def body(y_ref, t_ref, dy_ref, loss_ref):
        @pl.when(pl.program_id(0) == 0)
        def _():
            loss_ref[...] = jnp.zeros_like(loss_ref)

        e = y_ref[...] - t_ref[...]
        dy_ref[...] = e * (1.0 / D)
        part = jnp.sum(jnp.sum(e * e, axis=-1, keepdims=True), axis=0, keepdims=True) * (0.5 / D)
        loss_ref[...] += jnp.broadcast_to(part, loss_ref.shape)

    row = pl.BlockSpec((tm, D), lambda i: (i, 0))
    return pl.pallas_call(
        body, out_shape=(jax.ShapeDtypeStruct((S, D), F32), jax.ShapeDtypeStruct((8, LANES), F32)),
        grid=(S // tm,), in_specs=[row, row], out_specs=(row, pl.BlockSpec((8, LANES), lambda i: (0, 0))),
        compiler_params=_params(1), name=name)(y, target)


def adam_update(parts, w, m, v, name):
    n_layers, n_parts, R, C = parts.shape
    tr = _tile(R, max(8, (1 << 19) // C), 8)
    c1 = 1.0 - ADAM_B1 ** ADAM_STEP
    c2 = 1.0 - ADAM_B2 ** ADAM_STEP

    def body(p_ref, w_ref, m_ref, v_ref, g_ref, d_ref, nm_ref, nv_ref):
        g = p_ref[0].astype(F32)
        for pi in range(1, n_parts):
            g = g + p_ref[pi].astype(F32)
        nm = ADAM_B1 * m_ref[...] + (1.0 - ADAM_B1) * g
        nv = ADAM_B2 * v_ref[...] + (1.0 - ADAM_B2) * (g * g)
        g_ref[...] = g
        nm_ref[...] = nm
        nv_ref[...] = nv
        d_ref[...] = -ADAM_LR * ((nm / c1) / (jnp.sqrt(nv / c2) + ADAM_EPS) + ADAM_WD * w_ref[...])

    row = pl.BlockSpec((None, tr, C), lambda l, i: (l, i, 0))
    out = jax.ShapeDtypeStruct((n_layers, R, C), F32)
    return _call(
        body, grid=(n_layers, R // tr),
        in_specs=[pl.BlockSpec((None, n_parts, tr, C), lambda l, i: (l, 0, i, 0)), row, row, row],
        out_specs=(row, row, row, row), out_shape=(out, out, out, out), operands=[parts, w, m, v], name=name)


def pair_sum(own, got, core, name):
    _, R, C = own.shape
    tr = _tile(R, max(16, (1 << 19) // C), 16)

    def body(core_ref, a_ref, b_ref, o_ref):
        o_ref[...] = (a_ref[...].astype(F32) + b_ref[...].astype(F32)).astype(o_ref.dtype)

    grid_spec = pltpu.PrefetchScalarGridSpec(
        num_scalar_prefetch=1, grid=(N_CHIP, R // tr),
        in_specs=[pl.BlockSpec((None, tr, C), lambda j, i, core_ref: (2 * j + core_ref[0], i, 0)),
                  pl.BlockSpec((None, tr, C), lambda j, i, core_ref: (j, i, 0))],
        out_specs=pl.BlockSpec((None, tr, C), lambda j, i, core_ref: (j, i, 0)))
    return pl.pallas_call(
        body, out_shape=jax.ShapeDtypeStruct((N_CHIP, R, C), own.dtype), grid_spec=grid_spec,
        compiler_params=_params(2), name=name)(core, own, got)


def gather_small(block, name):
    m_per, n = block.shape

    def body(x_ref, out_ref, send_sems, recv_sems, local_sem):
        x, y, c = _place()
        me, sibling = (x, y, c), (x, y, 1 - c)
        chips = _other_chips(x, y)

        def rows(px, py, pc):
            return out_ref.at[pl.ds((4 * px + 2 * py + pc) * m_per, m_per), :]

        def copy(k, blk, to, src=None):
            return pltpu.make_async_remote_copy(
                src_ref=rows(*blk) if src is None else src, dst_ref=rows(*blk),
                send_sem=send_sems.at[k], recv_sem=recv_sems.at[k], device_id=to, device_id_type=MESH)

        mine = pltpu.make_async_copy(x_ref, rows(*me), local_sem)
        mine.start()
        first = [copy(0, me, sibling, src=x_ref)]
        first += [copy(1 + j, me, (*chip, c), src=x_ref) for j, chip in enumerate(chips)]
        for cp in first:
            cp.start()
        passed = [copy(4 + j, (*chip, c), sibling) for j, chip in enumerate(chips)]
        for j, chip in enumerate(chips):
            copy(1 + j, (*chip, c), me).wait_recv()
            passed[j].start()
        copy(0, sibling, me).wait_recv()
        for j, chip in enumerate(chips):
            copy(4 + j, (*chip, 1 - c), me).wait_recv()
        for cp in first + passed:
            cp.wait_send()
        mine.wait()

    return pl.pallas_call(
        body, out_shape=jax.ShapeDtypeStruct((N_DEV * m_per, n), block.dtype),
        in_specs=[pl.BlockSpec(memory_space=pltpu.VMEM)], out_specs=pl.BlockSpec(memory_space=pltpu.VMEM),
        scratch_shapes=[pltpu.SemaphoreType.DMA((7,)), pltpu.SemaphoreType.DMA((7,)), pltpu.SemaphoreType.DMA],
        compiler_params=pltpu.CompilerParams(vmem_limit_bytes=VMEM_LIMIT_BYTES), name=name)(block)


def _rope_tables(positions, d):
    inv_freq = ROPE_BASE ** (-jnp.arange(0, d, 2, dtype=F32) / d)
    ang = positions.astype(F32)[:, None] * inv_freq
    return jnp.cos(ang), jnp.sin(ang)


def _mla_pad(nope, rope):
    S, H, _ = nope.shape
    half = MLA_ROPE // 2
    z = jnp.zeros((S, H, LANES // 2 - half), nope.dtype)
    return jnp.concatenate([nope, rope[..., :half], z, rope[..., half:], z], axis=2).reshape(S, H * MLA_PAD)


def _mla_unpad(x, H):
    S = x.shape[0]
    half = MLA_ROPE // 2
    x3 = x.reshape(S, H, MLA_PAD)
    rope = jnp.concatenate([x3[..., MLA_NOPE:MLA_NOPE + half],
                            x3[..., MLA_NOPE + LANES // 2:MLA_NOPE + LANES // 2 + half]], axis=2)
    return x3[..., :MLA_NOPE], rope


def kernel(x, mem, positions, ret_w_in, ret_gn_g, ret_w_out, mla_w_in, mla_q_norm_g, mla_w_uq, mla_kv_norm_g, mla_w_ukv, mla_w_out, xa_w_q, xa_w_kv, xa_w_out, ffn_w_up, ffn_conv_w, ffn_conv_b, ffn_w_down, ln_mix_g, ln_mix_b, ln_mem_g, ln_mem_b, ln_ffn_g, ln_ffn_b, loss_target, m_ret_w_in, m_ret_gn_g, m_ret_w_out, m_mla_w_in, m_mla_q_norm_g, m_mla_w_uq, m_mla_kv_norm_g, m_mla_w_ukv, m_mla_w_out, m_xa_w_q, m_xa_w_kv, m_xa_w_out, m_ffn_w_up, m_ffn_conv_w, m_ffn_conv_b, m_ffn_w_down, m_ln_mix_g, m_ln_mix_b, m_ln_mem_g, m_ln_mem_b, m_ln_ffn_g, m_ln_ffn_b, v_ret_w_in, v_ret_gn_g, v_ret_w_out, v_mla_w_in, v_mla_q_norm_g, v_mla_w_uq, v_mla_kv_norm_g, v_mla_w_ukv, v_mla_w_out, v_xa_w_q, v_xa_w_kv, v_xa_w_out, v_ffn_w_up, v_ffn_conv_w, v_ffn_conv_b, v_ffn_w_down, v_ln_mix_g, v_ln_mix_b, v_ln_mem_g, v_ln_mem_b, v_ln_ffn_g, v_ln_ffn_b):
    weights = dict(ret_w_in=ret_w_in, ret_gn_g=ret_gn_g, ret_w_out=ret_w_out, mla_w_in=mla_w_in,
                   mla_q_norm_g=mla_q_norm_g, mla_w_uq=mla_w_uq, mla_kv_norm_g=mla_kv_norm_g,
                   mla_w_ukv=mla_w_ukv, mla_w_out=mla_w_out, xa_w_q=xa_w_q, xa_w_kv=xa_w_kv, xa_w_out=xa_w_out,
                   ffn_w_up=ffn_w_up, ffn_conv_w=ffn_conv_w, ffn_conv_b=ffn_conv_b, ffn_w_down=ffn_w_down,
                   ln_mix_g=ln_mix_g, ln_mix_b=ln_mix_b, ln_mem_g=ln_mem_g, ln_mem_b=ln_mem_b,
                   ln_ffn_g=ln_ffn_g, ln_ffn_b=ln_ffn_b)
    mom_m = dict(ret_w_in=m_ret_w_in, ret_gn_g=m_ret_gn_g, ret_w_out=m_ret_w_out, mla_w_in=m_mla_w_in,
                 mla_q_norm_g=m_mla_q_norm_g, mla_w_uq=m_mla_w_uq, mla_kv_norm_g=m_mla_kv_norm_g,
                 mla_w_ukv=m_mla_w_ukv, mla_w_out=m_mla_w_out, xa_w_q=m_xa_w_q, xa_w_kv=m_xa_w_kv,
                 xa_w_out=m_xa_w_out, ffn_w_up=m_ffn_w_up, ffn_conv_w=m_ffn_conv_w, ffn_conv_b=m_ffn_conv_b,
                 ffn_w_down=m_ffn_w_down, ln_mix_g=m_ln_mix_g, ln_mix_b=m_ln_mix_b, ln_mem_g=m_ln_mem_g,
                 ln_mem_b=m_ln_mem_b, ln_ffn_g=m_ln_ffn_g, ln_ffn_b=m_ln_ffn_b)
    mom_v = dict(ret_w_in=v_ret_w_in, ret_gn_g=v_ret_gn_g, ret_w_out=v_ret_w_out, mla_w_in=v_mla_w_in,
                 mla_q_norm_g=v_mla_q_norm_g, mla_w_uq=v_mla_w_uq, mla_kv_norm_g=v_mla_kv_norm_g,
                 mla_w_ukv=v_mla_w_ukv, mla_w_out=v_mla_w_out, xa_w_q=v_xa_w_q, xa_w_kv=v_xa_w_kv,
                 xa_w_out=v_xa_w_out, ffn_w_up=v_ffn_w_up, ffn_conv_w=v_ffn_conv_w, ffn_conv_b=v_ffn_conv_b,
                 ffn_w_down=v_ffn_w_down, ln_mix_g=v_ln_mix_g, ln_mix_b=v_ln_mix_b, ln_mem_g=v_ln_mem_g,
                 ln_mem_b=v_ln_mem_b, ln_ffn_g=v_ln_ffn_g, ln_ffn_b=v_ln_ffn_b)
    order = list(weights)
    BIG = dict(ret_w_in=2, ret_w_out=1, mla_w_in=1, mla_w_uq=2, mla_w_ukv=2, mla_w_out=1,
               xa_w_q=1, xa_w_kv=2, xa_w_out=1, ffn_w_up=2, ffn_w_down=1)
    SMALL_CUT = ("ffn_conv_w", "mla_q_norm_g", "mla_kv_norm_g")
    REPLICATED = ("ret_gn_g", "ffn_conv_b", "ln_mix_g", "ln_mix_b", "ln_mem_g", "ln_mem_b", "ln_ffn_g", "ln_ffn_b")

    x = x[0]
    mem = mem[0]
    positions = positions[0]
    target = loss_target[0]
    S, D = x.shape
    depth = ln_mix_g.shape[0]
    alpha = (2 * depth) ** 0.25
    ret_dk = D // RET_HEADS
    ret_dv = 2 * D // RET_HEADS
    ret_qkw = RET_HEADS * ret_dk
    ret_vw = RET_HEADS * ret_dv
    xa_d = D // XA_HEADS
    assert 2 * ret_qkw == ret_vw and MLA_NOPE == LANES and MLA_V == LANES and MLA_ROPE == LANES // 2
    assert MLA_Q_RANK == MLA_KV_RANK and CHUNK & (CHUNK - 1) == 0
    core = lax.axis_index("c").astype(jnp.int32).reshape(1)
    dev = 4 * lax.axis_index("x") + 2 * lax.axis_index("y") + lax.axis_index("c")

    gather, scatter = _Link(GATHER_BYTES_PER_US), _Link(SCATTER_BYTES_PER_US)
    shard_b = {n: weights[n].astype(PAY_DTYPE) for n in BIG}
    staged = set()

    def units(layer):
        mixer = ("ret_w_in", "ret_w_out") if layer % 2 == 0 else ("mla_w_in", "mla_w_uq", "mla_w_ukv", "mla_w_out")
        return [(n, layer // 2) for n in mixer] + [(n, layer) for n in
                                                    ("xa_w_q", "xa_w_kv", "xa_w_out", "ffn_w_up", "ffn_w_down")]

    def geometry(n):
        _, k, nn = weights[n].shape
        return BIG[n], k, nn

    def after_first_level(key, full, axis, k, nn):
        def to_sibling():
            chunk = _Chunk("gather2", None, key, full, 0, 0, k, nn, axis=axis, k=k, n=nn)
            gather.side.append((chunk, lambda: staged.add(key)))

        def relay():
            gather.push("relay", None, key, full, 0, k // 2, nn, done=to_sibling, front=True,
                        axis=axis, k=k, n=nn, half=k // 2)
        return relay

    for layer in range(depth):
        for n, j in units(layer):
            axis, k, nn = geometry(n)
            full = (N_DEV * k, nn) if axis == 1 else (k, N_DEV * nn)
            gather.push("gather", shard_b[n], (n, j), full, j, k, nn,
                        done=after_first_level((n, j), full, axis, k, nn), axis=axis, k=k, n=nn)

    def weight(n, j):
        key = (n, j)
        while key not in staged:
            gather.flush(key, name="gather_rest")
            gather.settle()
            if not any(ch.key == key for ch in gather.queue):
                break
        while key not in staged:
            assert gather.side
            gather.carry(LEVEL2_US, name="gather_level2")
            gather.settle()
        return gather.bufs[key]

    def pack(arrays):
        flat = jnp.concatenate([a.reshape(-1) for a in arrays])
        rows = -(-flat.shape[0] // (8 * LANES)) * 8
        return jnp.pad(flat, (0, rows * LANES - flat.shape[0])).reshape(rows, LANES)

    def unpack(flat, like):
        out, at = [], 0
        for a in like:
            out.append(flat[at:at + a.size].reshape(a.shape))
            at += a.size
        return out

    small_local = [weights[n] for n in SMALL_CUT]
    blk = pack(small_local)
    allsmall = gather_small(blk, name="gather_small_weights").reshape(N_DEV, -1)
    per_dev = [unpack(allsmall[d], small_local) for d in range(N_DEV)]
    conv_w_full, qg_full, kvg_full = [jnp.concatenate([per_dev[d][i] for d in range(N_DEV)], axis=-1)
                                      for i in range(len(SMALL_CUT))]

    rcos, rsin = _rope_tables(positions, ret_dk)
    mcos, msin = _rope_tables(positions, MLA_ROPE)
    zq = jnp.zeros((S, LANES // 2 - MLA_ROPE // 2), F32)
    mla_ta = jnp.concatenate([mcos, zq, mcos, zq], axis=1)
    mla_tb = jnp.concatenate([-msin, zq, msin, zq], axis=1)
    log_gamma = jnp.log(1.0 - 2.0 ** (-5.0 - jnp.arange(RET_HEADS, dtype=F32)))
    log_gamma = jnp.broadcast_to(log_gamma[:, None, None], (RET_HEADS, 8, LANES))
    mla_scale = (MLA_NOPE + MLA_ROPE) ** -0.5
    xa_scale = xa_d ** -0.5
    mem_b = mem.astype(MXU_DTYPE)

    def vec(a, l):
        return a[l][None, :]

    saved = []
    h = x
    hb = x.astype(MXU_DTYPE)
    for layer in range(depth):
        j = layer // 2
        sv = {}
        sv["hb_mix"] = hb
        if layer % 2 == 0:
            proj = mm(hb, weight("ret_w_in", j), link=gather, name="ret_in")
            qk = ret_rope_fwd(proj, rcos, rsin, 2 * RET_HEADS, ret_dk // 2, RET_HEADS, ret_dk ** -0.5, name="ret_rope")
            (o,) = attn_fwd((qk, ret_dk, 0, 1), (qk, ret_dk, RET_HEADS, 1), (proj, ret_dv, RET_HEADS, 1),
                            heads=RET_HEADS, softmax=False, masked=True, log_gamma=log_gamma, link=gather,
                            est_us=ATTN_US["ret"][0], name="ret_attn")
            yb = ret_gate_fwd(o, proj, 2, vec(ret_gn_g, j), RET_HEADS, name="ret_gate")
            mix = mm(yb, weight("ret_w_out", j), link=gather, name="ret_out")
            sv.update(proj=proj, qk=qk, o=o, yb=yb)
        else:
            proj = mm(hb, weight("mla_w_in", j), link=gather, name="mla_in")
            cq = rms_fwd(proj, 0, MLA_Q_RANK, vec(qg_full, j), name="mla_q_norm")
            ckv = rms_fwd(proj, 1, MLA_KV_RANK, vec(kvg_full, j), name="mla_kv_norm")
            qf = mm(cq, weight("mla_w_uq", j), link=gather, name="mla_uq")
            kvf = mm(ckv, weight("mla_w_ukv", j), link=gather, name="mla_ukv")
            q3 = qf.reshape(S, MLA_HEADS, MLA_NOPE + MLA_ROPE)
            kv3 = kvf.reshape(S, MLA_HEADS, MLA_NOPE + MLA_V)
            k_rope = jnp.broadcast_to(proj[:, None, MLA_Q_RANK + MLA_KV_RANK:], (S, MLA_HEADS, MLA_ROPE))
            q_pad = _mla_pad(q3[..., :MLA_NOPE], q3[..., MLA_NOPE:])
            k_pad = _mla_pad(kv3[..., :MLA_NOPE], k_rope)
            (qr,) = mla_rope(q_pad, mla_ta, mla_tb, MLA_HEADS, backward=False, head_sum=False,
                             out_dtype=MXU_DTYPE, name="mla_rope_q")
            (kr,) = mla_rope(k_pad, mla_ta, mla_tb, MLA_HEADS, backward=False, head_sum=False,
                             out_dtype=MXU_DTYPE, name="mla_rope_k")
            o, ob, lse = attn_fwd((qr, MLA_PAD, 0, 1), (kr, MLA_PAD, 0, 1), (kvf, MLA_V, 1, 2), heads=MLA_HEADS,
                              softmax=True, masked=True, scale=mla_scale, link=gather, est_us=ATTN_US["mla"][0],
                              name="mla_attn")
            mix = mm(ob, weight("mla_w_out", j), link=gather, name="mla_out")
            sv.update(proj=proj, cq=cq, ckv=ckv, kvf=kvf, qr=qr, kr=kr, o=o, ob=ob, lse=lse)
        h, hb, sv["xhat_mix"], sv["rstd_mix"] = ln_fwd(h, mix, vec(ln_mix_g, layer), vec(ln_mix_b, layer), alpha,
                                                       gather, name="ln_mix")
        sv["hb_mem"] = hb
        q = mm(hb, weight("xa_w_q", layer), link=gather, name="xa_q")
        kvm = mm(mem_b, weight("xa_w_kv", layer), link=gather, name="xa_kv")
        o, ob, lse = attn_fwd((q, xa_d, 0, 1), (kvm, xa_d, 0, 1), (kvm, xa_d, XA_HEADS, 1), heads=XA_HEADS,
                          softmax=True, masked=False, scale=xa_scale, link=gather, est_us=ATTN_US["xa"][0],
                          name="xa_attn")
        mix = mm(ob, weight("xa_w_out", layer), link=gather, name="xa_out")
        sv.update(xa_q=q, xa_kvm=kvm, xa_o=o, xa_ob=ob, xa_lse=lse)
        h, hb, sv["xhat_mem"], sv["rstd_mem"] = ln_fwd(h, mix, vec(ln_mem_g, layer), vec(ln_mem_b, layer), alpha,
                                                       gather, name="ln_mem")
        sv["hb_ffn"] = hb
        hup = mm(hb, weight("ffn_w_up", layer), link=gather, name="ffn_up")
        u = conv_glu_fwd(hup, conv_w_full[layer], vec(ffn_conv_b, layer), gather, name="ffn_conv")
        mix = mm(u, weight("ffn_w_down", layer), link=gather, name="ffn_down")
        sv.update(hup=hup, u=u)
        h, hb, sv["xhat_ffn"], sv["rstd_ffn"] = ln_fwd(h, mix, vec(ln_ffn_g, layer), vec(ln_ffn_b, layer), alpha,
                                                       gather, name="ln_ffn")
        saved.append(sv)

    dh, loss_blk = loss_head(h, target, name="loss_head")
    loss = lax.psum(loss_blk[0, 0], ("x", "y", "c"))

    small = {n: [None] * weights[n].shape[0] for n in REPLICATED + SMALL_CUT}
    W = gather.bufs

    def wgrad(n, a, d, l, tag):
        axis, k, nn = geometry(n)
        if axis == 2:
            slabs = mm(a, d, mode="tn", out_dtype=PAY_DTYPE, slab_width=nn, link=scatter, name=tag)
        else:
            slabs = mm(a, d, mode="tn", out_dtype=PAY_DTYPE, link=scatter, name=tag).reshape(N_DEV, k, nn)
        n_layers = weights[n].shape[0]
        pair = ("pair", n, l)

        def then():
            sums = pair_sum(slabs, scatter.bufs.pop(pair), core, name="pairsum_" + n)
            scatter.push("scatter", sums, n, (n_layers, N_CHIP, k, nn), l, k, nn)

        scatter.side.append((_Chunk("swap", slabs, pair, (N_CHIP, k, nn), 0, 0, k, nn), then))

    for layer in reversed(range(depth)):
        j = layer // 2
        sv = saved[layer]
        dz, dzb, dg, db = ln_bwd(dh, sv["xhat_ffn"], sv["rstd_ffn"], vec(ln_ffn_g, layer), scatter, name="ln_ffn_bwd")
        small["ln_ffn_g"][layer], small["ln_ffn_b"][layer] = dg[0], db[0]
        wgrad("ffn_w_down", sv["u"], dzb, layer, "ffn_down_dw")
        du = mm(dzb, W[("ffn_w_down", layer)], mode="nt", link=scatter, name="ffn_down_dx")
        dhg, dhv, dwg, dwv, dbg, dbv = conv_glu_bwd(du, sv["hup"], conv_w_full[layer], vec(ffn_conv_b, layer),
                                                    scatter, name="ffn_conv_bwd")
        small["ffn_conv_w"][layer] = jnp.concatenate([dwg, dwv], axis=1)
        small["ffn_conv_b"][layer] = jnp.concatenate([dbg, dbv], axis=1)[0]
        dhup = jnp.concatenate([dhg, dhv], axis=1)
        wgrad("ffn_w_up", sv["hb_ffn"], dhup, layer, "ffn_up_dw")
        dh = mm(dhup, W[("ffn_w_up", layer)], mode="nt", add=dz, add_scale=alpha, link=scatter, name="ffn_up_dx")
        dz, dzb, dg, db = ln_bwd(dh, sv["xhat_mem"], sv["rstd_mem"], vec(ln_mem_g, layer), scatter, name="ln_mem_bwd")
        small["ln_mem_g"][layer], small["ln_mem_b"][layer] = dg[0], db[0]
        wgrad("xa_w_out", sv["xa_ob"], dzb, layer, "xa_out_dw")
        do = mm(dzb, W[("xa_w_out", layer)], mode="nt", link=scatter, name="xa_out_dx")
        dq, dk, dv = attn_bwd((sv["xa_q"], xa_d, 0, 1), (sv["xa_kvm"], xa_d, 0, 1), (sv["xa_kvm"], xa_d, XA_HEADS, 1),
                              do, heads=XA_HEADS, softmax=True, masked=False, scale=xa_scale, o=sv["xa_o"],
                              lse=sv["xa_lse"], link=scatter, est_us=ATTN_US["xa"][1], name="xa_attn_bwd")
        dqb = dq.astype(MXU_DTYPE)
        wgrad("xa_w_kv", mem_b, jnp.concatenate([dk, dv], axis=1).astype(MXU_DTYPE), layer, "xa_kv_dw")
        wgrad("xa_w_q", sv["hb_mem"], dqb, layer, "xa_q_dw")
        dh = mm(dqb, W[("xa_w_q", layer)], mode="nt", add=dz, add_scale=alpha, link=scatter, name="xa_q_dx")
        dz, dzb, dg, db = ln_bwd(dh, sv["xhat_mix"], sv["rstd_mix"], vec(ln_mix_g, layer), scatter, name="ln_mix_bwd")
        small["ln_mix_g"][layer], small["ln_mix_b"][layer] = dg[0], db[0]
        if layer % 2 == 0:
            wgrad("ret_w_out", sv["yb"], dzb, j, "ret_out_dw")
            dy = mm(dzb, W[("ret_w_out", j)], mode="nt", link=scatter, name="ret_out_dx")
            do, dgate, dgn = ret_gate_bwd(dy, sv["o"], sv["proj"], 2, vec(ret_gn_g, j), RET_HEADS, name="ret_gate_bwd")
            small["ret_gn_g"][j] = dgn[0]
            dq, dk, dv = attn_bwd((sv["qk"], ret_dk, 0, 1), (sv["qk"], ret_dk, RET_HEADS, 1),
                                  (sv["proj"], ret_dv, RET_HEADS, 1), do, heads=RET_HEADS, softmax=False,
                                  masked=True, log_gamma=log_gamma, link=scatter, est_us=ATTN_US["ret"][1],
                                  name="ret_attn_bwd")
            dqk = ret_rope_bwd(dq, dk, rcos, rsin, ret_dk // 2, ret_dk ** -0.5, name="ret_rope_bwd")
            dproj = jnp.concatenate([dqk, dv.astype(MXU_DTYPE), dgate], axis=1)
            wgrad("ret_w_in", sv["hb_mix"], dproj, j, "ret_in_dw")
            dh = mm(dproj, W[("ret_w_in", j)], mode="nt", add=dz, add_scale=alpha, link=scatter, name="ret_in_dx")
        else:
            wgrad("mla_w_out", sv["ob"], dzb, j, "mla_out_dw")
            do = mm(dzb, W[("mla_w_out", j)], mode="nt", link=scatter, name="mla_out_dx")
            dq, dk, dv = attn_bwd((sv["qr"], MLA_PAD, 0, 1), (sv["kr"], MLA_PAD, 0, 1), (sv["kvf"], MLA_V, 1, 2), do,
                                  heads=MLA_HEADS, softmax=True, masked=True, scale=mla_scale, o=sv["o"],
                                  lse=sv["lse"], link=scatter, est_us=ATTN_US["mla"][1], name="mla_attn_bwd")
            (dq_un,) = mla_rope(dq, mla_ta, mla_tb, MLA_HEADS, backward=True, head_sum=False, out_dtype=F32,
                                name="mla_rope_q_bwd")
            dk_un, dk_rope_sum = mla_rope(dk, mla_ta, mla_tb, MLA_HEADS, backward=True, head_sum=True, out_dtype=F32,
                                          name="mla_rope_k_bwd")
            dq_nope, dq_rope = _mla_unpad(dq_un, MLA_HEADS)
            dqf = jnp.concatenate([dq_nope, dq_rope], axis=2).reshape(S, -1).astype(MXU_DTYPE)
            dk_nope, _ = _mla_unpad(dk_un, MLA_HEADS)
            dkvf = jnp.concatenate([dk_nope, dv.reshape(S, MLA_HEADS, MLA_V)], axis=2).reshape(S, -1).astype(MXU_DTYPE)
            half = MLA_ROPE // 2
            dk_rope = jnp.concatenate([dk_rope_sum[:, :half], dk_rope_sum[:, LANES // 2:LANES // 2 + half]], axis=1)
            wgrad("mla_w_uq", sv["cq"], dqf, j, "mla_uq_dw")
            dcq = mm(dqf, W[("mla_w_uq", j)], mode="nt", link=scatter, name="mla_uq_dx")
            wgrad("mla_w_ukv", sv["ckv"], dkvf, j, "mla_ukv_dw")
            dckv = mm(dkvf, W[("mla_w_ukv", j)], mode="nt", link=scatter, name="mla_ukv_dx")
            dcq_in, dqg = rms_bwd(dcq, sv["proj"], 0, MLA_Q_RANK, vec(qg_full, j), name="mla_q_norm_bwd")
            dckv_in, dkvg = rms_bwd(dckv, sv["proj"], 1, MLA_KV_RANK, vec(kvg_full, j), name="mla_kv_norm_bwd")
            small["mla_q_norm_g"][j], small["mla_kv_norm_g"][j] = dqg[0], dkvg[0]
            dproj = jnp.concatenate([dcq_in, dckv_in, dk_rope], axis=1).astype(MXU_DTYPE)
            wgrad("mla_w_in", sv["hb_mix"], dproj, j, "mla_in_dw")
            dh = mm(dproj, W[("mla_w_in", j)], mode="nt", add=dz, add_scale=alpha, link=scatter, name="mla_in_dx")
    grad_x = dh[None]

    grads, deltas, new_m, new_v = {}, {}, {}, {}
    scatter.ride_out(name="scatter_rest")
    scatter.flush(None, name="scatter_rest")
    assert not scatter.queue and not scatter.side
    for n in BIG:
        grads[n], deltas[n], new_m[n], new_v[n] = adam_update(scatter.bufs[n], weights[n], mom_m[n], mom_v[n],
                                                              name="adam_" + n)

    small_names = list(REPLICATED + SMALL_CUT)
    partial = [jnp.stack(small[n]) for n in small_names]
    allpart = gather_small(pack(partial), name="gather_small_grads")
    rows = allpart.shape[0] // N_DEV
    allpart = allpart.reshape(N_DEV, rows, LANES)

    rep_names = list(REPLICATED)
    rep_w = pack([weights[n] for n in rep_names])
    rep_m = pack([mom_m[n] for n in rep_names])
    rep_v = pack([mom_v[n] for n in rep_names])
    rep_rows = rep_w.shape[0]
    rep_size = sum(weights[n].size for n in rep_names)
    flat_parts = allpart.reshape(N_DEV, rows * LANES)
    rep_parts = jnp.pad(flat_parts[:, :rep_size], ((0, 0), (0, rep_rows * LANES - rep_size)))
    outs = adam_update(rep_parts.reshape(1, N_DEV, rep_rows, LANES), rep_w[None], rep_m[None], rep_v[None],
                       name="adam_replicated")
    for o_, dst in zip(outs, (grads, deltas, new_m, new_v)):
        for n, a in zip(rep_names, unpack(o_.reshape(-1), [weights[n] for n in rep_names])):
            dst[n] = a

    cut_names = list(SMALL_CUT)
    cut_full = [jnp.stack(small[n]) for n in cut_names]
    cut_parts = []
    at = rep_size
    for n, a in zip(cut_names, cut_full):
        whole = flat_parts[:, at:at + a.size].reshape((N_DEV,) + a.shape)
        at += a.size
        width = weights[n].shape[-1]
        start = (0,) * (whole.ndim - 1) + (dev * width,)
        mine = lax.dynamic_slice(whole, start, whole.shape[:-1] + (width,))
        cut_parts.append(mine.reshape(N_DEV, -1))
    cut_parts = jnp.concatenate(cut_parts, axis=1)
    cut_w = pack([weights[n] for n in cut_names])
    cut_m = pack([mom_m[n] for n in cut_names])
    cut_v = pack([mom_v[n] for n in cut_names])
    cut_rows = cut_w.shape[0]
    cut_parts = jnp.pad(cut_parts, ((0, 0), (0, cut_rows * LANES - cut_parts.shape[1])))
    outs = adam_update(cut_parts.reshape(1, N_DEV, cut_rows, LANES), cut_w[None], cut_m[None], cut_v[None],
                       name="adam_small_cut")
    for o_, dst in zip(outs, (grads, deltas, new_m, new_v)):
        for n, a in zip(cut_names, unpack(o_.reshape(-1), [weights[n] for n in cut_names])):
            dst[n] = a

    return (loss, grad_x, *[grads[n] for n in order], *[deltas[n] for n in order],
            *[new_m[n] for n in order], *[new_v[n] for n in order])
```

```python
import functools
import math

import jax
import jax.numpy as jnp
from jax import lax
from jax.experimental import pallas as pl
from jax.experimental.pallas import tpu as pltpu

F32 = jnp.float32
MXU_DTYPE = jnp.bfloat16
PAY_DTYPE = jnp.bfloat16
MESH = pl.DeviceIdType.MESH
N_DEV = 8
N_CHIP = 4

DEPTH = 4
CHUNK = 64
RET_HEADS = 8
MLA_HEADS = 16
MLA_Q_RANK = 512
MLA_KV_RANK = 512
MLA_NOPE = 128
MLA_ROPE = 64
MLA_V = 128
MLA_PAD = 256
XA_HEADS = 4
ROPE_BASE = 10000.0
LN_EPS = 1e-5
RMS_EPS = 1e-6
NEG_INF = -1e30
ADAM_LR = 0.001
ADAM_B1 = 0.9
ADAM_B2 = 0.999
ADAM_EPS = 1e-08
ADAM_WD = 0.01
ADAM_STEP = 10

LANES = 128
VMEM_LIMIT_BYTES = 48 * 1024 * 1024
TILE_M = 1024
TILE_N = 1024
TILE_K = 2048
SLAB_TILE_N = 1536
ATTN_TILE = 512
ATTN_PARTS = 2
ROW_TILE = 256

MXU_FLOPS_PER_US = 7.0e8
GATHER_BYTES_PER_US = 5.5e4
SCATTER_BYTES_PER_US = 1.5e4
CHUNK_BYTES = 1280 * 1024
HOST_CHUNKS = 12
LEVEL2_US = 30.0
SIDE_CHUNKS = 4
ATTN_US = {"ret": (125.0, 160.0), "mla": (320.0, 270.0), "xa": (35.0, 35.0)}
ROWWISE_US = {"ln": (25.0, 22.0), "conv": (65.0, 110.0), "rope": 20.0, "gate": 33.0}


def _tile(n, cap, mult=LANES):
    best = None
    for t in range(mult, min(n, cap) + 1, mult):
        if n % t == 0:
            best = t
    return n if best is None else best


def _params(n_axes):
    return pltpu.CompilerParams(dimension_semantics=("arbitrary",) * n_axes,
                                vmem_limit_bytes=VMEM_LIMIT_BYTES)


def _place():
    return lax.axis_index("x"), lax.axis_index("y"), lax.axis_index("c")


def _other_chips(x, y):
    return [(1 - x, y), (x, 1 - y), (1 - x, 1 - y)]


def _remote(src, dst, send, recv, to):
    return pltpu.make_async_remote_copy(src_ref=src, dst_ref=dst, send_sem=send, recv_sem=recv,
                                        device_id=to, device_id_type=MESH)


class _Chunk:
    def __init__(self, kind, src, key, dst_shape, layer, r0, rc, cols, axis=None, k=None, n=None, half=None):
        self.kind, self.src, self.key, self.dst_shape = kind, src, key, dst_shape
        self.layer, self.r0, self.rc, self.axis, self.k, self.n, self.half = layer, r0, rc, axis, k, n, half
        self.nbytes = rc * cols * jnp.dtype(PAY_DTYPE).itemsize

    def copies(self, src_ref, dst_ref, send, recv, loc, i):
        x, y, c = _place()
        chips = _other_chips(x, y)
        rows = pl.ds(self.r0, self.rc)
        sibling, x_nbr, y_nbr, diagonal = (x, y, 1 - c), (1 - x, y, c), (x, 1 - y, c), (1 - x, 1 - y, c)
        if self.kind == "gather":
            peers = [sibling, x_nbr, y_nbr]
            src = src_ref.at[self.layer, rows, :]

            def slab(px, py, pc):
                idx = 4 * px + 2 * py + pc
                if self.axis == 1:
                    return dst_ref.at[pl.ds(idx * self.k + self.r0, self.rc), :]
                return dst_ref.at[rows, pl.ds(idx * self.n, self.n)]

            sends = [_remote(src, slab(x, y, c), send.at[4 * i + t], recv.at[4 * i + t], p)
                     for t, p in enumerate(peers)]
            recvs = [_remote(src, slab(*p), send.at[4 * i + t], recv.at[4 * i + t], p)
                     for t, p in enumerate(peers)]
            return sends, recvs, pltpu.make_async_copy(src, slab(x, y, c), loc.at[i])
        if self.kind == "relay":
            def piece(p, r):
                idx = 4 * p[0] + 2 * p[1] + p[2]
                if self.axis == 1:
                    return dst_ref.at[pl.ds(idx * self.k + r, self.rc), :]
                return dst_ref.at[pl.ds(r, self.rc), pl.ds(idx * self.n, self.n)]

            top, bottom = self.r0, self.half + self.r0
            sends = [_remote(piece(y_nbr, top), piece(y_nbr, top), send.at[4 * i], recv.at[4 * i], x_nbr),
                     _remote(piece(x_nbr, bottom), piece(x_nbr, bottom), send.at[4 * i + 1], recv.at[4 * i + 1], y_nbr)]
            recvs = [_remote(piece(diagonal, top), piece(diagonal, top), send.at[4 * i], recv.at[4 * i], x_nbr),
                     _remote(piece(diagonal, bottom), piece(diagonal, bottom), send.at[4 * i + 1], recv.at[4 * i + 1],
                             y_nbr)]
            return sends, recvs, None
        if self.kind == "gather2":
            def slab(px, py, pc):
                idx = 4 * px + 2 * py + pc
                if self.axis == 1:
                    return dst_ref.at[pl.ds(idx * self.k, self.k), :]
                return dst_ref.at[:, pl.ds(idx * self.n, self.n)]

            sends = [_remote(slab(px, py, c), slab(px, py, c), send.at[4 * i + t], recv.at[4 * i + t], sibling)
                     for t, (px, py) in enumerate(chips)]
            recvs = [_remote(slab(px, py, 1 - c), slab(px, py, 1 - c), send.at[4 * i + t], recv.at[4 * i + t], sibling)
                     for t, (px, py) in enumerate(chips)]
            return sends, recvs, None
        if self.kind == "swap":
            sends = [_remote(src_ref.at[2 * j + (1 - c)], dst_ref.at[j], send.at[4 * i + j], recv.at[4 * i + j], sibling)
                     for j in range(N_CHIP)]
            recvs = [_remote(src_ref.at[2 * j + c], dst_ref.at[j], send.at[4 * i + j], recv.at[4 * i + j], sibling)
                     for j in range(N_CHIP)]
            return sends, recvs, None
        mine = 2 * x + y
        own = src_ref.at[mine, rows, :]
        sends = [_remote(src_ref.at[2 * px + py, rows, :], dst_ref.at[self.layer, mine, rows, :],
                         send.at[4 * i + t], recv.at[4 * i + t], (px, py, c)) for t, (px, py) in enumerate(chips)]
        recvs = [_remote(own, dst_ref.at[self.layer, 2 * px + py, rows, :],
                         send.at[4 * i + t], recv.at[4 * i + t], (px, py, c)) for t, (px, py) in enumerate(chips)]
        return sends, recvs, pltpu.make_async_copy(own, dst_ref.at[self.layer, mine, rows, :], loc.at[i])


def _call(body, *, grid, in_specs, out_specs, out_shape, operands, scratch=(), name, chunks=(), bufs=None):
    n_in, n_out, n_scr = len(operands), len(out_shape), len(scratch)
    params = _params(len(grid)) if grid else pltpu.CompilerParams(vmem_limit_bytes=VMEM_LIMIT_BYTES)
    if not chunks:
        return pl.pallas_call(
            body, out_shape=tuple(out_shape), grid=grid, in_specs=list(in_specs), out_specs=tuple(out_specs),
            scratch_shapes=list(scratch), compiler_params=params, name=name)(*operands)
    srcs, keys = [], []
    for ch in chunks:
        if ch.src is not None and not any(ch.src is s for s in srcs):
            srcs.append(ch.src)
        if ch.key not in keys:
            keys.append(ch.key)
    shape_of = {ch.key: ch.dst_shape for ch in chunks}
    held = [k for k in keys if bufs.get(k) is not None]
    extra = srcs + [bufs[k] for k in held]
    aliases = {n_in + len(srcs) + i: n_out + keys.index(k) for i, k in enumerate(held)}
    anywhere = pl.BlockSpec(memory_space=pl.ANY)
    n_extra, n_keys, n_ch = len(extra), len(keys), len(chunks)

    def hosted(*refs):
        src_refs = refs[n_in:n_in + len(srcs)]
        o0 = n_in + n_extra
        dst_refs = refs[o0 + n_out:o0 + n_out + n_keys]
        s0 = o0 + n_out + n_keys
        send, recv, loc = refs[s0 + n_scr:]
        sends, recvs, locs = [], [], []
        for i, ch in enumerate(chunks):
            src_ref = None if ch.src is None else src_refs[[ch.src is s for s in srcs].index(True)]
            s_, r_, l_ = ch.copies(src_ref, dst_refs[keys.index(ch.key)], send, recv, loc, i)
            sends += s_
            recvs += r_
            if l_ is not None:
                locs.append(l_)

        def start():
            for cp in locs + sends:
                cp.start()

        def finish():
            for cp in recvs:
                cp.wait_recv()
            for cp in sends:
                cp.wait_send()
            for cp in locs:
                cp.wait()

        if grid:
            first = functools.reduce(jnp.logical_and, [pl.program_id(a) == 0 for a in range(len(grid))])
            last = functools.reduce(jnp.logical_and, [pl.program_id(a) == grid[a] - 1 for a in range(len(grid))])
            pl.when(first)(start)
            body(*refs[:n_in], *refs[o0:o0 + n_out], *refs[s0:s0 + n_scr])
            pl.when(last)(finish)
        else:
            start()
            body(*refs[:n_in], *refs[o0:o0 + n_out], *refs[s0:s0 + n_scr])
            finish()

    res = pl.pallas_call(
        hosted, out_shape=tuple(out_shape) + tuple(jax.ShapeDtypeStruct(shape_of[k], PAY_DTYPE) for k in keys),
        grid=grid, in_specs=list(in_specs) + [anywhere] * n_extra, out_specs=tuple(out_specs) + (anywhere,) * n_keys,
        scratch_shapes=list(scratch) + [pltpu.SemaphoreType.DMA((4 * n_ch,)), pltpu.SemaphoreType.DMA((4 * n_ch,)),
                                        pltpu.SemaphoreType.DMA((n_ch,))],
        input_output_aliases=aliases, compiler_params=params, name=name)(*operands, *extra)
    for k, arr in zip(keys, res[n_out:]):
        bufs[k] = arr
    return res[:n_out]


class _Link:
    def __init__(self, bytes_per_us):
        self.rate = bytes_per_us
        self.queue, self.bufs = [], {}
        self.side = []
        self.after = []
        self.left = {}
        self.on_done = {}

    def push(self, kind, src, key, dst_shape, layer, rows, cols, done=None, front=False, **geometry):
        fits = [r for r in range(16, rows + 1, 16) if rows % r == 0 and r * cols * 2 <= CHUNK_BYTES]
        rc = max(fits) if fits else min(r for r in range(16, rows + 1, 16) if rows % r == 0)
        new = [_Chunk(kind, src, key, dst_shape, layer, r0, rc, cols, **geometry) for r0 in range(0, rows, rc)]
        self.queue = new + self.queue if front else self.queue + new
        self.left[key] = self.left.get(key, 0) + rows // rc
        if done is not None:
            self.on_done[key] = done

    def settle(self):
        todo, self.after = self.after, []
        for f in todo:
            f()

    def _pop(self, count):
        out = []
        for ch in self.queue[:count]:
            self.left[ch.key] -= 1
            if self.left[ch.key] == 0 and ch.key in self.on_done:
                self.after.append(self.on_done.pop(ch.key))
            out.append(ch)
        self.queue = self.queue[count:]
        return out

    def _side(self):
        riders, self.side = self.side[:SIDE_CHUNKS], self.side[SIDE_CHUNKS:]
        self.after += [then for _, then in riders if then is not None]
        return [ch for ch, _ in riders]

    def take(self, est_us):
        self.settle()
        out = self._side()
        budget, count = est_us * self.rate, 0
        while count < min(len(self.queue), HOST_CHUNKS) and self.queue[count].nbytes <= 2 * budget:
            budget -= self.queue[count].nbytes
            count += 1
        return out + self._pop(count)

    def flush(self, key, name):
        self.settle()
        if key is None:
            n = len(self.queue)
        else:
            n = max([i + 1 for i, ch in enumerate(self.queue) if ch.key == key], default=0)
        while n > 0:
            group = self._side() + self._pop(min(n, HOST_CHUNKS))
            n -= min(n, HOST_CHUNKS)
            _call(lambda: None, grid=(), in_specs=[], out_specs=(), out_shape=(), operands=[], name=name,
                  chunks=group, bufs=self.bufs)
            self.settle()

    def carry(self, est_us, name):
        _call(lambda: None, grid=(), in_specs=[], out_specs=(), out_shape=(), operands=[], name=name,
              chunks=self.take(est_us), bufs=self.bufs)

    def ride_out(self, name):
        self.settle()
        while self.side:
            _call(lambda: None, grid=(), in_specs=[], out_specs=(), out_shape=(), operands=[], name=name,
                  chunks=self._side(), bufs=self.bufs)
            self.settle()


def mm(a, b, *, mode="nn", out_dtype=F32, add=None, add_scale=1.0, slab_width=None, link=None, name):
    if mode == "nn":
        (M, K), N = a.shape, b.shape[-1]
    elif mode == "nt":
        (M, K), N = a.shape, b.shape[-2]
    else:
        (K, M), N = a.shape, b.shape[-1]
    tm, tk = _tile(M, TILE_M), _tile(K, TILE_K)
    tn = _tile(N, TILE_N) if slab_width is None else _tile(slab_width, SLAB_TILE_N)
    nk = K // tk
    dims = {"nn": (((1,), (0,)), ((), ())), "nt": (((1,), (1,)), ((), ())),
            "tn": (((0,), (0,)), ((), ()))}[mode]

    a_spec = (pl.BlockSpec((tk, tm), lambda i, j, k: (k, i)) if mode == "tn"
              else pl.BlockSpec((tm, tk), lambda i, j, k: (i, k)))
    b_spec = (pl.BlockSpec((tn, tk), lambda i, j, k: (j, k)) if mode == "nt"
              else pl.BlockSpec((tk, tn), lambda i, j, k: (k, j)))
    in_specs, operands = [a_spec, b_spec], [a, b]
    if add is not None:
        in_specs.append(pl.BlockSpec((tm, tn), lambda i, j, k: (i, j)))
        operands.append(add)
    if slab_width is None:
        out_shape = jax.ShapeDtypeStruct((M, N), out_dtype)
        out_spec = pl.BlockSpec((tm, tn), lambda i, j, k: (i, j))
    else:
        per = slab_width // tn
        out_shape = jax.ShapeDtypeStruct((N // slab_width, M, slab_width), out_dtype)
        out_spec = pl.BlockSpec((None, tm, tn), lambda i, j, k: (j // per, i, j % per))
    n_in = len(operands)

    def body(*refs):
        a_ref, b_ref = refs[0], refs[1]
        add_ref = refs[2] if add is not None else None
        o_ref = refs[n_in]
        k = pl.program_id(2)
        part = lax.dot_general(a_ref[...].astype(MXU_DTYPE), b_ref[...].astype(MXU_DTYPE),
                               dims, preferred_element_type=F32)

        def finish(r):
            if add_ref is not None:
                r = r + add_scale * add_ref[...].astype(F32)
            o_ref[...] = r.astype(o_ref.dtype)

        if nk == 1:
            finish(part)
        else:
            acc_ref = refs[n_in + 1]

            @pl.when(k == 0)
            def _():
                acc_ref[...] = part

            @pl.when(jnp.logical_and(k > 0, k < nk - 1))
            def _():
                acc_ref[...] += part

            @pl.when(k == nk - 1)
            def _():
                finish(acc_ref[...] + part)

    chunks = link.take(2.0 * M * N * K / MXU_FLOPS_PER_US) if link is not None else ()
    (out,) = _call(body, grid=(M // tm, N // tn, nk), in_specs=in_specs, out_specs=(out_spec,), out_shape=(out_shape,),
                   operands=operands, scratch=[pltpu.VMEM((tm, tn), F32)] if nk > 1 else [], name=name,
                   chunks=chunks, bufs=link.bufs if link is not None else None)
    return out


def ln_fwd(h, f, g, b, alpha, link, name):
    S, D = h.shape
    tm = _tile(S, ROW_TILE, 8)

    def body(h_ref, f_ref, g_ref, b_ref, y_ref, yb_ref, xhat_ref, rstd_ref):
        z = alpha * h_ref[...] + f_ref[...]
        mu = jnp.mean(z, axis=-1, keepdims=True)
        zc = z - mu
        var = jnp.mean(zc * zc, axis=-1, keepdims=True)
        rstd = lax.rsqrt(var + LN_EPS)
        xhat = zc * rstd
        y = xhat * g_ref[...] + b_ref[...]
        y_ref[...] = y
        yb_ref[...] = y.astype(yb_ref.dtype)
        xhat_ref[...] = xhat
        rstd_ref[...] = rstd

    row = pl.BlockSpec((tm, D), lambda i: (i, 0))
    vec = pl.BlockSpec((1, D), lambda i: (0, 0))
    return _call(
        body, grid=(S // tm,), in_specs=[row, row, vec, vec],
        out_specs=(row, row, row, pl.BlockSpec((tm, 1), lambda i: (i, 0))),
        out_shape=(jax.ShapeDtypeStruct((S, D), F32), jax.ShapeDtypeStruct((S, D), MXU_DTYPE),
                   jax.ShapeDtypeStruct((S, D), F32), jax.ShapeDtypeStruct((S, 1), F32)),
        operands=[h, f, g, b], name=name, chunks=link.take(ROWWISE_US["ln"][0]), bufs=link.bufs)


def ln_bwd(dy, xhat, rstd, g, link, name):
    S, D = dy.shape
    tm = _tile(S, ROW_TILE, 8)

    def body(dy_ref, xhat_ref, rstd_ref, g_ref, dz_ref, dzb_ref, dg_ref, db_ref):
        @pl.when(pl.program_id(0) == 0)
        def _():
            dg_ref[...] = jnp.zeros_like(dg_ref)
            db_ref[...] = jnp.zeros_like(db_ref)

        dy = dy_ref[...]
        xhat = xhat_ref[...]
        dxh = dy * g_ref[...]
        m1 = jnp.mean(dxh, axis=-1, keepdims=True)
        m2 = jnp.mean(dxh * xhat, axis=-1, keepdims=True)
        dz = rstd_ref[...] * (dxh - m1 - xhat * m2)
        dz_ref[...] = dz
        dzb_ref[...] = dz.astype(dzb_ref.dtype)
        dg_ref[...] += jnp.sum(dy * xhat, axis=0, keepdims=True)
        db_ref[...] += jnp.sum(dy, axis=0, keepdims=True)

    row = pl.BlockSpec((tm, D), lambda i: (i, 0))
    vec = pl.BlockSpec((1, D), lambda i: (0, 0))
    return _call(
        body, grid=(S // tm,), in_specs=[row, row, pl.BlockSpec((tm, 1), lambda i: (i, 0)), vec],
        out_specs=(row, row, vec, vec),
        out_shape=(jax.ShapeDtypeStruct((S, D), F32), jax.ShapeDtypeStruct((S, D), MXU_DTYPE),
                   jax.ShapeDtypeStruct((1, D), F32), jax.ShapeDtypeStruct((1, D), F32)),
        operands=[dy, xhat, rstd, g], name=name, chunks=link.take(ROWWISE_US["ln"][1]), bufs=link.bufs)


def rms_fwd(x, col, width, g, name):
    S = x.shape[0]
    tm = _tile(S, ROW_TILE, 8)

    def body(x_ref, g_ref, y_ref):
        xv = x_ref[...]
        r = lax.rsqrt(jnp.mean(xv * xv, axis=-1, keepdims=True) + RMS_EPS)
        y_ref[...] = (xv * r * g_ref[...]).astype(y_ref.dtype)

    return pl.pallas_call(
        body, out_shape=jax.ShapeDtypeStruct((S, width), MXU_DTYPE), grid=(S // tm,),
        in_specs=[pl.BlockSpec((tm, width), lambda i: (i, col)), pl.BlockSpec((1, width), lambda i: (0, 0))],
        out_specs=pl.BlockSpec((tm, width), lambda i: (i, 0)), compiler_params=_params(1), name=name)(x, g)


def rms_bwd(dy, x, col, width, g, name):
    S = x.shape[0]
    tm = _tile(S, ROW_TILE, 8)

    def body(dy_ref, x_ref, g_ref, dx_ref, dg_ref):
        @pl.when(pl.program_id(0) == 0)
        def _():
            dg_ref[...] = jnp.zeros_like(dg_ref)

        xv = x_ref[...]
        dy = dy_ref[...]
        r = lax.rsqrt(jnp.mean(xv * xv, axis=-1, keepdims=True) + RMS_EPS)
        dxn = dy * g_ref[...]
        m = jnp.mean(dxn * xv, axis=-1, keepdims=True)
        dx_ref[...] = r * (dxn - xv * (r * r * m))
        dg_ref[...] += jnp.sum(dy * xv * r, axis=0, keepdims=True)

    return pl.pallas_call(
        body, out_shape=(jax.ShapeDtypeStruct((S, width), F32), jax.ShapeDtypeStruct((1, width), F32)),
        grid=(S // tm,),
        in_specs=[pl.BlockSpec((tm, width), lambda i: (i, 0)), pl.BlockSpec((tm, width), lambda i: (i, col)),
                  pl.BlockSpec((1, width), lambda i: (0, 0))],
        out_specs=(pl.BlockSpec((tm, width), lambda i: (i, 0)), pl.BlockSpec((1, width), lambda i: (0, 0))),
        compiler_params=_params(1), name=name)(dy, x, g)


def ret_rope_fwd(proj, cos, sin, n_groups, half, k_from, k_scale, link, name):
    S = proj.shape[0]
    W = n_groups * 2 * half
    tm = _tile(S, ROW_TILE, 8)

    def body(x_ref, c_ref, s_ref, o_ref):
        c, s = c_ref[...], s_ref[...]
        for gi in range(n_groups):
            lo = gi * 2 * half
            x1 = x_ref[:, lo:lo + half]
            x2 = x_ref[:, lo + half:lo + 2 * half]
            sc = k_scale if gi >= k_from else 1.0
            o_ref[:, lo:lo + half] = ((x1 * c - x2 * s) * sc).astype(o_ref.dtype)
            o_ref[:, lo + half:lo + 2 * half] = ((x2 * c + x1 * s) * sc).astype(o_ref.dtype)

    tab = pl.BlockSpec((tm, half), lambda i: (i, 0))
    (out,) = _call(
        body, grid=(S // tm,), in_specs=[pl.BlockSpec((tm, W), lambda i: (i, 0)), tab, tab],
        out_specs=(pl.BlockSpec((tm, W), lambda i: (i, 0)),), out_shape=(jax.ShapeDtypeStruct((S, W), MXU_DTYPE),),
        operands=[proj, cos, sin], name=name, chunks=link.take(ROWWISE_US["rope"]), bufs=link.bufs)
    return out


def ret_rope_bwd(dq, dk, cos, sin, half, k_scale, name):
    S, Wq = dq.shape
    n_heads = Wq // (2 * half)
    tm = _tile(S, ROW_TILE, 8)

    def body(dq_ref, dk_ref, c_ref, s_ref, o_ref):
        c, s = c_ref[...], s_ref[...]
        for part, (d_ref, sc) in enumerate(((dq_ref, 1.0), (dk_ref, k_scale))):
            for hi in range(n_heads):
                lo = hi * 2 * half
                d1 = d_ref[:, lo:lo + half]
                d2 = d_ref[:, lo + half:lo + 2 * half]
                base = part * Wq + lo
                o_ref[:, base:base + half] = ((d1 * c + d2 * s) * sc).astype(o_ref.dtype)
                o_ref[:, base + half:base + 2 * half] = ((d2 * c - d1 * s) * sc).astype(o_ref.dtype)

    tab = pl.BlockSpec((tm, half), lambda i: (i, 0))
    row = pl.BlockSpec((tm, Wq), lambda i: (i, 0))
    return pl.pallas_call(
        body, out_shape=jax.ShapeDtypeStruct((S, 2 * Wq), MXU_DTYPE), grid=(S // tm,),
        in_specs=[row, row, tab, tab], out_specs=pl.BlockSpec((tm, 2 * Wq), lambda i: (i, 0)),
        compiler_params=_params(1), name=name)(dq, dk, cos, sin)


def mla_rope(x, ta, tb, n_heads, *, backward, head_sum, out_dtype, link=None, name):
    S = x.shape[0]
    W = n_heads * MLA_PAD
    tm = _tile(S, ROW_TILE, 8)

    def body(x_ref, a_ref, b_ref, o_ref, *rest):
        a, b = a_ref[...], b_ref[...]
        total = jnp.zeros((tm, LANES), F32)
        for hi in range(n_heads):
            lo = hi * MLA_PAD
            o_ref[:, lo:lo + MLA_NOPE] = x_ref[:, lo:lo + MLA_NOPE].astype(o_ref.dtype)
            t = x_ref[:, lo + MLA_NOPE:lo + MLA_PAD].astype(F32)
            if backward:
                r = t * a + pltpu.roll(t * b, LANES // 2, 1)
            else:
                r = t * a + pltpu.roll(t, LANES // 2, 1) * b
            o_ref[:, lo + MLA_NOPE:lo + MLA_PAD] = r.astype(o_ref.dtype)
            total = total + r
        if head_sum:
            rest[0][...] = total

    tab = pl.BlockSpec((tm, LANES), lambda i: (i, 0))
    row = pl.BlockSpec((tm, W), lambda i: (i, 0))
    out_shape = [jax.ShapeDtypeStruct((S, W), out_dtype)]
    out_specs = [row]
    if head_sum:
        out_shape.append(jax.ShapeDtypeStruct((S, LANES), F32))
        out_specs.append(tab)
    chunks = link.take(ROWWISE_US["rope"]) if link is not None else ()
    return _call(body, grid=(S // tm,), in_specs=[row, tab, tab], out_specs=out_specs, out_shape=out_shape,
                 operands=[x, ta, tb], name=name, chunks=chunks, bufs=link.bufs if link is not None else None)


def _visible(qi, kj, tq, tk, r0, rows):
    n = qi * tq + r0 + lax.broadcasted_iota(jnp.int32, (rows, tk), 0)
    m = kj * tk + lax.broadcasted_iota(jnp.int32, (rows, tk), 1)
    shift = CHUNK.bit_length() - 1
    vis = lax.shift_right_logical(m, shift) <= lax.shift_right_logical(n, shift)
    return vis, jnp.abs(n - m).astype(F32)


def attn_fwd(q, k, v, *, heads, softmax, masked, scale=1.0, log_gamma=None, link=None, est_us=0.0, name):
    (qa, dqk, q0, qs), (ka, _, k0, ks), (va, dv, v0, vs) = q, k, v
    Sq, Sk = qa.shape[0], ka.shape[0]
    tq, tk = _tile(Sq, ATTN_TILE, 8), _tile(Sk, ATTN_TILE, 8)
    nq, nk = Sq // tq, Sk // tk
    assert not masked or tq == tk
    sub = tq // ATTN_PARTS if tq % (16 * ATTN_PARTS) == 0 else tq

    def body(*refs):
        if softmax:
            q_ref, k_ref, v_ref, o_ref, ob_ref, lse_ref, m_ref, l_ref, acc_ref = refs
        else:
            q_ref, k_ref, v_ref, lg_ref, o_ref, acc_ref = refs
        qi, kj = pl.program_id(1), pl.program_id(2)

        @pl.when(kj == 0)
        def _():
            acc_ref[...] = jnp.zeros_like(acc_ref)
            if softmax:
                m_ref[...] = jnp.full_like(m_ref, NEG_INF)
                l_ref[...] = jnp.zeros_like(l_ref)

        def step(diagonal):
            kb = k_ref[...].astype(MXU_DTYPE)
            vb = v_ref[...].astype(MXU_DTYPE)
            parts = [pl.ds(r * sub, sub) for r in range(tq // sub)]
            scores = [lax.dot_general(q_ref[rows, :].astype(MXU_DTYPE), kb, (((1,), (1,)), ((), ())),
                                      preferred_element_type=F32) for rows in parts]
            for r, rows in enumerate(parts):
                s = scores[r]
                if diagonal or not softmax:
                    vis, dist = _visible(qi, kj, tq, tk, r * sub, sub)
                if softmax:
                    s = s * scale
                    if diagonal:
                        s = jnp.where(vis, s, NEG_INF)
                    m_old = m_ref[rows, :]
                    m_new = jnp.maximum(m_old, jnp.max(s, axis=-1, keepdims=True))
                    p = jnp.exp(s - m_new)
                    corr = jnp.exp(m_old - m_new)
                    l_ref[rows, :] = corr * l_ref[rows, :] + jnp.sum(p, axis=-1, keepdims=True)
                    acc_ref[rows, :] = (corr * acc_ref[rows, :]
                                        + jnp.dot(p.astype(MXU_DTYPE), vb, preferred_element_type=F32))
                    m_ref[rows, :] = m_new
                else:
                    decay = jnp.exp(lg_ref[0:1, 0:1] * dist)
                    if diagonal:
                        decay = jnp.where(vis, decay, 0.0)
                    acc_ref[rows, :] += jnp.dot((s * decay).astype(MXU_DTYPE), vb, preferred_element_type=F32)

        if masked:
            pl.when(kj < qi)(functools.partial(step, False))
            pl.when(kj == qi)(functools.partial(step, True))
        else:
            step(False)

        @pl.when(kj == nk - 1)
        def _():
            if softmax:
                out = acc_ref[...] / l_ref[...]
                o_ref[...] = out
                ob_ref[...] = out.astype(ob_ref.dtype)
                lse_ref[...] = m_ref[...] + jnp.log(l_ref[...])
            else:
                o_ref[...] = acc_ref[...].astype(o_ref.dtype)

    kcap = (lambda qi, kj: jnp.minimum(kj, qi)) if masked else (lambda qi, kj: kj)
    in_specs = [pl.BlockSpec((tq, dqk), lambda h, qi, kj: (qi, q0 + qs * h)),
                pl.BlockSpec((tk, dqk), lambda h, qi, kj: (kcap(qi, kj), k0 + ks * h)),
                pl.BlockSpec((tk, dv), lambda h, qi, kj: (kcap(qi, kj), v0 + vs * h))]
    operands = [qa, ka, va]
    out_shape = [jax.ShapeDtypeStruct((Sq, heads * dv), F32)]
    out_specs = [pl.BlockSpec((tq, dv), lambda h, qi, kj: (qi, h))]
    scratch = []
    if softmax:
        out_shape += [jax.ShapeDtypeStruct((Sq, heads * dv), MXU_DTYPE), jax.ShapeDtypeStruct((heads, Sq, 1), F32)]
        out_specs += [pl.BlockSpec((tq, dv), lambda h, qi, kj: (qi, h)),
                      pl.BlockSpec((None, tq, 1), lambda h, qi, kj: (h, qi, 0))]
        scratch += [pltpu.VMEM((tq, 1), F32), pltpu.VMEM((tq, 1), F32)]
    else:
        in_specs.append(pl.BlockSpec((None, 8, LANES), lambda h, qi, kj: (h, 0, 0)))
        operands.append(log_gamma)
    scratch.append(pltpu.VMEM((tq, dv), F32))
    chunks = link.take(est_us) if link is not None else ()
    return _call(body, grid=(heads, nq, nk), in_specs=in_specs, out_specs=out_specs, out_shape=out_shape,
                 operands=operands, scratch=scratch, name=name, chunks=chunks,
                 bufs=link.bufs if link is not None else None)


def attn_bwd(q, k, v, do, *, heads, softmax, masked, scale=1.0, log_gamma=None, o=None, lse=None,
             link=None, est_us=0.0, name):
    (qa, dqk, q0, qs), (ka, _, k0, ks), (va, dv, v0, vs) = q, k, v
    Sq, Sk = qa.shape[0], ka.shape[0]
    tq, tk = _tile(Sq, ATTN_TILE, 8), _tile(Sk, ATTN_TILE, 8)
    nq, nk = Sq // tq, Sk // tk
    assert not masked or tq == tk
    sub = tq // ATTN_PARTS if tq % (16 * ATTN_PARTS) == 0 else tq
    contract0 = (((0,), (0,)), ((), ()))

    def body(*refs):
        if softmax:
            q_ref, k_ref, v_ref, do_ref, o_ref, lse_ref, dq_ref, dk_ref, dv_ref = refs
        else:
            q_ref, k_ref, v_ref, do_ref, lg_ref, dq_ref, dk_ref, dv_ref = refs
        kj, qi = pl.program_id(1), pl.program_id(2)

        @pl.when(jnp.logical_and(kj == 0, qi == 0))
        def _():
            dq_ref[...] = jnp.zeros_like(dq_ref)

        @pl.when(qi == 0)
        def _():
            dk_ref[...] = jnp.zeros_like(dk_ref)
            dv_ref[...] = jnp.zeros_like(dv_ref)

        def step(diagonal):
            kb = k_ref[...].astype(MXU_DTYPE)
            vb = v_ref[...].astype(MXU_DTYPE)
            parts = [pl.ds(r * sub, sub) for r in range(tq // sub)]
            nt = (((1,), (1,)), ((), ()))
            qbs = [q_ref[rows, :].astype(MXU_DTYPE) for rows in parts]
            dobs = [do_ref[rows, :].astype(MXU_DTYPE) for rows in parts]
            scores = [lax.dot_general(qb, kb, nt, preferred_element_type=F32) for qb in qbs]
            dps = [lax.dot_general(dob, vb, nt, preferred_element_type=F32) for dob in dobs]
            dv_new, dk_new = dv_ref[...], dk_ref[...]
            for r, rows in enumerate(parts):
                s, dp, qb, dob = scores[r], dps[r], qbs[r], dobs[r]
                if diagonal or not softmax:
                    vis, dist = _visible(qi, kj, tq, tk, r * sub, sub)
                if softmax:
                    s = s * scale
                    if diagonal:
                        s = jnp.where(vis, s, NEG_INF)
                    p = jnp.exp(s - lse_ref[rows, :])
                    delta = jnp.sum(do_ref[rows, :].astype(F32) * o_ref[rows, :], axis=-1, keepdims=True)
                    ds = p * (dp - delta) * scale
                else:
                    decay = jnp.exp(lg_ref[0:1, 0:1] * dist)
                    if diagonal:
                        decay = jnp.where(vis, decay, 0.0)
                    p = s * decay
                    ds = dp * decay
                pb = p.astype(MXU_DTYPE)
                dsb = ds.astype(MXU_DTYPE)
                dv_new = dv_new + lax.dot_general(pb, dob, contract0, preferred_element_type=F32)
                dk_new = dk_new + lax.dot_general(dsb, qb, contract0, preferred_element_type=F32)
                out_rows = pl.ds(pl.multiple_of(qi * tq + r * sub, sub), sub)
                dq_ref[out_rows, :] += jnp.dot(dsb, kb, preferred_element_type=F32)
            dv_ref[...] = dv_new
            dk_ref[...] = dk_new

        if masked:
            pl.when(qi > kj)(functools.partial(step, False))
            pl.when(qi == kj)(functools.partial(step, True))
        else:
            step(False)

    qcap = (lambda kj, qi: jnp.maximum(qi, kj)) if masked else (lambda kj, qi: qi)
    in_specs = [pl.BlockSpec((tq, dqk), lambda h, kj, qi: (qcap(kj, qi), q0 + qs * h)),
                pl.BlockSpec((tk, dqk), lambda h, kj, qi: (kj, k0 + ks * h)),
                pl.BlockSpec((tk, dv), lambda h, kj, qi: (kj, v0 + vs * h)),
                pl.BlockSpec((tq, dv), lambda h, kj, qi: (qcap(kj, qi), h))]
    operands = [qa, ka, va, do]
    if softmax:
        in_specs += [pl.BlockSpec((tq, dv), lambda h, kj, qi: (qcap(kj, qi), h)),
                     pl.BlockSpec((None, tq, 1), lambda h, kj, qi: (h, qcap(kj, qi), 0))]
        operands += [o, lse]
    else:
        in_specs.append(pl.BlockSpec((None, 8, LANES), lambda h, kj, qi: (h, 0, 0)))
        operands.append(log_gamma)
    chunks = link.take(est_us) if link is not None else ()
    return _call(
        body, grid=(heads, nk, nq), in_specs=in_specs,
        out_specs=(pl.BlockSpec((Sq, dqk), lambda h, kj, qi: (0, h)),
                   pl.BlockSpec((tk, dqk), lambda h, kj, qi: (kj, h)),
                   pl.BlockSpec((tk, dv), lambda h, kj, qi: (kj, h))),
        out_shape=(jax.ShapeDtypeStruct((Sq, heads * dqk), F32), jax.ShapeDtypeStruct((Sk, heads * dqk), F32),
                   jax.ShapeDtypeStruct((Sk, heads * dv), F32)),
        operands=operands, name=name, chunks=chunks, bufs=link.bufs if link is not None else None)


def _sigmoid(x):
    return 0.5 * jnp.tanh(0.5 * x) + 0.5


def ret_gate_fwd(o, proj, gate_col, gn, heads, link, name):
    S, W = o.shape
    dv = W // heads
    tm = _tile(S, ROW_TILE // 2, 8)

    def body(o_ref, g_ref, gn_ref, y_ref):
        for hi in range(heads):
            cols = slice(hi * dv, (hi + 1) * dv)
            oh = o_ref[:, cols]
            mu = jnp.mean(oh, axis=-1, keepdims=True)
            oc = oh - mu
            rstd = lax.rsqrt(jnp.mean(oc * oc, axis=-1, keepdims=True) + LN_EPS)
            gt = g_ref[:, cols]
            y_ref[:, cols] = (gt * _sigmoid(gt) * (oc * rstd) * gn_ref[:, cols]).astype(y_ref.dtype)

    row = pl.BlockSpec((tm, W), lambda i: (i, 0))
    (y,) = _call(
        body, grid=(S // tm,),
        in_specs=[row, pl.BlockSpec((tm, W), lambda i: (i, gate_col)), pl.BlockSpec((1, W), lambda i: (0, 0))],
        out_specs=(row,), out_shape=(jax.ShapeDtypeStruct((S, W), MXU_DTYPE),), operands=[o, proj, gn], name=name,
        chunks=link.take(ROWWISE_US["gate"]), bufs=link.bufs)
    return y


def ret_gate_bwd(dy, o, proj, gate_col, gn, heads, name):
    S, W = o.shape
    dv = W // heads
    tm = _tile(S, ROW_TILE // 2, 8)

    def body(dy_ref, o_ref, g_ref, gn_ref, do_ref, dgt_ref, dgn_ref):
        @pl.when(pl.program_id(0) == 0)
        def _():
            dgn_ref[...] = jnp.zeros_like(dgn_ref)

        for hi in range(heads):
            cols = slice(hi * dv, (hi + 1) * dv)
            oh = o_ref[:, cols]
            mu = jnp.mean(oh, axis=-1, keepdims=True)
            oc = oh - mu
            rstd = lax.rsqrt(jnp.mean(oc * oc, axis=-1, keepdims=True) + LN_EPS)
            xhat = oc * rstd
            gt = g_ref[:, cols]
            sg = _sigmoid(gt)
            gain = gn_ref[:, cols]
            dy = dy_ref[:, cols]
            dgt_ref[:, cols] = (dy * xhat * gain * (sg * (1.0 + gt * (1.0 - sg)))).astype(dgt_ref.dtype)
            dn = dy * (gt * sg)
            dgn_ref[:, cols] += jnp.sum(dn * xhat, axis=0, keepdims=True)
            dxh = dn * gain
            m1 = jnp.mean(dxh, axis=-1, keepdims=True)
            m2 = jnp.mean(dxh * xhat, axis=-1, keepdims=True)
            do_ref[:, cols] = (rstd * (dxh - m1 - xhat * m2)).astype(do_ref.dtype)

    row = pl.BlockSpec((tm, W), lambda i: (i, 0))
    vec = pl.BlockSpec((1, W), lambda i: (0, 0))
    return pl.pallas_call(
        body,
        out_shape=(jax.ShapeDtypeStruct((S, W), MXU_DTYPE), jax.ShapeDtypeStruct((S, W), MXU_DTYPE),
                   jax.ShapeDtypeStruct((1, W), F32)),
        grid=(S // tm,), in_specs=[row, row, pl.BlockSpec((tm, W), lambda i: (i, gate_col)), vec],
        out_specs=(row, row, vec), compiler_params=_params(1), name=name)(dy, o, proj, gn)


def _shift_down(x, s):
    rows = lax.broadcasted_iota(jnp.int32, x.shape, 0)
    return jnp.where(rows >= s, pltpu.roll(x, s, 0), 0.0)


def _shift_up(x, s):
    n = x.shape[0]
    rows = lax.broadcasted_iota(jnp.int32, x.shape, 0)
    return jnp.where(rows < n - s, pltpu.roll(x, n - s, 0), 0.0)


def _conv3(x, w_ref, b_ref):
    return (w_ref[2:3, :] * x + w_ref[1:2, :] * _shift_down(x, 1) + w_ref[0:1, :] * _shift_down(x, 2)
            + b_ref[...])


def conv_glu_fwd(hup, w, b, link, name):
    S, W2 = hup.shape
    F = W2 // 2
    tc = _tile(F, LANES)
    nb = F // tc

    def body(g_ref, v_ref, wg_ref, wv_ref, bg_ref, bv_ref, u_ref):
        cg = _conv3(g_ref[...], wg_ref, bg_ref)
        cv = _conv3(v_ref[...], wv_ref, bv_ref)
        u_ref[...] = (cg * _sigmoid(cg) * cv).astype(u_ref.dtype)

    def col(rows, off):
        return pl.BlockSpec((rows, tc), lambda j: (0, j + off))

    (u,) = _call(
        body, grid=(nb,), in_specs=[col(S, 0), col(S, nb), col(3, 0), col(3, nb), col(1, 0), col(1, nb)],
        out_specs=(col(S, 0),), out_shape=(jax.ShapeDtypeStruct((S, F), MXU_DTYPE),),
        operands=[hup, hup, w, w, b, b], name=name, chunks=link.take(ROWWISE_US["conv"][0]), bufs=link.bufs)
    return u


def conv_glu_bwd(du, hup, w, b, link, name):
    S, W2 = hup.shape
    F = W2 // 2
    tc = _tile(F, LANES)
    nb = F // tc

    def back(dc, x, w_ref, dh_ref, dw_ref, db_ref):
        dw_ref[2:3, :] = jnp.sum(dc * x, axis=0, keepdims=True)
        dw_ref[1:2, :] = jnp.sum(dc * _shift_down(x, 1), axis=0, keepdims=True)
        dw_ref[0:1, :] = jnp.sum(dc * _shift_down(x, 2), axis=0, keepdims=True)
        db_ref[...] = jnp.sum(dc, axis=0, keepdims=True)
        dh_ref[...] = (w_ref[2:3, :] * dc + w_ref[1:2, :] * _shift_up(dc, 1)
                       + w_ref[0:1, :] * _shift_up(dc, 2)).astype(dh_ref.dtype)

    def body(du_ref, g_ref, v_ref, wg_ref, wv_ref, bg_ref, bv_ref,
             dhg_ref, dhv_ref, dwg_ref, dwv_ref, dbg_ref, dbv_ref):
        xg, xv = g_ref[...], v_ref[...]
        cg = _conv3(xg, wg_ref, bg_ref)
        cv = _conv3(xv, wv_ref, bv_ref)
        sg = _sigmoid(cg)
        du = du_ref[...]
        back(du * cv * (sg * (1.0 + cg * (1.0 - sg))), xg, wg_ref, dhg_ref, dwg_ref, dbg_ref)
        back(du * (cg * sg), xv, wv_ref, dhv_ref, dwv_ref, dbv_ref)

    def col(rows, off):
        return pl.BlockSpec((rows, tc), lambda j: (0, j + off))

    return _call(
        body, grid=(nb,),
        in_specs=[col(S, 0), col(S, 0), col(S, nb), col(3, 0), col(3, nb), col(1, 0), col(1, nb)],
        out_specs=(col(S, 0), col(S, 0), col(3, 0), col(3, 0), col(1, 0), col(1, 0)),
        out_shape=(jax.ShapeDtypeStruct((S, F), MXU_DTYPE), jax.ShapeDtypeStruct((S, F), MXU_DTYPE),
                   jax.ShapeDtypeStruct((3, F), F32), jax.ShapeDtypeStruct((3, F), F32),
                   jax.ShapeDtypeStruct((1, F), F32), jax.ShapeDtypeStruct((1, F), F32)),
        operands=[du, hup, hup, w, w, b, b], name=name, chunks=link.take(ROWWISE_US["conv"][1]), bufs=link.bufs)


def loss_head(y, target, name):
    S, D = y.shape
    tm = _tile(S, ROW_TILE, 8)

    def body(y_ref, t_ref, dy_ref, loss_ref):
        @pl.when(pl.program_id(0) == 0)
        def _():
            loss_ref[...] = jnp.zeros_like(loss_ref)

        e = y_ref[...] - t_ref[...]
        dy_ref[...] = e * (1.0 / D)
        part = jnp.sum(jnp.sum(e * e, axis=-1, keepdims=True), axis=0, keepdims=True) * (0.5 / D)
        loss_ref[...] += jnp.broadcast_to(part, loss_ref.shape)

    row = pl.BlockSpec((tm, D), lambda i: (i, 0))
    return pl.pallas_call(
        body, out_shape=(jax.ShapeDtypeStruct((S, D), F32), jax.ShapeDtypeStruct((8, LANES), F32)),
        grid=(S // tm,), in_specs=[row, row], out_specs=(row, pl.BlockSpec((8, LANES), lambda i: (0, 0))),
        compiler_params=_params(1), name=name)(y, target)


def adam_update(parts, w, m, v, name):
    n_layers, n_parts, R, C = parts.shape
    tr = _tile(R, max(8, (1 << 19) // C), 8)
    c1 = 1.0 - ADAM_B1 ** ADAM_STEP
    c2 = 1.0 - ADAM_B2 ** ADAM_STEP

    def body(p_ref, w_ref, m_ref, v_ref, g_ref, d_ref, nm_ref, nv_ref):
        g = p_ref[0].astype(F32)
        for pi in range(1, n_parts):
            g = g + p_ref[pi].astype(F32)
        nm = ADAM_B1 * m_ref[...] + (1.0 - ADAM_B1) * g
        nv = ADAM_B2 * v_ref[...] + (1.0 - ADAM_B2) * (g * g)
        g_ref[...] = g
        nm_ref[...] = nm
        nv_ref[...] = nv
        d_ref[...] = -ADAM_LR * ((nm / c1) / (jnp.sqrt(nv / c2) + ADAM_EPS) + ADAM_WD * w_ref[...])

    row = pl.BlockSpec((None, tr, C), lambda l, i: (l, i, 0))
    out = jax.ShapeDtypeStruct((n_layers, R, C), F32)
    return _call(
        body, grid=(n_layers, R // tr),
        in_specs=[pl.BlockSpec((None, n_parts, tr, C), lambda l, i: (l, 0, i, 0)), row, row, row],
        out_specs=(row, row, row, row), out_shape=(out, out, out, out), operands=[parts, w, m, v], name=name)


def pair_sum(own, got, core, name):
    _, R, C = own.shape
    tr = _tile(R, max(16, (1 << 19) // C), 16)

    def body(core_ref, a_ref, b_ref, o_ref):
        o_ref[...] = (a_ref[...].astype(F32) + b_ref[...].astype(F32)).astype(o_ref.dtype)

    grid_spec = pltpu.PrefetchScalarGridSpec(
        num_scalar_prefetch=1, grid=(N_CHIP, R // tr),
        in_specs=[pl.BlockSpec((None, tr, C), lambda j, i, core_ref: (2 * j + core_ref[0], i, 0)),
                  pl.BlockSpec((None, tr, C), lambda j, i, core_ref: (j, i, 0))],
        out_specs=pl.BlockSpec((None, tr, C), lambda j, i, core_ref: (j, i, 0)))
    return pl.pallas_call(
        body, out_shape=jax.ShapeDtypeStruct((N_CHIP, R, C), own.dtype), grid_spec=grid_spec,
        compiler_params=_params(2), name=name)(core, own, got)


def gather_small(block, name):
    m_per, n = block.shape

    def body(x_ref, out_ref, send_sems, recv_sems, local_sem):
        x, y, c = _place()
        me, sibling = (x, y, c), (x, y, 1 - c)
        chips = _other_chips(x, y)

        def rows(px, py, pc):
            return out_ref.at[pl.ds((4 * px + 2 * py + pc) * m_per, m_per), :]

        def copy(k, blk, to, src=None):
            return pltpu.make_async_remote_copy(
                src_ref=rows(*blk) if src is None else src, dst_ref=rows(*blk),
                send_sem=send_sems.at[k], recv_sem=recv_sems.at[k], device_id=to, device_id_type=MESH)

        mine = pltpu.make_async_copy(x_ref, rows(*me), local_sem)
        mine.start()
        first = [copy(0, me, sibling, src=x_ref)]
        first += [copy(1 + j, me, (*chip, c), src=x_ref) for j, chip in enumerate(chips)]
        for cp in first:
            cp.start()
        passed = [copy(4 + j, (*chip, c), sibling) for j, chip in enumerate(chips)]
        for j, chip in enumerate(chips):
            copy(1 + j, (*chip, c), me).wait_recv()
            passed[j].start()
        copy(0, sibling, me).wait_recv()
        for j, chip in enumerate(chips):
            copy(4 + j, (*chip, 1 - c), me).wait_recv()
        for cp in first + passed:
            cp.wait_send()
        mine.wait()

    return pl.pallas_call(
        body, out_shape=jax.ShapeDtypeStruct((N_DEV * m_per, n), block.dtype),
        in_specs=[pl.BlockSpec(memory_space=pltpu.VMEM)], out_specs=pl.BlockSpec(memory_space=pltpu.VMEM),
        scratch_shapes=[pltpu.SemaphoreType.DMA((7,)), pltpu.SemaphoreType.DMA((7,)), pltpu.SemaphoreType.DMA],
        compiler_params=pltpu.CompilerParams(vmem_limit_bytes=VMEM_LIMIT_BYTES), name=name)(block)


def _rope_tables(positions, d):
    inv_freq = ROPE_BASE ** (-jnp.arange(0, d, 2, dtype=F32) / d)
    ang = positions.astype(F32)[:, None] * inv_freq
    return jnp.cos(ang), jnp.sin(ang)


def _mla_pad(nope, rope):
    S, H, _ = nope.shape
    half = MLA_ROPE // 2
    z = jnp.zeros((S, H, LANES // 2 - half), nope.dtype)
    return jnp.concatenate([nope, rope[..., :half], z, rope[..., half:], z], axis=2).reshape(S, H * MLA_PAD)


def _mla_unpad(x, H):
    S = x.shape[0]
    half = MLA_ROPE // 2
    x3 = x.reshape(S, H, MLA_PAD)
    rope = jnp.concatenate([x3[..., MLA_NOPE:MLA_NOPE + half],
                            x3[..., MLA_NOPE + LANES // 2:MLA_NOPE + LANES // 2 + half]], axis=2)
    return x3[..., :MLA_NOPE], rope


def kernel(x, mem, positions, ret_w_in, ret_gn_g, ret_w_out, mla_w_in, mla_q_norm_g, mla_w_uq, mla_kv_norm_g, mla_w_ukv, mla_w_out, xa_w_q, xa_w_kv, xa_w_out, ffn_w_up, ffn_conv_w, ffn_conv_b, ffn_w_down, ln_mix_g, ln_mix_b, ln_mem_g, ln_mem_b, ln_ffn_g, ln_ffn_b, loss_target, m_ret_w_in, m_ret_gn_g, m_ret_w_out, m_mla_w_in, m_mla_q_norm_g, m_mla_w_uq, m_mla_kv_norm_g, m_mla_w_ukv, m_mla_w_out, m_xa_w_q, m_xa_w_kv, m_xa_w_out, m_ffn_w_up, m_ffn_conv_w, m_ffn_conv_b, m_ffn_w_down, m_ln_mix_g, m_ln_mix_b, m_ln_mem_g, m_ln_mem_b, m_ln_ffn_g, m_ln_ffn_b, v_ret_w_in, v_ret_gn_g, v_ret_w_out, v_mla_w_in, v_mla_q_norm_g, v_mla_w_uq, v_mla_kv_norm_g, v_mla_w_ukv, v_mla_w_out, v_xa_w_q, v_xa_w_kv, v_xa_w_out, v_ffn_w_up, v_ffn_conv_w, v_ffn_conv_b, v_ffn_w_down, v_ln_mix_g, v_ln_mix_b, v_ln_mem_g, v_ln_mem_b, v_ln_ffn_g, v_ln_ffn_b):
    weights = dict(ret_w_in=ret_w_in, ret_gn_g=ret_gn_g, ret_w_out=ret_w_out, mla_w_in=mla_w_in,
                   mla_q_norm_g=mla_q_norm_g, mla_w_uq=mla_w_uq, mla_kv_norm_g=mla_kv_norm_g,
                   mla_w_ukv=mla_w_ukv, mla_w_out=mla_w_out, xa_w_q=xa_w_q, xa_w_kv=xa_w_kv, xa_w_out=xa_w_out,
                   ffn_w_up=ffn_w_up, ffn_conv_w=ffn_conv_w, ffn_conv_b=ffn_conv_b, ffn_w_down=ffn_w_down,
                   ln_mix_g=ln_mix_g, ln_mix_b=ln_mix_b, ln_mem_g=ln_mem_g, ln_mem_b=ln_mem_b,
                   ln_ffn_g=ln_ffn_g, ln_ffn_b=ln_ffn_b)
    mom_m = dict(ret_w_in=m_ret_w_in, ret_gn_g=m_ret_gn_g, ret_w_out=m_ret_w_out, mla_w_in=m_mla_w_in,
                 mla_q_norm_g=m_mla_q_norm_g, mla_w_uq=m_mla_w_uq, mla_kv_norm_g=m_mla_kv_norm_g,
                 mla_w_ukv=m_mla_w_ukv, mla_w_out=m_mla_w_out, xa_w_q=m_xa_w_q, xa_w_kv=m_xa_w_kv,
                 xa_w_out=m_xa_w_out, ffn_w_up=m_ffn_w_up, ffn_conv_w=m_ffn_conv_w, ffn_conv_b=m_ffn_conv_b,
                 ffn_w_down=m_ffn_w_down, ln_mix_g=m_ln_mix_g, ln_mix_b=m_ln_mix_b, ln_mem_g=m_ln_mem_g,
                 ln_mem_b=m_ln_mem_b, ln_ffn_g=m_ln_ffn_g, ln_ffn_b=m_ln_ffn_b)
    mom_v = dict(ret_w_in=v_ret_w_in, ret_gn_g=v_ret_gn_g, ret_w_out=v_ret_w_out, mla_w_in=v_mla_w_in,
                 mla_q_norm_g=v_mla_q_norm_g, mla_w_uq=v_mla_w_uq, mla_kv_norm_g=v_mla_kv_norm_g,
                 mla_w_ukv=v_mla_w_ukv, mla_w_out=v_mla_w_out, xa_w_q=v_xa_w_q, xa_w_kv=v_xa_w_kv,
                 xa_w_out=v_xa_w_out, ffn_w_up=v_ffn_w_up, ffn_conv_w=v_ffn_conv_w, ffn_conv_b=v_ffn_conv_b,
                 ffn_w_down=v_ffn_w_down, ln_mix_g=v_ln_mix_g, ln_mix_b=v_ln_mix_b, ln_mem_g=v_ln_mem_g,
                 ln_mem_b=v_ln_mem_b, ln_ffn_g=v_ln_ffn_g, ln_ffn_b=v_ln_ffn_b)
    order = list(weights)
    BIG = dict(ret_w_in=2, ret_w_out=1, mla_w_in=1, mla_w_uq=2, mla_w_ukv=2, mla_w_out=1,
               xa_w_q=1, xa_w_kv=2, xa_w_out=1, ffn_w_up=2, ffn_w_down=1)
    SMALL_CUT = ("ffn_conv_w", "mla_q_norm_g", "mla_kv_norm_g")
    REPLICATED = ("ret_gn_g", "ffn_conv_b", "ln_mix_g", "ln_mix_b", "ln_mem_g", "ln_mem_b", "ln_ffn_g", "ln_ffn_b")

    x = x[0]
    mem = mem[0]
    positions = positions[0]
    target = loss_target[0]
    S, D = x.shape
    depth = ln_mix_g.shape[0]
    alpha = (2 * depth) ** 0.25
    ret_dk = D // RET_HEADS
    ret_dv = 2 * D // RET_HEADS
    ret_qkw = RET_HEADS * ret_dk
    ret_vw = RET_HEADS * ret_dv
    xa_d = D // XA_HEADS
    assert 2 * ret_qkw == ret_vw and MLA_NOPE == LANES and MLA_V == LANES and MLA_ROPE == LANES // 2
    assert MLA_Q_RANK == MLA_KV_RANK and CHUNK & (CHUNK - 1) == 0
    core = lax.axis_index("c").astype(jnp.int32).reshape(1)
    dev = 4 * lax.axis_index("x") + 2 * lax.axis_index("y") + lax.axis_index("c")

    gather, scatter = _Link(GATHER_BYTES_PER_US), _Link(SCATTER_BYTES_PER_US)
    shard_b = {n: weights[n].astype(PAY_DTYPE) for n in BIG}
    staged = set()

    def units(layer):
        mixer = ("ret_w_in", "ret_w_out") if layer % 2 == 0 else ("mla_w_in", "mla_w_uq", "mla_w_ukv", "mla_w_out")
        return [(n, layer // 2) for n in mixer] + [(n, layer) for n in
                                                    ("xa_w_q", "xa_w_kv", "xa_w_out", "ffn_w_up", "ffn_w_down")]

    def geometry(n):
        _, k, nn = weights[n].shape
        return BIG[n], k, nn

    def after_first_level(key, full, axis, k, nn):
        def to_sibling():
            chunk = _Chunk("gather2", None, key, full, 0, 0, k, nn, axis=axis, k=k, n=nn)
            gather.side.append((chunk, lambda: staged.add(key)))

        def relay():
            gather.push("relay", None, key, full, 0, k // 2, nn, done=to_sibling, front=True,
                        axis=axis, k=k, n=nn, half=k // 2)
        return relay

    for layer in range(depth):
        for n, j in units(layer):
            axis, k, nn = geometry(n)
            full = (N_DEV * k, nn) if axis == 1 else (k, N_DEV * nn)
            gather.push("gather", shard_b[n], (n, j), full, j, k, nn,
                        done=after_first_level((n, j), full, axis, k, nn), axis=axis, k=k, n=nn)

    def weight(n, j):
        key = (n, j)
        while key not in staged:
            gather.flush(key, name="gather_rest")
            gather.settle()
            if not any(ch.key == key for ch in gather.queue):
                break
        while key not in staged:
            assert gather.side
            gather.carry(LEVEL2_US, name="gather_level2")
            gather.settle()
        return gather.bufs[key]

    def pack(arrays):
        flat = jnp.concatenate([a.reshape(-1) for a in arrays])
        rows = -(-flat.shape[0] // (8 * LANES)) * 8
        return jnp.pad(flat, (0, rows * LANES - flat.shape[0])).reshape(rows, LANES)

    def unpack(flat, like):
        out, at = [], 0
        for a in like:
            out.append(flat[at:at + a.size].reshape(a.shape))
            at += a.size
        return out

    small_local = [weights[n] for n in SMALL_CUT]
    blk = pack(small_local)
    allsmall = gather_small(blk, name="gather_small_weights").reshape(N_DEV, -1)
    per_dev = [unpack(allsmall[d], small_local) for d in range(N_DEV)]
    conv_w_full, qg_full, kvg_full = [jnp.concatenate([per_dev[d][i] for d in range(N_DEV)], axis=-1)
                                      for i in range(len(SMALL_CUT))]

    rcos, rsin = _rope_tables(positions, ret_dk)
    mcos, msin = _rope_tables(positions, MLA_ROPE)
    zq = jnp.zeros((S, LANES // 2 - MLA_ROPE // 2), F32)
    mla_ta = jnp.concatenate([mcos, zq, mcos, zq], axis=1)
    mla_tb = jnp.concatenate([-msin, zq, msin, zq], axis=1)
    log_gamma = jnp.log(1.0 - 2.0 ** (-5.0 - jnp.arange(RET_HEADS, dtype=F32)))
    log_gamma = jnp.broadcast_to(log_gamma[:, None, None], (RET_HEADS, 8, LANES))
    mla_scale = (MLA_NOPE + MLA_ROPE) ** -0.5
    xa_scale = xa_d ** -0.5
    mem_b = mem.astype(MXU_DTYPE)

    def vec(a, l):
        return a[l][None, :]

    saved = []
    h = x
    hb = x.astype(MXU_DTYPE)
    for layer in range(depth):
        j = layer // 2
        sv = {}
        sv["hb_mix"] = hb
        if layer % 2 == 0:
            proj = mm(hb, weight("ret_w_in", j), link=gather, name="ret_in")
            qk = ret_rope_fwd(proj, rcos, rsin, 2 * RET_HEADS, ret_dk // 2, RET_HEADS, ret_dk ** -0.5, gather,
                              name="ret_rope")
            (o,) = attn_fwd((qk, ret_dk, 0, 1), (qk, ret_dk, RET_HEADS, 1), (proj, ret_dv, RET_HEADS, 1),
                            heads=RET_HEADS, softmax=False, masked=True, log_gamma=log_gamma, link=gather,
                            est_us=ATTN_US["ret"][0], name="ret_attn")
            yb = ret_gate_fwd(o, proj, 2, vec(ret_gn_g, j), RET_HEADS, gather, name="ret_gate")
            mix = mm(yb, weight("ret_w_out", j), link=gather, name="ret_out")
            sv.update(proj=proj, qk=qk, o=o, yb=yb)
        else:
            proj = mm(hb, weight("mla_w_in", j), link=gather, name="mla_in")
            cq = rms_fwd(proj, 0, MLA_Q_RANK, vec(qg_full, j), name="mla_q_norm")
            ckv = rms_fwd(proj, 1, MLA_KV_RANK, vec(kvg_full, j), name="mla_kv_norm")
            qf = mm(cq, weight("mla_w_uq", j), link=gather, name="mla_uq")
            kvf = mm(ckv, weight("mla_w_ukv", j), link=gather, name="mla_ukv")
            q3 = qf.reshape(S, MLA_HEADS, MLA_NOPE + MLA_ROPE)
            kv3 = kvf.reshape(S, MLA_HEADS, MLA_NOPE + MLA_V)
            k_rope = jnp.broadcast_to(proj[:, None, MLA_Q_RANK + MLA_KV_RANK:], (S, MLA_HEADS, MLA_ROPE))
            q_pad = _mla_pad(q3[..., :MLA_NOPE], q3[..., MLA_NOPE:])
            k_pad = _mla_pad(kv3[..., :MLA_NOPE], k_rope)
            (qr,) = mla_rope(q_pad, mla_ta, mla_tb, MLA_HEADS, backward=False, head_sum=False,
                             out_dtype=MXU_DTYPE, link=gather, name="mla_rope_q")
            (kr,) = mla_rope(k_pad, mla_ta, mla_tb, MLA_HEADS, backward=False, head_sum=False,
                             out_dtype=MXU_DTYPE, link=gather, name="mla_rope_k")
            o, ob, lse = attn_fwd((qr, MLA_PAD, 0, 1), (kr, MLA_PAD, 0, 1), (kvf, MLA_V, 1, 2), heads=MLA_HEADS,
                              softmax=True, masked=True, scale=mla_scale, link=gather, est_us=ATTN_US["mla"][0],
                              name="mla_attn")
            mix = mm(ob, weight("mla_w_out", j), link=gather, name="mla_out")
            sv.update(proj=proj, cq=cq, ckv=ckv, kvf=kvf, qr=qr, kr=kr, o=o, ob=ob, lse=lse)
        h, hb, sv["xhat_mix"], sv["rstd_mix"] = ln_fwd(h, mix, vec(ln_mix_g, layer), vec(ln_mix_b, layer), alpha,
                                                       gather, name="ln_mix")
        sv["hb_mem"] = hb
        q = mm(hb, weight("xa_w_q", layer), link=gather, name="xa_q")
        kvm = mm(mem_b, weight("xa_w_kv", layer), link=gather, name="xa_kv")
        o, ob, lse = attn_fwd((q, xa_d, 0, 1), (kvm, xa_d, 0, 1), (kvm, xa_d, XA_HEADS, 1), heads=XA_HEADS,
                          softmax=True, masked=False, scale=xa_scale, link=gather, est_us=ATTN_US["xa"][0],
                          name="xa_attn")
        mix = mm(ob, weight("xa_w_out", layer), link=gather, name="xa_out")
        sv.update(xa_q=q, xa_kvm=kvm, xa_o=o, xa_ob=ob, xa_lse=lse)
        h, hb, sv["xhat_mem"], sv["rstd_mem"] = ln_fwd(h, mix, vec(ln_mem_g, layer), vec(ln_mem_b, layer), alpha,
                                                       gather, name="ln_mem")
        sv["hb_ffn"] = hb
        hup = mm(hb, weight("ffn_w_up", layer), link=gather, name="ffn_up")
        u = conv_glu_fwd(hup, conv_w_full[layer], vec(ffn_conv_b, layer), gather, name="ffn_conv")
        mix = mm(u, weight("ffn_w_down", layer), link=gather, name="ffn_down")
        sv.update(hup=hup, u=u)
        h, hb, sv["xhat_ffn"], sv["rstd_ffn"] = ln_fwd(h, mix, vec(ln_ffn_g, layer), vec(ln_ffn_b, layer), alpha,
                                                       gather, name="ln_ffn")
        saved.append(sv)

    dh, loss_blk = loss_head(h, target, name="loss_head")
    loss = lax.psum(loss_blk[0, 0], ("x", "y", "c"))

    small = {n: [None] * weights[n].shape[0] for n in REPLICATED + SMALL_CUT}
    W = gather.bufs

    def wgrad(n, a, d, l, tag):
        axis, k, nn = geometry(n)
        if axis == 2:
            slabs = mm(a, d, mode="tn", out_dtype=PAY_DTYPE, slab_width=nn, link=scatter, name=tag)
        else:
            slabs = mm(a, d, mode="tn", out_dtype=PAY_DTYPE, link=scatter, name=tag).reshape(N_DEV, k, nn)
        n_layers = weights[n].shape[0]
        pair = ("pair", n, l)

        def then():
            sums = pair_sum(slabs, scatter.bufs.pop(pair), core, name="pairsum_" + n)
            scatter.push("scatter", sums, n, (n_layers, N_CHIP, k, nn), l, k, nn)

        scatter.side.append((_Chunk("swap", slabs, pair, (N_CHIP, k, nn), 0, 0, k, nn), then))

    for layer in reversed(range(depth)):
        j = layer // 2
        sv = saved[layer]
        dz, dzb, dg, db = ln_bwd(dh, sv["xhat_ffn"], sv["rstd_ffn"], vec(ln_ffn_g, layer), scatter, name="ln_ffn_bwd")
        small["ln_ffn_g"][layer], small["ln_ffn_b"][layer] = dg[0], db[0]
        wgrad("ffn_w_down", sv["u"], dzb, layer, "ffn_down_dw")
        du = mm(dzb, W[("ffn_w_down", layer)], mode="nt", link=scatter, name="ffn_down_dx")
        dhg, dhv, dwg, dwv, dbg, dbv = conv_glu_bwd(du, sv["hup"], conv_w_full[layer], vec(ffn_conv_b, layer),
                                                    scatter, name="ffn_conv_bwd")
        small["ffn_conv_w"][layer] = jnp.concatenate([dwg, dwv], axis=1)
        small["ffn_conv_b"][layer] = jnp.concatenate([dbg, dbv], axis=1)[0]
        dhup = jnp.concatenate([dhg, dhv], axis=1)
        wgrad("ffn_w_up", sv["hb_ffn"], dhup, layer, "ffn_up_dw")
        dh = mm(dhup, W[("ffn_w_up", layer)], mode="nt", add=dz, add_scale=alpha, link=scatter, name="ffn_up_dx")
        dz, dzb, dg, db = ln_bwd(dh, sv["xhat_mem"], sv["rstd_mem"], vec(ln_mem_g, layer), scatter, name="ln_mem_bwd")
        small["ln_mem_g"][layer], small["ln_mem_b"][layer] = dg[0], db[0]
        wgrad("xa_w_out", sv["xa_ob"], dzb, layer, "xa_out_dw")
        do = mm(dzb, W[("xa_w_out", layer)], mode="nt", link=scatter, name="xa_out_dx")
        dq, dk, dv = attn_bwd((sv["xa_q"], xa_d, 0, 1), (sv["xa_kvm"], xa_d, 0, 1), (sv["xa_kvm"], xa_d, XA_HEADS, 1),
                              do, heads=XA_HEADS, softmax=True, masked=False, scale=xa_scale, o=sv["xa_o"],
                              lse=sv["xa_lse"], link=scatter, est_us=ATTN_US["xa"][1], name="xa_attn_bwd")
        dqb = dq.astype(MXU_DTYPE)
        wgrad("xa_w_kv", mem_b, jnp.concatenate([dk, dv], axis=1).astype(MXU_DTYPE), layer, "xa_kv_dw")
        wgrad("xa_w_q", sv["hb_mem"], dqb, layer, "xa_q_dw")
        dh = mm(dqb, W[("xa_w_q", layer)], mode="nt", add=dz, add_scale=alpha, link=scatter, name="xa_q_dx")
        dz, dzb, dg, db = ln_bwd(dh, sv["xhat_mix"], sv["rstd_mix"], vec(ln_mix_g, layer), scatter, name="ln_mix_bwd")
        small["ln_mix_g"][layer], small["ln_mix_b"][layer] = dg[0], db[0]
        if layer % 2 == 0:
            wgrad("ret_w_out", sv["yb"], dzb, j, "ret_out_dw")
            dy = mm(dzb, W[("ret_w_out", j)], mode="nt", link=scatter, name="ret_out_dx")
            do, dgate, dgn = ret_gate_bwd(dy, sv["o"], sv["proj"], 2, vec(ret_gn_g, j), RET_HEADS, name="ret_gate_bwd")
            small["ret_gn_g"][j] = dgn[0]
            dq, dk, dv = attn_bwd((sv["qk"], ret_dk, 0, 1), (sv["qk"], ret_dk, RET_HEADS, 1),
                                  (sv["proj"], ret_dv, RET_HEADS, 1), do, heads=RET_HEADS, softmax=False,
                                  masked=True, log_gamma=log_gamma, link=scatter, est_us=ATTN_US["ret"][1],
                                  name="ret_attn_bwd")
            dqk = ret_rope_bwd(dq, dk, rcos, rsin, ret_dk // 2, ret_dk ** -0.5, name="ret_rope_bwd")
            dproj = jnp.concatenate([dqk, dv.astype(MXU_DTYPE), dgate], axis=1)
            wgrad("ret_w_in", sv["hb_mix"], dproj, j, "ret_in_dw")
            dh = mm(dproj, W[("ret_w_in", j)], mode="nt", add=dz, add_scale=alpha, link=scatter, name="ret_in_dx")
        else:
            wgrad("mla_w_out", sv["ob"], dzb, j, "mla_out_dw")
            do = mm(dzb, W[("mla_w_out", j)], mode="nt", link=scatter, name="mla_out_dx")
            dq, dk, dv = attn_bwd((sv["qr"], MLA_PAD, 0, 1), (sv["kr"], MLA_PAD, 0, 1), (sv["kvf"], MLA_V, 1, 2), do,
                                  heads=MLA_HEADS, softmax=True, masked=True, scale=mla_scale, o=sv["o"],
                                  lse=sv["lse"], link=scatter, est_us=ATTN_US["mla"][1], name="mla_attn_bwd")
            (dq_un,) = mla_rope(dq, mla_ta, mla_tb, MLA_HEADS, backward=True, head_sum=False, out_dtype=F32,
                                name="mla_rope_q_bwd")
            dk_un, dk_rope_sum = mla_rope(dk, mla_ta, mla_tb, MLA_HEADS, backward=True, head_sum=True, out_dtype=F32,
                                          name="mla_rope_k_bwd")
            dq_nope, dq_rope = _mla_unpad(dq_un, MLA_HEADS)
            dqf = jnp.concatenate([dq_nope, dq_rope], axis=2).reshape(S, -1).astype(MXU_DTYPE)
            dk_nope, _ = _mla_unpad(dk_un, MLA_HEADS)
            dkvf = jnp.concatenate([dk_nope, dv.reshape(S, MLA_HEADS, MLA_V)], axis=2).reshape(S, -1).astype(MXU_DTYPE)
            half = MLA_ROPE // 2
            dk_rope = jnp.concatenate([dk_rope_sum[:, :half], dk_rope_sum[:, LANES // 2:LANES // 2 + half]], axis=1)
            wgrad("mla_w_uq", sv["cq"], dqf, j, "mla_uq_dw")
            dcq = mm(dqf, W[("mla_w_uq", j)], mode="nt", link=scatter, name="mla_uq_dx")
            wgrad("mla_w_ukv", sv["ckv"], dkvf, j, "mla_ukv_dw")
            dckv = mm(dkvf, W[("mla_w_ukv", j)], mode="nt", link=scatter, name="mla_ukv_dx")
            dcq_in, dqg = rms_bwd(dcq, sv["proj"], 0, MLA_Q_RANK, vec(qg_full, j), name="mla_q_norm_bwd")
            dckv_in, dkvg = rms_bwd(dckv, sv["proj"], 1, MLA_KV_RANK, vec(kvg_full, j), name="mla_kv_norm_bwd")
            small["mla_q_norm_g"][j], small["mla_kv_norm_g"][j] = dqg[0], dkvg[0]
            dproj = jnp.concatenate([dcq_in, dckv_in, dk_rope], axis=1).astype(MXU_DTYPE)
            wgrad("mla_w_in", sv["hb_mix"], dproj, j, "mla_in_dw")
            dh = mm(dproj, W[("mla_w_in", j)], mode="nt", add=dz, add_scale=alpha, link=scatter, name="mla_in_dx")
    grad_x = dh[None]

    grads, deltas, new_m, new_v = {}, {}, {}, {}
    scatter.ride_out(name="scatter_rest")
    scatter.flush(None, name="scatter_rest")
    assert not scatter.queue and not scatter.side
    for n in BIG:
        grads[n], deltas[n], new_m[n], new_v[n] = adam_update(scatter.bufs[n], weights[n], mom_m[n], mom_v[n],
                                                              name="adam_" + n)

    small_names = list(REPLICATED + SMALL_CUT)
    partial = [jnp.stack(small[n]) for n in small_names]
    allpart = gather_small(pack(partial), name="gather_small_grads")
    rows = allpart.shape[0] // N_DEV
    allpart = allpart.reshape(N_DEV, rows, LANES)

    rep_names = list(REPLICATED)
    rep_w = pack([weights[n] for n in rep_names])
    rep_m = pack([mom_m[n] for n in rep_names])
    rep_v = pack([mom_v[n] for n in rep_names])
    rep_rows = rep_w.shape[0]
    rep_size = sum(weights[n].size for n in rep_names)
    flat_parts = allpart.reshape(N_DEV, rows * LANES)
    rep_parts = jnp.pad(flat_parts[:, :rep_size], ((0, 0), (0, rep_rows * LANES - rep_size)))
    outs = adam_update(rep_parts.reshape(1, N_DEV, rep_rows, LANES), rep_w[None], rep_m[None], rep_v[None],
                       name="adam_replicated")
    for o_, dst in zip(outs, (grads, deltas, new_m, new_v)):
        for n, a in zip(rep_names, unpack(o_.reshape(-1), [weights[n] for n in rep_names])):
            dst[n] = a

    cut_names = list(SMALL_CUT)
    cut_full = [jnp.stack(small[n]) for n in cut_names]
    cut_parts = []
    at = rep_size
    for n, a in zip(cut_names, cut_full):
        whole = flat_parts[:, at:at + a.size].reshape((N_DEV,) + a.shape)
        at += a.size
        width = weights[n].shape[-1]
        start = (0,) * (whole.ndim - 1) + (dev * width,)
        mine = lax.dynamic_slice(whole, start, whole.shape[:-1] + (width,))
        cut_parts.append(mine.reshape(N_DEV, -1))
    cut_parts = jnp.concatenate(cut_parts, axis=1)
    cut_w = pack([weights[n] for n in cut_names])
    cut_m = pack([mom_m[n] for n in cut_names])
    cut_v = pack([mom_v[n] for n in cut_names])
    cut_rows = cut_w.shape[0]
    cut_parts = jnp.pad(cut_parts, ((0, 0), (0, cut_rows * LANES - cut_parts.shape[1])))
    outs = adam_update(cut_parts.reshape(1, N_DEV, cut_rows, LANES), cut_w[None], cut_m[None], cut_v[None],
                       name="adam_small_cut")
    for o_, dst in zip(outs, (grads, deltas, new_m, new_v)):
        for n, a in zip(cut_names, unpack(o_.reshape(-1), [weights[n] for n in cut_names])):
            dst[n] = a

    return (loss, grad_x, *[grads[n] for n in order], *[deltas[n] for n in order],
            *[new_m[n] for n in order], *[new_v[n] for n in order])
```

```python
import functools
import math

import jax
import jax.numpy as jnp
from jax import lax
from jax.experimental import pallas as pl
from jax.experimental.pallas import tpu as pltpu

F32 = jnp.float32
MXU_DTYPE = jnp.bfloat16
PAY_DTYPE = jnp.bfloat16
MESH = pl.DeviceIdType.MESH
N_DEV = 8
N_CHIP = 4

DEPTH = 4
CHUNK = 64
RET_HEADS = 8
MLA_HEADS = 16
MLA_Q_RANK = 512
MLA_KV_RANK = 512
MLA_NOPE = 128
MLA_ROPE = 64
MLA_V = 128
MLA_PAD = 256
XA_HEADS = 4
ROPE_BASE = 10000.0
LN_EPS = 1e-5
RMS_EPS = 1e-6
NEG_INF = -1e30
ADAM_LR = 0.001
ADAM_B1 = 0.9
ADAM_B2 = 0.999
ADAM_EPS = 1e-08
ADAM_WD = 0.01
ADAM_STEP = 10

LANES = 128
VMEM_LIMIT_BYTES = 48 * 1024 * 1024
TILE_M = 1024
TILE_N = 1024
TILE_K = 2048
SLAB_TILE_N = 1536
ATTN_TILE = 512
ATTN_PARTS = 2
ROW_TILE = 256

MXU_FLOPS_PER_US = 7.0e8
GATHER_BYTES_PER_US = 5.5e4
SCATTER_BYTES_PER_US = 1.5e4
CHUNK_BYTES = 1280 * 1024
HOST_CHUNKS = 12
LEVEL2_US = 30.0
MIN_HOST_US = 45.0
SIDE_CHUNKS = 4
ATTN_US = {"ret": (125.0, 160.0), "mla": (320.0, 270.0), "xa": (35.0, 35.0)}
ROWWISE_US = {"ln": (25.0, 22.0), "conv": (65.0, 110.0), "rope": 20.0, "gate": 33.0}


def _tile(n, cap, mult=LANES):
    best = None
    for t in range(mult, min(n, cap) + 1, mult):
        if n % t == 0:
            best = t
    return n if best is None else best


def _params(n_axes):
    return pltpu.CompilerParams(dimension_semantics=("arbitrary",) * n_axes,
                                vmem_limit_bytes=VMEM_LIMIT_BYTES)


def _place():
    return lax.axis_index("x"), lax.axis_index("y"), lax.axis_index("c")


def _other_chips(x, y):
    return [(1 - x, y), (x, 1 - y), (1 - x, 1 - y)]


def _remote(src, dst, send, recv, to):
    return pltpu.make_async_remote_copy(src_ref=src, dst_ref=dst, send_sem=send, recv_sem=recv,
                                        device_id=to, device_id_type=MESH)


class _Chunk:
    def __init__(self, kind, src, key, dst_shape, layer, r0, rc, cols, axis=None, k=None, n=None, half=None):
        self.kind, self.src, self.key, self.dst_shape = kind, src, key, dst_shape
        self.layer, self.r0, self.rc, self.axis, self.k, self.n, self.half = layer, r0, rc, axis, k, n, half
        self.nbytes = rc * cols * jnp.dtype(PAY_DTYPE).itemsize

    def copies(self, src_ref, dst_ref, send, recv, loc, i):
        x, y, c = _place()
        chips = _other_chips(x, y)
        rows = pl.ds(self.r0, self.rc)
        sibling, x_nbr, y_nbr, diagonal = (x, y, 1 - c), (1 - x, y, c), (x, 1 - y, c), (1 - x, 1 - y, c)
        if self.kind == "gather":
            peers = [sibling, x_nbr, y_nbr]
            src = src_ref.at[self.layer, rows, :]

            def slab(px, py, pc):
                idx = 4 * px + 2 * py + pc
                if self.axis == 1:
                    return dst_ref.at[pl.ds(idx * self.k + self.r0, self.rc), :]
                return dst_ref.at[rows, pl.ds(idx * self.n, self.n)]

            sends = [_remote(src, slab(x, y, c), send.at[4 * i + t], recv.at[4 * i + t], p)
                     for t, p in enumerate(peers)]
            recvs = [_remote(src, slab(*p), send.at[4 * i + t], recv.at[4 * i + t], p)
                     for t, p in enumerate(peers)]
            return sends, recvs, pltpu.make_async_copy(src, slab(x, y, c), loc.at[i])
        if self.kind == "relay":
            def piece(p, r):
                idx = 4 * p[0] + 2 * p[1] + p[2]
                if self.axis == 1:
                    return dst_ref.at[pl.ds(idx * self.k + r, self.rc), :]
                return dst_ref.at[pl.ds(r, self.rc), pl.ds(idx * self.n, self.n)]

            top, bottom = self.r0, self.half + self.r0
            sends = [_remote(piece(y_nbr, top), piece(y_nbr, top), send.at[4 * i], recv.at[4 * i], x_nbr),
                     _remote(piece(x_nbr, bottom), piece(x_nbr, bottom), send.at[4 * i + 1], recv.at[4 * i + 1], y_nbr)]
            recvs = [_remote(piece(diagonal, top), piece(diagonal, top), send.at[4 * i], recv.at[4 * i], x_nbr),
                     _remote(piece(diagonal, bottom), piece(diagonal, bottom), send.at[4 * i + 1], recv.at[4 * i + 1],
                             y_nbr)]
            return sends, recvs, None
        if self.kind == "gather2":
            def slab(px, py, pc):
                idx = 4 * px + 2 * py + pc
                if self.axis == 1:
                    return dst_ref.at[pl.ds(idx * self.k, self.k), :]
                return dst_ref.at[:, pl.ds(idx * self.n, self.n)]

            sends = [_remote(slab(px, py, c), slab(px, py, c), send.at[4 * i + t], recv.at[4 * i + t], sibling)
                     for t, (px, py) in enumerate(chips)]
            recvs = [_remote(slab(px, py, 1 - c), slab(px, py, 1 - c), send.at[4 * i + t], recv.at[4 * i + t], sibling)
                     for t, (px, py) in enumerate(chips)]
            return sends, recvs, None
        if self.kind == "swap":
            sends = [_remote(src_ref.at[2 * j + (1 - c)], dst_ref.at[j], send.at[4 * i + j], recv.at[4 * i + j], sibling)
                     for j in range(N_CHIP)]
            recvs = [_remote(src_ref.at[2 * j + c], dst_ref.at[j], send.at[4 * i + j], recv.at[4 * i + j], sibling)
                     for j in range(N_CHIP)]
            return sends, recvs, None
        mine = 2 * x + y
        own = src_ref.at[mine, rows, :]
        sends = [_remote(src_ref.at[2 * px + py, rows, :], dst_ref.at[self.layer, mine, rows, :],
                         send.at[4 * i + t], recv.at[4 * i + t], (px, py, c)) for t, (px, py) in enumerate(chips)]
        recvs = [_remote(own, dst_ref.at[self.layer, 2 * px + py, rows, :],
                         send.at[4 * i + t], recv.at[4 * i + t], (px, py, c)) for t, (px, py) in enumerate(chips)]
        return sends, recvs, pltpu.make_async_copy(own, dst_ref.at[self.layer, mine, rows, :], loc.at[i])


def _call(body, *, grid, in_specs, out_specs, out_shape, operands, scratch=(), name, chunks=(), bufs=None):
    n_in, n_out, n_scr = len(operands), len(out_shape), len(scratch)
    params = _params(len(grid)) if grid else pltpu.CompilerParams(vmem_limit_bytes=VMEM_LIMIT_BYTES)
    if not chunks:
        return pl.pallas_call(
            body, out_shape=tuple(out_shape), grid=grid, in_specs=list(in_specs), out_specs=tuple(out_specs),
            scratch_shapes=list(scratch), compiler_params=params, name=name)(*operands)
    srcs, keys = [], []
    for ch in chunks:
        if ch.src is not None and not any(ch.src is s for s in srcs):
            srcs.append(ch.src)
        if ch.key not in keys:
            keys.append(ch.key)
    shape_of = {ch.key: ch.dst_shape for ch in chunks}
    held = [k for k in keys if bufs.get(k) is not None]
    extra = srcs + [bufs[k] for k in held]
    aliases = {n_in + len(srcs) + i: n_out + keys.index(k) for i, k in enumerate(held)}
    anywhere = pl.BlockSpec(memory_space=pl.ANY)
    n_extra, n_keys, n_ch = len(extra), len(keys), len(chunks)

    def hosted(*refs):
        src_refs = refs[n_in:n_in + len(srcs)]
        o0 = n_in + n_extra
        dst_refs = refs[o0 + n_out:o0 + n_out + n_keys]
        s0 = o0 + n_out + n_keys
        send, recv, loc = refs[s0 + n_scr:]
        sends, recvs, locs = [], [], []
        for i, ch in enumerate(chunks):
            src_ref = None if ch.src is None else src_refs[[ch.src is s for s in srcs].index(True)]
            s_, r_, l_ = ch.copies(src_ref, dst_refs[keys.index(ch.key)], send, recv, loc, i)
            sends += s_
            recvs += r_
            if l_ is not None:
                locs.append(l_)

        def start():
            for cp in locs + sends:
                cp.start()

        def finish():
            for cp in recvs:
                cp.wait_recv()
            for cp in sends:
                cp.wait_send()
            for cp in locs:
                cp.wait()

        if grid:
            first = functools.reduce(jnp.logical_and, [pl.program_id(a) == 0 for a in range(len(grid))])
            last = functools.reduce(jnp.logical_and, [pl.program_id(a) == grid[a] - 1 for a in range(len(grid))])
            pl.when(first)(start)
            body(*refs[:n_in], *refs[o0:o0 + n_out], *refs[s0:s0 + n_scr])
            pl.when(last)(finish)
        else:
            start()
            body(*refs[:n_in], *refs[o0:o0 + n_out], *refs[s0:s0 + n_scr])
            finish()

    res = pl.pallas_call(
        hosted, out_shape=tuple(out_shape) + tuple(jax.ShapeDtypeStruct(shape_of[k], PAY_DTYPE) for k in keys),
        grid=grid, in_specs=list(in_specs) + [anywhere] * n_extra, out_specs=tuple(out_specs) + (anywhere,) * n_keys,
        scratch_shapes=list(scratch) + [pltpu.SemaphoreType.DMA((4 * n_ch,)), pltpu.SemaphoreType.DMA((4 * n_ch,)),
                                        pltpu.SemaphoreType.DMA((n_ch,))],
        input_output_aliases=aliases, compiler_params=params, name=name)(*operands, *extra)
    for k, arr in zip(keys, res[n_out:]):
        bufs[k] = arr
    return res[:n_out]


class _Link:
    def __init__(self, bytes_per_us):
        self.rate = bytes_per_us
        self.queue, self.bufs = [], {}
        self.side = []
        self.after = []
        self.left = {}
        self.on_done = {}

    def push(self, kind, src, key, dst_shape, layer, rows, cols, done=None, front=False, **geometry):
        fits = [r for r in range(16, rows + 1, 16) if rows % r == 0 and r * cols * 2 <= CHUNK_BYTES]
        rc = max(fits) if fits else min(r for r in range(16, rows + 1, 16) if rows % r == 0)
        new = [_Chunk(kind, src, key, dst_shape, layer, r0, rc, cols, **geometry) for r0 in range(0, rows, rc)]
        self.queue = new + self.queue if front else self.queue + new
        self.left[key] = self.left.get(key, 0) + rows // rc
        if done is not None:
            self.on_done[key] = done

    def settle(self):
        todo, self.after = self.after, []
        for f in todo:
            f()

    def _pop(self, count):
        out = []
        for ch in self.queue[:count]:
            self.left[ch.key] -= 1
            if self.left[ch.key] == 0 and ch.key in self.on_done:
                self.after.append(self.on_done.pop(ch.key))
            out.append(ch)
        self.queue = self.queue[count:]
        return out

    def _side(self):
        riders, self.side = self.side[:SIDE_CHUNKS], self.side[SIDE_CHUNKS:]
        self.after += [then for _, then in riders if then is not None]
        return [ch for ch, _ in riders]

    def take(self, est_us, short_too=False):
        self.settle()
        if est_us < MIN_HOST_US and not short_too:
            return []
        out = self._side()
        budget, count = est_us * self.rate, 0
        while count < min(len(self.queue), HOST_CHUNKS) and self.queue[count].nbytes <= 2 * budget:
            budget -= self.queue[count].nbytes
            count += 1
        return out + self._pop(count)

    def flush(self, key, name):
        self.settle()
        if key is None:
            n = len(self.queue)
        else:
            n = max([i + 1 for i, ch in enumerate(self.queue) if ch.key == key], default=0)
        while n > 0:
            group = self._side() + self._pop(min(n, HOST_CHUNKS))
            n -= min(n, HOST_CHUNKS)
            _call(lambda: None, grid=(), in_specs=[], out_specs=(), out_shape=(), operands=[], name=name,
                  chunks=group, bufs=self.bufs)
            self.settle()

    def carry(self, est_us, name):
        _call(lambda: None, grid=(), in_specs=[], out_specs=(), out_shape=(), operands=[], name=name,
              chunks=self.take(est_us, short_too=True), bufs=self.bufs)

    def ride_out(self, name):
        self.settle()
        while self.side:
            _call(lambda: None, grid=(), in_specs=[], out_specs=(), out_shape=(), operands=[], name=name,
                  chunks=self._side(), bufs=self.bufs)
            self.settle()


def mm(a, b, *, mode="nn", out_dtype=F32, add=None, add_scale=1.0, slab_width=None, link=None, name):
    if mode == "nn":
        (M, K), N = a.shape, b.shape[-1]
    elif mode == "nt":
        (M, K), N = a.shape, b.shape[-2]
    else:
        (K, M), N = a.shape, b.shape[-1]
    tm, tk = _tile(M, TILE_M), _tile(K, TILE_K)
    tn = _tile(N, TILE_N) if slab_width is None else _tile(slab_width, SLAB_TILE_N)
    nk = K // tk
    dims = {"nn": (((1,), (0,)), ((), ())), "nt": (((1,), (1,)), ((), ())),
            "tn": (((0,), (0,)), ((), ()))}[mode]

    a_spec = (pl.BlockSpec((tk, tm), lambda i, j, k: (k, i)) if mode == "tn"
              else pl.BlockSpec((tm, tk), lambda i, j, k: (i, k)))
    b_spec = (pl.BlockSpec((tn, tk), lambda i, j, k: (j, k)) if mode == "nt"
              else pl.BlockSpec((tk, tn), lambda i, j, k: (k, j)))
    in_specs, operands = [a_spec, b_spec], [a, b]
    if add is not None:
        in_specs.append(pl.BlockSpec((tm, tn), lambda i, j, k: (i, j)))
        operands.append(add)
    if slab_width is None:
        out_shape = jax.ShapeDtypeStruct((M, N), out_dtype)
        out_spec = pl.BlockSpec((tm, tn), lambda i, j, k: (i, j))
    else:
        per = slab_width // tn
        out_shape = jax.ShapeDtypeStruct((N // slab_width, M, slab_width), out_dtype)
        out_spec = pl.BlockSpec((None, tm, tn), lambda i, j, k: (j // per, i, j % per))
    n_in = len(operands)

    def body(*refs):
        a_ref, b_ref = refs[0], refs[1]
        add_ref = refs[2] if add is not None else None
        o_ref = refs[n_in]
        k = pl.program_id(2)
        part = lax.dot_general(a_ref[...].astype(MXU_DTYPE), b_ref[...].astype(MXU_DTYPE),
                               dims, preferred_element_type=F32)

        def finish(r):
            if add_ref is not None:
                r = r + add_scale * add_ref[...].astype(F32)
            o_ref[...] = r.astype(o_ref.dtype)

        if nk == 1:
            finish(part)
        else:
            acc_ref = refs[n_in + 1]

            @pl.when(k == 0)
            def _():
                acc_ref[...] = part

            @pl.when(jnp.logical_and(k > 0, k < nk - 1))
            def _():
                acc_ref[...] += part

            @pl.when(k == nk - 1)
            def _():
                finish(acc_ref[...] + part)

    chunks = link.take(2.0 * M * N * K / MXU_FLOPS_PER_US) if link is not None else ()
    (out,) = _call(body, grid=(M // tm, N // tn, nk), in_specs=in_specs, out_specs=(out_spec,), out_shape=(out_shape,),
                   operands=operands, scratch=[pltpu.VMEM((tm, tn), F32)] if nk > 1 else [], name=name,
                   chunks=chunks, bufs=link.bufs if link is not None else None)
    return out


def ln_fwd(h, f, g, b, alpha, link, name):
    S, D = h.shape
    tm = _tile(S, ROW_TILE, 8)

    def body(h_ref, f_ref, g_ref, b_ref, y_ref, yb_ref, xhat_ref, rstd_ref):
        z = alpha * h_ref[...] + f_ref[...]
        mu = jnp.mean(z, axis=-1, keepdims=True)
        zc = z - mu
        var = jnp.mean(zc * zc, axis=-1, keepdims=True)
        rstd = lax.rsqrt(var + LN_EPS)
        xhat = zc * rstd
        y = xhat * g_ref[...] + b_ref[...]
        y_ref[...] = y
        yb_ref[...] = y.astype(yb_ref.dtype)
        xhat_ref[...] = xhat
        rstd_ref[...] = rstd

    row = pl.BlockSpec((tm, D), lambda i: (i, 0))
    vec = pl.BlockSpec((1, D), lambda i: (0, 0))
    return _call(
        body, grid=(S // tm,), in_specs=[row, row, vec, vec],
        out_specs=(row, row, row, pl.BlockSpec((tm, 1), lambda i: (i, 0))),
        out_shape=(jax.ShapeDtypeStruct((S, D), F32), jax.ShapeDtypeStruct((S, D), MXU_DTYPE),
                   jax.ShapeDtypeStruct((S, D), F32), jax.ShapeDtypeStruct((S, 1), F32)),
        operands=[h, f, g, b], name=name, chunks=link.take(ROWWISE_US["ln"][0]), bufs=link.bufs)


def ln_bwd(dy, xhat, rstd, g, link, name):
    S, D = dy.shape
    tm = _tile(S, ROW_TILE, 8)

    def body(dy_ref, xhat_ref, rstd_ref, g_ref, dz_ref, dzb_ref, dg_ref, db_ref):
        @pl.when(pl.program_id(0) == 0)
        def _():
            dg_ref[...] = jnp.zeros_like(dg_ref)
            db_ref[...] = jnp.zeros_like(db_ref)

        dy = dy_ref[...]
        xhat = xhat_ref[...]
        dxh = dy * g_ref[...]
        m1 = jnp.mean(dxh, axis=-1, keepdims=True)
        m2 = jnp.mean(dxh * xhat, axis=-1, keepdims=True)
        dz = rstd_ref[...] * (dxh - m1 - xhat * m2)
        dz_ref[...] = dz
        dzb_ref[...] = dz.astype(dzb_ref.dtype)
        dg_ref[...] += jnp.sum(dy * xhat, axis=0, keepdims=True)
        db_ref[...] += jnp.sum(dy, axis=0, keepdims=True)

    row = pl.BlockSpec((tm, D), lambda i: (i, 0))
    vec = pl.BlockSpec((1, D), lambda i: (0, 0))
    return _call(
        body, grid=(S // tm,), in_specs=[row, row, pl.BlockSpec((tm, 1), lambda i: (i, 0)), vec],
        out_specs=(row, row, vec, vec),
        out_shape=(jax.ShapeDtypeStruct((S, D), F32), jax.ShapeDtypeStruct((S, D), MXU_DTYPE),
                   jax.ShapeDtypeStruct((1, D), F32), jax.ShapeDtypeStruct((1, D), F32)),
        operands=[dy, xhat, rstd, g], name=name, chunks=link.take(ROWWISE_US["ln"][1]), bufs=link.bufs)


def rms_fwd(x, col, width, g, name):
    S = x.shape[0]
    tm = _tile(S, ROW_TILE, 8)

    def body(x_ref, g_ref, y_ref):
        xv = x_ref[...]
        r = lax.rsqrt(jnp.mean(xv * xv, axis=-1, keepdims=True) + RMS_EPS)
        y_ref[...] = (xv * r * g_ref[...]).astype(y_ref.dtype)

    return pl.pallas_call(
        body, out_shape=jax.ShapeDtypeStruct((S, width), MXU_DTYPE), grid=(S // tm,),
        in_specs=[pl.BlockSpec((tm, width), lambda i: (i, col)), pl.BlockSpec((1, width), lambda i: (0, 0))],
        out_specs=pl.BlockSpec((tm, width), lambda i: (i, 0)), compiler_params=_params(1), name=name)(x, g)


def rms_bwd(dy, x, col, width, g, name):
    S = x.shape[0]
    tm = _tile(S, ROW_TILE, 8)

    def body(dy_ref, x_ref, g_ref, dx_ref, dg_ref):
        @pl.when(pl.program_id(0) == 0)
        def _():
            dg_ref[...] = jnp.zeros_like(dg_ref)

        xv = x_ref[...]
        dy = dy_ref[...]
        r = lax.rsqrt(jnp.mean(xv * xv, axis=-1, keepdims=True) + RMS_EPS)
        dxn = dy * g_ref[...]
        m = jnp.mean(dxn * xv, axis=-1, keepdims=True)
        dx_ref[...] = r * (dxn - xv * (r * r * m))
        dg_ref[...] += jnp.sum(dy * xv * r, axis=0, keepdims=True)

    return pl.pallas_call(
        body, out_shape=(jax.ShapeDtypeStruct((S, width), F32), jax.ShapeDtypeStruct((1, width), F32)),
        grid=(S // tm,),
        in_specs=[pl.BlockSpec((tm, width), lambda i: (i, 0)), pl.BlockSpec((tm, width), lambda i: (i, col)),
                  pl.BlockSpec((1, width), lambda i: (0, 0))],
        out_specs=(pl.BlockSpec((tm, width), lambda i: (i, 0)), pl.BlockSpec((1, width), lambda i: (0, 0))),
        compiler_params=_params(1), name=name)(dy, x, g)


def ret_rope_fwd(proj, cos, sin, n_groups, half, k_from, k_scale, link, name):
    S = proj.shape[0]
    W = n_groups * 2 * half
    tm = _tile(S, ROW_TILE, 8)

    def body(x_ref, c_ref, s_ref, o_ref):
        c, s = c_ref[...], s_ref[...]
        for gi in range(n_groups):
            lo = gi * 2 * half
            x1 = x_ref[:, lo:lo + half]
            x2 = x_ref[:, lo + half:lo + 2 * half]
            sc = k_scale if gi >= k_from else 1.0
            o_ref[:, lo:lo + half] = ((x1 * c - x2 * s) * sc).astype(o_ref.dtype)
            o_ref[:, lo + half:lo + 2 * half] = ((x2 * c + x1 * s) * sc).astype(o_ref.dtype)

    tab = pl.BlockSpec((tm, half), lambda i: (i, 0))
    (out,) = _call(
        body, grid=(S // tm,), in_specs=[pl.BlockSpec((tm, W), lambda i: (i, 0)), tab, tab],
        out_specs=(pl.BlockSpec((tm, W), lambda i: (i, 0)),), out_shape=(jax.ShapeDtypeStruct((S, W), MXU_DTYPE),),
        operands=[proj, cos, sin], name=name, chunks=link.take(ROWWISE_US["rope"]), bufs=link.bufs)
    return out


def ret_rope_bwd(dq, dk, cos, sin, half, k_scale, name):
    S, Wq = dq.shape
    n_heads = Wq // (2 * half)
    tm = _tile(S, ROW_TILE, 8)

    def body(dq_ref, dk_ref, c_ref, s_ref, o_ref):
        c, s = c_ref[...], s_ref[...]
        for part, (d_ref, sc) in enumerate(((dq_ref, 1.0), (dk_ref, k_scale))):
            for hi in range(n_heads):
                lo = hi * 2 * half
                d1 = d_ref[:, lo:lo + half]
                d2 = d_ref[:, lo + half:lo + 2 * half]
                base = part * Wq + lo
                o_ref[:, base:base + half] = ((d1 * c + d2 * s) * sc).astype(o_ref.dtype)
                o_ref[:, base + half:base + 2 * half] = ((d2 * c - d1 * s) * sc).astype(o_ref.dtype)

    tab = pl.BlockSpec((tm, half), lambda i: (i, 0))
    row = pl.BlockSpec((tm, Wq), lambda i: (i, 0))
    return pl.pallas_call(
        body, out_shape=jax.ShapeDtypeStruct((S, 2 * Wq), MXU_DTYPE), grid=(S // tm,),
        in_specs=[row, row, tab, tab], out_specs=pl.BlockSpec((tm, 2 * Wq), lambda i: (i, 0)),
        compiler_params=_params(1), name=name)(dq, dk, cos, sin)


def mla_rope(x, ta, tb, n_heads, *, backward, head_sum, out_dtype, link=None, name):
    S = x.shape[0]
    W = n_heads * MLA_PAD
    tm = _tile(S, ROW_TILE, 8)

    def body(x_ref, a_ref, b_ref, o_ref, *rest):
        a, b = a_ref[...], b_ref[...]
        total = jnp.zeros((tm, LANES), F32)
        for hi in range(n_heads):
            lo = hi * MLA_PAD
            o_ref[:, lo:lo + MLA_NOPE] = x_ref[:, lo:lo + MLA_NOPE].astype(o_ref.dtype)
            t = x_ref[:, lo + MLA_NOPE:lo + MLA_PAD].astype(F32)
            if backward:
                r = t * a + pltpu.roll(t * b, LANES // 2, 1)
            else:
                r = t * a + pltpu.roll(t, LANES // 2, 1) * b
            o_ref[:, lo + MLA_NOPE:lo + MLA_PAD] = r.astype(o_ref.dtype)
            total = total + r
        if head_sum:
            rest[0][...] = total

    tab = pl.BlockSpec((tm, LANES), lambda i: (i, 0))
    row = pl.BlockSpec((tm, W), lambda i: (i, 0))
    out_shape = [jax.ShapeDtypeStruct((S, W), out_dtype)]
    out_specs = [row]
    if head_sum:
        out_shape.append(jax.ShapeDtypeStruct((S, LANES), F32))
        out_specs.append(tab)
    chunks = link.take(ROWWISE_US["rope"]) if link is not None else ()
    return _call(body, grid=(S // tm,), in_specs=[row, tab, tab], out_specs=out_specs, out_shape=out_shape,
                 operands=[x, ta, tb], name=name, chunks=chunks, bufs=link.bufs if link is not None else None)


def _visible(qi, kj, tq, tk, r0, rows):
    n = qi * tq + r0 + lax.broadcasted_iota(jnp.int32, (rows, tk), 0)
    m = kj * tk + lax.broadcasted_iota(jnp.int32, (rows, tk), 1)
    shift = CHUNK.bit_length() - 1
    vis = lax.shift_right_logical(m, shift) <= lax.shift_right_logical(n, shift)
    return vis, jnp.abs(n - m).astype(F32)


def attn_fwd(q, k, v, *, heads, softmax, masked, scale=1.0, log_gamma=None, link=None, est_us=0.0, name):
    (qa, dqk, q0, qs), (ka, _, k0, ks), (va, dv, v0, vs) = q, k, v
    Sq, Sk = qa.shape[0], ka.shape[0]
    tq, tk = _tile(Sq, ATTN_TILE, 8), _tile(Sk, ATTN_TILE, 8)
    nq, nk = Sq // tq, Sk // tk
    assert not masked or tq == tk
    sub = tq // ATTN_PARTS if tq % (16 * ATTN_PARTS) == 0 else tq

    def body(*refs):
        if softmax:
            q_ref, k_ref, v_ref, o_ref, ob_ref, lse_ref, m_ref, l_ref, acc_ref = refs
        else:
            q_ref, k_ref, v_ref, lg_ref, o_ref, acc_ref = refs
        qi, kj = pl.program_id(1), pl.program_id(2)

        @pl.when(kj == 0)
        def _():
            acc_ref[...] = jnp.zeros_like(acc_ref)
            if softmax:
                m_ref[...] = jnp.full_like(m_ref, NEG_INF)
                l_ref[...] = jnp.zeros_like(l_ref)

        def step(diagonal):
            kb = k_ref[...].astype(MXU_DTYPE)
            vb = v_ref[...].astype(MXU_DTYPE)
            parts = [pl.ds(r * sub, sub) for r in range(tq // sub)]
            scores = [lax.dot_general(q_ref[rows, :].astype(MXU_DTYPE), kb, (((1,), (1,)), ((), ())),
                                      preferred_element_type=F32) for rows in parts]
            for r, rows in enumerate(parts):
                s = scores[r]
                if diagonal or not softmax:
                    vis, dist = _visible(qi, kj, tq, tk, r * sub, sub)
                if softmax:
                    s = s * scale
                    if diagonal:
                        s = jnp.where(vis, s, NEG_INF)
                    m_old = m_ref[rows, :]
                    m_new = jnp.maximum(m_old, jnp.max(s, axis=-1, keepdims=True))
                    p = jnp.exp(s - m_new)
                    corr = jnp.exp(m_old - m_new)
                    l_ref[rows, :] = corr * l_ref[rows, :] + jnp.sum(p, axis=-1, keepdims=True)
                    acc_ref[rows, :] = (corr * acc_ref[rows, :]
                                        + jnp.dot(p.astype(MXU_DTYPE), vb, preferred_element_type=F32))
                    m_ref[rows, :] = m_new
                else:
                    decay = jnp.exp(lg_ref[0:1, 0:1] * dist)
                    if diagonal:
                        decay = jnp.where(vis, decay, 0.0)
                    acc_ref[rows, :] += jnp.dot((s * decay).astype(MXU_DTYPE), vb, preferred_element_type=F32)

        if masked:
            pl.when(kj < qi)(functools.partial(step, False))
            pl.when(kj == qi)(functools.partial(step, True))
        else:
            step(False)

        @pl.when(kj == nk - 1)
        def _():
            if softmax:
                out = acc_ref[...] / l_ref[...]
                o_ref[...] = out
                ob_ref[...] = out.astype(ob_ref.dtype)
                lse_ref[...] = m_ref[...] + jnp.log(l_ref[...])
            else:
                o_ref[...] = acc_ref[...].astype(o_ref.dtype)

    kcap = (lambda qi, kj: jnp.minimum(kj, qi)) if masked else (lambda qi, kj: kj)
    in_specs = [pl.BlockSpec((tq, dqk), lambda h, qi, kj: (qi, q0 + qs * h)),
                pl.BlockSpec((tk, dqk), lambda h, qi, kj: (kcap(qi, kj), k0 + ks * h)),
                pl.BlockSpec((tk, dv), lambda h, qi, kj: (kcap(qi, kj), v0 + vs * h))]
    operands = [qa, ka, va]
    out_shape = [jax.ShapeDtypeStruct((Sq, heads * dv), F32)]
    out_specs = [pl.BlockSpec((tq, dv), lambda h, qi, kj: (qi, h))]
    scratch = []
    if softmax:
        out_shape += [jax.ShapeDtypeStruct((Sq, heads * dv), MXU_DTYPE), jax.ShapeDtypeStruct((heads, Sq, 1), F32)]
        out_specs += [pl.BlockSpec((tq, dv), lambda h, qi, kj: (qi, h)),
                      pl.BlockSpec((None, tq, 1), lambda h, qi, kj: (h, qi, 0))]
        scratch += [pltpu.VMEM((tq, 1), F32), pltpu.VMEM((tq, 1), F32)]
    else:
        in_specs.append(pl.BlockSpec((None, 8, LANES), lambda h, qi, kj: (h, 0, 0)))
        operands.append(log_gamma)
    scratch.append(pltpu.VMEM((tq, dv), F32))
    chunks = link.take(est_us) if link is not None else ()
    return _call(body, grid=(heads, nq, nk), in_specs=in_specs, out_specs=out_specs, out_shape=out_shape,
                 operands=operands, scratch=scratch, name=name, chunks=chunks,
                 bufs=link.bufs if link is not None else None)


def attn_bwd(q, k, v, do, *, heads, softmax, masked, scale=1.0, log_gamma=None, o=None, lse=None,
             link=None, est_us=0.0, name):
    (qa, dqk, q0, qs), (ka, _, k0, ks), (va, dv, v0, vs) = q, k, v
    Sq, Sk = qa.shape[0], ka.shape[0]
    tq, tk = _tile(Sq, ATTN_TILE, 8), _tile(Sk, ATTN_TILE, 8)
    nq, nk = Sq // tq, Sk // tk
    assert not masked or tq == tk
    sub = tq // ATTN_PARTS if tq % (16 * ATTN_PARTS) == 0 else tq
    contract0 = (((0,), (0,)), ((), ()))

    def body(*refs):
        if softmax:
            q_ref, k_ref, v_ref, do_ref, o_ref, lse_ref, dq_ref, dk_ref, dv_ref = refs
        else:
            q_ref, k_ref, v_ref, do_ref, lg_ref, dq_ref, dk_ref, dv_ref = refs
        kj, qi = pl.program_id(1), pl.program_id(2)

        @pl.when(jnp.logical_and(kj == 0, qi == 0))
        def _():
            dq_ref[...] = jnp.zeros_like(dq_ref)

        @pl.when(qi == 0)
        def _():
            dk_ref[...] = jnp.zeros_like(dk_ref)
            dv_ref[...] = jnp.zeros_like(dv_ref)

        def step(diagonal):
            kb = k_ref[...].astype(MXU_DTYPE)
            vb = v_ref[...].astype(MXU_DTYPE)
            parts = [pl.ds(r * sub, sub) for r in range(tq // sub)]
            nt = (((1,), (1,)), ((), ()))
            qbs = [q_ref[rows, :].astype(MXU_DTYPE) for rows in parts]
            dobs = [do_ref[rows, :].astype(MXU_DTYPE) for rows in parts]
            scores = [lax.dot_general(qb, kb, nt, preferred_element_type=F32) for qb in qbs]
            dps = [lax.dot_general(dob, vb, nt, preferred_element_type=F32) for dob in dobs]
            dv_new, dk_new = dv_ref[...], dk_ref[...]
            for r, rows in enumerate(parts):
                s, dp, qb, dob = scores[r], dps[r], qbs[r], dobs[r]
                if diagonal or not softmax:
                    vis, dist = _visible(qi, kj, tq, tk, r * sub, sub)
                if softmax:
                    s = s * scale
                    if diagonal:
                        s = jnp.where(vis, s, NEG_INF)
                    p = jnp.exp(s - lse_ref[rows, :])
                    delta = jnp.sum(do_ref[rows, :].astype(F32) * o_ref[rows, :], axis=-1, keepdims=True)
                    ds = p * (dp - delta) * scale
                else:
                    decay = jnp.exp(lg_ref[0:1, 0:1] * dist)
                    if diagonal:
                        decay = jnp.where(vis, decay, 0.0)
                    p = s * decay
                    ds = dp * decay
                pb = p.astype(MXU_DTYPE)
                dsb = ds.astype(MXU_DTYPE)
                dv_new = dv_new + lax.dot_general(pb, dob, contract0, preferred_element_type=F32)
                dk_new = dk_new + lax.dot_general(dsb, qb, contract0, preferred_element_type=F32)
                out_rows = pl.ds(pl.multiple_of(qi * tq + r * sub, sub), sub)
                dq_ref[out_rows, :] += jnp.dot(dsb, kb, preferred_element_type=F32)
            dv_ref[...] = dv_new
            dk_ref[...] = dk_new

        if masked:
            pl.when(qi > kj)(functools.partial(step, False))
            pl.when(qi == kj)(functools.partial(step, True))
        else:
            step(False)

    qcap = (lambda kj, qi: jnp.maximum(qi, kj)) if masked else (lambda kj, qi: qi)
    in_specs = [pl.BlockSpec((tq, dqk), lambda h, kj, qi: (qcap(kj, qi), q0 + qs * h)),
                pl.BlockSpec((tk, dqk), lambda h, kj, qi: (kj, k0 + ks * h)),
                pl.BlockSpec((tk, dv), lambda h, kj, qi: (kj, v0 + vs * h)),
                pl.BlockSpec((tq, dv), lambda h, kj, qi: (qcap(kj, qi), h))]
    operands = [qa, ka, va, do]
    if softmax:
        in_specs += [pl.BlockSpec((tq, dv), lambda h, kj, qi: (qcap(kj, qi), h)),
                     pl.BlockSpec((None, tq, 1), lambda h, kj, qi: (h, qcap(kj, qi), 0))]
        operands += [o, lse]
    else:
        in_specs.append(pl.BlockSpec((None, 8, LANES), lambda h, kj, qi: (h, 0, 0)))
        operands.append(log_gamma)
    chunks = link.take(est_us) if link is not None else ()
    return _call(
        body, grid=(heads, nk, nq), in_specs=in_specs,
        out_specs=(pl.BlockSpec((Sq, dqk), lambda h, kj, qi: (0, h)),
                   pl.BlockSpec((tk, dqk), lambda h, kj, qi: (kj, h)),
                   pl.BlockSpec((tk, dv), lambda h, kj, qi: (kj, h))),
        out_shape=(jax.ShapeDtypeStruct((Sq, heads * dqk), F32), jax.ShapeDtypeStruct((Sk, heads * dqk), F32),
                   jax.ShapeDtypeStruct((Sk, heads * dv), F32)),
        operands=operands, name=name, chunks=chunks, bufs=link.bufs if link is not None else None)


def _sigmoid(x):
    return 0.5 * jnp.tanh(0.5 * x) + 0.5


def ret_gate_fwd(o, proj, gate_col, gn, heads, link, name):
    S, W = o.shape
    dv = W // heads
    tm = _tile(S, ROW_TILE // 2, 8)

    def body(o_ref, g_ref, gn_ref, y_ref):
        for hi in range(heads):
            cols = slice(hi * dv, (hi + 1) * dv)
            oh = o_ref[:, cols]
            mu = jnp.mean(oh, axis=-1, keepdims=True)
            oc = oh - mu
            rstd = lax.rsqrt(jnp.mean(oc * oc, axis=-1, keepdims=True) + LN_EPS)
            gt = g_ref[:, cols]
            y_ref[:, cols] = (gt * _sigmoid(gt) * (oc * rstd) * gn_ref[:, cols]).astype(y_ref.dtype)

    row = pl.BlockSpec((tm, W), lambda i: (i, 0))
    (y,) = _call(
        body, grid=(S // tm,),
        in_specs=[row, pl.BlockSpec((tm, W), lambda i: (i, gate_col)), pl.BlockSpec((1, W), lambda i: (0, 0))],
        out_specs=(row,), out_shape=(jax.ShapeDtypeStruct((S, W), MXU_DTYPE),), operands=[o, proj, gn], name=name,
        chunks=link.take(ROWWISE_US["gate"]), bufs=link.bufs)
    return y


def ret_gate_bwd(dy, o, proj, gate_col, gn, heads, name):
    S, W = o.shape
    dv = W // heads
    tm = _tile(S, ROW_TILE // 2, 8)

    def body(dy_ref, o_ref, g_ref, gn_ref, do_ref, dgt_ref, dgn_ref):
        @pl.when(pl.program_id(0) == 0)
        def _():
            dgn_ref[...] = jnp.zeros_like(dgn_ref)

        for hi in range(heads):
            cols = slice(hi * dv, (hi + 1) * dv)
            oh = o_ref[:, cols]
            mu = jnp.mean(oh, axis=-1, keepdims=True)
            oc = oh - mu
            rstd = lax.rsqrt(jnp.mean(oc * oc, axis=-1, keepdims=True) + LN_EPS)
            xhat = oc * rstd
            gt = g_ref[:, cols]
            sg = _sigmoid(gt)
            gain = gn_ref[:, cols]
            dy = dy_ref[:, cols]
            dgt_ref[:, cols] = (dy * xhat * gain * (sg * (1.0 + gt * (1.0 - sg)))).astype(dgt_ref.dtype)
            dn = dy * (gt * sg)
            dgn_ref[:, cols] += jnp.sum(dn * xhat, axis=0, keepdims=True)
            dxh = dn * gain
            m1 = jnp.mean(dxh, axis=-1, keepdims=True)
            m2 = jnp.mean(dxh * xhat, axis=-1, keepdims=True)
            do_ref[:, cols] = (rstd * (dxh - m1 - xhat * m2)).astype(do_ref.dtype)

    row = pl.BlockSpec((tm, W), lambda i: (i, 0))
    vec = pl.BlockSpec((1, W), lambda i: (0, 0))
    return pl.pallas_call(
        body,
        out_shape=(jax.ShapeDtypeStruct((S, W), MXU_DTYPE), jax.ShapeDtypeStruct((S, W), MXU_DTYPE),
                   jax.ShapeDtypeStruct((1, W), F32)),
        grid=(S // tm,), in_specs=[row, row, pl.BlockSpec((tm, W), lambda i: (i, gate_col)), vec],
        out_specs=(row, row, vec), compiler_params=_params(1), name=name)(dy, o, proj, gn)


def _shift_down(x, s):
    rows = lax.broadcasted_iota(jnp.int32, x.shape, 0)
    return jnp.where(rows >= s, pltpu.roll(x, s, 0), 0.0)


def _shift_up(x, s):
    n = x.shape[0]
    rows = lax.broadcasted_iota(jnp.int32, x.shape, 0)
    return jnp.where(rows < n - s, pltpu.roll(x, n - s, 0), 0.0)


def _conv3(x, w_ref, b_ref):
    return (w_ref[2:3, :] * x + w_ref[1:2, :] * _shift_down(x, 1) + w_ref[0:1, :] * _shift_down(x, 2)
            + b_ref[...])


def conv_glu_fwd(hup, w, b, link, name):
    S, W2 = hup.shape
    F = W2 // 2
    tc = _tile(F, LANES)
    nb = F // tc

    def body(g_ref, v_ref, wg_ref, wv_ref, bg_ref, bv_ref, u_ref):
        cg = _conv3(g_ref[...], wg_ref, bg_ref)
        cv = _conv3(v_ref[...], wv_ref, bv_ref)
        u_ref[...] = (cg * _sigmoid(cg) * cv).astype(u_ref.dtype)

    def col(rows, off):
        return pl.BlockSpec((rows, tc), lambda j: (0, j + off))

    (u,) = _call(
        body, grid=(nb,), in_specs=[col(S, 0), col(S, nb), col(3, 0), col(3, nb), col(1, 0), col(1, nb)],
        out_specs=(col(S, 0),), out_shape=(jax.ShapeDtypeStruct((S, F), MXU_DTYPE),),
        operands=[hup, hup, w, w, b, b], name=name, chunks=link.take(ROWWISE_US["conv"][0]), bufs=link.bufs)
    return u


def conv_glu_bwd(du, hup, w, b, link, name):
    S, W2 = hup.shape
    F = W2 // 2
    tc = _tile(F, LANES)
    nb = F // tc

    def back(dc, x, w_ref, dh_ref, dw_ref, db_ref):
        dw_ref[2:3, :] = jnp.sum(dc * x, axis=0, keepdims=True)
        dw_ref[1:2, :] = jnp.sum(dc * _shift_down(x, 1), axis=0, keepdims=True)
        dw_ref[0:1, :] = jnp.sum(dc * _shift_down(x, 2), axis=0, keepdims=True)
        db_ref[...] = jnp.sum(dc, axis=0, keepdims=True)
        dh_ref[...] = (w_ref[2:3, :] * dc + w_ref[1:2, :] * _shift_up(dc, 1)
                       + w_ref[0:1, :] * _shift_up(dc, 2)).astype(dh_ref.dtype)

    def body(du_ref, g_ref, v_ref, wg_ref, wv_ref, bg_ref, bv_ref,
             dhg_ref, dhv_ref, dwg_ref, dwv_ref, dbg_ref, dbv_ref):
        xg, xv = g_ref[...], v_ref[...]
        cg = _conv3(xg, wg_ref, bg_ref)
        cv = _conv3(xv, wv_ref, bv_ref)
        sg = _sigmoid(cg)
        du = du_ref[...]
        back(du * cv * (sg * (1.0 + cg * (1.0 - sg))), xg, wg_ref, dhg_ref, dwg_ref, dbg_ref)
        back(du * (cg * sg), xv, wv_ref, dhv_ref, dwv_ref, dbv_ref)

    def col(rows, off):
        return pl.BlockSpec((rows, tc), lambda j: (0, j + off))

    return _call(
        body, grid=(nb,),
        in_specs=[col(S, 0), col(S, 0), col(S, nb), col(3, 0), col(3, nb), col(1, 0), col(1, nb)],
        out_specs=(col(S, 0), col(S, 0), col(3, 0), col(3, 0), col(1, 0), col(1, 0)),
        out_shape=(jax.ShapeDtypeStruct((S, F), MXU_DTYPE), jax.ShapeDtypeStruct((S, F), MXU_DTYPE),
                   jax.ShapeDtypeStruct((3, F), F32), jax.ShapeDtypeStruct((3, F), F32),
                   jax.ShapeDtypeStruct((1, F), F32), jax.ShapeDtypeStruct((1, F), F32)),
        operands=[du, hup, hup, w, w, b, b], name=name, chunks=link.take(ROWWISE_US["conv"][1]), bufs=link.bufs)


def loss_head(y, target, name):
    S, D = y.shape
    tm = _tile(S, ROW_TILE, 8)

    def body(y_ref, t_ref, dy_ref, loss_ref):
        @pl.when(pl.program_id(0) == 0)
        def _():
            loss_ref[...] = jnp.zeros_like(loss_ref)

        e = y_ref[...] - t_ref[...]
        dy_ref[...] = e * (1.0 / D)
        part = jnp.sum(jnp.sum(e * e, axis=-1, keepdims=True), axis=0, keepdims=True) * (0.5 / D)
        loss_ref[...] += jnp.broadcast_to(part, loss_ref.shape)

    row = pl.BlockSpec((tm, D), lambda i: (i, 0))
    return pl.pallas_call(
        body, out_shape=(jax.ShapeDtypeStruct((S, D), F32), jax.ShapeDtypeStruct((8, LANES), F32)),
        grid=(S // tm,), in_specs=[row, row], out_specs=(row, pl.BlockSpec((8, LANES), lambda i: (0, 0))),
        compiler_params=_params(1), name=name)(y, target)


def adam_update(parts, w, m, v, name):
    n_layers, n_parts, R, C = parts.shape
    tr = _tile(R, max(8, (1 << 19) // C), 8)
    c1 = 1.0 - ADAM_B1 ** ADAM_STEP
    c2 = 1.0 - ADAM_B2 ** ADAM_STEP

    def body(p_ref, w_ref, m_ref, v_ref, g_ref, d_ref, nm_ref, nv_ref):
        g = p_ref[0].astype(F32)
        for pi in range(1, n_parts):
            g = g + p_ref[pi].astype(F32)
        nm = ADAM_B1 * m_ref[...] + (1.0 - ADAM_B1) * g
        nv = ADAM_B2 * v_ref[...] + (1.0 - ADAM_B2) * (g * g)
        g_ref[...] = g
        nm_ref[...] = nm
        nv_ref[...] = nv
        d_ref[...] = -ADAM_LR * ((nm / c1) / (jnp.sqrt(nv / c2) + ADAM_EPS) + ADAM_WD * w_ref[...])

    row = pl.BlockSpec((None, tr, C), lambda l, i: (l, i, 0))
    out = jax.ShapeDtypeStruct((n_layers, R, C), F32)
    return _call(
        body, grid=(n_layers, R // tr),
        in_specs=[pl.BlockSpec((None, n_parts, tr, C), lambda l, i: (l, 0, i, 0)), row, row, row],
        out_specs=(row, row, row, row), out_shape=(out, out, out, out), operands=[parts, w, m, v], name=name)


def pair_sum(own, got, core, name):
    _, R, C = own.shape
    tr = _tile(R, max(16, (1 << 19) // C), 16)

    def body(core_ref, a_ref, b_ref, o_ref):
        o_ref[...] = (a_ref[...].astype(F32) + b_ref[...].astype(F32)).astype(o_ref.dtype)

    grid_spec = pltpu.PrefetchScalarGridSpec(
        num_scalar_prefetch=1, grid=(N_CHIP, R // tr),
        in_specs=[pl.BlockSpec((None, tr, C), lambda j, i, core_ref: (2 * j + core_ref[0], i, 0)),
                  pl.BlockSpec((None, tr, C), lambda j, i, core_ref: (j, i, 0))],
        out_specs=pl.BlockSpec((None, tr, C), lambda j, i, core_ref: (j, i, 0)))
    return pl.pallas_call(
        body, out_shape=jax.ShapeDtypeStruct((N_CHIP, R, C), own.dtype), grid_spec=grid_spec,
        compiler_params=_params(2), name=name)(core, own, got)


def gather_small(block, name):
    m_per, n = block.shape

    def body(x_ref, out_ref, send_sems, recv_sems, local_sem):
        x, y, c = _place()
        me, sibling = (x, y, c), (x, y, 1 - c)
        chips = _other_chips(x, y)

        def rows(px, py, pc):
            return out_ref.at[pl.ds((4 * px + 2 * py + pc) * m_per, m_per), :]

        def copy(k, blk, to, src=None):
            return pltpu.make_async_remote_copy(
                src_ref=rows(*blk) if src is None else src, dst_ref=rows(*blk),
                send_sem=send_sems.at[k], recv_sem=recv_sems.at[k], device_id=to, device_id_type=MESH)

        mine = pltpu.make_async_copy(x_ref, rows(*me), local_sem)
        mine.start()
        first = [copy(0, me, sibling, src=x_ref)]
        first += [copy(1 + j, me, (*chip, c), src=x_ref) for j, chip in enumerate(chips)]
        for cp in first:
            cp.start()
        passed = [copy(4 + j, (*chip, c), sibling) for j, chip in enumerate(chips)]
        for j, chip in enumerate(chips):
            copy(1 + j, (*chip, c), me).wait_recv()
            passed[j].start()
        copy(0, sibling, me).wait_recv()
        for j, chip in enumerate(chips):
            copy(4 + j, (*chip, 1 - c), me).wait_recv()
        for cp in first + passed:
            cp.wait_send()
        mine.wait()

    return pl.pallas_call(
        body, out_shape=jax.ShapeDtypeStruct((N_DEV * m_per, n), block.dtype),
        in_specs=[pl.BlockSpec(memory_space=pltpu.VMEM)], out_specs=pl.BlockSpec(memory_space=pltpu.VMEM),
        scratch_shapes=[pltpu.SemaphoreType.DMA((7,)), pltpu.SemaphoreType.DMA((7,)), pltpu.SemaphoreType.DMA],
        compiler_params=pltpu.CompilerParams(vmem_limit_bytes=VMEM_LIMIT_BYTES), name=name)(block)


def _rope_tables(positions, d):
    inv_freq = ROPE_BASE ** (-jnp.arange(0, d, 2, dtype=F32) / d)
    ang = positions.astype(F32)[:, None] * inv_freq
    return jnp.cos(ang), jnp.sin(ang)


def _mla_pad(nope, rope):
    S, H, _ = nope.shape
    half = MLA_ROPE // 2
    z = jnp.zeros((S, H, LANES // 2 - half), nope.dtype)
    return jnp.concatenate([nope, rope[..., :half], z, rope[..., half:], z], axis=2).reshape(S, H * MLA_PAD)


def _mla_unpad(x, H):
    S = x.shape[0]
    half = MLA_ROPE // 2
    x3 = x.reshape(S, H, MLA_PAD)
    rope = jnp.concatenate([x3[..., MLA_NOPE:MLA_NOPE + half],
                            x3[..., MLA_NOPE + LANES // 2:MLA_NOPE + LANES // 2 + half]], axis=2)
    return x3[..., :MLA_NOPE], rope


def kernel(x, mem, positions, ret_w_in, ret_gn_g, ret_w_out, mla_w_in, mla_q_norm_g, mla_w_uq, mla_kv_norm_g, mla_w_ukv, mla_w_out, xa_w_q, xa_w_kv, xa_w_out, ffn_w_up, ffn_conv_w, ffn_conv_b, ffn_w_down, ln_mix_g, ln_mix_b, ln_mem_g, ln_mem_b, ln_ffn_g, ln_ffn_b, loss_target, m_ret_w_in, m_ret_gn_g, m_ret_w_out, m_mla_w_in, m_mla_q_norm_g, m_mla_w_uq, m_mla_kv_norm_g, m_mla_w_ukv, m_mla_w_out, m_xa_w_q, m_xa_w_kv, m_xa_w_out, m_ffn_w_up, m_ffn_conv_w, m_ffn_conv_b, m_ffn_w_down, m_ln_mix_g, m_ln_mix_b, m_ln_mem_g, m_ln_mem_b, m_ln_ffn_g, m_ln_ffn_b, v_ret_w_in, v_ret_gn_g, v_ret_w_out, v_mla_w_in, v_mla_q_norm_g, v_mla_w_uq, v_mla_kv_norm_g, v_mla_w_ukv, v_mla_w_out, v_xa_w_q, v_xa_w_kv, v_xa_w_out, v_ffn_w_up, v_ffn_conv_w, v_ffn_conv_b, v_ffn_w_down, v_ln_mix_g, v_ln_mix_b, v_ln_mem_g, v_ln_mem_b, v_ln_ffn_g, v_ln_ffn_b):
    weights = dict(ret_w_in=ret_w_in, ret_gn_g=ret_gn_g, ret_w_out=ret_w_out, mla_w_in=mla_w_in,
                   mla_q_norm_g=mla_q_norm_g, mla_w_uq=mla_w_uq, mla_kv_norm_g=mla_kv_norm_g,
                   mla_w_ukv=mla_w_ukv, mla_w_out=mla_w_out, xa_w_q=xa_w_q, xa_w_kv=xa_w_kv, xa_w_out=xa_w_out,
                   ffn_w_up=ffn_w_up, ffn_conv_w=ffn_conv_w, ffn_conv_b=ffn_conv_b, ffn_w_down=ffn_w_down,
                   ln_mix_g=ln_mix_g, ln_mix_b=ln_mix_b, ln_mem_g=ln_mem_g, ln_mem_b=ln_mem_b,
                   ln_ffn_g=ln_ffn_g, ln_ffn_b=ln_ffn_b)
    mom_m = dict(ret_w_in=m_ret_w_in, ret_gn_g=m_ret_gn_g, ret_w_out=m_ret_w_out, mla_w_in=m_mla_w_in,
                 mla_q_norm_g=m_mla_q_norm_g, mla_w_uq=m_mla_w_uq, mla_kv_norm_g=m_mla_kv_norm_g,
                 mla_w_ukv=m_mla_w_ukv, mla_w_out=m_mla_w_out, xa_w_q=m_xa_w_q, xa_w_kv=m_xa_w_kv,
                 xa_w_out=m_xa_w_out, ffn_w_up=m_ffn_w_up, ffn_conv_w=m_ffn_conv_w, ffn_conv_b=m_ffn_conv_b,
                 ffn_w_down=m_ffn_w_down, ln_mix_g=m_ln_mix_g, ln_mix_b=m_ln_mix_b, ln_mem_g=m_ln_mem_g,
                 ln_mem_b=m_ln_mem_b, ln_ffn_g=m_ln_ffn_g, ln_ffn_b=m_ln_ffn_b)
    mom_v = dict(ret_w_in=v_ret_w_in, ret_gn_g=v_ret_gn_g, ret_w_out=v_ret_w_out, mla_w_in=v_mla_w_in,
                 mla_q_norm_g=v_mla_q_norm_g, mla_w_uq=v_mla_w_uq, mla_kv_norm_g=v_mla_kv_norm_g,
                 mla_w_ukv=v_mla_w_ukv, mla_w_out=v_mla_w_out, xa_w_q=v_xa_w_q, xa_w_kv=v_xa_w_kv,
                 xa_w_out=v_xa_w_out, ffn_w_up=v_ffn_w_up, ffn_conv_w=v_ffn_conv_w, ffn_conv_b=v_ffn_conv_b,
                 ffn_w_down=v_ffn_w_down, ln_mix_g=v_ln_mix_g, ln_mix_b=v_ln_mix_b, ln_mem_g=v_ln_mem_g,
                 ln_mem_b=v_ln_mem_b, ln_ffn_g=v_ln_ffn_g, ln_ffn_b=v_ln_ffn_b)
    order = list(weights)
    BIG = dict(ret_w_in=2, ret_w_out=1, mla_w_in=1, mla_w_uq=2, mla_w_ukv=2, mla_w_out=1,
               xa_w_q=1, xa_w_kv=2, xa_w_out=1, ffn_w_up=2, ffn_w_down=1)
    SMALL_CUT = ("ffn_conv_w", "mla_q_norm_g", "mla_kv_norm_g")
    REPLICATED = ("ret_gn_g", "ffn_conv_b", "ln_mix_g", "ln_mix_b", "ln_mem_g", "ln_mem_b", "ln_ffn_g", "ln_ffn_b")

    x = x[0]
    mem = mem[0]
    positions = positions[0]
    target = loss_target[0]
    S, D = x.shape
    depth = ln_mix_g.shape[0]
    alpha = (2 * depth) ** 0.25
    ret_dk = D // RET_HEADS
    ret_dv = 2 * D // RET_HEADS
    ret_qkw = RET_HEADS * ret_dk
    ret_vw = RET_HEADS * ret_dv
    xa_d = D // XA_HEADS
    assert 2 * ret_qkw == ret_vw and MLA_NOPE == LANES and MLA_V == LANES and MLA_ROPE == LANES // 2
    assert MLA_Q_RANK == MLA_KV_RANK and CHUNK & (CHUNK - 1) == 0
    core = lax.axis_index("c").astype(jnp.int32).reshape(1)
    dev = 4 * lax.axis_index("x") + 2 * lax.axis_index("y") + lax.axis_index("c")

    gather, scatter = _Link(GATHER_BYTES_PER_US), _Link(SCATTER_BYTES_PER_US)
    shard_b = {n: weights[n].astype(PAY_DTYPE) for n in BIG}
    staged = set()

    def units(layer):
        mixer = ("ret_w_in", "ret_w_out") if layer % 2 == 0 else ("mla_w_in", "mla_w_uq", "mla_w_ukv", "mla_w_out")
        return [(n, layer // 2) for n in mixer] + [(n, layer) for n in
                                                    ("xa_w_q", "xa_w_kv", "xa_w_out", "ffn_w_up", "ffn_w_down")]

    def geometry(n):
        _, k, nn = weights[n].shape
        return BIG[n], k, nn

    def after_first_level(key, full, axis, k, nn):
        def to_sibling():
            chunk = _Chunk("gather2", None, key, full, 0, 0, k, nn, axis=axis, k=k, n=nn)
            gather.side.append((chunk, lambda: staged.add(key)))

        def relay():
            gather.push("relay", None, key, full, 0, k // 2, nn, done=to_sibling, front=True,
                        axis=axis, k=k, n=nn, half=k // 2)
        return relay

    for layer in range(depth):
        for n, j in units(layer):
            axis, k, nn = geometry(n)
            full = (N_DEV * k, nn) if axis == 1 else (k, N_DEV * nn)
            gather.push("gather", shard_b[n], (n, j), full, j, k, nn,
                        done=after_first_level((n, j), full, axis, k, nn), axis=axis, k=k, n=nn)

    def weight(n, j):
        key = (n, j)
        while key not in staged:
            gather.flush(key, name="gather_rest")
            gather.settle()
            if not any(ch.key == key for ch in gather.queue):
                break
        while key not in staged:
            assert gather.side
            gather.carry(LEVEL2_US, name="gather_level2")
            gather.settle()
        return gather.bufs[key]

    def pack(arrays):
        flat = jnp.concatenate([a.reshape(-1) for a in arrays])
        rows = -(-flat.shape[0] // (8 * LANES)) * 8
        return jnp.pad(flat, (0, rows * LANES - flat.shape[0])).reshape(rows, LANES)

    def unpack(flat, like):
        out, at = [], 0
        for a in like:
            out.append(flat[at:at + a.size].reshape(a.shape))
            at += a.size
        return out

    small_local = [weights[n] for n in SMALL_CUT]
    blk = pack(small_local)
    allsmall = gather_small(blk, name="gather_small_weights").reshape(N_DEV, -1)
    per_dev = [unpack(allsmall[d], small_local) for d in range(N_DEV)]
    conv_w_full, qg_full, kvg_full = [jnp.concatenate([per_dev[d][i] for d in range(N_DEV)], axis=-1)
                                      for i in range(len(SMALL_CUT))]

    rcos, rsin = _rope_tables(positions, ret_dk)
    mcos, msin = _rope_tables(positions, MLA_ROPE)
    zq = jnp.zeros((S, LANES // 2 - MLA_ROPE // 2), F32)
    mla_ta = jnp.concatenate([mcos, zq, mcos, zq], axis=1)
    mla_tb = jnp.concatenate([-msin, zq, msin, zq], axis=1)
    log_gamma = jnp.log(1.0 - 2.0 ** (-5.0 - jnp.arange(RET_HEADS, dtype=F32)))
    log_gamma = jnp.broadcast_to(log_gamma[:, None, None], (RET_HEADS, 8, LANES))
    mla_scale = (MLA_NOPE + MLA_ROPE) ** -0.5
    xa_scale = xa_d ** -0.5
    mem_b = mem.astype(MXU_DTYPE)

    def vec(a, l):
        return a[l][None, :]

    saved = []
    h = x
    hb = x.astype(MXU_DTYPE)
    for layer in range(depth):
        j = layer // 2
        sv = {}
        sv["hb_mix"] = hb
        if layer % 2 == 0:
            proj = mm(hb, weight("ret_w_in", j), link=gather, name="ret_in")
            qk = ret_rope_fwd(proj, rcos, rsin, 2 * RET_HEADS, ret_dk // 2, RET_HEADS, ret_dk ** -0.5, gather,
                              name="ret_rope")
            (o,) = attn_fwd((qk, ret_dk, 0, 1), (qk, ret_dk, RET_HEADS, 1), (proj, ret_dv, RET_HEADS, 1),
                            heads=RET_HEADS, softmax=False, masked=True, log_gamma=log_gamma, link=gather,
                            est_us=ATTN_US["ret"][0], name="ret_attn")
            yb = ret_gate_fwd(o, proj, 2, vec(ret_gn_g, j), RET_HEADS, gather, name="ret_gate")
            mix = mm(yb, weight("ret_w_out", j), link=gather, name="ret_out")
            sv.update(proj=proj, qk=qk, o=o, yb=yb)
        else:
            proj = mm(hb, weight("mla_w_in", j), link=gather, name="mla_in")
            cq = rms_fwd(proj, 0, MLA_Q_RANK, vec(qg_full, j), name="mla_q_norm")
            ckv = rms_fwd(proj, 1, MLA_KV_RANK, vec(kvg_full, j), name="mla_kv_norm")
            qf = mm(cq, weight("mla_w_uq", j), link=gather, name="mla_uq")
            kvf = mm(ckv, weight("mla_w_ukv", j), link=gather, name="mla_ukv")
            q3 = qf.reshape(S, MLA_HEADS, MLA_NOPE + MLA_ROPE)
            kv3 = kvf.reshape(S, MLA_HEADS, MLA_NOPE + MLA_V)
            k_rope = jnp.broadcast_to(proj[:, None, MLA_Q_RANK + MLA_KV_RANK:], (S, MLA_HEADS, MLA_ROPE))
            q_pad = _mla_pad(q3[..., :MLA_NOPE], q3[..., MLA_NOPE:])
            k_pad = _mla_pad(kv3[..., :MLA_NOPE], k_rope)
            (qr,) = mla_rope(q_pad, mla_ta, mla_tb, MLA_HEADS, backward=False, head_sum=False,
                             out_dtype=MXU_DTYPE, link=gather, name="mla_rope_q")
            (kr,) = mla_rope(k_pad, mla_ta, mla_tb, MLA_HEADS, backward=False, head_sum=False,
                             out_dtype=MXU_DTYPE, link=gather, name="mla_rope_k")
            o, ob, lse = attn_fwd((qr, MLA_PAD, 0, 1), (kr, MLA_PAD, 0, 1), (kvf, MLA_V, 1, 2), heads=MLA_HEADS,
                              softmax=True, masked=True, scale=mla_scale, link=gather, est_us=ATTN_US["mla"][0],
                              name="mla_attn")
            mix = mm(ob, weight("mla_w_out", j), link=gather, name="mla_out")
            sv.update(proj=proj, cq=cq, ckv=ckv, kvf=kvf, qr=qr, kr=kr, o=o, ob=ob, lse=lse)
        h, hb, sv["xhat_mix"], sv["rstd_mix"] = ln_fwd(h, mix, vec(ln_mix_g, layer), vec(ln_mix_b, layer), alpha,
                                                       gather, name="ln_mix")
        sv["hb_mem"] = hb
        q = mm(hb, weight("xa_w_q", layer), link=gather, name="xa_q")
        kvm = mm(mem_b, weight("xa_w_kv", layer), link=gather, name="xa_kv")
        o, ob, lse = attn_fwd((q, xa_d, 0, 1), (kvm, xa_d, 0, 1), (kvm, xa_d, XA_HEADS, 1), heads=XA_HEADS,
                          softmax=True, masked=False, scale=xa_scale, link=gather, est_us=ATTN_US["xa"][0],
                          name="xa_attn")
        mix = mm(ob, weight("xa_w_out", layer), link=gather, name="xa_out")
        sv.update(xa_q=q, xa_kvm=kvm, xa_o=o, xa_ob=ob, xa_lse=lse)
        h, hb, sv["xhat_mem"], sv["rstd_mem"] = ln_fwd(h, mix, vec(ln_mem_g, layer), vec(ln_mem_b, layer), alpha,
                                                       gather, name="ln_mem")
        sv["hb_ffn"] = hb
        hup = mm(hb, weight("ffn_w_up", layer), link=gather, name="ffn_up")
        u = conv_glu_fwd(hup, conv_w_full[layer], vec(ffn_conv_b, layer), gather, name="ffn_conv")
        mix = mm(u, weight("ffn_w_down", layer), link=gather, name="ffn_down")
        sv.update(hup=hup, u=u)
        h, hb, sv["xhat_ffn"], sv["rstd_ffn"] = ln_fwd(h, mix, vec(ln_ffn_g, layer), vec(ln_ffn_b, layer), alpha,
                                                       gather, name="ln_ffn")
        saved.append(sv)

    dh, loss_blk = loss_head(h, target, name="loss_head")
    loss = lax.psum(loss_blk[0, 0], ("x", "y", "c"))

    small = {n: [None] * weights[n].shape[0] for n in REPLICATED + SMALL_CUT}
    W = gather.bufs

    def wgrad(n, a, d, l, tag):
        axis, k, nn = geometry(n)
        if axis == 2:
            slabs = mm(a, d, mode="tn", out_dtype=PAY_DTYPE, slab_width=nn, link=scatter, name=tag)
        else:
            slabs = mm(a, d, mode="tn", out_dtype=PAY_DTYPE, link=scatter, name=tag).reshape(N_DEV, k, nn)
        n_layers = weights[n].shape[0]
        pair = ("pair", n, l)

        def then():
            sums = pair_sum(slabs, scatter.bufs.pop(pair), core, name="pairsum_" + n)
            scatter.push("scatter", sums, n, (n_layers, N_CHIP, k, nn), l, k, nn)

        scatter.side.append((_Chunk("swap", slabs, pair, (N_CHIP, k, nn), 0, 0, k, nn), then))

    for layer in reversed(range(depth)):
        j = layer // 2
        sv = saved[layer]
        dz, dzb, dg, db = ln_bwd(dh, sv["xhat_ffn"], sv["rstd_ffn"], vec(ln_ffn_g, layer), scatter, name="ln_ffn_bwd")
        small["ln_ffn_g"][layer], small["ln_ffn_b"][layer] = dg[0], db[0]
        wgrad("ffn_w_down", sv["u"], dzb, layer, "ffn_down_dw")
        du = mm(dzb, W[("ffn_w_down", layer)], mode="nt", link=scatter, name="ffn_down_dx")
        dhg, dhv, dwg, dwv, dbg, dbv = conv_glu_bwd(du, sv["hup"], conv_w_full[layer], vec(ffn_conv_b, layer),
                                                    scatter, name="ffn_conv_bwd")
        small["ffn_conv_w"][layer] = jnp.concatenate([dwg, dwv], axis=1)
        small["ffn_conv_b"][layer] = jnp.concatenate([dbg, dbv], axis=1)[0]
        dhup = jnp.concatenate([dhg, dhv], axis=1)
        wgrad("ffn_w_up", sv["hb_ffn"], dhup, layer, "ffn_up_dw")
        dh = mm(dhup, W[("ffn_w_up", layer)], mode="nt", add=dz, add_scale=alpha, link=scatter, name="ffn_up_dx")
        dz, dzb, dg, db = ln_bwd(dh, sv["xhat_mem"], sv["rstd_mem"], vec(ln_mem_g, layer), scatter, name="ln_mem_bwd")
        small["ln_mem_g"][layer], small["ln_mem_b"][layer] = dg[0], db[0]
        wgrad("xa_w_out", sv["xa_ob"], dzb, layer, "xa_out_dw")
        do = mm(dzb, W[("xa_w_out", layer)], mode="nt", link=scatter, name="xa_out_dx")
        dq, dk, dv = attn_bwd((sv["xa_q"], xa_d, 0, 1), (sv["xa_kvm"], xa_d, 0, 1), (sv["xa_kvm"], xa_d, XA_HEADS, 1),
                              do, heads=XA_HEADS, softmax=True, masked=False, scale=xa_scale, o=sv["xa_o"],
                              lse=sv["xa_lse"], link=scatter, est_us=ATTN_US["xa"][1], name="xa_attn_bwd")
        dqb = dq.astype(MXU_DTYPE)
        wgrad("xa_w_kv", mem_b, jnp.concatenate([dk, dv], axis=1).astype(MXU_DTYPE), layer, "xa_kv_dw")
        wgrad("xa_w_q", sv["hb_mem"], dqb, layer, "xa_q_dw")
        dh = mm(dqb, W[("xa_w_q", layer)], mode="nt", add=dz, add_scale=alpha, link=scatter, name="xa_q_dx")
        dz, dzb, dg, db = ln_bwd(dh, sv["xhat_mix"], sv["rstd_mix"], vec(ln_mix_g, layer), scatter, name="ln_mix_bwd")
        small["ln_mix_g"][layer], small["ln_mix_b"][layer] = dg[0], db[0]
        if layer % 2 == 0:
            wgrad("ret_w_out", sv["yb"], dzb, j, "ret_out_dw")
            dy = mm(dzb, W[("ret_w_out", j)], mode="nt", link=scatter, name="ret_out_dx")
            do, dgate, dgn = ret_gate_bwd(dy, sv["o"], sv["proj"], 2, vec(ret_gn_g, j), RET_HEADS, name="ret_gate_bwd")
            small["ret_gn_g"][j] = dgn[0]
            dq, dk, dv = attn_bwd((sv["qk"], ret_dk, 0, 1), (sv["qk"], ret_dk, RET_HEADS, 1),
                                  (sv["proj"], ret_dv, RET_HEADS, 1), do, heads=RET_HEADS, softmax=False,
                                  masked=True, log_gamma=log_gamma, link=scatter, est_us=ATTN_US["ret"][1],
                                  name="ret_attn_bwd")
            dqk = ret_rope_bwd(dq, dk, rcos, rsin, ret_dk // 2, ret_dk ** -0.5, name="ret_rope_bwd")
            dproj = jnp.concatenate([dqk, dv.astype(MXU_DTYPE), dgate], axis=1)
            wgrad("ret_w_in", sv["hb_mix"], dproj, j, "ret_in_dw")
            dh = mm(dproj, W[("ret_w_in", j)], mode="nt", add=dz, add_scale=alpha, link=scatter, name="ret_in_dx")
        else:
            wgrad("mla_w_out", sv["ob"], dzb, j, "mla_out_dw")
            do = mm(dzb, W[("mla_w_out", j)], mode="nt", link=scatter, name="mla_out_dx")
            dq, dk, dv = attn_bwd((sv["qr"], MLA_PAD, 0, 1), (sv["kr"], MLA_PAD, 0, 1), (sv["kvf"], MLA_V, 1, 2), do,
                                  heads=MLA_HEADS, softmax=True, masked=True, scale=mla_scale, o=sv["o"],
                                  lse=sv["lse"], link=scatter, est_us=ATTN_US["mla"][1], name="mla_attn_bwd")
            (dq_un,) = mla_rope(dq, mla_ta, mla_tb, MLA_HEADS, backward=True, head_sum=False, out_dtype=F32,
                                name="mla_rope_q_bwd")
            dk_un, dk_rope_sum = mla_rope(dk, mla_ta, mla_tb, MLA_HEADS, backward=True, head_sum=True, out_dtype=F32,
                                          name="mla_rope_k_bwd")
            dq_nope, dq_rope = _mla_unpad(dq_un, MLA_HEADS)
            dqf = jnp.concatenate([dq_nope, dq_rope], axis=2).reshape(S, -1).astype(MXU_DTYPE)
            dk_nope, _ = _mla_unpad(dk_un, MLA_HEADS)
            dkvf = jnp.concatenate([dk_nope, dv.reshape(S, MLA_HEADS, MLA_V)], axis=2).reshape(S, -1).astype(MXU_DTYPE)
            half = MLA_ROPE // 2
            dk_rope = jnp.concatenate([dk_rope_sum[:, :half], dk_rope_sum[:, LANES // 2:LANES // 2 + half]], axis=1)
            wgrad("mla_w_uq", sv["cq"], dqf, j, "mla_uq_dw")
            dcq = mm(dqf, W[("mla_w_uq", j)], mode="nt", link=scatter, name="mla_uq_dx")
            wgrad("mla_w_ukv", sv["ckv"], dkvf, j, "mla_ukv_dw")
            dckv = mm(dkvf, W[("mla_w_ukv", j)], mode="nt", link=scatter, name="mla_ukv_dx")
            dcq_in, dqg = rms_bwd(dcq, sv["proj"], 0, MLA_Q_RANK, vec(qg_full, j), name="mla_q_norm_bwd")
            dckv_in, dkvg = rms_bwd(dckv, sv["proj"], 1, MLA_KV_RANK, vec(kvg_full, j), name="mla_kv_norm_bwd")
            small["mla_q_norm_g"][j], small["mla_kv_norm_g"][j] = dqg[0], dkvg[0]
            dproj = jnp.concatenate([dcq_in, dckv_in, dk_rope], axis=1).astype(MXU_DTYPE)
            wgrad("mla_w_in", sv["hb_mix"], dproj, j, "mla_in_dw")
            dh = mm(dproj, W[("mla_w_in", j)], mode="nt", add=dz, add_scale=alpha, link=scatter, name="mla_in_dx")
    grad_x = dh[None]

    grads, deltas, new_m, new_v = {}, {}, {}, {}
    scatter.ride_out(name="scatter_rest")
    scatter.flush(None, name="scatter_rest")
    assert not scatter.queue and not scatter.side
    for n in BIG:
        grads[n], deltas[n], new_m[n], new_v[n] = adam_update(scatter.bufs[n], weights[n], mom_m[n], mom_v[n],
                                                              name="adam_" + n)

    small_names = list(REPLICATED + SMALL_CUT)
    partial = [piece for n in small_names for piece in small[n]]
    allpart = gather_small(pack(partial), name="gather_small_grads")
    rows = allpart.shape[0] // N_DEV
    allpart = allpart.reshape(N_DEV, rows, LANES)

    rep_names = list(REPLICATED)
    rep_w = pack([weights[n] for n in rep_names])
    rep_m = pack([mom_m[n] for n in rep_names])
    rep_v = pack([mom_v[n] for n in rep_names])
    rep_rows = rep_w.shape[0]
    rep_size = sum(weights[n].size for n in rep_names)
    flat_parts = allpart.reshape(N_DEV, rows * LANES)
    rep_parts = jnp.pad(flat_parts[:, :rep_size], ((0, 0), (0, rep_rows * LANES - rep_size)))
    outs = adam_update(rep_parts.reshape(1, N_DEV, rep_rows, LANES), rep_w[None], rep_m[None], rep_v[None],
                       name="adam_replicated")
    for o_, dst in zip(outs, (grads, deltas, new_m, new_v)):
        for n, a in zip(rep_names, unpack(o_.reshape(-1), [weights[n] for n in rep_names])):
            dst[n] = a

    cut_names = list(SMALL_CUT)
    cut_parts = []
    at = rep_size
    for n in cut_names:
        full_shape = (len(small[n]),) + small[n][0].shape
        whole = flat_parts[:, at:at + math.prod(full_shape)].reshape((N_DEV,) + full_shape)
        at += math.prod(full_shape)
        width = weights[n].shape[-1]
        start = (0,) * (whole.ndim - 1) + (dev * width,)
        mine = lax.dynamic_slice(whole, start, whole.shape[:-1] + (width,))
        cut_parts.append(mine.reshape(N_DEV, -1))
    cut_parts = jnp.concatenate(cut_parts, axis=1)
    cut_w = pack([weights[n] for n in cut_names])
    cut_m = pack([mom_m[n] for n in cut_names])
    cut_v = pack([mom_v[n] for n in cut_names])
    cut_rows = cut_w.shape[0]
    cut_parts = jnp.pad(cut_parts, ((0, 0), (0, cut_rows * LANES - cut_parts.shape[1])))
    outs = adam_update(cut_parts.reshape(1, N_DEV, cut_rows, LANES), cut_w[None], cut_m[None], cut_v[None],
                       name="adam_small_cut")
    for o_, dst in zip(outs, (grads, deltas, new_m, new_v)):
        for n, a in zip(cut_names, unpack(o_.reshape(-1), [weights[n] for n in cut_names])):
            dst[n] = a

    return (loss, grad_x, *[grads[n] for n in order], *[deltas[n] for n in order],
            *[new_m[n] for n in order], *[new_v[n] for n in order])
```

```python
import functools
import math

import jax
import jax.numpy as jnp
from jax import lax
from jax.experimental import pallas as pl
from jax.experimental.pallas import tpu as pltpu

F32 = jnp.float32
MXU_DTYPE = jnp.bfloat16
PAY_DTYPE = jnp.bfloat16
MESH = pl.DeviceIdType.MESH
N_DEV = 8
N_CHIP = 4

DEPTH = 4
CHUNK = 64
RET_HEADS = 8
MLA_HEADS = 16
MLA_Q_RANK = 512
MLA_KV_RANK = 512
MLA_NOPE = 128
MLA_ROPE = 64
MLA_V = 128
MLA_PAD = 256
XA_HEADS = 4
ROPE_BASE = 10000.0
LN_EPS = 1e-5
RMS_EPS = 1e-6
NEG_INF = -1e30
ADAM_LR = 0.001
ADAM_B1 = 0.9
ADAM_B2 = 0.999
ADAM_EPS = 1e-08
ADAM_WD = 0.01
ADAM_STEP = 10

LANES = 128
VMEM_LIMIT_BYTES = 48 * 1024 * 1024
TILE_M = 1024
TILE_N = 1024
TILE_K = 2048
SLAB_TILE_N = 1536
ATTN_TILE = 512
ATTN_PARTS = 2
ROW_TILE = 256

MXU_FLOPS_PER_US = 7.0e8
GATHER_BYTES_PER_US = 5.5e4
SCATTER_BYTES_PER_US = 2.2e4
CHUNK_BYTES = 1280 * 1024
HOST_CHUNKS = 12
LEVEL2_US = 30.0
MIN_HOST_US = 45.0
SIDE_CHUNKS = 4
ATTN_US = {"ret": (125.0, 160.0), "mla": (320.0, 270.0), "xa": (35.0, 35.0)}
ROWWISE_US = {"ln": (25.0, 22.0), "conv": (65.0, 110.0), "rope": 20.0, "gate": 33.0}


def _tile(n, cap, mult=LANES):
    best = None
    for t in range(mult, min(n, cap) + 1, mult):
        if n % t == 0:
            best = t
    return n if best is None else best


def _params(n_axes):
    return pltpu.CompilerParams(dimension_semantics=("arbitrary",) * n_axes,
                                vmem_limit_bytes=VMEM_LIMIT_BYTES)


def _place():
    return lax.axis_index("x"), lax.axis_index("y"), lax.axis_index("c")


def _other_chips(x, y):
    return [(1 - x, y), (x, 1 - y), (1 - x, 1 - y)]


def _remote(src, dst, send, recv, to):
    return pltpu.make_async_remote_copy(src_ref=src, dst_ref=dst, send_sem=send, recv_sem=recv,
                                        device_id=to, device_id_type=MESH)


class _Chunk:
    def __init__(self, kind, src, key, dst_shape, layer, r0, rc, cols, axis=None, k=None, n=None, half=None):
        self.kind, self.src, self.key, self.dst_shape = kind, src, key, dst_shape
        self.layer, self.r0, self.rc, self.axis, self.k, self.n, self.half = layer, r0, rc, axis, k, n, half
        self.nbytes = rc * cols * jnp.dtype(PAY_DTYPE).itemsize

    def copies(self, src_ref, dst_ref, send, recv, loc, i):
        x, y, c = _place()
        chips = _other_chips(x, y)
        rows = pl.ds(self.r0, self.rc)
        sibling, x_nbr, y_nbr, diagonal = (x, y, 1 - c), (1 - x, y, c), (x, 1 - y, c), (1 - x, 1 - y, c)
        if self.kind == "gather":
            peers = [sibling, x_nbr, y_nbr]
            src = src_ref.at[self.layer, rows, :]

            def slab(px, py, pc):
                idx = 4 * px + 2 * py + pc
                if self.axis == 1:
                    return dst_ref.at[pl.ds(idx * self.k + self.r0, self.rc), :]
                return dst_ref.at[rows, pl.ds(idx * self.n, self.n)]

            sends = [_remote(src, slab(x, y, c), send.at[4 * i + t], recv.at[4 * i + t], p)
                     for t, p in enumerate(peers)]
            recvs = [_remote(src, slab(*p), send.at[4 * i + t], recv.at[4 * i + t], p)
                     for t, p in enumerate(peers)]
            return sends, recvs, pltpu.make_async_copy(src, slab(x, y, c), loc.at[i])
        if self.kind == "relay":
            def piece(p, r):
                idx = 4 * p[0] + 2 * p[1] + p[2]
                if self.axis == 1:
                    return dst_ref.at[pl.ds(idx * self.k + r, self.rc), :]
                return dst_ref.at[pl.ds(r, self.rc), pl.ds(idx * self.n, self.n)]

            top, bottom = self.r0, self.half + self.r0
            sends = [_remote(piece(y_nbr, top), piece(y_nbr, top), send.at[4 * i], recv.at[4 * i], x_nbr),
                     _remote(piece(x_nbr, bottom), piece(x_nbr, bottom), send.at[4 * i + 1], recv.at[4 * i + 1], y_nbr)]
            recvs = [_remote(piece(diagonal, top), piece(diagonal, top), send.at[4 * i], recv.at[4 * i], x_nbr),
                     _remote(piece(diagonal, bottom), piece(diagonal, bottom), send.at[4 * i + 1], recv.at[4 * i + 1],
                             y_nbr)]
            return sends, recvs, None
        if self.kind == "gather2":
            def slab(px, py, pc):
                idx = 4 * px + 2 * py + pc
                if self.axis == 1:
                    return dst_ref.at[pl.ds(idx * self.k, self.k), :]
                return dst_ref.at[:, pl.ds(idx * self.n, self.n)]

            sends = [_remote(slab(px, py, c), slab(px, py, c), send.at[4 * i + t], recv.at[4 * i + t], sibling)
                     for t, (px, py) in enumerate(chips)]
            recvs = [_remote(slab(px, py, 1 - c), slab(px, py, 1 - c), send.at[4 * i + t], recv.at[4 * i + t], sibling)
                     for t, (px, py) in enumerate(chips)]
            return sends, recvs, None
        if self.kind == "swap":
            sends = [_remote(src_ref.at[2 * j + (1 - c)], dst_ref.at[j], send.at[4 * i + j], recv.at[4 * i + j], sibling)
                     for j in range(N_CHIP)]
            recvs = [_remote(src_ref.at[2 * j + c], dst_ref.at[j], send.at[4 * i + j], recv.at[4 * i + j], sibling)
                     for j in range(N_CHIP)]
            return sends, recvs, None
        mine = 2 * x + y
        own = src_ref.at[mine, rows, :]
        sends = [_remote(src_ref.at[2 * px + py, rows, :], dst_ref.at[self.layer, mine, rows, :],
                         send.at[4 * i + t], recv.at[4 * i + t], (px, py, c)) for t, (px, py) in enumerate(chips)]
        recvs = [_remote(own, dst_ref.at[self.layer, 2 * px + py, rows, :],
                         send.at[4 * i + t], recv.at[4 * i + t], (px, py, c)) for t, (px, py) in enumerate(chips)]
        return sends, recvs, pltpu.make_async_copy(own, dst_ref.at[self.layer, mine, rows, :], loc.at[i])


def _call(body, *, grid, in_specs, out_specs, out_shape, operands, scratch=(), name, chunks=(), bufs=None):
    n_in, n_out, n_scr = len(operands), len(out_shape), len(scratch)
    params = _params(len(grid)) if grid else pltpu.CompilerParams(vmem_limit_bytes=VMEM_LIMIT_BYTES)
    if not chunks:
        return pl.pallas_call(
            body, out_shape=tuple(out_shape), grid=grid, in_specs=list(in_specs), out_specs=tuple(out_specs),
            scratch_shapes=list(scratch), compiler_params=params, name=name)(*operands)
    srcs, keys = [], []
    for ch in chunks:
        if ch.src is not None and not any(ch.src is s for s in srcs):
            srcs.append(ch.src)
        if ch.key not in keys:
            keys.append(ch.key)
    shape_of = {ch.key: ch.dst_shape for ch in chunks}
    held = [k for k in keys if bufs.get(k) is not None]
    extra = srcs + [bufs[k] for k in held]
    aliases = {n_in + len(srcs) + i: n_out + keys.index(k) for i, k in enumerate(held)}
    anywhere = pl.BlockSpec(memory_space=pl.ANY)
    n_extra, n_keys, n_ch = len(extra), len(keys), len(chunks)

    def hosted(*refs):
        src_refs = refs[n_in:n_in + len(srcs)]
        o0 = n_in + n_extra
        dst_refs = refs[o0 + n_out:o0 + n_out + n_keys]
        s0 = o0 + n_out + n_keys
        send, recv, loc = refs[s0 + n_scr:]
        sends, recvs, locs = [], [], []
        for i, ch in enumerate(chunks):
            src_ref = None if ch.src is None else src_refs[[ch.src is s for s in srcs].index(True)]
            s_, r_, l_ = ch.copies(src_ref, dst_refs[keys.index(ch.key)], send, recv, loc, i)
            sends += s_
            recvs += r_
            if l_ is not None:
                locs.append(l_)

        def start():
            for cp in locs + sends:
                cp.start()

        def finish():
            for cp in recvs:
                cp.wait_recv()
            for cp in sends:
                cp.wait_send()
            for cp in locs:
                cp.wait()

        if grid:
            first = functools.reduce(jnp.logical_and, [pl.program_id(a) == 0 for a in range(len(grid))])
            last = functools.reduce(jnp.logical_and, [pl.program_id(a) == grid[a] - 1 for a in range(len(grid))])
            pl.when(first)(start)
            body(*refs[:n_in], *refs[o0:o0 + n_out], *refs[s0:s0 + n_scr])
            pl.when(last)(finish)
        else:
            start()
            body(*refs[:n_in], *refs[o0:o0 + n_out], *refs[s0:s0 + n_scr])
            finish()

    res = pl.pallas_call(
        hosted, out_shape=tuple(out_shape) + tuple(jax.ShapeDtypeStruct(shape_of[k], PAY_DTYPE) for k in keys),
        grid=grid, in_specs=list(in_specs) + [anywhere] * n_extra, out_specs=tuple(out_specs) + (anywhere,) * n_keys,
        scratch_shapes=list(scratch) + [pltpu.SemaphoreType.DMA((4 * n_ch,)), pltpu.SemaphoreType.DMA((4 * n_ch,)),
                                        pltpu.SemaphoreType.DMA((n_ch,))],
        input_output_aliases=aliases, compiler_params=params, name=name)(*operands, *extra)
    for k, arr in zip(keys, res[n_out:]):
        bufs[k] = arr
    return res[:n_out]


class _Link:
    def __init__(self, bytes_per_us):
        self.rate = bytes_per_us
        self.queue, self.bufs = [], {}
        self.side = []
        self.after = []
        self.left = {}
        self.on_done = {}

    def push(self, kind, src, key, dst_shape, layer, rows, cols, done=None, front=False, **geometry):
        fits = [r for r in range(16, rows + 1, 16) if rows % r == 0 and r * cols * 2 <= CHUNK_BYTES]
        rc = max(fits) if fits else min(r for r in range(16, rows + 1, 16) if rows % r == 0)
        new = [_Chunk(kind, src, key, dst_shape, layer, r0, rc, cols, **geometry) for r0 in range(0, rows, rc)]
        self.queue = new + self.queue if front else self.queue + new
        self.left[key] = self.left.get(key, 0) + rows // rc
        if done is not None:
            self.on_done[key] = done

    def settle(self):
        todo, self.after = self.after, []
        for f in todo:
            f()

    def _pop(self, count):
        out = []
        for ch in self.queue[:count]:
            self.left[ch.key] -= 1
            if self.left[ch.key] == 0 and ch.key in self.on_done:
                self.after.append(self.on_done.pop(ch.key))
            out.append(ch)
        self.queue = self.queue[count:]
        return out

    def _side(self):
        riders, self.side = self.side[:SIDE_CHUNKS], self.side[SIDE_CHUNKS:]
        self.after += [then for _, then in riders if then is not None]
        return [ch for ch, _ in riders]

    def take(self, est_us, short_too=False):
        self.settle()
        if est_us < MIN_HOST_US and not short_too:
            return []
        out = self._side()
        budget, count = est_us * self.rate, 0
        while count < min(len(self.queue), HOST_CHUNKS) and self.queue[count].nbytes <= 2 * budget:
            budget -= self.queue[count].nbytes
            count += 1
        return out + self._pop(count)

    def flush(self, key, name):
        self.settle()
        if key is None:
            n = len(self.queue)
        else:
            n = max([i + 1 for i, ch in enumerate(self.queue) if ch.key == key], default=0)
        while n > 0:
            group = self._side() + self._pop(min(n, HOST_CHUNKS))
            n -= min(n, HOST_CHUNKS)
            _call(lambda: None, grid=(), in_specs=[], out_specs=(), out_shape=(), operands=[], name=name,
                  chunks=group, bufs=self.bufs)
            self.settle()

    def carry(self, est_us, name):
        _call(lambda: None, grid=(), in_specs=[], out_specs=(), out_shape=(), operands=[], name=name,
              chunks=self.take(est_us, short_too=True), bufs=self.bufs)

    def ride_out(self, name):
        self.settle()
        while self.side:
            _call(lambda: None, grid=(), in_specs=[], out_specs=(), out_shape=(), operands=[], name=name,
                  chunks=self._side(), bufs=self.bufs)
            self.settle()


def mm(a, b, *, mode="nn", out_dtype=F32, add=None, add_scale=1.0, slab_width=None, link=None, name):
    if mode == "nn":
        (M, K), N = a.shape, b.shape[-1]
    elif mode == "nt":
        (M, K), N = a.shape, b.shape[-2]
    else:
        (K, M), N = a.shape, b.shape[-1]
    tm, tk = _tile(M, TILE_M), _tile(K, TILE_K)
    tn = _tile(N, TILE_N) if slab_width is None else _tile(slab_width, SLAB_TILE_N)
    nk = K // tk
    dims = {"nn": (((1,), (0,)), ((), ())), "nt": (((1,), (1,)), ((), ())),
            "tn": (((0,), (0,)), ((), ()))}[mode]

    a_spec = (pl.BlockSpec((tk, tm), lambda i, j, k: (k, i)) if mode == "tn"
              else pl.BlockSpec((tm, tk), lambda i, j, k: (i, k)))
    b_spec = (pl.BlockSpec((tn, tk), lambda i, j, k: (j, k)) if mode == "nt"
              else pl.BlockSpec((tk, tn), lambda i, j, k: (k, j)))
    in_specs, operands = [a_spec, b_spec], [a, b]
    if add is not None:
        in_specs.append(pl.BlockSpec((tm, tn), lambda i, j, k: (i, j)))
        operands.append(add)
    if slab_width is None:
        out_shape = jax.ShapeDtypeStruct((M, N), out_dtype)
        out_spec = pl.BlockSpec((tm, tn), lambda i, j, k: (i, j))
    else:
        per = slab_width // tn
        out_shape = jax.ShapeDtypeStruct((N // slab_width, M, slab_width), out_dtype)
        out_spec = pl.BlockSpec((None, tm, tn), lambda i, j, k: (j // per, i, j % per))
    n_in = len(operands)

    def body(*refs):
        a_ref, b_ref = refs[0], refs[1]
        add_ref = refs[2] if add is not None else None
        o_ref = refs[n_in]
        k = pl.program_id(2)
        part = lax.dot_general(a_ref[...].astype(MXU_DTYPE), b_ref[...].astype(MXU_DTYPE),
                               dims, preferred_element_type=F32)

        def finish(r):
            if add_ref is not None:
                r = r + add_scale * add_ref[...].astype(F32)
            o_ref[...] = r.astype(o_ref.dtype)

        if nk == 1:
            finish(part)
        else:
            acc_ref = refs[n_in + 1]

            @pl.when(k == 0)
            def _():
                acc_ref[...] = part

            @pl.when(jnp.logical_and(k > 0, k < nk - 1))
            def _():
                acc_ref[...] += part

            @pl.when(k == nk - 1)
            def _():
                finish(acc_ref[...] + part)

    chunks = link.take(2.0 * M * N * K / MXU_FLOPS_PER_US) if link is not None else ()
    (out,) = _call(body, grid=(M // tm, N // tn, nk), in_specs=in_specs, out_specs=(out_spec,), out_shape=(out_shape,),
                   operands=operands, scratch=[pltpu.VMEM((tm, tn), F32)] if nk > 1 else [], name=name,
                   chunks=chunks, bufs=link.bufs if link is not None else None)
    return out


def ln_fwd(h, f, g, b, alpha, link, name):
    S, D = h.shape
    tm = _tile(S, ROW_TILE, 8)

    def body(h_ref, f_ref, g_ref, b_ref, y_ref, yb_ref, xhat_ref, rstd_ref):
        z = alpha * h_ref[...] + f_ref[...]
        mu = jnp.mean(z, axis=-1, keepdims=True)
        zc = z - mu
        var = jnp.mean(zc * zc, axis=-1, keepdims=True)
        rstd = lax.rsqrt(var + LN_EPS)
        xhat = zc * rstd
        y = xhat * g_ref[...] + b_ref[...]
        y_ref[...] = y
        yb_ref[...] = y.astype(yb_ref.dtype)
        xhat_ref[...] = xhat
        rstd_ref[...] = rstd

    row = pl.BlockSpec((tm, D), lambda i: (i, 0))
    vec = pl.BlockSpec((1, D), lambda i: (0, 0))
    return _call(
        body, grid=(S // tm,), in_specs=[row, row, vec, vec],
        out_specs=(row, row, row, pl.BlockSpec((tm, 1), lambda i: (i, 0))),
        out_shape=(jax.ShapeDtypeStruct((S, D), F32), jax.ShapeDtypeStruct((S, D), MXU_DTYPE),
                   jax.ShapeDtypeStruct((S, D), F32), jax.ShapeDtypeStruct((S, 1), F32)),
        operands=[h, f, g, b], name=name, chunks=link.take(ROWWISE_US["ln"][0]), bufs=link.bufs)


def ln_bwd(dy, xhat, rstd, g, link, name):
    S, D = dy.shape
    tm = _tile(S, ROW_TILE, 8)

    def body(dy_ref, xhat_ref, rstd_ref, g_ref, dz_ref, dzb_ref, dg_ref, db_ref):
        @pl.when(pl.program_id(0) == 0)
        def _():
            dg_ref[...] = jnp.zeros_like(dg_ref)
            db_ref[...] = jnp.zeros_like(db_ref)

        dy = dy_ref[...]
        xhat = xhat_ref[...]
        dxh = dy * g_ref[...]
        m1 = jnp.mean(dxh, axis=-1, keepdims=True)
        m2 = jnp.mean(dxh * xhat, axis=-1, keepdims=True)
        dz = rstd_ref[...] * (dxh - m1 - xhat * m2)
        dz_ref[...] = dz
        dzb_ref[...] = dz.astype(dzb_ref.dtype)
        dg_ref[...] += jnp.sum(dy * xhat, axis=0, keepdims=True)
        db_ref[...] += jnp.sum(dy, axis=0, keepdims=True)

    row = pl.BlockSpec((tm, D), lambda i: (i, 0))
    vec = pl.BlockSpec((1, D), lambda i: (0, 0))
    return _call(
        body, grid=(S // tm,), in_specs=[row, row, pl.BlockSpec((tm, 1), lambda i: (i, 0)), vec],
        out_specs=(row, row, vec, vec),
        out_shape=(jax.ShapeDtypeStruct((S, D), F32), jax.ShapeDtypeStruct((S, D), MXU_DTYPE),
                   jax.ShapeDtypeStruct((1, D), F32), jax.ShapeDtypeStruct((1, D), F32)),
        operands=[dy, xhat, rstd, g], name=name, chunks=link.take(ROWWISE_US["ln"][1]), bufs=link.bufs)


def rms_fwd(x, col, width, g, name):
    S = x.shape[0]
    tm = _tile(S, ROW_TILE, 8)

    def body(x_ref, g_ref, y_ref):
        xv = x_ref[...]
        r = lax.rsqrt(jnp.mean(xv * xv, axis=-1, keepdims=True) + RMS_EPS)
        y_ref[...] = (xv * r * g_ref[...]).astype(y_ref.dtype)

    return pl.pallas_call(
        body, out_shape=jax.ShapeDtypeStruct((S, width), MXU_DTYPE), grid=(S // tm,),
        in_specs=[pl.BlockSpec((tm, width), lambda i: (i, col)), pl.BlockSpec((1, width), lambda i: (0, 0))],
        out_specs=pl.BlockSpec((tm, width), lambda i: (i, 0)), compiler_params=_params(1), name=name)(x, g)


def rms_bwd(dy, x, col, width, g, name):
    S = x.shape[0]
    tm = _tile(S, ROW_TILE, 8)

    def body(dy_ref, x_ref, g_ref, dx_ref, dg_ref):
        @pl.when(pl.program_id(0) == 0)
        def _():
            dg_ref[...] = jnp.zeros_like(dg_ref)

        xv = x_ref[...]
        dy = dy_ref[...]
        r = lax.rsqrt(jnp.mean(xv * xv, axis=-1, keepdims=True) + RMS_EPS)
        dxn = dy * g_ref[...]
        m = jnp.mean(dxn * xv, axis=-1, keepdims=True)
        dx_ref[...] = r * (dxn - xv * (r * r * m))
        dg_ref[...] += jnp.sum(dy * xv * r, axis=0, keepdims=True)

    return pl.pallas_call(
        body, out_shape=(jax.ShapeDtypeStruct((S, width), F32), jax.ShapeDtypeStruct((1, width), F32)),
        grid=(S // tm,),
        in_specs=[pl.BlockSpec((tm, width), lambda i: (i, 0)), pl.BlockSpec((tm, width), lambda i: (i, col)),
                  pl.BlockSpec((1, width), lambda i: (0, 0))],
        out_specs=(pl.BlockSpec((tm, width), lambda i: (i, 0)), pl.BlockSpec((1, width), lambda i: (0, 0))),
        compiler_params=_params(1), name=name)(dy, x, g)


def ret_rope_fwd(proj, cos, sin, n_groups, half, k_from, k_scale, link, name):
    S = proj.shape[0]
    W = n_groups * 2 * half
    tm = _tile(S, ROW_TILE, 8)

    def body(x_ref, c_ref, s_ref, o_ref):
        c, s = c_ref[...], s_ref[...]
        for gi in range(n_groups):
            lo = gi * 2 * half
            x1 = x_ref[:, lo:lo + half]
            x2 = x_ref[:, lo + half:lo + 2 * half]
            sc = k_scale if gi >= k_from else 1.0
            o_ref[:, lo:lo + half] = ((x1 * c - x2 * s) * sc).astype(o_ref.dtype)
            o_ref[:, lo + half:lo + 2 * half] = ((x2 * c + x1 * s) * sc).astype(o_ref.dtype)

    tab = pl.BlockSpec((tm, half), lambda i: (i, 0))
    (out,) = _call(
        body, grid=(S // tm,), in_specs=[pl.BlockSpec((tm, W), lambda i: (i, 0)), tab, tab],
        out_specs=(pl.BlockSpec((tm, W), lambda i: (i, 0)),), out_shape=(jax.ShapeDtypeStruct((S, W), MXU_DTYPE),),
        operands=[proj, cos, sin], name=name, chunks=link.take(ROWWISE_US["rope"]), bufs=link.bufs)
    return out


def ret_rope_bwd(dq, dk, cos, sin, half, k_scale, name):
    S, Wq = dq.shape
    n_heads = Wq // (2 * half)
    tm = _tile(S, ROW_TILE, 8)

    def body(dq_ref, dk_ref, c_ref, s_ref, o_ref):
        c, s = c_ref[...], s_ref[...]
        for part, (d_ref, sc) in enumerate(((dq_ref, 1.0), (dk_ref, k_scale))):
            for hi in range(n_heads):
                lo = hi * 2 * half
                d1 = d_ref[:, lo:lo + half]
                d2 = d_ref[:, lo + half:lo + 2 * half]
                base = part * Wq + lo
                o_ref[:, base:base + half] = ((d1 * c + d2 * s) * sc).astype(o_ref.dtype)
                o_ref[:, base + half:base + 2 * half] = ((d2 * c - d1 * s) * sc).astype(o_ref.dtype)

    tab = pl.BlockSpec((tm, half), lambda i: (i, 0))
    row = pl.BlockSpec((tm, Wq), lambda i: (i, 0))
    return pl.pallas_call(
        body, out_shape=jax.ShapeDtypeStruct((S, 2 * Wq), MXU_DTYPE), grid=(S // tm,),
        in_specs=[row, row, tab, tab], out_specs=pl.BlockSpec((tm, 2 * Wq), lambda i: (i, 0)),
        compiler_params=_params(1), name=name)(dq, dk, cos, sin)


def mla_rope(x, ta, tb, n_heads, *, backward, head_sum, out_dtype, link=None, name):
    S = x.shape[0]
    W = n_heads * MLA_PAD
    tm = _tile(S, ROW_TILE, 8)

    def body(x_ref, a_ref, b_ref, o_ref, *rest):
        a, b = a_ref[...], b_ref[...]
        total = jnp.zeros((tm, LANES), F32)
        for hi in range(n_heads):
            lo = hi * MLA_PAD
            o_ref[:, lo:lo + MLA_NOPE] = x_ref[:, lo:lo + MLA_NOPE].astype(o_ref.dtype)
            t = x_ref[:, lo + MLA_NOPE:lo + MLA_PAD].astype(F32)
            if backward:
                r = t * a + pltpu.roll(t * b, LANES // 2, 1)
            else:
                r = t * a + pltpu.roll(t, LANES // 2, 1) * b
            o_ref[:, lo + MLA_NOPE:lo + MLA_PAD] = r.astype(o_ref.dtype)
            total = total + r
        if head_sum:
            rest[0][...] = total

    tab = pl.BlockSpec((tm, LANES), lambda i: (i, 0))
    row = pl.BlockSpec((tm, W), lambda i: (i, 0))
    out_shape = [jax.ShapeDtypeStruct((S, W), out_dtype)]
    out_specs = [row]
    if head_sum:
        out_shape.append(jax.ShapeDtypeStruct((S, LANES), F32))
        out_specs.append(tab)
    chunks = link.take(ROWWISE_US["rope"]) if link is not None else ()
    return _call(body, grid=(S // tm,), in_specs=[row, tab, tab], out_specs=out_specs, out_shape=out_shape,
                 operands=[x, ta, tb], name=name, chunks=chunks, bufs=link.bufs if link is not None else None)


def _visible(qi, kj, tq, tk, r0, rows):
    n = qi * tq + r0 + lax.broadcasted_iota(jnp.int32, (rows, tk), 0)
    m = kj * tk + lax.broadcasted_iota(jnp.int32, (rows, tk), 1)
    shift = CHUNK.bit_length() - 1
    vis = lax.shift_right_logical(m, shift) <= lax.shift_right_logical(n, shift)
    return vis, jnp.abs(n - m).astype(F32)


def attn_fwd(q, k, v, *, heads, softmax, masked, scale=1.0, log_gamma=None, link=None, est_us=0.0, name):
    (qa, dqk, q0, qs), (ka, _, k0, ks), (va, dv, v0, vs) = q, k, v
    Sq, Sk = qa.shape[0], ka.shape[0]
    tq, tk = _tile(Sq, ATTN_TILE, 8), _tile(Sk, ATTN_TILE, 8)
    nq, nk = Sq // tq, Sk // tk
    assert not masked or tq == tk
    sub = tq // ATTN_PARTS if tq % (16 * ATTN_PARTS) == 0 else tq

    def body(*refs):
        if softmax:
            q_ref, k_ref, v_ref, o_ref, ob_ref, lse_ref, m_ref, l_ref, acc_ref = refs
        else:
            q_ref, k_ref, v_ref, lg_ref, o_ref, acc_ref = refs
        qi, kj = pl.program_id(1), pl.program_id(2)

        @pl.when(kj == 0)
        def _():
            acc_ref[...] = jnp.zeros_like(acc_ref)
            if softmax:
                m_ref[...] = jnp.full_like(m_ref, NEG_INF)
                l_ref[...] = jnp.zeros_like(l_ref)

        def step(diagonal):
            kb = k_ref[...].astype(MXU_DTYPE)
            vb = v_ref[...].astype(MXU_DTYPE)
            parts = [pl.ds(r * sub, sub) for r in range(tq // sub)]
            scores = [lax.dot_general(q_ref[rows, :].astype(MXU_DTYPE), kb, (((1,), (1,)), ((), ())),
                                      preferred_element_type=F32) for rows in parts]
            for r, rows in enumerate(parts):
                s = scores[r]
                if diagonal or not softmax:
                    vis, dist = _visible(qi, kj, tq, tk, r * sub, sub)
                if softmax:
                    s = s * scale
                    if diagonal:
                        s = jnp.where(vis, s, NEG_INF)
                    m_old = m_ref[rows, :]
                    m_new = jnp.maximum(m_old, jnp.max(s, axis=-1, keepdims=True))
                    p = jnp.exp(s - m_new)
                    corr = jnp.exp(m_old - m_new)
                    l_ref[rows, :] = corr * l_ref[rows, :] + jnp.sum(p, axis=-1, keepdims=True)
                    acc_ref[rows, :] = (corr * acc_ref[rows, :]
                                        + jnp.dot(p.astype(MXU_DTYPE), vb, preferred_element_type=F32))
                    m_ref[rows, :] = m_new
                else:
                    decay = jnp.exp(lg_ref[0:1, 0:1] * dist)
                    if diagonal:
                        decay = jnp.where(vis, decay, 0.0)
                    acc_ref[rows, :] += jnp.dot((s * decay).astype(MXU_DTYPE), vb, preferred_element_type=F32)

        if masked:
            pl.when(kj < qi)(functools.partial(step, False))
            pl.when(kj == qi)(functools.partial(step, True))
        else:
            step(False)

        @pl.when(kj == nk - 1)
        def _():
            if softmax:
                out = acc_ref[...] / l_ref[...]
                o_ref[...] = out
                ob_ref[...] = out.astype(ob_ref.dtype)
                lse_ref[...] = m_ref[...] + jnp.log(l_ref[...])
            else:
                o_ref[...] = acc_ref[...].astype(o_ref.dtype)

    kcap = (lambda qi, kj: jnp.minimum(kj, qi)) if masked else (lambda qi, kj: kj)
    in_specs = [pl.BlockSpec((tq, dqk), lambda h, qi, kj: (qi, q0 + qs * h)),
                pl.BlockSpec((tk, dqk), lambda h, qi, kj: (kcap(qi, kj), k0 + ks * h)),
                pl.BlockSpec((tk, dv), lambda h, qi, kj: (kcap(qi, kj), v0 + vs * h))]
    operands = [qa, ka, va]
    out_shape = [jax.ShapeDtypeStruct((Sq, heads * dv), F32)]
    out_specs = [pl.BlockSpec((tq, dv), lambda h, qi, kj: (qi, h))]
    scratch = []
    if softmax:
        out_shape += [jax.ShapeDtypeStruct((Sq, heads * dv), MXU_DTYPE), jax.ShapeDtypeStruct((heads, Sq, 1), F32)]
        out_specs += [pl.BlockSpec((tq, dv), lambda h, qi, kj: (qi, h)),
                      pl.BlockSpec((None, tq, 1), lambda h, qi, kj: (h, qi, 0))]
        scratch += [pltpu.VMEM((tq, 1), F32), pltpu.VMEM((tq, 1), F32)]
    else:
        in_specs.append(pl.BlockSpec((None, 8, LANES), lambda h, qi, kj: (h, 0, 0)))
        operands.append(log_gamma)
    scratch.append(pltpu.VMEM((tq, dv), F32))
    chunks = link.take(est_us) if link is not None else ()
    return _call(body, grid=(heads, nq, nk), in_specs=in_specs, out_specs=out_specs, out_shape=out_shape,
                 operands=operands, scratch=scratch, name=name, chunks=chunks,
                 bufs=link.bufs if link is not None else None)


def attn_bwd(q, k, v, do, *, heads, softmax, masked, scale=1.0, log_gamma=None, o=None, lse=None,
             link=None, est_us=0.0, name):
    (qa, dqk, q0, qs), (ka, _, k0, ks), (va, dv, v0, vs) = q, k, v
    Sq, Sk = qa.shape[0], ka.shape[0]
    tq, tk = _tile(Sq, ATTN_TILE, 8), _tile(Sk, ATTN_TILE, 8)
    nq, nk = Sq // tq, Sk // tk
    assert not masked or tq == tk
    sub = tq // ATTN_PARTS if tq % (16 * ATTN_PARTS) == 0 else tq
    contract0 = (((0,), (0,)), ((), ()))

    def body(*refs):
        if softmax:
            q_ref, k_ref, v_ref, do_ref, o_ref, lse_ref, dq_ref, dk_ref, dv_ref = refs
        else:
            q_ref, k_ref, v_ref, do_ref, lg_ref, dq_ref, dk_ref, dv_ref = refs
        kj, qi = pl.program_id(1), pl.program_id(2)

        @pl.when(jnp.logical_and(kj == 0, qi == 0))
        def _():
            dq_ref[...] = jnp.zeros_like(dq_ref)

        @pl.when(qi == 0)
        def _():
            dk_ref[...] = jnp.zeros_like(dk_ref)
            dv_ref[...] = jnp.zeros_like(dv_ref)

        def step(diagonal):
            kb = k_ref[...].astype(MXU_DTYPE)
            vb = v_ref[...].astype(MXU_DTYPE)
            parts = [pl.ds(r * sub, sub) for r in range(tq // sub)]
            nt = (((1,), (1,)), ((), ()))
            qbs = [q_ref[rows, :].astype(MXU_DTYPE) for rows in parts]
            dobs = [do_ref[rows, :].astype(MXU_DTYPE) for rows in parts]
            scores = [lax.dot_general(qb, kb, nt, preferred_element_type=F32) for qb in qbs]
            dps = [lax.dot_general(dob, vb, nt, preferred_element_type=F32) for dob in dobs]
            dv_new, dk_new = dv_ref[...], dk_ref[...]
            for r, rows in enumerate(parts):
                s, dp, qb, dob = scores[r], dps[r], qbs[r], dobs[r]
                if diagonal or not softmax:
                    vis, dist = _visible(qi, kj, tq, tk, r * sub, sub)
                if softmax:
                    s = s * scale
                    if diagonal:
                        s = jnp.where(vis, s, NEG_INF)
                    p = jnp.exp(s - lse_ref[rows, :])
                    delta = jnp.sum(do_ref[rows, :].astype(F32) * o_ref[rows, :], axis=-1, keepdims=True)
                    ds = p * (dp - delta) * scale
                else:
                    decay = jnp.exp(lg_ref[0:1, 0:1] * dist)
                    if diagonal:
                        decay = jnp.where(vis, decay, 0.0)
                    p = s * decay
                    ds = dp * decay
                pb = p.astype(MXU_DTYPE)
                dsb = ds.astype(MXU_DTYPE)
                dv_new = dv_new + lax.dot_general(pb, dob, contract0, preferred_element_type=F32)
                dk_new = dk_new + lax.dot_general(dsb, qb, contract0, preferred_element_type=F32)
                out_rows = pl.ds(pl.multiple_of(qi * tq + r * sub, sub), sub)
                dq_ref[out_rows, :] += jnp.dot(dsb, kb, preferred_element_type=F32)
            dv_ref[...] = dv_new
            dk_ref[...] = dk_new

        if masked:
            pl.when(qi > kj)(functools.partial(step, False))
            pl.when(qi == kj)(functools.partial(step, True))
        else:
            step(False)

    qcap = (lambda kj, qi: jnp.maximum(qi, kj)) if masked else (lambda kj, qi: qi)
    in_specs = [pl.BlockSpec((tq, dqk), lambda h, kj, qi: (qcap(kj, qi), q0 + qs * h)),
                pl.BlockSpec((tk, dqk), lambda h, kj, qi: (kj, k0 + ks * h)),
                pl.BlockSpec((tk, dv), lambda h, kj, qi: (kj, v0 + vs * h)),
                pl.BlockSpec((tq, dv), lambda h, kj, qi: (qcap(kj, qi), h))]
    operands = [qa, ka, va, do]
    if softmax:
        in_specs += [pl.BlockSpec((tq, dv), lambda h, kj, qi: (qcap(kj, qi), h)),
                     pl.BlockSpec((None, tq, 1), lambda h, kj, qi: (h, qcap(kj, qi), 0))]
        operands += [o, lse]
    else:
        in_specs.append(pl.BlockSpec((None, 8, LANES), lambda h, kj, qi: (h, 0, 0)))
        operands.append(log_gamma)
    chunks = link.take(est_us) if link is not None else ()
    return _call(
        body, grid=(heads, nk, nq), in_specs=in_specs,
        out_specs=(pl.BlockSpec((Sq, dqk), lambda h, kj, qi: (0, h)),
                   pl.BlockSpec((tk, dqk), lambda h, kj, qi: (kj, h)),
                   pl.BlockSpec((tk, dv), lambda h, kj, qi: (kj, h))),
        out_shape=(jax.ShapeDtypeStruct((Sq, heads * dqk), F32), jax.ShapeDtypeStruct((Sk, heads * dqk), F32),
                   jax.ShapeDtypeStruct((Sk, heads * dv), F32)),
        operands=operands, name=name, chunks=chunks, bufs=link.bufs if link is not None else None)


def _sigmoid(x):
    return 0.5 * jnp.tanh(0.5 * x) + 0.5


def ret_gate_fwd(o, proj, gate_col, gn, heads, link, name):
    S, W = o.shape
    dv = W // heads
    tm = _tile(S, ROW_TILE // 2, 8)

    def body(o_ref, g_ref, gn_ref, y_ref):
        for hi in range(heads):
            cols = slice(hi * dv, (hi + 1) * dv)
            oh = o_ref[:, cols]
            mu = jnp.mean(oh, axis=-1, keepdims=True)
            oc = oh - mu
            rstd = lax.rsqrt(jnp.mean(oc * oc, axis=-1, keepdims=True) + LN_EPS)
            gt = g_ref[:, cols]
            y_ref[:, cols] = (gt * _sigmoid(gt) * (oc * rstd) * gn_ref[:, cols]).astype(y_ref.dtype)

    row = pl.BlockSpec((tm, W), lambda i: (i, 0))
    (y,) = _call(
        body, grid=(S // tm,),
        in_specs=[row, pl.BlockSpec((tm, W), lambda i: (i, gate_col)), pl.BlockSpec((1, W), lambda i: (0, 0))],
        out_specs=(row,), out_shape=(jax.ShapeDtypeStruct((S, W), MXU_DTYPE),), operands=[o, proj, gn], name=name,
        chunks=link.take(ROWWISE_US["gate"]), bufs=link.bufs)
    return y


def ret_gate_bwd(dy, o, proj, gate_col, gn, heads, name):
    S, W = o.shape
    dv = W // heads
    tm = _tile(S, ROW_TILE // 2, 8)

    def body(dy_ref, o_ref, g_ref, gn_ref, do_ref, dgt_ref, dgn_ref):
        @pl.when(pl.program_id(0) == 0)
        def _():
            dgn_ref[...] = jnp.zeros_like(dgn_ref)

        for hi in range(heads):
            cols = slice(hi * dv, (hi + 1) * dv)
            oh = o_ref[:, cols]
            mu = jnp.mean(oh, axis=-1, keepdims=True)
            oc = oh - mu
            rstd = lax.rsqrt(jnp.mean(oc * oc, axis=-1, keepdims=True) + LN_EPS)
            xhat = oc * rstd
            gt = g_ref[:, cols]
            sg = _sigmoid(gt)
            gain = gn_ref[:, cols]
            dy = dy_ref[:, cols]
            dgt_ref[:, cols] = (dy * xhat * gain * (sg * (1.0 + gt * (1.0 - sg)))).astype(dgt_ref.dtype)
            dn = dy * (gt * sg)
            dgn_ref[:, cols] += jnp.sum(dn * xhat, axis=0, keepdims=True)
            dxh = dn * gain
            m1 = jnp.mean(dxh, axis=-1, keepdims=True)
            m2 = jnp.mean(dxh * xhat, axis=-1, keepdims=True)
            do_ref[:, cols] = (rstd * (dxh - m1 - xhat * m2)).astype(do_ref.dtype)

    row = pl.BlockSpec((tm, W), lambda i: (i, 0))
    vec = pl.BlockSpec((1, W), lambda i: (0, 0))
    return pl.pallas_call(
        body,
        out_shape=(jax.ShapeDtypeStruct((S, W), MXU_DTYPE), jax.ShapeDtypeStruct((S, W), MXU_DTYPE),
                   jax.ShapeDtypeStruct((1, W), F32)),
        grid=(S // tm,), in_specs=[row, row, pl.BlockSpec((tm, W), lambda i: (i, gate_col)), vec],
        out_specs=(row, row, vec), compiler_params=_params(1), name=name)(dy, o, proj, gn)


def _shift_down(x, s):
    rows = lax.broadcasted_iota(jnp.int32, x.shape, 0)
    return jnp.where(rows >= s, pltpu.roll(x, s, 0), 0.0)


def _shift_up(x, s):
    n = x.shape[0]
    rows = lax.broadcasted_iota(jnp.int32, x.shape, 0)
    return jnp.where(rows < n - s, pltpu.roll(x, n - s, 0), 0.0)


def _conv3(x, w_ref, b_ref):
    return (w_ref[2:3, :] * x + w_ref[1:2, :] * _shift_down(x, 1) + w_ref[0:1, :] * _shift_down(x, 2)
            + b_ref[...])


def conv_glu_fwd(hup, w, b, link, name):
    S, W2 = hup.shape
    F = W2 // 2
    tc = _tile(F, LANES)
    nb = F // tc

    def body(g_ref, v_ref, wg_ref, wv_ref, bg_ref, bv_ref, u_ref):
        cg = _conv3(g_ref[...], wg_ref, bg_ref)
        cv = _conv3(v_ref[...], wv_ref, bv_ref)
        u_ref[...] = (cg * _sigmoid(cg) * cv).astype(u_ref.dtype)

    def col(rows, off):
        return pl.BlockSpec((rows, tc), lambda j: (0, j + off))

    (u,) = _call(
        body, grid=(nb,), in_specs=[col(S, 0), col(S, nb), col(3, 0), col(3, nb), col(1, 0), col(1, nb)],
        out_specs=(col(S, 0),), out_shape=(jax.ShapeDtypeStruct((S, F), MXU_DTYPE),),
        operands=[hup, hup, w, w, b, b], name=name, chunks=link.take(ROWWISE_US["conv"][0]), bufs=link.bufs)
    return u


def conv_glu_bwd(du, hup, w, b, link, name):
    S, W2 = hup.shape
    F = W2 // 2
    tc = _tile(F, LANES)
    nb = F // tc

    def back(dc, x, w_ref, dh_ref, dw_ref, db_ref):
        dw_ref[2:3, :] = jnp.sum(dc * x, axis=0, keepdims=True)
        dw_ref[1:2, :] = jnp.sum(dc * _shift_down(x, 1), axis=0, keepdims=True)
        dw_ref[0:1, :] = jnp.sum(dc * _shift_down(x, 2), axis=0, keepdims=True)
        db_ref[...] = jnp.sum(dc, axis=0, keepdims=True)
        dh_ref[...] = (w_ref[2:3, :] * dc + w_ref[1:2, :] * _shift_up(dc, 1)
                       + w_ref[0:1, :] * _shift_up(dc, 2)).astype(dh_ref.dtype)

    def body(du_ref, g_ref, v_ref, wg_ref, wv_ref, bg_ref, bv_ref,
             dhg_ref, dhv_ref, dwg_ref, dwv_ref, dbg_ref, dbv_ref):
        xg, xv = g_ref[...], v_ref[...]
        cg = _conv3(xg, wg_ref, bg_ref)
        cv = _conv3(xv, wv_ref, bv_ref)
        sg = _sigmoid(cg)
        du = du_ref[...]
        back(du * cv * (sg * (1.0 + cg * (1.0 - sg))), xg, wg_ref, dhg_ref, dwg_ref, dbg_ref)
        back(du * (cg * sg), xv, wv_ref, dhv_ref, dwv_ref, dbv_ref)

    def col(rows, off):
        return pl.BlockSpec((rows, tc), lambda j: (0, j + off))

    return _call(
        body, grid=(nb,),
        in_specs=[col(S, 0), col(S, 0), col(S, nb), col(3, 0), col(3, nb), col(1, 0), col(1, nb)],
        out_specs=(col(S, 0), col(S, 0), col(3, 0), col(3, 0), col(1, 0), col(1, 0)),
        out_shape=(jax.ShapeDtypeStruct((S, F), MXU_DTYPE), jax.ShapeDtypeStruct((S, F), MXU_DTYPE),
                   jax.ShapeDtypeStruct((3, F), F32), jax.ShapeDtypeStruct((3, F), F32),
                   jax.ShapeDtypeStruct((1, F), F32), jax.ShapeDtypeStruct((1, F), F32)),
        operands=[du, hup, hup, w, w, b, b], name=name, chunks=link.take(ROWWISE_US["conv"][1]), bufs=link.bufs)


def loss_head(y, target, name):
    S, D = y.shape
    tm = _tile(S, ROW_TILE, 8)

    def body(y_ref, t_ref, dy_ref, loss_ref):
        @pl.when(pl.program_id(0) == 0)
        def _():
            loss_ref[...] = jnp.zeros_like(loss_ref)

        e = y_ref[...] - t_ref[...]
        dy_ref[...] = e * (1.0 / D)
        part = jnp.sum(jnp.sum(e * e, axis=-1, keepdims=True), axis=0, keepdims=True) * (0.5 / D)
        loss_ref[...] += jnp.broadcast_to(part, loss_ref.shape)

    row = pl.BlockSpec((tm, D), lambda i: (i, 0))
    return pl.pallas_call(
        body, out_shape=(jax.ShapeDtypeStruct((S, D), F32), jax.ShapeDtypeStruct((8, LANES), F32)),
        grid=(S // tm,), in_specs=[row, row], out_specs=(row, pl.BlockSpec((8, LANES), lambda i: (0, 0))),
        compiler_params=_params(1), name=name)(y, target)


def adam_update(parts, w, m, v, name):
    n_layers, n_parts, R, C = parts.shape
    tr = _tile(R, max(8, (1 << 19) // C), 8)
    c1 = 1.0 - ADAM_B1 ** ADAM_STEP
    c2 = 1.0 - ADAM_B2 ** ADAM_STEP

    def body(p_ref, w_ref, m_ref, v_ref, g_ref, d_ref, nm_ref, nv_ref):
        g = p_ref[0].astype(F32)
        for pi in range(1, n_parts):
            g = g + p_ref[pi].astype(F32)
        nm = ADAM_B1 * m_ref[...] + (1.0 - ADAM_B1) * g
        nv = ADAM_B2 * v_ref[...] + (1.0 - ADAM_B2) * (g * g)
        g_ref[...] = g
        nm_ref[...] = nm
        nv_ref[...] = nv
        d_ref[...] = -ADAM_LR * ((nm / c1) / (jnp.sqrt(nv / c2) + ADAM_EPS) + ADAM_WD * w_ref[...])

    row = pl.BlockSpec((None, tr, C), lambda l, i: (l, i, 0))
    out = jax.ShapeDtypeStruct((n_layers, R, C), F32)
    return _call(
        body, grid=(n_layers, R // tr),
        in_specs=[pl.BlockSpec((None, n_parts, tr, C), lambda l, i: (l, 0, i, 0)), row, row, row],
        out_specs=(row, row, row, row), out_shape=(out, out, out, out), operands=[parts, w, m, v], name=name)


def pair_sum(own, got, core, name):
    _, R, C = own.shape
    tr = _tile(R, max(16, (1 << 19) // C), 16)

    def body(core_ref, a_ref, b_ref, o_ref):
        o_ref[...] = (a_ref[...].astype(F32) + b_ref[...].astype(F32)).astype(o_ref.dtype)

    grid_spec = pltpu.PrefetchScalarGridSpec(
        num_scalar_prefetch=1, grid=(N_CHIP, R // tr),
        in_specs=[pl.BlockSpec((None, tr, C), lambda j, i, core_ref: (2 * j + core_ref[0], i, 0)),
                  pl.BlockSpec((None, tr, C), lambda j, i, core_ref: (j, i, 0))],
        out_specs=pl.BlockSpec((None, tr, C), lambda j, i, core_ref: (j, i, 0)))
    return pl.pallas_call(
        body, out_shape=jax.ShapeDtypeStruct((N_CHIP, R, C), own.dtype), grid_spec=grid_spec,
        compiler_params=_params(2), name=name)(core, own, got)


def gather_small(block, name):
    m_per, n = block.shape

    def body(x_ref, out_ref, send_sems, recv_sems, local_sem):
        x, y, c = _place()
        me, sibling = (x, y, c), (x, y, 1 - c)
        chips = _other_chips(x, y)

        def rows(px, py, pc):
            return out_ref.at[pl.ds((4 * px + 2 * py + pc) * m_per, m_per), :]

        def copy(k, blk, to, src=None):
            return pltpu.make_async_remote_copy(
                src_ref=rows(*blk) if src is None else src, dst_ref=rows(*blk),
                send_sem=send_sems.at[k], recv_sem=recv_sems.at[k], device_id=to, device_id_type=MESH)

        mine = pltpu.make_async_copy(x_ref, rows(*me), local_sem)
        mine.start()
        first = [copy(0, me, sibling, src=x_ref)]
        first += [copy(1 + j, me, (*chip, c), src=x_ref) for j, chip in enumerate(chips)]
        for cp in first:
            cp.start()
        passed = [copy(4 + j, (*chip, c), sibling) for j, chip in enumerate(chips)]
        for j, chip in enumerate(chips):
            copy(1 + j, (*chip, c), me).wait_recv()
            passed[j].start()
        copy(0, sibling, me).wait_recv()
        for j, chip in enumerate(chips):
            copy(4 + j, (*chip, 1 - c), me).wait_recv()
        for cp in first + passed:
            cp.wait_send()
        mine.wait()

    return pl.pallas_call(
        body, out_shape=jax.ShapeDtypeStruct((N_DEV * m_per, n), block.dtype),
        in_specs=[pl.BlockSpec(memory_space=pltpu.VMEM)], out_specs=pl.BlockSpec(memory_space=pltpu.VMEM),
        scratch_shapes=[pltpu.SemaphoreType.DMA((7,)), pltpu.SemaphoreType.DMA((7,)), pltpu.SemaphoreType.DMA],
        compiler_params=pltpu.CompilerParams(vmem_limit_bytes=VMEM_LIMIT_BYTES), name=name)(block)


def _rope_tables(positions, d):
    inv_freq = ROPE_BASE ** (-jnp.arange(0, d, 2, dtype=F32) / d)
    ang = positions.astype(F32)[:, None] * inv_freq
    return jnp.cos(ang), jnp.sin(ang)


def _mla_pad(nope, rope):
    S, H, _ = nope.shape
    half = MLA_ROPE // 2
    z = jnp.zeros((S, H, LANES // 2 - half), nope.dtype)
    return jnp.concatenate([nope, rope[..., :half], z, rope[..., half:], z], axis=2).reshape(S, H * MLA_PAD)


def _mla_unpad(x, H):
    S = x.shape[0]
    half = MLA_ROPE // 2
    x3 = x.reshape(S, H, MLA_PAD)
    rope = jnp.concatenate([x3[..., MLA_NOPE:MLA_NOPE + half],
                            x3[..., MLA_NOPE + LANES // 2:MLA_NOPE + LANES // 2 + half]], axis=2)
    return x3[..., :MLA_NOPE], rope


def kernel(x, mem, positions, ret_w_in, ret_gn_g, ret_w_out, mla_w_in, mla_q_norm_g, mla_w_uq, mla_kv_norm_g, mla_w_ukv, mla_w_out, xa_w_q, xa_w_kv, xa_w_out, ffn_w_up, ffn_conv_w, ffn_conv_b, ffn_w_down, ln_mix_g, ln_mix_b, ln_mem_g, ln_mem_b, ln_ffn_g, ln_ffn_b, loss_target, m_ret_w_in, m_ret_gn_g, m_ret_w_out, m_mla_w_in, m_mla_q_norm_g, m_mla_w_uq, m_mla_kv_norm_g, m_mla_w_ukv, m_mla_w_out, m_xa_w_q, m_xa_w_kv, m_xa_w_out, m_ffn_w_up, m_ffn_conv_w, m_ffn_conv_b, m_ffn_w_down, m_ln_mix_g, m_ln_mix_b, m_ln_mem_g, m_ln_mem_b, m_ln_ffn_g, m_ln_ffn_b, v_ret_w_in, v_ret_gn_g, v_ret_w_out, v_mla_w_in, v_mla_q_norm_g, v_mla_w_uq, v_mla_kv_norm_g, v_mla_w_ukv, v_mla_w_out, v_xa_w_q, v_xa_w_kv, v_xa_w_out, v_ffn_w_up, v_ffn_conv_w, v_ffn_conv_b, v_ffn_w_down, v_ln_mix_g, v_ln_mix_b, v_ln_mem_g, v_ln_mem_b, v_ln_ffn_g, v_ln_ffn_b):
    weights = dict(ret_w_in=ret_w_in, ret_gn_g=ret_gn_g, ret_w_out=ret_w_out, mla_w_in=mla_w_in,
                   mla_q_norm_g=mla_q_norm_g, mla_w_uq=mla_w_uq, mla_kv_norm_g=mla_kv_norm_g,
                   mla_w_ukv=mla_w_ukv, mla_w_out=mla_w_out, xa_w_q=xa_w_q, xa_w_kv=xa_w_kv, xa_w_out=xa_w_out,
                   ffn_w_up=ffn_w_up, ffn_conv_w=ffn_conv_w, ffn_conv_b=ffn_conv_b, ffn_w_down=ffn_w_down,
                   ln_mix_g=ln_mix_g, ln_mix_b=ln_mix_b, ln_mem_g=ln_mem_g, ln_mem_b=ln_mem_b,
                   ln_ffn_g=ln_ffn_g, ln_ffn_b=ln_ffn_b)
    mom_m = dict(ret_w_in=m_ret_w_in, ret_gn_g=m_ret_gn_g, ret_w_out=m_ret_w_out, mla_w_in=m_mla_w_in,
                 mla_q_norm_g=m_mla_q_norm_g, mla_w_uq=m_mla_w_uq, mla_kv_norm_g=m_mla_kv_norm_g,
                 mla_w_ukv=m_mla_w_ukv, mla_w_out=m_mla_w_out, xa_w_q=m_xa_w_q, xa_w_kv=m_xa_w_kv,
                 xa_w_out=m_xa_w_out, ffn_w_up=m_ffn_w_up, ffn_conv_w=m_ffn_conv_w, ffn_conv_b=m_ffn_conv_b,
                 ffn_w_down=m_ffn_w_down, ln_mix_g=m_ln_mix_g, ln_mix_b=m_ln_mix_b, ln_mem_g=m_ln_mem_g,
                 ln_mem_b=m_ln_mem_b, ln_ffn_g=m_ln_ffn_g, ln_ffn_b=m_ln_ffn_b)
    mom_v = dict(ret_w_in=v_ret_w_in, ret_gn_g=v_ret_gn_g, ret_w_out=v_ret_w_out, mla_w_in=v_mla_w_in,
                 mla_q_norm_g=v_mla_q_norm_g, mla_w_uq=v_mla_w_uq, mla_kv_norm_g=v_mla_kv_norm_g,
                 mla_w_ukv=v_mla_w_ukv, mla_w_out=v_mla_w_out, xa_w_q=v_xa_w_q, xa_w_kv=v_xa_w_kv,
                 xa_w_out=v_xa_w_out, ffn_w_up=v_ffn_w_up, ffn_conv_w=v_ffn_conv_w, ffn_conv_b=v_ffn_conv_b,
                 ffn_w_down=v_ffn_w_down, ln_mix_g=v_ln_mix_g, ln_mix_b=v_ln_mix_b, ln_mem_g=v_ln_mem_g,
                 ln_mem_b=v_ln_mem_b, ln_ffn_g=v_ln_ffn_g, ln_ffn_b=v_ln_ffn_b)
    order = list(weights)
    BIG = dict(ret_w_in=2, ret_w_out=1, mla_w_in=1, mla_w_uq=2, mla_w_ukv=2, mla_w_out=1,
               xa_w_q=1, xa_w_kv=2, xa_w_out=1, ffn_w_up=2, ffn_w_down=1)
    SMALL_CUT = ("ffn_conv_w", "mla_q_norm_g", "mla_kv_norm_g")
    REPLICATED = ("ret_gn_g", "ffn_conv_b", "ln_mix_g", "ln_mix_b", "ln_mem_g", "ln_mem_b", "ln_ffn_g", "ln_ffn_b")

    x = x[0]
    mem = mem[0]
    positions = positions[0]
    target = loss_target[0]
    S, D = x.shape
    depth = ln_mix_g.shape[0]
    alpha = (2 * depth) ** 0.25
    ret_dk = D // RET_HEADS
    ret_dv = 2 * D // RET_HEADS
    ret_qkw = RET_HEADS * ret_dk
    ret_vw = RET_HEADS * ret_dv
    xa_d = D // XA_HEADS
    assert 2 * ret_qkw == ret_vw and MLA_NOPE == LANES and MLA_V == LANES and MLA_ROPE == LANES // 2
    assert MLA_Q_RANK == MLA_KV_RANK and CHUNK & (CHUNK - 1) == 0
    core = lax.axis_index("c").astype(jnp.int32).reshape(1)
    dev = 4 * lax.axis_index("x") + 2 * lax.axis_index("y") + lax.axis_index("c")

    gather, scatter = _Link(GATHER_BYTES_PER_US), _Link(SCATTER_BYTES_PER_US)
    shard_b = {n: weights[n].astype(PAY_DTYPE) for n in BIG}
    staged = set()

    def units(layer):
        mixer = ("ret_w_in", "ret_w_out") if layer % 2 == 0 else ("mla_w_in", "mla_w_uq", "mla_w_ukv", "mla_w_out")
        return [(n, layer // 2) for n in mixer] + [(n, layer) for n in
                                                    ("xa_w_q", "xa_w_kv", "xa_w_out", "ffn_w_up", "ffn_w_down")]

    def geometry(n):
        _, k, nn = weights[n].shape
        return BIG[n], k, nn

    def after_first_level(key, full, axis, k, nn):
        def to_sibling():
            chunk = _Chunk("gather2", None, key, full, 0, 0, k, nn, axis=axis, k=k, n=nn)
            gather.side.append((chunk, lambda: staged.add(key)))

        def relay():
            gather.push("relay", None, key, full, 0, k // 2, nn, done=to_sibling, front=True,
                        axis=axis, k=k, n=nn, half=k // 2)
        return relay

    for layer in range(depth):
        for n, j in units(layer):
            axis, k, nn = geometry(n)
            full = (N_DEV * k, nn) if axis == 1 else (k, N_DEV * nn)
            gather.push("gather", shard_b[n], (n, j), full, j, k, nn,
                        done=after_first_level((n, j), full, axis, k, nn), axis=axis, k=k, n=nn)

    def weight(n, j):
        key = (n, j)
        while key not in staged:
            gather.flush(key, name="gather_rest")
            gather.settle()
            if not any(ch.key == key for ch in gather.queue):
                break
        while key not in staged:
            assert gather.side
            gather.carry(LEVEL2_US, name="gather_level2")
            gather.settle()
        return gather.bufs[key]

    def pack(arrays):
        flat = jnp.concatenate([a.reshape(-1) for a in arrays])
        rows = -(-flat.shape[0] // (8 * LANES)) * 8
        return jnp.pad(flat, (0, rows * LANES - flat.shape[0])).reshape(rows, LANES)

    def unpack(flat, like):
        out, at = [], 0
        for a in like:
            out.append(flat[at:at + a.size].reshape(a.shape))
            at += a.size
        return out

    small_local = [weights[n] for n in SMALL_CUT]
    blk = pack(small_local)
    allsmall = gather_small(blk, name="gather_small_weights").reshape(N_DEV, -1)
    per_dev = [unpack(allsmall[d], small_local) for d in range(N_DEV)]
    conv_w_full, qg_full, kvg_full = [jnp.concatenate([per_dev[d][i] for d in range(N_DEV)], axis=-1)
                                      for i in range(len(SMALL_CUT))]

    rcos, rsin = _rope_tables(positions, ret_dk)
    mcos, msin = _rope_tables(positions, MLA_ROPE)
    zq = jnp.zeros((S, LANES // 2 - MLA_ROPE // 2), F32)
    mla_ta = jnp.concatenate([mcos, zq, mcos, zq], axis=1)
    mla_tb = jnp.concatenate([-msin, zq, msin, zq], axis=1)
    log_gamma = jnp.log(1.0 - 2.0 ** (-5.0 - jnp.arange(RET_HEADS, dtype=F32)))
    log_gamma = jnp.broadcast_to(log_gamma[:, None, None], (RET_HEADS, 8, LANES))
    mla_scale = (MLA_NOPE + MLA_ROPE) ** -0.5
    xa_scale = xa_d ** -0.5
    mem_b = mem.astype(MXU_DTYPE)

    def vec(a, l):
        return a[l][None, :]

    saved = []
    h = x
    hb = x.astype(MXU_DTYPE)
    for layer in range(depth):
        j = layer // 2
        sv = {}
        sv["hb_mix"] = hb
        if layer % 2 == 0:
            proj = mm(hb, weight("ret_w_in", j), link=gather, name="ret_in")
            qk = ret_rope_fwd(proj, rcos, rsin, 2 * RET_HEADS, ret_dk // 2, RET_HEADS, ret_dk ** -0.5, gather,
                              name="ret_rope")
            (o,) = attn_fwd((qk, ret_dk, 0, 1), (qk, ret_dk, RET_HEADS, 1), (proj, ret_dv, RET_HEADS, 1),
                            heads=RET_HEADS, softmax=False, masked=True, log_gamma=log_gamma, link=gather,
                            est_us=ATTN_US["ret"][0], name="ret_attn")
            yb = ret_gate_fwd(o, proj, 2, vec(ret_gn_g, j), RET_HEADS, gather, name="ret_gate")
            mix = mm(yb, weight("ret_w_out", j), link=gather, name="ret_out")
            sv.update(proj=proj, qk=qk, o=o, yb=yb)
        else:
            proj = mm(hb, weight("mla_w_in", j), link=gather, name="mla_in")
            cq = rms_fwd(proj, 0, MLA_Q_RANK, vec(qg_full, j), name="mla_q_norm")
            ckv = rms_fwd(proj, 1, MLA_KV_RANK, vec(kvg_full, j), name="mla_kv_norm")
            qf = mm(cq, weight("mla_w_uq", j), link=gather, name="mla_uq")
            kvf = mm(ckv, weight("mla_w_ukv", j), link=gather, name="mla_ukv")
            q3 = qf.reshape(S, MLA_HEADS, MLA_NOPE + MLA_ROPE)
            kv3 = kvf.reshape(S, MLA_HEADS, MLA_NOPE + MLA_V)
            k_rope = jnp.broadcast_to(proj[:, None, MLA_Q_RANK + MLA_KV_RANK:], (S, MLA_HEADS, MLA_ROPE))
            q_pad = _mla_pad(q3[..., :MLA_NOPE], q3[..., MLA_NOPE:])
            k_pad = _mla_pad(kv3[..., :MLA_NOPE], k_rope)
            (qr,) = mla_rope(q_pad, mla_ta, mla_tb, MLA_HEADS, backward=False, head_sum=False,
                             out_dtype=MXU_DTYPE, link=gather, name="mla_rope_q")
            (kr,) = mla_rope(k_pad, mla_ta, mla_tb, MLA_HEADS, backward=False, head_sum=False,
                             out_dtype=MXU_DTYPE, link=gather, name="mla_rope_k")
            o, ob, lse = attn_fwd((qr, MLA_PAD, 0, 1), (kr, MLA_PAD, 0, 1), (kvf, MLA_V, 1, 2), heads=MLA_HEADS,
                              softmax=True, masked=True, scale=mla_scale, link=gather, est_us=ATTN_US["mla"][0],
                              name="mla_attn")
            mix = mm(ob, weight("mla_w_out", j), link=gather, name="mla_out")
            sv.update(proj=proj, cq=cq, ckv=ckv, kvf=kvf, qr=qr, kr=kr, o=o, ob=ob, lse=lse)
        h, hb, sv["xhat_mix"], sv["rstd_mix"] = ln_fwd(h, mix, vec(ln_mix_g, layer), vec(ln_mix_b, layer), alpha,
                                                       gather, name="ln_mix")
        sv["hb_mem"] = hb
        q = mm(hb, weight("xa_w_q", layer), link=gather, name="xa_q")
        kvm = mm(mem_b, weight("xa_w_kv", layer), link=gather, name="xa_kv")
        o, ob, lse = attn_fwd((q, xa_d, 0, 1), (kvm, xa_d, 0, 1), (kvm, xa_d, XA_HEADS, 1), heads=XA_HEADS,
                          softmax=True, masked=False, scale=xa_scale, link=gather, est_us=ATTN_US["xa"][0],
                          name="xa_attn")
        mix = mm(ob, weight("xa_w_out", layer), link=gather, name="xa_out")
        sv.update(xa_q=q, xa_kvm=kvm, xa_o=o, xa_ob=ob, xa_lse=lse)
        h, hb, sv["xhat_mem"], sv["rstd_mem"] = ln_fwd(h, mix, vec(ln_mem_g, layer), vec(ln_mem_b, layer), alpha,
                                                       gather, name="ln_mem")
        sv["hb_ffn"] = hb
        hup = mm(hb, weight("ffn_w_up", layer), link=gather, name="ffn_up")
        u = conv_glu_fwd(hup, conv_w_full[layer], vec(ffn_conv_b, layer), gather, name="ffn_conv")
        mix = mm(u, weight("ffn_w_down", layer), link=gather, name="ffn_down")
        sv.update(hup=hup, u=u)
        h, hb, sv["xhat_ffn"], sv["rstd_ffn"] = ln_fwd(h, mix, vec(ln_ffn_g, layer), vec(ln_ffn_b, layer), alpha,
                                                       gather, name="ln_ffn")
        saved.append(sv)

    dh, loss_blk = loss_head(h, target, name="loss_head")
    loss = lax.psum(loss_blk[0, 0], ("x", "y", "c"))

    small = {n: [None] * weights[n].shape[0] for n in REPLICATED + SMALL_CUT}
    W = gather.bufs

    def wgrad(n, a, d, l, tag):
        axis, k, nn = geometry(n)
        if axis == 2:
            slabs = mm(a, d, mode="tn", out_dtype=PAY_DTYPE, slab_width=nn, link=scatter, name=tag)
        else:
            slabs = mm(a, d, mode="tn", out_dtype=PAY_DTYPE, link=scatter, name=tag).reshape(N_DEV, k, nn)
        n_layers = weights[n].shape[0]
        pair = ("pair", n, l)

        def then():
            sums = pair_sum(slabs, scatter.bufs.pop(pair), core, name="pairsum_" + n)
            scatter.push("scatter", sums, n, (n_layers, N_CHIP, k, nn), l, k, nn)

        scatter.side.append((_Chunk("swap", slabs, pair, (N_CHIP, k, nn), 0, 0, k, nn), then))

    for layer in reversed(range(depth)):
        j = layer // 2
        sv = saved[layer]
        dz, dzb, dg, db = ln_bwd(dh, sv["xhat_ffn"], sv["rstd_ffn"], vec(ln_ffn_g, layer), scatter, name="ln_ffn_bwd")
        small["ln_ffn_g"][layer], small["ln_ffn_b"][layer] = dg[0], db[0]
        wgrad("ffn_w_down", sv["u"], dzb, layer, "ffn_down_dw")
        du = mm(dzb, W[("ffn_w_down", layer)], mode="nt", link=scatter, name="ffn_down_dx")
        dhg, dhv, dwg, dwv, dbg, dbv = conv_glu_bwd(du, sv["hup"], conv_w_full[layer], vec(ffn_conv_b, layer),
                                                    scatter, name="ffn_conv_bwd")
        small["ffn_conv_w"][layer] = jnp.concatenate([dwg, dwv], axis=1)
        small["ffn_conv_b"][layer] = jnp.concatenate([dbg, dbv], axis=1)[0]
        dhup = jnp.concatenate([dhg, dhv], axis=1)
        wgrad("ffn_w_up", sv["hb_ffn"], dhup, layer, "ffn_up_dw")
        dh = mm(dhup, W[("ffn_w_up", layer)], mode="nt", add=dz, add_scale=alpha, link=scatter, name="ffn_up_dx")
        dz, dzb, dg, db = ln_bwd(dh, sv["xhat_mem"], sv["rstd_mem"], vec(ln_mem_g, layer), scatter, name="ln_mem_bwd")
        small["ln_mem_g"][layer], small["ln_mem_b"][layer] = dg[0], db[0]
        wgrad("xa_w_out", sv["xa_ob"], dzb, layer, "xa_out_dw")
        do = mm(dzb, W[("xa_w_out", layer)], mode="nt", link=scatter, name="xa_out_dx")
        dq, dk, dv = attn_bwd((sv["xa_q"], xa_d, 0, 1), (sv["xa_kvm"], xa_d, 0, 1), (sv["xa_kvm"], xa_d, XA_HEADS, 1),
                              do, heads=XA_HEADS, softmax=True, masked=False, scale=xa_scale, o=sv["xa_o"],
                              lse=sv["xa_lse"], link=scatter, est_us=ATTN_US["xa"][1], name="xa_attn_bwd")
        dqb = dq.astype(MXU_DTYPE)
        wgrad("xa_w_kv", mem_b, jnp.concatenate([dk, dv], axis=1).astype(MXU_DTYPE), layer, "xa_kv_dw")
        wgrad("xa_w_q", sv["hb_mem"], dqb, layer, "xa_q_dw")
        dh = mm(dqb, W[("xa_w_q", layer)], mode="nt", add=dz, add_scale=alpha, link=scatter, name="xa_q_dx")
        dz, dzb, dg, db = ln_bwd(dh, sv["xhat_mix"], sv["rstd_mix"], vec(ln_mix_g, layer), scatter, name="ln_mix_bwd")
        small["ln_mix_g"][layer], small["ln_mix_b"][layer] = dg[0], db[0]
        if layer % 2 == 0:
            wgrad("ret_w_out", sv["yb"], dzb, j, "ret_out_dw")
            dy = mm(dzb, W[("ret_w_out", j)], mode="nt", link=scatter, name="ret_out_dx")
            do, dgate, dgn = ret_gate_bwd(dy, sv["o"], sv["proj"], 2, vec(ret_gn_g, j), RET_HEADS, name="ret_gate_bwd")
            small["ret_gn_g"][j] = dgn[0]
            dq, dk, dv = attn_bwd((sv["qk"], ret_dk, 0, 1), (sv["qk"], ret_dk, RET_HEADS, 1),
                                  (sv["proj"], ret_dv, RET_HEADS, 1), do, heads=RET_HEADS, softmax=False,
                                  masked=True, log_gamma=log_gamma, link=scatter, est_us=ATTN_US["ret"][1],
                                  name="ret_attn_bwd")
            dqk = ret_rope_bwd(dq, dk, rcos, rsin, ret_dk // 2, ret_dk ** -0.5, name="ret_rope_bwd")
            dproj = jnp.concatenate([dqk, dv.astype(MXU_DTYPE), dgate], axis=1)
            wgrad("ret_w_in", sv["hb_mix"], dproj, j, "ret_in_dw")
            dh = mm(dproj, W[("ret_w_in", j)], mode="nt", add=dz, add_scale=alpha, link=scatter, name="ret_in_dx")
        else:
            wgrad("mla_w_out", sv["ob"], dzb, j, "mla_out_dw")
            do = mm(dzb, W[("mla_w_out", j)], mode="nt", link=scatter, name="mla_out_dx")
            dq, dk, dv = attn_bwd((sv["qr"], MLA_PAD, 0, 1), (sv["kr"], MLA_PAD, 0, 1), (sv["kvf"], MLA_V, 1, 2), do,
                                  heads=MLA_HEADS, softmax=True, masked=True, scale=mla_scale, o=sv["o"],
                                  lse=sv["lse"], link=scatter, est_us=ATTN_US["mla"][1], name="mla_attn_bwd")
            (dq_un,) = mla_rope(dq, mla_ta, mla_tb, MLA_HEADS, backward=True, head_sum=False, out_dtype=F32,
                                name="mla_rope_q_bwd")
            dk_un, dk_rope_sum = mla_rope(dk, mla_ta, mla_tb, MLA_HEADS, backward=True, head_sum=True, out_dtype=F32,
                                          name="mla_rope_k_bwd")
            dq_nope, dq_rope = _mla_unpad(dq_un, MLA_HEADS)
            dqf = jnp.concatenate([dq_nope, dq_rope], axis=2).reshape(S, -1).astype(MXU_DTYPE)
            dk_nope, _ = _mla_unpad(dk_un, MLA_HEADS)
            dkvf = jnp.concatenate([dk_nope, dv.reshape(S, MLA_HEADS, MLA_V)], axis=2).reshape(S, -1).astype(MXU_DTYPE)
            half = MLA_ROPE // 2
            dk_rope = jnp.concatenate([dk_rope_sum[:, :half], dk_rope_sum[:, LANES // 2:LANES // 2 + half]], axis=1)
            wgrad("mla_w_uq", sv["cq"], dqf, j, "mla_uq_dw")
            dcq = mm(dqf, W[("mla_w_uq", j)], mode="nt", link=scatter, name="mla_uq_dx")
            wgrad("mla_w_ukv", sv["ckv"], dkvf, j, "mla_ukv_dw")
            dckv = mm(dkvf, W[("mla_w_ukv", j)], mode="nt", link=scatter, name="mla_ukv_dx")
            dcq_in, dqg = rms_bwd(dcq, sv["proj"], 0, MLA_Q_RANK, vec(qg_full, j), name="mla_q_norm_bwd")
            dckv_in, dkvg = rms_bwd(dckv, sv["proj"], 1, MLA_KV_RANK, vec(kvg_full, j), name="mla_kv_norm_bwd")
            small["mla_q_norm_g"][j], small["mla_kv_norm_g"][j] = dqg[0], dkvg[0]
            dproj = jnp.concatenate([dcq_in, dckv_in, dk_rope], axis=1).astype(MXU_DTYPE)
            wgrad("mla_w_in", sv["hb_mix"], dproj, j, "mla_in_dw")
            dh = mm(dproj, W[("mla_w_in", j)], mode="nt", add=dz, add_scale=alpha, link=scatter, name="mla_in_dx")
    grad_x = dh[None]

    grads, deltas, new_m, new_v = {}, {}, {}, {}
    scatter.ride_out(name="scatter_rest")
    scatter.flush(None, name="scatter_rest")
    assert not scatter.queue and not scatter.side
    for n in BIG:
        grads[n], deltas[n], new_m[n], new_v[n] = adam_update(scatter.bufs[n], weights[n], mom_m[n], mom_v[n],
                                                              name="adam_" + n)

    small_names = list(REPLICATED + SMALL_CUT)
    partial = [piece for n in small_names for piece in small[n]]
    allpart = gather_small(pack(partial), name="gather_small_grads")
    rows = allpart.shape[0] // N_DEV
    allpart = allpart.reshape(N_DEV, rows, LANES)

    rep_names = list(REPLICATED)
    rep_w = pack([weights[n] for n in rep_names])
    rep_m = pack([mom_m[n] for n in rep_names])
    rep_v = pack([mom_v[n] for n in rep_names])
    rep_rows = rep_w.shape[0]
    rep_size = sum(weights[n].size for n in rep_names)
    flat_parts = allpart.reshape(N_DEV, rows * LANES)
    rep_parts = jnp.pad(flat_parts[:, :rep_size], ((0, 0), (0, rep_rows * LANES - rep_size)))
    outs = adam_update(rep_parts.reshape(1, N_DEV, rep_rows, LANES), rep_w[None], rep_m[None], rep_v[None],
                       name="adam_replicated")
    for o_, dst in zip(outs, (grads, deltas, new_m, new_v)):
        for n, a in zip(rep_names, unpack(o_.reshape(-1), [weights[n] for n in rep_names])):
            dst[n] = a

    cut_names = list(SMALL_CUT)
    cut_parts = []
    at = rep_size
    for n in cut_names:
        full_shape = (len(small[n]),) + small[n][0].shape
        whole = flat_parts[:, at:at + math.prod(full_shape)].reshape((N_DEV,) + full_shape)
        at += math.prod(full_shape)
        width = weights[n].shape[-1]
        start = (0,) * (whole.ndim - 1) + (dev * width,)
        mine = lax.dynamic_slice(whole, start, whole.shape[:-1] + (width,))
        cut_parts.append(mine.reshape(N_DEV, -1))
    cut_parts = jnp.concatenate(cut_parts, axis=1)
    cut_w = pack([weights[n] for n in cut_names])
    cut_m = pack([mom_m[n] for n in cut_names])
    cut_v = pack([mom_v[n] for n in cut_names])
    cut_rows = cut_w.shape[0]
    cut_parts = jnp.pad(cut_parts, ((0, 0), (0, cut_rows * LANES - cut_parts.shape[1])))
    outs = adam_update(cut_parts.reshape(1, N_DEV, cut_rows, LANES), cut_w[None], cut_m[None], cut_v[None],
                       name="adam_small_cut")
    for o_, dst in zip(outs, (grads, deltas, new_m, new_v)):
        for n, a in zip(cut_names, unpack(o_.reshape(-1), [weights[n] for n in cut_names])):
            dst[n] = a

    return (loss, grad_x, *[grads[n] for n in order], *[deltas[n] for n in order],
            *[new_m[n] for n in order], *[new_v[n] for n in order])
```

```python
import functools
import math

import jax
import jax.numpy as jnp
from jax import lax
from jax.experimental import pallas as pl
from jax.experimental.pallas import tpu as pltpu

F32 = jnp.float32
MXU_DTYPE = jnp.bfloat16
PAY_DTYPE = jnp.bfloat16
MESH = pl.DeviceIdType.MESH
N_DEV = 8
N_CHIP = 4

DEPTH = 4
CHUNK = 64
RET_HEADS = 8
MLA_HEADS = 16
MLA_Q_RANK = 512
MLA_KV_RANK = 512
MLA_NOPE = 128
MLA_ROPE = 64
MLA_V = 128
MLA_PAD = 256
XA_HEADS = 4
ROPE_BASE = 10000.0
LN_EPS = 1e-5
RMS_EPS = 1e-6
NEG_INF = -1e30
ADAM_LR = 0.001
ADAM_B1 = 0.9
ADAM_B2 = 0.999
ADAM_EPS = 1e-08
ADAM_WD = 0.01
ADAM_STEP = 10

LANES = 128
VMEM_LIMIT_BYTES = 48 * 1024 * 1024
TILE_M = 1024
TILE_N = 1024
TILE_K = 2048
SLAB_TILE_N = 1536
ATTN_TILE = 512
ATTN_PARTS = 2
ROW_TILE = 256

MXU_FLOPS_PER_US = 7.0e8
GATHER_BYTES_PER_US = 5.5e4
SCATTER_BYTES_PER_US = 2.2e4
CHUNK_BYTES = 1280 * 1024
HOST_CHUNKS = 12
LEVEL2_US = 30.0
MIN_HOST_US = 45.0
SIDE_CHUNKS = 4
ATTN_US = {"ret": (125.0, 160.0), "mla": (320.0, 270.0), "xa": (35.0, 35.0)}
ROWWISE_US = {"ln": (25.0, 22.0), "conv": (65.0, 110.0), "rope": 20.0, "gate": 33.0}


def _tile(n, cap, mult=LANES):
    best = None
    for t in range(mult, min(n, cap) + 1, mult):
        if n % t == 0:
            best = t
    return n if best is None else best


def _params(n_axes):
    return pltpu.CompilerParams(dimension_semantics=("arbitrary",) * n_axes,
                                vmem_limit_bytes=VMEM_LIMIT_BYTES)


def _place():
    return lax.axis_index("x"), lax.axis_index("y"), lax.axis_index("c")


def _other_chips(x, y):
    return [(1 - x, y), (x, 1 - y), (1 - x, 1 - y)]


def _remote(src, dst, send, recv, to):
    return pltpu.make_async_remote_copy(src_ref=src, dst_ref=dst, send_sem=send, recv_sem=recv,
                                        device_id=to, device_id_type=MESH)


class _Chunk:
    def __init__(self, kind, src, key, dst_shape, layer, r0, rc, cols, axis=None, k=None, n=None, half=None):
        self.kind, self.src, self.key, self.dst_shape = kind, src, key, dst_shape
        self.layer, self.r0, self.rc, self.axis, self.k, self.n, self.half = layer, r0, rc, axis, k, n, half
        self.nbytes = rc * cols * jnp.dtype(PAY_DTYPE).itemsize

    def copies(self, src_ref, dst_ref, send, recv, loc, i):
        x, y, c = _place()
        chips = _other_chips(x, y)
        rows = pl.ds(self.r0, self.rc)
        sibling, x_nbr, y_nbr, diagonal = (x, y, 1 - c), (1 - x, y, c), (x, 1 - y, c), (1 - x, 1 - y, c)
        if self.kind == "gather":
            peers = [sibling, x_nbr, y_nbr]
            src = src_ref.at[self.layer, rows, :]

            def slab(px, py, pc):
                idx = 4 * px + 2 * py + pc
                if self.axis == 1:
                    return dst_ref.at[pl.ds(idx * self.k + self.r0, self.rc), :]
                return dst_ref.at[rows, pl.ds(idx * self.n, self.n)]

            sends = [_remote(src, slab(x, y, c), send.at[4 * i + t], recv.at[4 * i + t], p)
                     for t, p in enumerate(peers)]
            recvs = [_remote(src, slab(*p), send.at[4 * i + t], recv.at[4 * i + t], p)
                     for t, p in enumerate(peers)]
            return sends, recvs, pltpu.make_async_copy(src, slab(x, y, c), loc.at[i])
        if self.kind == "relay":
            def piece(p, r):
                idx = 4 * p[0] + 2 * p[1] + p[2]
                if self.axis == 1:
                    return dst_ref.at[pl.ds(idx * self.k + r, self.rc), :]
                return dst_ref.at[pl.ds(r, self.rc), pl.ds(idx * self.n, self.n)]

            top, bottom = self.r0, self.half + self.r0
            sends = [_remote(piece(y_nbr, top), piece(y_nbr, top), send.at[4 * i], recv.at[4 * i], x_nbr),
                     _remote(piece(x_nbr, bottom), piece(x_nbr, bottom), send.at[4 * i + 1], recv.at[4 * i + 1], y_nbr)]
            recvs = [_remote(piece(diagonal, top), piece(diagonal, top), send.at[4 * i], recv.at[4 * i], x_nbr),
                     _remote(piece(diagonal, bottom), piece(diagonal, bottom), send.at[4 * i + 1], recv.at[4 * i + 1],
                             y_nbr)]
            return sends, recvs, None
        if self.kind == "gather2":
            def slab(px, py, pc):
                idx = 4 * px + 2 * py + pc
                if self.axis == 1:
                    return dst_ref.at[pl.ds(idx * self.k, self.k), :]
                return dst_ref.at[:, pl.ds(idx * self.n, self.n)]

            sends = [_remote(slab(px, py, c), slab(px, py, c), send.at[4 * i + t], recv.at[4 * i + t], sibling)
                     for t, (px, py) in enumerate(chips)]
            recvs = [_remote(slab(px, py, 1 - c), slab(px, py, 1 - c), send.at[4 * i + t], recv.at[4 * i + t], sibling)
                     for t, (px, py) in enumerate(chips)]
            return sends, recvs, None
        if self.kind == "swap":
            sends = [_remote(src_ref.at[2 * j + (1 - c)], dst_ref.at[j], send.at[4 * i + j], recv.at[4 * i + j], sibling)
                     for j in range(N_CHIP)]
            recvs = [_remote(src_ref.at[2 * j + c], dst_ref.at[j], send.at[4 * i + j], recv.at[4 * i + j], sibling)
                     for j in range(N_CHIP)]
            return sends, recvs, None
        mine = 2 * x + y
        own = src_ref.at[mine, rows, :]
        sends = [_remote(src_ref.at[2 * px + py, rows, :], dst_ref.at[self.layer, mine, rows, :],
                         send.at[4 * i + t], recv.at[4 * i + t], (px, py, c)) for t, (px, py) in enumerate(chips)]
        recvs = [_remote(own, dst_ref.at[self.layer, 2 * px + py, rows, :],
                         send.at[4 * i + t], recv.at[4 * i + t], (px, py, c)) for t, (px, py) in enumerate(chips)]
        return sends, recvs, pltpu.make_async_copy(own, dst_ref.at[self.layer, mine, rows, :], loc.at[i])


def _call(body, *, grid, in_specs, out_specs, out_shape, operands, scratch=(), name, chunks=(), bufs=None):
    n_in, n_out, n_scr = len(operands), len(out_shape), len(scratch)
    params = _params(len(grid)) if grid else pltpu.CompilerParams(vmem_limit_bytes=VMEM_LIMIT_BYTES)
    if not chunks:
        return pl.pallas_call(
            body, out_shape=tuple(out_shape), grid=grid, in_specs=list(in_specs), out_specs=tuple(out_specs),
            scratch_shapes=list(scratch), compiler_params=params, name=name)(*operands)
    srcs, keys = [], []
    for ch in chunks:
        if ch.src is not None and not any(ch.src is s for s in srcs):
            srcs.append(ch.src)
        if ch.key not in keys:
            keys.append(ch.key)
    shape_of = {ch.key: ch.dst_shape for ch in chunks}
    held = [k for k in keys if bufs.get(k) is not None]
    extra = srcs + [bufs[k] for k in held]
    aliases = {n_in + len(srcs) + i: n_out + keys.index(k) for i, k in enumerate(held)}
    anywhere = pl.BlockSpec(memory_space=pl.ANY)
    n_extra, n_keys, n_ch = len(extra), len(keys), len(chunks)

    def hosted(*refs):
        src_refs = refs[n_in:n_in + len(srcs)]
        o0 = n_in + n_extra
        dst_refs = refs[o0 + n_out:o0 + n_out + n_keys]
        s0 = o0 + n_out + n_keys
        send, recv, loc = refs[s0 + n_scr:]
        sends, recvs, locs = [], [], []
        for i, ch in enumerate(chunks):
            src_ref = None if ch.src is None else src_refs[[ch.src is s for s in srcs].index(True)]
            s_, r_, l_ = ch.copies(src_ref, dst_refs[keys.index(ch.key)], send, recv, loc, i)
            sends += s_
            recvs += r_
            if l_ is not None:
                locs.append(l_)

        def start():
            for cp in locs + sends:
                cp.start()

        def finish():
            for cp in recvs:
                cp.wait_recv()
            for cp in sends:
                cp.wait_send()
            for cp in locs:
                cp.wait()

        if grid:
            first = functools.reduce(jnp.logical_and, [pl.program_id(a) == 0 for a in range(len(grid))])
            last = functools.reduce(jnp.logical_and, [pl.program_id(a) == grid[a] - 1 for a in range(len(grid))])
            pl.when(first)(start)
            body(*refs[:n_in], *refs[o0:o0 + n_out], *refs[s0:s0 + n_scr])
            pl.when(last)(finish)
        else:
            start()
            body(*refs[:n_in], *refs[o0:o0 + n_out], *refs[s0:s0 + n_scr])
            finish()

    res = pl.pallas_call(
        hosted, out_shape=tuple(out_shape) + tuple(jax.ShapeDtypeStruct(shape_of[k], PAY_DTYPE) for k in keys),
        grid=grid, in_specs=list(in_specs) + [anywhere] * n_extra, out_specs=tuple(out_specs) + (anywhere,) * n_keys,
        scratch_shapes=list(scratch) + [pltpu.SemaphoreType.DMA((4 * n_ch,)), pltpu.SemaphoreType.DMA((4 * n_ch,)),
                                        pltpu.SemaphoreType.DMA((n_ch,))],
        input_output_aliases=aliases, compiler_params=params, name=name)(*operands, *extra)
    for k, arr in zip(keys, res[n_out:]):
        bufs[k] = arr
    return res[:n_out]


class _Link:
    def __init__(self, bytes_per_us):
        self.rate = bytes_per_us
        self.queue, self.bufs = [], {}
        self.side = []
        self.after = []
        self.left = {}
        self.on_done = {}

    def push(self, kind, src, key, dst_shape, layer, rows, cols, done=None, front=False, **geometry):
        fits = [r for r in range(16, rows + 1, 16) if rows % r == 0 and r * cols * 2 <= CHUNK_BYTES]
        rc = max(fits) if fits else min(r for r in range(16, rows + 1, 16) if rows % r == 0)
        new = [_Chunk(kind, src, key, dst_shape, layer, r0, rc, cols, **geometry) for r0 in range(0, rows, rc)]
        self.queue = new + self.queue if front else self.queue + new
        self.left[key] = self.left.get(key, 0) + rows // rc
        if done is not None:
            self.on_done[key] = done

    def settle(self):
        todo, self.after = self.after, []
        for f in todo:
            f()

    def _pop(self, count):
        out = []
        for ch in self.queue[:count]:
            self.left[ch.key] -= 1
            if self.left[ch.key] == 0 and ch.key in self.on_done:
                self.after.append(self.on_done.pop(ch.key))
            out.append(ch)
        self.queue = self.queue[count:]
        return out

    def _side(self):
        riders, self.side = self.side[:SIDE_CHUNKS], self.side[SIDE_CHUNKS:]
        self.after += [then for _, then in riders if then is not None]
        return [ch for ch, _ in riders]

    def take(self, est_us, short_too=False):
        self.settle()
        out = self._side()
        if est_us < MIN_HOST_US and not short_too:
            return out
        budget, count = est_us * self.rate, 0
        while count < min(len(self.queue), HOST_CHUNKS) and self.queue[count].nbytes <= 2 * budget:
            budget -= self.queue[count].nbytes
            count += 1
        return out + self._pop(count)

    def flush(self, key, name):
        self.settle()
        if key is None:
            n = len(self.queue)
        else:
            n = max([i + 1 for i, ch in enumerate(self.queue) if ch.key == key], default=0)
        while n > 0:
            group = self._side() + self._pop(min(n, HOST_CHUNKS))
            n -= min(n, HOST_CHUNKS)
            _call(lambda: None, grid=(), in_specs=[], out_specs=(), out_shape=(), operands=[], name=name,
                  chunks=group, bufs=self.bufs)
            self.settle()

    def carry(self, est_us, name):
        _call(lambda: None, grid=(), in_specs=[], out_specs=(), out_shape=(), operands=[], name=name,
              chunks=self.take(est_us, short_too=True), bufs=self.bufs)

    def ride_out(self, name):
        self.settle()
        while self.side:
            _call(lambda: None, grid=(), in_specs=[], out_specs=(), out_shape=(), operands=[], name=name,
                  chunks=self._side(), bufs=self.bufs)
            self.settle()


def mm(a, b, *, mode="nn", out_dtype=F32, add=None, add_scale=1.0, slab_width=None, link=None, name):
    if mode == "nn":
        (M, K), N = a.shape, b.shape[-1]
    elif mode == "nt":
        (M, K), N = a.shape, b.shape[-2]
    else:
        (K, M), N = a.shape, b.shape[-1]
    tm, tk = _tile(M, TILE_M), _tile(K, TILE_K)
    tn = _tile(N, TILE_N) if slab_width is None else _tile(slab_width, SLAB_TILE_N)
    nk = K // tk
    dims = {"nn": (((1,), (0,)), ((), ())), "nt": (((1,), (1,)), ((), ())),
            "tn": (((0,), (0,)), ((), ()))}[mode]

    a_spec = (pl.BlockSpec((tk, tm), lambda i, j, k: (k, i)) if mode == "tn"
              else pl.BlockSpec((tm, tk), lambda i, j, k: (i, k)))
    b_spec = (pl.BlockSpec((tn, tk), lambda i, j, k: (j, k)) if mode == "nt"
              else pl.BlockSpec((tk, tn), lambda i, j, k: (k, j)))
    in_specs, operands = [a_spec, b_spec], [a, b]
    if add is not None:
        in_specs.append(pl.BlockSpec((tm, tn), lambda i, j, k: (i, j)))
        operands.append(add)
    if slab_width is None:
        out_shape = jax.ShapeDtypeStruct((M, N), out_dtype)
        out_spec = pl.BlockSpec((tm, tn), lambda i, j, k: (i, j))
    else:
        per = slab_width // tn
        out_shape = jax.ShapeDtypeStruct((N // slab_width, M, slab_width), out_dtype)
        out_spec = pl.BlockSpec((None, tm, tn), lambda i, j, k: (j // per, i, j % per))
    n_in = len(operands)

    def body(*refs):
        a_ref, b_ref = refs[0], refs[1]
        add_ref = refs[2] if add is not None else None
        o_ref = refs[n_in]
        k = pl.program_id(2)
        part = lax.dot_general(a_ref[...].astype(MXU_DTYPE), b_ref[...].astype(MXU_DTYPE),
                               dims, preferred_element_type=F32)

        def finish(r):
            if add_ref is not None:
                r = r + add_scale * add_ref[...].astype(F32)
            o_ref[...] = r.astype(o_ref.dtype)

        if nk == 1:
            finish(part)
        else:
            acc_ref = refs[n_in + 1]

            @pl.when(k == 0)
            def _():
                acc_ref[...] = part

            @pl.when(jnp.logical_and(k > 0, k < nk - 1))
            def _():
                acc_ref[...] += part

            @pl.when(k == nk - 1)
            def _():
                finish(acc_ref[...] + part)

    chunks = link.take(2.0 * M * N * K / MXU_FLOPS_PER_US) if link is not None else ()
    (out,) = _call(body, grid=(M // tm, N // tn, nk), in_specs=in_specs, out_specs=(out_spec,), out_shape=(out_shape,),
                   operands=operands, scratch=[pltpu.VMEM((tm, tn), F32)] if nk > 1 else [], name=name,
                   chunks=chunks, bufs=link.bufs if link is not None else None)
    return out


def ln_fwd(h, f, g, b, alpha, link, name):
    S, D = h.shape
    tm = _tile(S, ROW_TILE, 8)

    def body(h_ref, f_ref, g_ref, b_ref, y_ref, yb_ref, xhat_ref, rstd_ref):
        z = alpha * h_ref[...] + f_ref[...]
        mu = jnp.mean(z, axis=-1, keepdims=True)
        zc = z - mu
        var = jnp.mean(zc * zc, axis=-1, keepdims=True)
        rstd = lax.rsqrt(var + LN_EPS)
        xhat = zc * rstd
        y = xhat * g_ref[...] + b_ref[...]
        y_ref[...] = y
        yb_ref[...] = y.astype(yb_ref.dtype)
        xhat_ref[...] = xhat
        rstd_ref[...] = rstd

    row = pl.BlockSpec((tm, D), lambda i: (i, 0))
    vec = pl.BlockSpec((1, D), lambda i: (0, 0))
    return _call(
        body, grid=(S // tm,), in_specs=[row, row, vec, vec],
        out_specs=(row, row, row, pl.BlockSpec((tm, 1), lambda i: (i, 0))),
        out_shape=(jax.ShapeDtypeStruct((S, D), F32), jax.ShapeDtypeStruct((S, D), MXU_DTYPE),
                   jax.ShapeDtypeStruct((S, D), F32), jax.ShapeDtypeStruct((S, 1), F32)),
        operands=[h, f, g, b], name=name, chunks=link.take(ROWWISE_US["ln"][0]), bufs=link.bufs)


def ln_bwd(dy, xhat, rstd, g, link, name):
    S, D = dy.shape
    tm = _tile(S, ROW_TILE, 8)

    def body(dy_ref, xhat_ref, rstd_ref, g_ref, dz_ref, dzb_ref, dg_ref, db_ref):
        @pl.when(pl.program_id(0) == 0)
        def _():
            dg_ref[...] = jnp.zeros_like(dg_ref)
            db_ref[...] = jnp.zeros_like(db_ref)

        dy = dy_ref[...]
        xhat = xhat_ref[...]
        dxh = dy * g_ref[...]
        m1 = jnp.mean(dxh, axis=-1, keepdims=True)
        m2 = jnp.mean(dxh * xhat, axis=-1, keepdims=True)
        dz = rstd_ref[...] * (dxh - m1 - xhat * m2)
        dz_ref[...] = dz
        dzb_ref[...] = dz.astype(dzb_ref.dtype)
        dg_ref[...] += jnp.sum(dy * xhat, axis=0, keepdims=True)
        db_ref[...] += jnp.sum(dy, axis=0, keepdims=True)

    row = pl.BlockSpec((tm, D), lambda i: (i, 0))
    vec = pl.BlockSpec((1, D), lambda i: (0, 0))
    return _call(
        body, grid=(S // tm,), in_specs=[row, row, pl.BlockSpec((tm, 1), lambda i: (i, 0)), vec],
        out_specs=(row, row, vec, vec),
        out_shape=(jax.ShapeDtypeStruct((S, D), F32), jax.ShapeDtypeStruct((S, D), MXU_DTYPE),
                   jax.ShapeDtypeStruct((1, D), F32), jax.ShapeDtypeStruct((1, D), F32)),
        operands=[dy, xhat, rstd, g], name=name, chunks=link.take(ROWWISE_US["ln"][1]), bufs=link.bufs)


def rms_fwd(x, col, width, g, name):
    S = x.shape[0]
    tm = _tile(S, ROW_TILE, 8)

    def body(x_ref, g_ref, y_ref):
        xv = x_ref[...]
        r = lax.rsqrt(jnp.mean(xv * xv, axis=-1, keepdims=True) + RMS_EPS)
        y_ref[...] = (xv * r * g_ref[...]).astype(y_ref.dtype)

    return pl.pallas_call(
        body, out_shape=jax.ShapeDtypeStruct((S, width), MXU_DTYPE), grid=(S // tm,),
        in_specs=[pl.BlockSpec((tm, width), lambda i: (i, col)), pl.BlockSpec((1, width), lambda i: (0, 0))],
        out_specs=pl.BlockSpec((tm, width), lambda i: (i, 0)), compiler_params=_params(1), name=name)(x, g)


def rms_bwd(dy, x, col, width, g, name):
    S = x.shape[0]
    tm = _tile(S, ROW_TILE, 8)

    def body(dy_ref, x_ref, g_ref, dx_ref, dg_ref):
        @pl.when(pl.program_id(0) == 0)
        def _():
            dg_ref[...] = jnp.zeros_like(dg_ref)

        xv = x_ref[...]
        dy = dy_ref[...]
        r = lax.rsqrt(jnp.mean(xv * xv, axis=-1, keepdims=True) + RMS_EPS)
        dxn = dy * g_ref[...]
        m = jnp.mean(dxn * xv, axis=-1, keepdims=True)
        dx_ref[...] = r * (dxn - xv * (r * r * m))
        dg_ref[...] += jnp.sum(dy * xv * r, axis=0, keepdims=True)

    return pl.pallas_call(
        body, out_shape=(jax.ShapeDtypeStruct((S, width), F32), jax.ShapeDtypeStruct((1, width), F32)),
        grid=(S // tm,),
        in_specs=[pl.BlockSpec((tm, width), lambda i: (i, 0)), pl.BlockSpec((tm, width), lambda i: (i, col)),
                  pl.BlockSpec((1, width), lambda i: (0, 0))],
        out_specs=(pl.BlockSpec((tm, width), lambda i: (i, 0)), pl.BlockSpec((1, width), lambda i: (0, 0))),
        compiler_params=_params(1), name=name)(dy, x, g)


def ret_rope_fwd(proj, cos, sin, n_groups, half, k_from, k_scale, link, name):
    S = proj.shape[0]
    W = n_groups * 2 * half
    tm = _tile(S, ROW_TILE, 8)

    def body(x_ref, c_ref, s_ref, o_ref):
        c, s = c_ref[...], s_ref[...]
        for gi in range(n_groups):
            lo = gi * 2 * half
            x1 = x_ref[:, lo:lo + half]
            x2 = x_ref[:, lo + half:lo + 2 * half]
            sc = k_scale if gi >= k_from else 1.0
            o_ref[:, lo:lo + half] = ((x1 * c - x2 * s) * sc).astype(o_ref.dtype)
            o_ref[:, lo + half:lo + 2 * half] = ((x2 * c + x1 * s) * sc).astype(o_ref.dtype)

    tab = pl.BlockSpec((tm, half), lambda i: (i, 0))
    (out,) = _call(
        body, grid=(S // tm,), in_specs=[pl.BlockSpec((tm, W), lambda i: (i, 0)), tab, tab],
        out_specs=(pl.BlockSpec((tm, W), lambda i: (i, 0)),), out_shape=(jax.ShapeDtypeStruct((S, W), MXU_DTYPE),),
        operands=[proj, cos, sin], name=name, chunks=link.take(ROWWISE_US["rope"]), bufs=link.bufs)
    return out


def ret_rope_bwd(dq, dk, cos, sin, half, k_scale, name):
    S, Wq = dq.shape
    n_heads = Wq // (2 * half)
    tm = _tile(S, ROW_TILE, 8)

    def body(dq_ref, dk_ref, c_ref, s_ref, o_ref):
        c, s = c_ref[...], s_ref[...]
        for part, (d_ref, sc) in enumerate(((dq_ref, 1.0), (dk_ref, k_scale))):
            for hi in range(n_heads):
                lo = hi * 2 * half
                d1 = d_ref[:, lo:lo + half]
                d2 = d_ref[:, lo + half:lo + 2 * half]
                base = part * Wq + lo
                o_ref[:, base:base + half] = ((d1 * c + d2 * s) * sc).astype(o_ref.dtype)
                o_ref[:, base + half:base + 2 * half] = ((d2 * c - d1 * s) * sc).astype(o_ref.dtype)

    tab = pl.BlockSpec((tm, half), lambda i: (i, 0))
    row = pl.BlockSpec((tm, Wq), lambda i: (i, 0))
    return pl.pallas_call(
        body, out_shape=jax.ShapeDtypeStruct((S, 2 * Wq), MXU_DTYPE), grid=(S // tm,),
        in_specs=[row, row, tab, tab], out_specs=pl.BlockSpec((tm, 2 * Wq), lambda i: (i, 0)),
        compiler_params=_params(1), name=name)(dq, dk, cos, sin)


def mla_rope(x, ta, tb, n_heads, *, backward, head_sum, out_dtype, link=None, name):
    S = x.shape[0]
    W = n_heads * MLA_PAD
    tm = _tile(S, ROW_TILE, 8)

    def body(x_ref, a_ref, b_ref, o_ref, *rest):
        a, b = a_ref[...], b_ref[...]
        total = jnp.zeros((tm, LANES), F32)
        for hi in range(n_heads):
            lo = hi * MLA_PAD
            o_ref[:, lo:lo + MLA_NOPE] = x_ref[:, lo:lo + MLA_NOPE].astype(o_ref.dtype)
            t = x_ref[:, lo + MLA_NOPE:lo + MLA_PAD].astype(F32)
            if backward:
                r = t * a + pltpu.roll(t * b, LANES // 2, 1)
            else:
                r = t * a + pltpu.roll(t, LANES // 2, 1) * b
            o_ref[:, lo + MLA_NOPE:lo + MLA_PAD] = r.astype(o_ref.dtype)
            total = total + r
        if head_sum:
            rest[0][...] = total

    tab = pl.BlockSpec((tm, LANES), lambda i: (i, 0))
    row = pl.BlockSpec((tm, W), lambda i: (i, 0))
    out_shape = [jax.ShapeDtypeStruct((S, W), out_dtype)]
    out_specs = [row]
    if head_sum:
        out_shape.append(jax.ShapeDtypeStruct((S, LANES), F32))
        out_specs.append(tab)
    chunks = link.take(ROWWISE_US["rope"]) if link is not None else ()
    return _call(body, grid=(S // tm,), in_specs=[row, tab, tab], out_specs=out_specs, out_shape=out_shape,
                 operands=[x, ta, tb], name=name, chunks=chunks, bufs=link.bufs if link is not None else None)


def _visible(qi, kj, tq, tk, r0, rows):
    n = qi * tq + r0 + lax.broadcasted_iota(jnp.int32, (rows, tk), 0)
    m = kj * tk + lax.broadcasted_iota(jnp.int32, (rows, tk), 1)
    shift = CHUNK.bit_length() - 1
    vis = lax.shift_right_logical(m, shift) <= lax.shift_right_logical(n, shift)
    return vis, jnp.abs(n - m).astype(F32)


def attn_fwd(q, k, v, *, heads, softmax, masked, scale=1.0, log_gamma=None, link=None, est_us=0.0, name):
    (qa, dqk, q0, qs), (ka, _, k0, ks), (va, dv, v0, vs) = q, k, v
    Sq, Sk = qa.shape[0], ka.shape[0]
    tq, tk = _tile(Sq, ATTN_TILE, 8), _tile(Sk, ATTN_TILE, 8)
    nq, nk = Sq // tq, Sk // tk
    assert not masked or tq == tk
    sub = tq // ATTN_PARTS if tq % (16 * ATTN_PARTS) == 0 else tq

    def body(*refs):
        if softmax:
            q_ref, k_ref, v_ref, o_ref, ob_ref, lse_ref, m_ref, l_ref, acc_ref = refs
        else:
            q_ref, k_ref, v_ref, lg_ref, o_ref, acc_ref = refs
        qi, kj = pl.program_id(1), pl.program_id(2)

        @pl.when(kj == 0)
        def _():
            acc_ref[...] = jnp.zeros_like(acc_ref)
            if softmax:
                m_ref[...] = jnp.full_like(m_ref, NEG_INF)
                l_ref[...] = jnp.zeros_like(l_ref)

        def step(diagonal):
            kb = k_ref[...].astype(MXU_DTYPE)
            vb = v_ref[...].astype(MXU_DTYPE)
            parts = [pl.ds(r * sub, sub) for r in range(tq // sub)]
            scores = [lax.dot_general(q_ref[rows, :].astype(MXU_DTYPE), kb, (((1,), (1,)), ((), ())),
                                      preferred_element_type=F32) for rows in parts]
            for r, rows in enumerate(parts):
                s = scores[r]
                if diagonal or not softmax:
                    vis, dist = _visible(qi, kj, tq, tk, r * sub, sub)
                if softmax:
                    s = s * scale
                    if diagonal:
                        s = jnp.where(vis, s, NEG_INF)
                    m_old = m_ref[rows, :]
                    m_new = jnp.maximum(m_old, jnp.max(s, axis=-1, keepdims=True))
                    p = jnp.exp(s - m_new)
                    corr = jnp.exp(m_old - m_new)
                    l_ref[rows, :] = corr * l_ref[rows, :] + jnp.sum(p, axis=-1, keepdims=True)
                    acc_ref[rows, :] = (corr * acc_ref[rows, :]
                                        + jnp.dot(p.astype(MXU_DTYPE), vb, preferred_element_type=F32))
                    m_ref[rows, :] = m_new
                else:
                    decay = jnp.exp(lg_ref[0:1, 0:1] * dist)
                    if diagonal:
                        decay = jnp.where(vis, decay, 0.0)
                    acc_ref[rows, :] += jnp.dot((s * decay).astype(MXU_DTYPE), vb, preferred_element_type=F32)

        if masked:
            pl.when(kj < qi)(functools.partial(step, False))
            pl.when(kj == qi)(functools.partial(step, True))
        else:
            step(False)

        @pl.when(kj == nk - 1)
        def _():
            if softmax:
                out = acc_ref[...] / l_ref[...]
                o_ref[...] = out
                ob_ref[...] = out.astype(ob_ref.dtype)
                lse_ref[...] = m_ref[...] + jnp.log(l_ref[...])
            else:
                o_ref[...] = acc_ref[...].astype(o_ref.dtype)

    kcap = (lambda qi, kj: jnp.minimum(kj, qi)) if masked else (lambda qi, kj: kj)
    in_specs = [pl.BlockSpec((tq, dqk), lambda h, qi, kj: (qi, q0 + qs * h)),
                pl.BlockSpec((tk, dqk), lambda h, qi, kj: (kcap(qi, kj), k0 + ks * h)),
                pl.BlockSpec((tk, dv), lambda h, qi, kj: (kcap(qi, kj), v0 + vs * h))]
    operands = [qa, ka, va]
    out_shape = [jax.ShapeDtypeStruct((Sq, heads * dv), F32)]
    out_specs = [pl.BlockSpec((tq, dv), lambda h, qi, kj: (qi, h))]
    scratch = []
    if softmax:
        out_shape += [jax.ShapeDtypeStruct((Sq, heads * dv), MXU_DTYPE), jax.ShapeDtypeStruct((heads, Sq, 1), F32)]
        out_specs += [pl.BlockSpec((tq, dv), lambda h, qi, kj: (qi, h)),
                      pl.BlockSpec((None, tq, 1), lambda h, qi, kj: (h, qi, 0))]
        scratch += [pltpu.VMEM((tq, 1), F32), pltpu.VMEM((tq, 1), F32)]
    else:
        in_specs.append(pl.BlockSpec((None, 8, LANES), lambda h, qi, kj: (h, 0, 0)))
        operands.append(log_gamma)
    scratch.append(pltpu.VMEM((tq, dv), F32))
    chunks = link.take(est_us) if link is not None else ()
    return _call(body, grid=(heads, nq, nk), in_specs=in_specs, out_specs=out_specs, out_shape=out_shape,
                 operands=operands, scratch=scratch, name=name, chunks=chunks,
                 bufs=link.bufs if link is not None else None)


def attn_bwd(q, k, v, do, *, heads, softmax, masked, scale=1.0, log_gamma=None, o=None, lse=None,
             link=None, est_us=0.0, name):
    (qa, dqk, q0, qs), (ka, _, k0, ks), (va, dv, v0, vs) = q, k, v
    Sq, Sk = qa.shape[0], ka.shape[0]
    tq, tk = _tile(Sq, ATTN_TILE, 8), _tile(Sk, ATTN_TILE, 8)
    nq, nk = Sq // tq, Sk // tk
    assert not masked or tq == tk
    sub = tq // ATTN_PARTS if tq % (16 * ATTN_PARTS) == 0 else tq
    contract0 = (((0,), (0,)), ((), ()))

    def body(*refs):
        if softmax:
            q_ref, k_ref, v_ref, do_ref, o_ref, lse_ref, dq_ref, dk_ref, dv_ref = refs
        else:
            q_ref, k_ref, v_ref, do_ref, lg_ref, dq_ref, dk_ref, dv_ref = refs
        kj, qi = pl.program_id(1), pl.program_id(2)

        @pl.when(jnp.logical_and(kj == 0, qi == 0))
        def _():
            dq_ref[...] = jnp.zeros_like(dq_ref)

        @pl.when(qi == 0)
        def _():
            dk_ref[...] = jnp.zeros_like(dk_ref)
            dv_ref[...] = jnp.zeros_like(dv_ref)

        def step(diagonal):
            kb = k_ref[...].astype(MXU_DTYPE)
            vb = v_ref[...].astype(MXU_DTYPE)
            parts = [pl.ds(r * sub, sub) for r in range(tq // sub)]
            nt = (((1,), (1,)), ((), ()))
            qbs = [q_ref[rows, :].astype(MXU_DTYPE) for rows in parts]
            dobs = [do_ref[rows, :].astype(MXU_DTYPE) for rows in parts]
            scores = [lax.dot_general(qb, kb, nt, preferred_element_type=F32) for qb in qbs]
            dps = [lax.dot_general(dob, vb, nt, preferred_element_type=F32) for dob in dobs]
            dv_new, dk_new = dv_ref[...], dk_ref[...]
            for r, rows in enumerate(parts):
                s, dp, qb, dob = scores[r], dps[r], qbs[r], dobs[r]
                if diagonal or not softmax:
                    vis, dist = _visible(qi, kj, tq, tk, r * sub, sub)
                if softmax:
                    s = s * scale
                    if diagonal:
                        s = jnp.where(vis, s, NEG_INF)
                    p = jnp.exp(s - lse_ref[rows, :])
                    delta = jnp.sum(do_ref[rows, :].astype(F32) * o_ref[rows, :], axis=-1, keepdims=True)
                    ds = p * (dp - delta) * scale
                else:
                    decay = jnp.exp(lg_ref[0:1, 0:1] * dist)
                    if diagonal:
                        decay = jnp.where(vis, decay, 0.0)
                    p = s * decay
                    ds = dp * decay
                pb = p.astype(MXU_DTYPE)
                dsb = ds.astype(MXU_DTYPE)
                dv_new = dv_new + lax.dot_general(pb, dob, contract0, preferred_element_type=F32)
                dk_new = dk_new + lax.dot_general(dsb, qb, contract0, preferred_element_type=F32)
                out_rows = pl.ds(pl.multiple_of(qi * tq + r * sub, sub), sub)
                dq_ref[out_rows, :] += jnp.dot(dsb, kb, preferred_element_type=F32)
            dv_ref[...] = dv_new
            dk_ref[...] = dk_new

        if masked:
            pl.when(qi > kj)(functools.partial(step, False))
            pl.when(qi == kj)(functools.partial(step, True))
        else:
            step(False)

    qcap = (lambda kj, qi: jnp.maximum(qi, kj)) if masked else (lambda kj, qi: qi)
    in_specs = [pl.BlockSpec((tq, dqk), lambda h, kj, qi: (qcap(kj, qi), q0 + qs * h)),
                pl.BlockSpec((tk, dqk), lambda h, kj, qi: (kj, k0 + ks * h)),
                pl.BlockSpec((tk, dv), lambda h, kj, qi: (kj, v0 + vs * h)),
                pl.BlockSpec((tq, dv), lambda h, kj, qi: (qcap(kj, qi), h))]
    operands = [qa, ka, va, do]
    if softmax:
        in_specs += [pl.BlockSpec((tq, dv), lambda h, kj, qi: (qcap(kj, qi), h)),
                     pl.BlockSpec((None, tq, 1), lambda h, kj, qi: (h, qcap(kj, qi), 0))]
        operands += [o, lse]
    else:
        in_specs.append(pl.BlockSpec((None, 8, LANES), lambda h, kj, qi: (h, 0, 0)))
        operands.append(log_gamma)
    chunks = link.take(est_us) if link is not None else ()
    return _call(
        body, grid=(heads, nk, nq), in_specs=in_specs,
        out_specs=(pl.BlockSpec((Sq, dqk), lambda h, kj, qi: (0, h)),
                   pl.BlockSpec((tk, dqk), lambda h, kj, qi: (kj, h)),
                   pl.BlockSpec((tk, dv), lambda h, kj, qi: (kj, h))),
        out_shape=(jax.ShapeDtypeStruct((Sq, heads * dqk), F32), jax.ShapeDtypeStruct((Sk, heads * dqk), F32),
                   jax.ShapeDtypeStruct((Sk, heads * dv), F32)),
        operands=operands, name=name, chunks=chunks, bufs=link.bufs if link is not None else None)


def _sigmoid(x):
    return 0.5 * jnp.tanh(0.5 * x) + 0.5


def ret_gate_fwd(o, proj, gate_col, gn, heads, link, name):
    S, W = o.shape
    dv = W // heads
    tm = _tile(S, ROW_TILE // 2, 8)

    def body(o_ref, g_ref, gn_ref, y_ref):
        for hi in range(heads):
            cols = slice(hi * dv, (hi + 1) * dv)
            oh = o_ref[:, cols]
            mu = jnp.mean(oh, axis=-1, keepdims=True)
            oc = oh - mu
            rstd = lax.rsqrt(jnp.mean(oc * oc, axis=-1, keepdims=True) + LN_EPS)
            gt = g_ref[:, cols]
            y_ref[:, cols] = (gt * _sigmoid(gt) * (oc * rstd) * gn_ref[:, cols]).astype(y_ref.dtype)

    row = pl.BlockSpec((tm, W), lambda i: (i, 0))
    (y,) = _call(
        body, grid=(S // tm,),
        in_specs=[row, pl.BlockSpec((tm, W), lambda i: (i, gate_col)), pl.BlockSpec((1, W), lambda i: (0, 0))],
        out_specs=(row,), out_shape=(jax.ShapeDtypeStruct((S, W), MXU_DTYPE),), operands=[o, proj, gn], name=name,
        chunks=link.take(ROWWISE_US["gate"]), bufs=link.bufs)
    return y


def ret_gate_bwd(dy, o, proj, gate_col, gn, heads, name):
    S, W = o.shape
    dv = W // heads
    tm = _tile(S, ROW_TILE // 2, 8)

    def body(dy_ref, o_ref, g_ref, gn_ref, do_ref, dgt_ref, dgn_ref):
        @pl.when(pl.program_id(0) == 0)
        def _():
            dgn_ref[...] = jnp.zeros_like(dgn_ref)

        for hi in range(heads):
            cols = slice(hi * dv, (hi + 1) * dv)
            oh = o_ref[:, cols]
            mu = jnp.mean(oh, axis=-1, keepdims=True)
            oc = oh - mu
            rstd = lax.rsqrt(jnp.mean(oc * oc, axis=-1, keepdims=True) + LN_EPS)
            xhat = oc * rstd
            gt = g_ref[:, cols]
            sg = _sigmoid(gt)
            gain = gn_ref[:, cols]
            dy = dy_ref[:, cols]
            dgt_ref[:, cols] = (dy * xhat * gain * (sg * (1.0 + gt * (1.0 - sg)))).astype(dgt_ref.dtype)
            dn = dy * (gt * sg)
            dgn_ref[:, cols] += jnp.sum(dn * xhat, axis=0, keepdims=True)
            dxh = dn * gain
            m1 = jnp.mean(dxh, axis=-1, keepdims=True)
            m2 = jnp.mean(dxh * xhat, axis=-1, keepdims=True)
            do_ref[:, cols] = (rstd * (dxh - m1 - xhat * m2)).astype(do_ref.dtype)

    row = pl.BlockSpec((tm, W), lambda i: (i, 0))
    vec = pl.BlockSpec((1, W), lambda i: (0, 0))
    return pl.pallas_call(
        body,
        out_shape=(jax.ShapeDtypeStruct((S, W), MXU_DTYPE), jax.ShapeDtypeStruct((S, W), MXU_DTYPE),
                   jax.ShapeDtypeStruct((1, W), F32)),
        grid=(S // tm,), in_specs=[row, row, pl.BlockSpec((tm, W), lambda i: (i, gate_col)), vec],
        out_specs=(row, row, vec), compiler_params=_params(1), name=name)(dy, o, proj, gn)


def _shift_down(x, s):
    rows = lax.broadcasted_iota(jnp.int32, x.shape, 0)
    return jnp.where(rows >= s, pltpu.roll(x, s, 0), 0.0)


def _shift_up(x, s):
    n = x.shape[0]
    rows = lax.broadcasted_iota(jnp.int32, x.shape, 0)
    return jnp.where(rows < n - s, pltpu.roll(x, n - s, 0), 0.0)


def _conv3(x, w_ref, b_ref):
    return (w_ref[2:3, :] * x + w_ref[1:2, :] * _shift_down(x, 1) + w_ref[0:1, :] * _shift_down(x, 2)
            + b_ref[...])


def conv_glu_fwd(hup, w, b, link, name):
    S, W2 = hup.shape
    F = W2 // 2
    tc = _tile(F, LANES)
    nb = F // tc

    def body(g_ref, v_ref, wg_ref, wv_ref, bg_ref, bv_ref, u_ref):
        cg = _conv3(g_ref[...], wg_ref, bg_ref)
        cv = _conv3(v_ref[...], wv_ref, bv_ref)
        u_ref[...] = (cg * _sigmoid(cg) * cv).astype(u_ref.dtype)

    def col(rows, off):
        return pl.BlockSpec((rows, tc), lambda j: (0, j + off))

    (u,) = _call(
        body, grid=(nb,), in_specs=[col(S, 0), col(S, nb), col(3, 0), col(3, nb), col(1, 0), col(1, nb)],
        out_specs=(col(S, 0),), out_shape=(jax.ShapeDtypeStruct((S, F), MXU_DTYPE),),
        operands=[hup, hup, w, w, b, b], name=name, chunks=link.take(ROWWISE_US["conv"][0]), bufs=link.bufs)
    return u


def conv_glu_bwd(du, hup, w, b, link, name):
    S, W2 = hup.shape
    F = W2 // 2
    tc = _tile(F, LANES)
    nb = F // tc

    def back(dc, x, w_ref, dh_ref, dw_ref, db_ref):
        dw_ref[2:3, :] = jnp.sum(dc * x, axis=0, keepdims=True)
        dw_ref[1:2, :] = jnp.sum(dc * _shift_down(x, 1), axis=0, keepdims=True)
        dw_ref[0:1, :] = jnp.sum(dc * _shift_down(x, 2), axis=0, keepdims=True)
        db_ref[...] = jnp.sum(dc, axis=0, keepdims=True)
        dh_ref[...] = (w_ref[2:3, :] * dc + w_ref[1:2, :] * _shift_up(dc, 1)
                       + w_ref[0:1, :] * _shift_up(dc, 2)).astype(dh_ref.dtype)

    def body(du_ref, g_ref, v_ref, wg_ref, wv_ref, bg_ref, bv_ref,
             dhg_ref, dhv_ref, dwg_ref, dwv_ref, dbg_ref, dbv_ref):
        xg, xv = g_ref[...], v_ref[...]
        cg = _conv3(xg, wg_ref, bg_ref)
        cv = _conv3(xv, wv_ref, bv_ref)
        sg = _sigmoid(cg)
        du = du_ref[...]
        back(du * cv * (sg * (1.0 + cg * (1.0 - sg))), xg, wg_ref, dhg_ref, dwg_ref, dbg_ref)
        back(du * (cg * sg), xv, wv_ref, dhv_ref, dwv_ref, dbv_ref)

    def col(rows, off):
        return pl.BlockSpec((rows, tc), lambda j: (0, j + off))

    return _call(
        body, grid=(nb,),
        in_specs=[col(S, 0), col(S, 0), col(S, nb), col(3, 0), col(3, nb), col(1, 0), col(1, nb)],
        out_specs=(col(S, 0), col(S, 0), col(3, 0), col(3, 0), col(1, 0), col(1, 0)),
        out_shape=(jax.ShapeDtypeStruct((S, F), MXU_DTYPE), jax.ShapeDtypeStruct((S, F), MXU_DTYPE),
                   jax.ShapeDtypeStruct((3, F), F32), jax.ShapeDtypeStruct((3, F), F32),
                   jax.ShapeDtypeStruct((1, F), F32), jax.ShapeDtypeStruct((1, F), F32)),
        operands=[du, hup, hup, w, w, b, b], name=name, chunks=link.take(ROWWISE_US["conv"][1]), bufs=link.bufs)


def loss_head(y, target, name):
    S, D = y.shape
    tm = _tile(S, ROW_TILE, 8)

    def body(y_ref, t_ref, dy_ref, loss_ref):
        @pl.when(pl.program_id(0) == 0)
        def _():
            loss_ref[...] = jnp.zeros_like(loss_ref)

        e = y_ref[...] - t_ref[...]
        dy_ref[...] = e * (1.0 / D)
        part = jnp.sum(jnp.sum(e * e, axis=-1, keepdims=True), axis=0, keepdims=True) * (0.5 / D)
        loss_ref[...] += jnp.broadcast_to(part, loss_ref.shape)

    row = pl.BlockSpec((tm, D), lambda i: (i, 0))
    return pl.pallas_call(
        body, out_shape=(jax.ShapeDtypeStruct((S, D), F32), jax.ShapeDtypeStruct((8, LANES), F32)),
        grid=(S // tm,), in_specs=[row, row], out_specs=(row, pl.BlockSpec((8, LANES), lambda i: (0, 0))),
        compiler_params=_params(1), name=name)(y, target)


def adam_update(parts, w, m, v, name):
    n_layers, n_parts, R, C = parts.shape
    tr = _tile(R, max(8, (1 << 19) // C), 8)
    c1 = 1.0 - ADAM_B1 ** ADAM_STEP
    c2 = 1.0 - ADAM_B2 ** ADAM_STEP

    def body(p_ref, w_ref, m_ref, v_ref, g_ref, d_ref, nm_ref, nv_ref):
        g = p_ref[0].astype(F32)
        for pi in range(1, n_parts):
            g = g + p_ref[pi].astype(F32)
        nm = ADAM_B1 * m_ref[...] + (1.0 - ADAM_B1) * g
        nv = ADAM_B2 * v_ref[...] + (1.0 - ADAM_B2) * (g * g)
        g_ref[...] = g
        nm_ref[...] = nm
        nv_ref[...] = nv
        d_ref[...] = -ADAM_LR * ((nm / c1) / (jnp.sqrt(nv / c2) + ADAM_EPS) + ADAM_WD * w_ref[...])

    row = pl.BlockSpec((None, tr, C), lambda l, i: (l, i, 0))
    out = jax.ShapeDtypeStruct((n_layers, R, C), F32)
    return _call(
        body, grid=(n_layers, R // tr),
        in_specs=[pl.BlockSpec((None, n_parts, tr, C), lambda l, i: (l, 0, i, 0)), row, row, row],
        out_specs=(row, row, row, row), out_shape=(out, out, out, out), operands=[parts, w, m, v], name=name)


def pair_sum(own, got, core, name):
    _, R, C = own.shape
    tr = _tile(R, max(16, (1 << 19) // C), 16)

    def body(core_ref, a_ref, b_ref, o_ref):
        o_ref[...] = (a_ref[...].astype(F32) + b_ref[...].astype(F32)).astype(o_ref.dtype)

    grid_spec = pltpu.PrefetchScalarGridSpec(
        num_scalar_prefetch=1, grid=(N_CHIP, R // tr),
        in_specs=[pl.BlockSpec((None, tr, C), lambda j, i, core_ref: (2 * j + core_ref[0], i, 0)),
                  pl.BlockSpec((None, tr, C), lambda j, i, core_ref: (j, i, 0))],
        out_specs=pl.BlockSpec((None, tr, C), lambda j, i, core_ref: (j, i, 0)))
    return pl.pallas_call(
        body, out_shape=jax.ShapeDtypeStruct((N_CHIP, R, C), own.dtype), grid_spec=grid_spec,
        compiler_params=_params(2), name=name)(core, own, got)


def gather_small(block, name):
    m_per, n = block.shape

    def body(x_ref, out_ref, send_sems, recv_sems, local_sem):
        x, y, c = _place()
        me, sibling = (x, y, c), (x, y, 1 - c)
        chips = _other_chips(x, y)

        def rows(px, py, pc):
            return out_ref.at[pl.ds((4 * px + 2 * py + pc) * m_per, m_per), :]

        def copy(k, blk, to, src=None):
            return pltpu.make_async_remote_copy(
                src_ref=rows(*blk) if src is None else src, dst_ref=rows(*blk),
                send_sem=send_sems.at[k], recv_sem=recv_sems.at[k], device_id=to, device_id_type=MESH)

        mine = pltpu.make_async_copy(x_ref, rows(*me), local_sem)
        mine.start()
        first = [copy(0, me, sibling, src=x_ref)]
        first += [copy(1 + j, me, (*chip, c), src=x_ref) for j, chip in enumerate(chips)]
        for cp in first:
            cp.start()
        passed = [copy(4 + j, (*chip, c), sibling) for j, chip in enumerate(chips)]
        for j, chip in enumerate(chips):
            copy(1 + j, (*chip, c), me).wait_recv()
            passed[j].start()
        copy(0, sibling, me).wait_recv()
        for j, chip in enumerate(chips):
            copy(4 + j, (*chip, 1 - c), me).wait_recv()
        for cp in first + passed:
            cp.wait_send()
        mine.wait()

    return pl.pallas_call(
        body, out_shape=jax.ShapeDtypeStruct((N_DEV * m_per, n), block.dtype),
        in_specs=[pl.BlockSpec(memory_space=pltpu.VMEM)], out_specs=pl.BlockSpec(memory_space=pltpu.VMEM),
        scratch_shapes=[pltpu.SemaphoreType.DMA((7,)), pltpu.SemaphoreType.DMA((7,)), pltpu.SemaphoreType.DMA],
        compiler_params=pltpu.CompilerParams(vmem_limit_bytes=VMEM_LIMIT_BYTES), name=name)(block)


def _rope_tables(positions, d):
    inv_freq = ROPE_BASE ** (-jnp.arange(0, d, 2, dtype=F32) / d)
    ang = positions.astype(F32)[:, None] * inv_freq
    return jnp.cos(ang), jnp.sin(ang)


def _mla_pad(nope, rope):
    S, H, _ = nope.shape
    half = MLA_ROPE // 2
    z = jnp.zeros((S, H, LANES // 2 - half), nope.dtype)
    return jnp.concatenate([nope, rope[..., :half], z, rope[..., half:], z], axis=2).reshape(S, H * MLA_PAD)


def _mla_unpad(x, H):
    S = x.shape[0]
    half = MLA_ROPE // 2
    x3 = x.reshape(S, H, MLA_PAD)
    rope = jnp.concatenate([x3[..., MLA_NOPE:MLA_NOPE + half],
                            x3[..., MLA_NOPE + LANES // 2:MLA_NOPE + LANES // 2 + half]], axis=2)
    return x3[..., :MLA_NOPE], rope


def kernel(x, mem, positions, ret_w_in, ret_gn_g, ret_w_out, mla_w_in, mla_q_norm_g, mla_w_uq, mla_kv_norm_g, mla_w_ukv, mla_w_out, xa_w_q, xa_w_kv, xa_w_out, ffn_w_up, ffn_conv_w, ffn_conv_b, ffn_w_down, ln_mix_g, ln_mix_b, ln_mem_g, ln_mem_b, ln_ffn_g, ln_ffn_b, loss_target, m_ret_w_in, m_ret_gn_g, m_ret_w_out, m_mla_w_in, m_mla_q_norm_g, m_mla_w_uq, m_mla_kv_norm_g, m_mla_w_ukv, m_mla_w_out, m_xa_w_q, m_xa_w_kv, m_xa_w_out, m_ffn_w_up, m_ffn_conv_w, m_ffn_conv_b, m_ffn_w_down, m_ln_mix_g, m_ln_mix_b, m_ln_mem_g, m_ln_mem_b, m_ln_ffn_g, m_ln_ffn_b, v_ret_w_in, v_ret_gn_g, v_ret_w_out, v_mla_w_in, v_mla_q_norm_g, v_mla_w_uq, v_mla_kv_norm_g, v_mla_w_ukv, v_mla_w_out, v_xa_w_q, v_xa_w_kv, v_xa_w_out, v_ffn_w_up, v_ffn_conv_w, v_ffn_conv_b, v_ffn_w_down, v_ln_mix_g, v_ln_mix_b, v_ln_mem_g, v_ln_mem_b, v_ln_ffn_g, v_ln_ffn_b):
    weights = dict(ret_w_in=ret_w_in, ret_gn_g=ret_gn_g, ret_w_out=ret_w_out, mla_w_in=mla_w_in,
                   mla_q_norm_g=mla_q_norm_g, mla_w_uq=mla_w_uq, mla_kv_norm_g=mla_kv_norm_g,
                   mla_w_ukv=mla_w_ukv, mla_w_out=mla_w_out, xa_w_q=xa_w_q, xa_w_kv=xa_w_kv, xa_w_out=xa_w_out,
                   ffn_w_up=ffn_w_up, ffn_conv_w=ffn_conv_w, ffn_conv_b=ffn_conv_b, ffn_w_down=ffn_w_down,
                   ln_mix_g=ln_mix_g, ln_mix_b=ln_mix_b, ln_mem_g=ln_mem_g, ln_mem_b=ln_mem_b,
                   ln_ffn_g=ln_ffn_g, ln_ffn_b=ln_ffn_b)
    mom_m = dict(ret_w_in=m_ret_w_in, ret_gn_g=m_ret_gn_g, ret_w_out=m_ret_w_out, mla_w_in=m_mla_w_in,
                 mla_q_norm_g=m_mla_q_norm_g, mla_w_uq=m_mla_w_uq, mla_kv_norm_g=m_mla_kv_norm_g,
                 mla_w_ukv=m_mla_w_ukv, mla_w_out=m_mla_w_out, xa_w_q=m_xa_w_q, xa_w_kv=m_xa_w_kv,
                 xa_w_out=m_xa_w_out, ffn_w_up=m_ffn_w_up, ffn_conv_w=m_ffn_conv_w, ffn_conv_b=m_ffn_conv_b,
                 ffn_w_down=m_ffn_w_down, ln_mix_g=m_ln_mix_g, ln_mix_b=m_ln_mix_b, ln_mem_g=m_ln_mem_g,
                 ln_mem_b=m_ln_mem_b, ln_ffn_g=m_ln_ffn_g, ln_ffn_b=m_ln_ffn_b)
    mom_v = dict(ret_w_in=v_ret_w_in, ret_gn_g=v_ret_gn_g, ret_w_out=v_ret_w_out, mla_w_in=v_mla_w_in,
                 mla_q_norm_g=v_mla_q_norm_g, mla_w_uq=v_mla_w_uq, mla_kv_norm_g=v_mla_kv_norm_g,
                 mla_w_ukv=v_mla_w_ukv, mla_w_out=v_mla_w_out, xa_w_q=v_xa_w_q, xa_w_kv=v_xa_w_kv,
                 xa_w_out=v_xa_w_out, ffn_w_up=v_ffn_w_up, ffn_conv_w=v_ffn_conv_w, ffn_conv_b=v_ffn_conv_b,
                 ffn_w_down=v_ffn_w_down, ln_mix_g=v_ln_mix_g, ln_mix_b=v_ln_mix_b, ln_mem_g=v_ln_mem_g,
                 ln_mem_b=v_ln_mem_b, ln_ffn_g=v_ln_ffn_g, ln_ffn_b=v_ln_ffn_b)
    order = list(weights)
    BIG = dict(ret_w_in=2, ret_w_out=1, mla_w_in=1, mla_w_uq=2, mla_w_ukv=2, mla_w_out=1,
               xa_w_q=1, xa_w_kv=2, xa_w_out=1, ffn_w_up=2, ffn_w_down=1)
    SMALL_CUT = ("ffn_conv_w", "mla_q_norm_g", "mla_kv_norm_g")
    REPLICATED = ("ret_gn_g", "ffn_conv_b", "ln_mix_g", "ln_mix_b", "ln_mem_g", "ln_mem_b", "ln_ffn_g", "ln_ffn_b")

    x = x[0]
    mem = mem[0]
    positions = positions[0]
    target = loss_target[0]
    S, D = x.shape
    depth = ln_mix_g.shape[0]
    alpha = (2 * depth) ** 0.25
    ret_dk = D // RET_HEADS
    ret_dv = 2 * D // RET_HEADS
    ret_qkw = RET_HEADS * ret_dk
    ret_vw = RET_HEADS * ret_dv
    xa_d = D // XA_HEADS
    assert 2 * ret_qkw == ret_vw and MLA_NOPE == LANES and MLA_V == LANES and MLA_ROPE == LANES // 2
    assert MLA_Q_RANK == MLA_KV_RANK and CHUNK & (CHUNK - 1) == 0
    core = lax.axis_index("c").astype(jnp.int32).reshape(1)
    dev = 4 * lax.axis_index("x") + 2 * lax.axis_index("y") + lax.axis_index("c")

    gather, scatter = _Link(GATHER_BYTES_PER_US), _Link(SCATTER_BYTES_PER_US)
    shard_b = {n: weights[n].astype(PAY_DTYPE) for n in BIG}
    staged = set()

    def units(layer):
        mixer = ("ret_w_in", "ret_w_out") if layer % 2 == 0 else ("mla_w_in", "mla_w_uq", "mla_w_ukv", "mla_w_out")
        return [(n, layer // 2) for n in mixer] + [(n, layer) for n in
                                                    ("xa_w_q", "xa_w_kv", "xa_w_out", "ffn_w_up", "ffn_w_down")]

    def geometry(n):
        _, k, nn = weights[n].shape
        return BIG[n], k, nn

    def after_first_level(key, full, axis, k, nn):
        def to_sibling():
            chunk = _Chunk("gather2", None, key, full, 0, 0, k, nn, axis=axis, k=k, n=nn)
            gather.side.append((chunk, lambda: staged.add(key)))

        def relay():
            gather.push("relay", None, key, full, 0, k // 2, nn, done=to_sibling, front=True,
                        axis=axis, k=k, n=nn, half=k // 2)
        return relay

    for layer in range(depth):
        for n, j in units(layer):
            axis, k, nn = geometry(n)
            full = (N_DEV * k, nn) if axis == 1 else (k, N_DEV * nn)
            gather.push("gather", shard_b[n], (n, j), full, j, k, nn,
                        done=after_first_level((n, j), full, axis, k, nn), axis=axis, k=k, n=nn)

    def weight(n, j):
        key = (n, j)
        while key not in staged:
            gather.flush(key, name="gather_rest")
            gather.settle()
            if not any(ch.key == key for ch in gather.queue):
                break
        while key not in staged:
            assert gather.side
            gather.carry(LEVEL2_US, name="gather_level2")
            gather.settle()
        return gather.bufs[key]

    def pack(arrays):
        flat = jnp.concatenate([a.reshape(-1) for a in arrays])
        rows = -(-flat.shape[0] // (8 * LANES)) * 8
        return jnp.pad(flat, (0, rows * LANES - flat.shape[0])).reshape(rows, LANES)

    def unpack(flat, like):
        out, at = [], 0
        for a in like:
            out.append(flat[at:at + a.size].reshape(a.shape))
            at += a.size
        return out

    small_local = [weights[n] for n in SMALL_CUT]
    blk = pack(small_local)
    allsmall = gather_small(blk, name="gather_small_weights").reshape(N_DEV, -1)
    per_dev = [unpack(allsmall[d], small_local) for d in range(N_DEV)]
    conv_w_full, qg_full, kvg_full = [jnp.concatenate([per_dev[d][i] for d in range(N_DEV)], axis=-1)
                                      for i in range(len(SMALL_CUT))]

    rcos, rsin = _rope_tables(positions, ret_dk)
    mcos, msin = _rope_tables(positions, MLA_ROPE)
    zq = jnp.zeros((S, LANES // 2 - MLA_ROPE // 2), F32)
    mla_ta = jnp.concatenate([mcos, zq, mcos, zq], axis=1)
    mla_tb = jnp.concatenate([-msin, zq, msin, zq], axis=1)
    log_gamma = jnp.log(1.0 - 2.0 ** (-5.0 - jnp.arange(RET_HEADS, dtype=F32)))
    log_gamma = jnp.broadcast_to(log_gamma[:, None, None], (RET_HEADS, 8, LANES))
    mla_scale = (MLA_NOPE + MLA_ROPE) ** -0.5
    xa_scale = xa_d ** -0.5
    mem_b = mem.astype(MXU_DTYPE)

    def vec(a, l):
        return a[l][None, :]

    saved = []
    h = x
    hb = x.astype(MXU_DTYPE)
    for layer in range(depth):
        j = layer // 2
        sv = {}
        sv["hb_mix"] = hb
        if layer % 2 == 0:
            proj = mm(hb, weight("ret_w_in", j), link=gather, name="ret_in")
            qk = ret_rope_fwd(proj, rcos, rsin, 2 * RET_HEADS, ret_dk // 2, RET_HEADS, ret_dk ** -0.5, gather,
                              name="ret_rope")
            (o,) = attn_fwd((qk, ret_dk, 0, 1), (qk, ret_dk, RET_HEADS, 1), (proj, ret_dv, RET_HEADS, 1),
                            heads=RET_HEADS, softmax=False, masked=True, log_gamma=log_gamma, link=gather,
                            est_us=ATTN_US["ret"][0], name="ret_attn")
            yb = ret_gate_fwd(o, proj, 2, vec(ret_gn_g, j), RET_HEADS, gather, name="ret_gate")
            mix = mm(yb, weight("ret_w_out", j), link=gather, name="ret_out")
            sv.update(proj=proj, qk=qk, o=o, yb=yb)
        else:
            proj = mm(hb, weight("mla_w_in", j), link=gather, name="mla_in")
            cq = rms_fwd(proj, 0, MLA_Q_RANK, vec(qg_full, j), name="mla_q_norm")
            ckv = rms_fwd(proj, 1, MLA_KV_RANK, vec(kvg_full, j), name="mla_kv_norm")
            qf = mm(cq, weight("mla_w_uq", j), link=gather, name="mla_uq")
            kvf = mm(ckv, weight("mla_w_ukv", j), link=gather, name="mla_ukv")
            q3 = qf.reshape(S, MLA_HEADS, MLA_NOPE + MLA_ROPE)
            kv3 = kvf.reshape(S, MLA_HEADS, MLA_NOPE + MLA_V)
            k_rope = jnp.broadcast_to(proj[:, None, MLA_Q_RANK + MLA_KV_RANK:], (S, MLA_HEADS, MLA_ROPE))
            q_pad = _mla_pad(q3[..., :MLA_NOPE], q3[..., MLA_NOPE:])
            k_pad = _mla_pad(kv3[..., :MLA_NOPE], k_rope)
            (qr,) = mla_rope(q_pad, mla_ta, mla_tb, MLA_HEADS, backward=False, head_sum=False,
                             out_dtype=MXU_DTYPE, link=gather, name="mla_rope_q")
            (kr,) = mla_rope(k_pad, mla_ta, mla_tb, MLA_HEADS, backward=False, head_sum=False,
                             out_dtype=MXU_DTYPE, link=gather, name="mla_rope_k")
            o, ob, lse = attn_fwd((qr, MLA_PAD, 0, 1), (kr, MLA_PAD, 0, 1), (kvf, MLA_V, 1, 2), heads=MLA_HEADS,
                              softmax=True, masked=True, scale=mla_scale, link=gather, est_us=ATTN_US["mla"][0],
                              name="mla_attn")
            mix = mm(ob, weight("mla_w_out", j), link=gather, name="mla_out")
            sv.update(proj=proj, cq=cq, ckv=ckv, kvf=kvf, qr=qr, kr=kr, o=o, ob=ob, lse=lse)
        h, hb, sv["xhat_mix"], sv["rstd_mix"] = ln_fwd(h, mix, vec(ln_mix_g, layer), vec(ln_mix_b, layer), alpha,
                                                       gather, name="ln_mix")
        sv["hb_mem"] = hb
        q = mm(hb, weight("xa_w_q", layer), link=gather, name="xa_q")
        kvm = mm(mem_b, weight("xa_w_kv", layer), link=gather, name="xa_kv")
        o, ob, lse = attn_fwd((q, xa_d, 0, 1), (kvm, xa_d, 0, 1), (kvm, xa_d, XA_HEADS, 1), heads=XA_HEADS,
                          softmax=True, masked=False, scale=xa_scale, link=gather, est_us=ATTN_US["xa"][0],
                          name="xa_attn")
        mix = mm(ob, weight("xa_w_out", layer), link=gather, name="xa_out")
        sv.update(xa_q=q, xa_kvm=kvm, xa_o=o, xa_ob=ob, xa_lse=lse)
        h, hb, sv["xhat_mem"], sv["rstd_mem"] = ln_fwd(h, mix, vec(ln_mem_g, layer), vec(ln_mem_b, layer), alpha,
                                                       gather, name="ln_mem")
        sv["hb_ffn"] = hb
        hup = mm(hb, weight("ffn_w_up", layer), link=gather, name="ffn_up")
        u = conv_glu_fwd(hup, conv_w_full[layer], vec(ffn_conv_b, layer), gather, name="ffn_conv")
        mix = mm(u, weight("ffn_w_down", layer), link=gather, name="ffn_down")
        sv.update(hup=hup, u=u)
        h, hb, sv["xhat_ffn"], sv["rstd_ffn"] = ln_fwd(h, mix, vec(ln_ffn_g, layer), vec(ln_ffn_b, layer), alpha,
                                                       gather, name="ln_ffn")
        saved.append(sv)

    dh, loss_blk = loss_head(h, target, name="loss_head")
    loss = lax.psum(loss_blk[0, 0], ("x", "y", "c"))

    small = {n: [None] * weights[n].shape[0] for n in REPLICATED + SMALL_CUT}
    W = gather.bufs

    def wgrad(n, a, d, l, tag):
        axis, k, nn = geometry(n)
        if axis == 2:
            slabs = mm(a, d, mode="tn", out_dtype=PAY_DTYPE, slab_width=nn, link=scatter, name=tag)
        else:
            slabs = mm(a, d, mode="tn", out_dtype=PAY_DTYPE, link=scatter, name=tag).reshape(N_DEV, k, nn)
        n_layers = weights[n].shape[0]
        pair = ("pair", n, l)

        def then():
            sums = pair_sum(slabs, scatter.bufs.pop(pair), core, name="pairsum_" + n)
            scatter.push("scatter", sums, n, (n_layers, N_CHIP, k, nn), l, k, nn)

        scatter.side.append((_Chunk("swap", slabs, pair, (N_CHIP, k, nn), 0, 0, k, nn), then))

    for layer in reversed(range(depth)):
        j = layer // 2
        sv = saved[layer]
        dz, dzb, dg, db = ln_bwd(dh, sv["xhat_ffn"], sv["rstd_ffn"], vec(ln_ffn_g, layer), scatter, name="ln_ffn_bwd")
        small["ln_ffn_g"][layer], small["ln_ffn_b"][layer] = dg[0], db[0]
        wgrad("ffn_w_down", sv["u"], dzb, layer, "ffn_down_dw")
        du = mm(dzb, W[("ffn_w_down", layer)], mode="nt", link=scatter, name="ffn_down_dx")
        dhg, dhv, dwg, dwv, dbg, dbv = conv_glu_bwd(du, sv["hup"], conv_w_full[layer], vec(ffn_conv_b, layer),
                                                    scatter, name="ffn_conv_bwd")
        small["ffn_conv_w"][layer] = jnp.concatenate([dwg, dwv], axis=1)
        small["ffn_conv_b"][layer] = jnp.concatenate([dbg, dbv], axis=1)[0]
        dhup = jnp.concatenate([dhg, dhv], axis=1)
        wgrad("ffn_w_up", sv["hb_ffn"], dhup, layer, "ffn_up_dw")
        dh = mm(dhup, W[("ffn_w_up", layer)], mode="nt", add=dz, add_scale=alpha, link=scatter, name="ffn_up_dx")
        dz, dzb, dg, db = ln_bwd(dh, sv["xhat_mem"], sv["rstd_mem"], vec(ln_mem_g, layer), scatter, name="ln_mem_bwd")
        small["ln_mem_g"][layer], small["ln_mem_b"][layer] = dg[0], db[0]
        wgrad("xa_w_out", sv["xa_ob"], dzb, layer, "xa_out_dw")
        do = mm(dzb, W[("xa_w_out", layer)], mode="nt", link=scatter, name="xa_out_dx")
        dq, dk, dv = attn_bwd((sv["xa_q"], xa_d, 0, 1), (sv["xa_kvm"], xa_d, 0, 1), (sv["xa_kvm"], xa_d, XA_HEADS, 1),
                              do, heads=XA_HEADS, softmax=True, masked=False, scale=xa_scale, o=sv["xa_o"],
                              lse=sv["xa_lse"], link=scatter, est_us=ATTN_US["xa"][1], name="xa_attn_bwd")
        dqb = dq.astype(MXU_DTYPE)
        wgrad("xa_w_kv", mem_b, jnp.concatenate([dk, dv], axis=1).astype(MXU_DTYPE), layer, "xa_kv_dw")
        wgrad("xa_w_q", sv["hb_mem"], dqb, layer, "xa_q_dw")
        dh = mm(dqb, W[("xa_w_q", layer)], mode="nt", add=dz, add_scale=alpha, link=scatter, name="xa_q_dx")
        dz, dzb, dg, db = ln_bwd(dh, sv["xhat_mix"], sv["rstd_mix"], vec(ln_mix_g, layer), scatter, name="ln_mix_bwd")
        small["ln_mix_g"][layer], small["ln_mix_b"][layer] = dg[0], db[0]
        if layer % 2 == 0:
            wgrad("ret_w_out", sv["yb"], dzb, j, "ret_out_dw")
            dy = mm(dzb, W[("ret_w_out", j)], mode="nt", link=scatter, name="ret_out_dx")
            do, dgate, dgn = ret_gate_bwd(dy, sv["o"], sv["proj"], 2, vec(ret_gn_g, j), RET_HEADS, name="ret_gate_bwd")
            small["ret_gn_g"][j] = dgn[0]
            dq, dk, dv = attn_bwd((sv["qk"], ret_dk, 0, 1), (sv["qk"], ret_dk, RET_HEADS, 1),
                                  (sv["proj"], ret_dv, RET_HEADS, 1), do, heads=RET_HEADS, softmax=False,
                                  masked=True, log_gamma=log_gamma, link=scatter, est_us=ATTN_US["ret"][1],
                                  name="ret_attn_bwd")
            dqk = ret_rope_bwd(dq, dk, rcos, rsin, ret_dk // 2, ret_dk ** -0.5, name="ret_rope_bwd")
            dproj = jnp.concatenate([dqk, dv.astype(MXU_DTYPE), dgate], axis=1)
            wgrad("ret_w_in", sv["hb_mix"], dproj, j, "ret_in_dw")
            dh = mm(dproj, W[("ret_w_in", j)], mode="nt", add=dz, add_scale=alpha, link=scatter, name="ret_in_dx")
        else:
            wgrad("mla_w_out", sv["ob"], dzb, j, "mla_out_dw")
            do = mm(dzb, W[("mla_w_out", j)], mode="nt", link=scatter, name="mla_out_dx")
            dq, dk, dv = attn_bwd((sv["qr"], MLA_PAD, 0, 1), (sv["kr"], MLA_PAD, 0, 1), (sv["kvf"], MLA_V, 1, 2), do,
                                  heads=MLA_HEADS, softmax=True, masked=True, scale=mla_scale, o=sv["o"],
                                  lse=sv["lse"], link=scatter, est_us=ATTN_US["mla"][1], name="mla_attn_bwd")
            (dq_un,) = mla_rope(dq, mla_ta, mla_tb, MLA_HEADS, backward=True, head_sum=False, out_dtype=F32,
                                name="mla_rope_q_bwd")
            dk_un, dk_rope_sum = mla_rope(dk, mla_ta, mla_tb, MLA_HEADS, backward=True, head_sum=True, out_dtype=F32,
                                          name="mla_rope_k_bwd")
            dq_nope, dq_rope = _mla_unpad(dq_un, MLA_HEADS)
            dqf = jnp.concatenate([dq_nope, dq_rope], axis=2).reshape(S, -1).astype(MXU_DTYPE)
            dk_nope, _ = _mla_unpad(dk_un, MLA_HEADS)
            dkvf = jnp.concatenate([dk_nope, dv.reshape(S, MLA_HEADS, MLA_V)], axis=2).reshape(S, -1).astype(MXU_DTYPE)
            half = MLA_ROPE // 2
            dk_rope = jnp.concatenate([dk_rope_sum[:, :half], dk_rope_sum[:, LANES // 2:LANES // 2 + half]], axis=1)
            wgrad("mla_w_uq", sv["cq"], dqf, j, "mla_uq_dw")
            dcq = mm(dqf, W[("mla_w_uq", j)], mode="nt", link=scatter, name="mla_uq_dx")
            wgrad("mla_w_ukv", sv["ckv"], dkvf, j, "mla_ukv_dw")
            dckv = mm(dkvf, W[("mla_w_ukv", j)], mode="nt", link=scatter, name="mla_ukv_dx")
            dcq_in, dqg = rms_bwd(dcq, sv["proj"], 0, MLA_Q_RANK, vec(qg_full, j), name="mla_q_norm_bwd")
            dckv_in, dkvg = rms_bwd(dckv, sv["proj"], 1, MLA_KV_RANK, vec(kvg_full, j), name="mla_kv_norm_bwd")
            small["mla_q_norm_g"][j], small["mla_kv_norm_g"][j] = dqg[0], dkvg[0]
            dproj = jnp.concatenate([dcq_in, dckv_in, dk_rope], axis=1).astype(MXU_DTYPE)
            wgrad("mla_w_in", sv["hb_mix"], dproj, j, "mla_in_dw")
            dh = mm(dproj, W[("mla_w_in", j)], mode="nt", add=dz, add_scale=alpha, link=scatter, name="mla_in_dx")
    grad_x = dh[None]

    grads, deltas, new_m, new_v = {}, {}, {}, {}
    scatter.ride_out(name="scatter_rest")
    scatter.flush(None, name="scatter_rest")
    assert not scatter.queue and not scatter.side
    for n in BIG:
        grads[n], deltas[n], new_m[n], new_v[n] = adam_update(scatter.bufs[n], weights[n], mom_m[n], mom_v[n],
                                                              name="adam_" + n)

    small_names = list(REPLICATED + SMALL_CUT)
    partial = [piece for n in small_names for piece in small[n]]
    allpart = gather_small(pack(partial), name="gather_small_grads")
    rows = allpart.shape[0] // N_DEV
    allpart = allpart.reshape(N_DEV, rows, LANES)

    rep_names = list(REPLICATED)
    rep_w = pack([weights[n] for n in rep_names])
    rep_m = pack([mom_m[n] for n in rep_names])
    rep_v = pack([mom_v[n] for n in rep_names])
    rep_rows = rep_w.shape[0]
    rep_size = sum(weights[n].size for n in rep_names)
    flat_parts = allpart.reshape(N_DEV, rows * LANES)
    rep_parts = jnp.pad(flat_parts[:, :rep_size], ((0, 0), (0, rep_rows * LANES - rep_size)))
    outs = adam_update(rep_parts.reshape(1, N_DEV, rep_rows, LANES), rep_w[None], rep_m[None], rep_v[None],
                       name="adam_replicated")
    for o_, dst in zip(outs, (grads, deltas, new_m, new_v)):
        for n, a in zip(rep_names, unpack(o_.reshape(-1), [weights[n] for n in rep_names])):
            dst[n] = a

    cut_names = list(SMALL_CUT)
    cut_parts = []
    at = rep_size
    for n in cut_names:
        full_shape = (len(small[n]),) + small[n][0].shape
        whole = flat_parts[:, at:at + math.prod(full_shape)].reshape((N_DEV,) + full_shape)
        at += math.prod(full_shape)
        width = weights[n].shape[-1]
        start = (0,) * (whole.ndim - 1) + (dev * width,)
        mine = lax.dynamic_slice(whole, start, whole.shape[:-1] + (width,))
        cut_parts.append(mine.reshape(N_DEV, -1))
    cut_parts = jnp.concatenate(cut_parts, axis=1)
    cut_w = pack([weights[n] for n in cut_names])
    cut_m = pack([mom_m[n] for n in cut_names])
    cut_v = pack([mom_v[n] for n in cut_names])
    cut_rows = cut_w.shape[0]
    cut_parts = jnp.pad(cut_parts, ((0, 0), (0, cut_rows * LANES - cut_parts.shape[1])))
    outs = adam_update(cut_parts.reshape(1, N_DEV, cut_rows, LANES), cut_w[None], cut_m[None], cut_v[None],
                       name="adam_small_cut")
    for o_, dst in zip(outs, (grads, deltas, new_m, new_v)):
        for n, a in zip(cut_names, unpack(o_.reshape(-1), [weights[n] for n in cut_names])):
            dst[n] = a

    return (loss, grad_x, *[grads[n] for n in order], *[deltas[n] for n in order],
            *[new_m[n] for n in order], *[new_v[n] for n in order])
```
